```python
import math
import jax, jax.numpy as jnp
from jax import lax
import numpy as np

D_MODEL = 1024
BATCH = 4
SEQ = 8192
DEPTH = 1
DEC_BATCH = 4
DEC_SEQ = 4096
PAST_LEN = 128

POOL_WIDTH = 512
POOL_WINDOWS = (2, 4, 8, 16)
N_POOL_GROUPS = 4
POOL_GROUP_CH = POOL_WIDTH // N_POOL_GROUPS
N_Q_HEADS = 16
N_KV_HEADS = 4
HEAD_DIM = 64
GQA_GROUP = N_Q_HEADS // N_KV_HEADS
ATTN_WIDTH = N_Q_HEADS * HEAD_DIM
KV_WIDTH = N_KV_HEADS * HEAD_DIM
WINDOW = 128
BLOCK = 128
N_REL_BUCKETS = 32
REL_MAX_DISTANCE = 128
MEM_TOKENS = 256
MEM_HEADS = 4
MEM_HEAD_DIM = 128
MEM_WIDTH = MEM_HEADS * MEM_HEAD_DIM
N_BRANCHES = 3
IN_WIDTH = POOL_WIDTH + ATTN_WIDTH + 2 * KV_WIDTH + MEM_WIDTH + N_BRANCHES * D_MODEL
N_EXPERT_GROUPS = 4
EXPERTS_PER_GROUP = 8
N_EXPERTS = N_EXPERT_GROUPS * EXPERTS_PER_GROUP
TOP_K = 2
EXPERT_HIDDEN = 512
MOE_BLOCK = 128
ALPHA = (2 * DEPTH) ** 0.25
BETA = (8 * DEPTH) ** -0.25
LN_EPS = 1e-5
NEG_INF = -1e30

kernel_name = 'hybrid_pool_swa_mem_hmoe_encoder'


def layer_norm(x, g, b):
    xf = x.astype(jnp.float32)
    mu = jnp.mean(xf, axis=-1, keepdims=True)
    var = jnp.mean(jnp.square(xf - mu), axis=-1, keepdims=True)
    return ((xf - mu) * lax.rsqrt(var + LN_EPS) * g.astype(jnp.float32) + b.astype(jnp.float32)).astype(x.dtype)


def t5_bucket(rel):
    nb = N_REL_BUCKETS // 2
    max_exact = nb // 2
    ret = jnp.where(rel > 0, nb, 0)
    n = jnp.abs(rel)
    nf = jnp.maximum(n, 1).astype(jnp.float32)
    large = max_exact + (jnp.log(nf / max_exact) / math.log(REL_MAX_DISTANCE / max_exact) * (nb - max_exact)).astype(jnp.int32)
    large = jnp.minimum(large, nb - 1)
    return ret + jnp.where(n < max_exact, n, large)


def multiscale_pool(u, w_pool, pool_scale):
    B, S, _ = u.shape
    ug = u.astype(jnp.float32).reshape(B, S, N_POOL_GROUPS, POOL_GROUP_CH)
    cs = jnp.pad(jnp.cumsum(ug, axis=1), ((0, 0), (1, 0), (0, 0), (0, 0)))
    pos = jnp.arange(S)
    pooled = []
    for gi, win in enumerate(POOL_WINDOWS):
        lo = jnp.clip(pos - win // 2, 0, S)
        hi = jnp.clip(pos + win // 2, 0, S)
        cnt = (hi - lo).astype(jnp.float32)[None, :, None]
        mean = (cs[:, hi, gi] - cs[:, lo, gi]) / cnt
        pooled.append(mean - ug[:, :, gi])
    pooled = jnp.stack(pooled, axis=2).astype(u.dtype)
    mixed = jnp.einsum('bsgc,gcd->bsgd', pooled, w_pool) * pool_scale.reshape(N_POOL_GROUPS, POOL_GROUP_CH)
    return mixed.reshape(B, S, POOL_WIDTH)


def windowed_gqa(q, k, v, rel_table, sink):
    B, S, _ = q.shape
    nb = S // BLOCK
    qb = q.reshape(B, nb, BLOCK, N_KV_HEADS, GQA_GROUP, HEAD_DIM)
    pad = ((0, 0), (BLOCK, BLOCK), (0, 0))
    kp = jnp.pad(k, pad).reshape(B, nb + 2, BLOCK, N_KV_HEADS, HEAD_DIM)
    vp = jnp.pad(v, pad).reshape(B, nb + 2, BLOCK, N_KV_HEADS, HEAD_DIM)
    kb = jnp.concatenate([kp[:, :-2], kp[:, 1:-1], kp[:, 2:]], axis=2)
    vb = jnp.concatenate([vp[:, :-2], vp[:, 1:-1], vp[:, 2:]], axis=2)
    t = jnp.arange(BLOCK)[:, None]
    j = jnp.arange(3 * BLOCK)[None, :]
    rel = j - BLOCK - t
    bias = rel_table[t5_bucket(rel)].astype(jnp.float32)
    bias = jnp.transpose(bias, (2, 0, 1)).reshape(N_KV_HEADS, GQA_GROUP, BLOCK, 3 * BLOCK)
    kpos = jnp.arange(nb)[:, None, None] * BLOCK - BLOCK + j[None]
    valid = (jnp.abs(rel) <= WINDOW)[None] & (kpos >= 0) & (kpos < S)
    s = jnp.einsum('bnqhgd,bnkhd->bnhgqk', qb, kb, preferred_element_type=jnp.float32) * (HEAD_DIM ** -0.5)
    s = jnp.where(valid[None, :, None, None], s + bias[None, None], NEG_INF)
    sink_b = sink.astype(jnp.float32).reshape(N_KV_HEADS, GQA_GROUP)[None, None, :, :, None, None]
    m = jnp.maximum(jnp.max(s, axis=-1, keepdims=True), sink_b)
    p = jnp.exp(s - m)
    denom = jnp.sum(p, axis=-1, keepdims=True) + jnp.exp(sink_b - m)
    o = jnp.einsum('bnhgqk,bnkhd->bnqhgd', (p / denom).astype(v.dtype), vb)
    return o.reshape(B, S, ATTN_WIDTH)


def memory_attention(qm, mem, w_mem_kv):
    B, S, _ = qm.shape
    kv = mem @ w_mem_kv
    km = kv[..., :MEM_WIDTH].reshape(B, -1, MEM_HEADS, MEM_HEAD_DIM)
    vm = kv[..., MEM_WIDTH:].reshape(B, -1, MEM_HEADS, MEM_HEAD_DIM)
    q4 = qm.reshape(B, S, MEM_HEADS, MEM_HEAD_DIM)
    s = jnp.einsum('bshd,bmhd->bhsm', q4, km, preferred_element_type=jnp.float32) * (MEM_HEAD_DIM ** -0.5)
    p = jax.nn.softmax(s, axis=-1).astype(vm.dtype)
    o = jnp.einsum('bhsm,bmhd->bshd', p, vm)
    return o.reshape(B, S, MEM_WIDTH)


def hier_moe(x, w_router_group, w_router_expert, w_gu, w_down):
    B, S, D = x.shape
    T = B * S
    A = T * TOP_K
    xt = x.reshape(T, D)
    ar = jnp.arange(T)
    gl = (xt @ w_router_group).astype(jnp.float32)
    gp = jax.nn.softmax(gl, axis=-1)
    grp = jnp.argmax(gl, axis=-1)
    el = (xt @ w_router_expert).astype(jnp.float32).reshape(T, N_EXPERT_GROUPS, EXPERTS_PER_GROUP)
    el_sel = el[ar, grp]
    top_l, top_i = lax.top_k(el_sel, TOP_K)
    wts = jax.nn.softmax(top_l, axis=-1) * gp[ar, grp][:, None]
    eid = (grp[:, None] * EXPERTS_PER_GROUP + top_i).reshape(A)
    tok = jnp.repeat(ar, TOP_K)
    wflat = wts.reshape(A)
    order = jnp.argsort(eid, stable=True)
    se, stok, sw = eid[order], tok[order], wflat[order]
    counts = jax.ops.segment_sum(jnp.ones((A,), jnp.int32), eid, num_segments=N_EXPERTS)
    starts = jnp.cumsum(counts) - counts
    padc = (counts + MOE_BLOCK - 1) // MOE_BLOCK * MOE_BLOCK
    pends = jnp.cumsum(padc)
    pstarts = pends - padc
    dest = pstarts[se] + jnp.arange(A) - starts[se]
    P = A + N_EXPERTS * MOE_BLOCK
    nblk = P // MOE_BLOCK
    buf_tok = jnp.full((P,), T, jnp.int32).at[dest].set(stok)
    buf_w = jnp.zeros((P,), jnp.float32).at[dest].set(sw)
    xpad = jnp.concatenate([xt, jnp.zeros((1, D), xt.dtype)], axis=0)
    buf_x = xpad[buf_tok].reshape(nblk, MOE_BLOCK, D)
    blk_e = jnp.minimum(jnp.searchsorted(pends, jnp.arange(nblk) * MOE_BLOCK, side='right'), N_EXPERTS - 1)

    def expert_block(args):
        xb, e = args
        gu = xb @ w_gu[e]
        h = jax.nn.silu(gu[:, :EXPERT_HIDDEN]) * gu[:, EXPERT_HIDDEN:]
        return h @ w_down[e]

    yb = lax.map(expert_block, (buf_x, blk_e)).reshape(P, D)
    out = jnp.zeros((T + 1, D), yb.dtype).at[buf_tok].add(yb * buf_w[:, None].astype(yb.dtype))
    return out[:T].reshape(B, S, D)


def encoder_trunk(x, mem, ln_in_g, ln_in_b, rel_bias_table, w_in, b_in, w_pool, pool_scale, p_pool,
                  sink, p_attn, w_mem_kv, p_mem, w_out, ln1_g, ln1_b, w_router_group, w_router_expert,
                  w_gu, w_down, ln2_g, ln2_b):
    B, S, D = x.shape
    x = layer_norm(x, ln_in_g, ln_in_b)
    o1 = POOL_WIDTH
    o2 = o1 + ATTN_WIDTH
    o3 = o2 + KV_WIDTH
    o4 = o3 + KV_WIDTH
    o5 = o4 + MEM_WIDTH
    for l in range(DEPTH):
        h = x @ w_in[l] + b_in[l]
        a = multiscale_pool(h[..., :o1], w_pool[l], pool_scale[l]) @ p_pool[l]
        b = windowed_gqa(h[..., o1:o2], h[..., o2:o3], h[..., o3:o4], rel_bias_table, sink[l]) @ p_attn[l]
        c = memory_attention(h[..., o4:o5], mem, w_mem_kv[l]) @ p_mem[l]
        g = jax.nn.sigmoid(h[..., o5:]).reshape(B, S, N_BRANCHES, D)
        merged = g[:, :, 0] * a + g[:, :, 1] * b + g[:, :, 2] * c
        x = layer_norm(ALPHA * x + merged @ w_out[l], ln1_g[l], ln1_b[l])
        x = layer_norm(ALPHA * x + hier_moe(x, w_router_group[l], w_router_expert[l], w_gu[l], w_down[l]), ln2_g[l], ln2_b[l])
    return x


def setup_inputs(seed: int = 0) -> dict:
    key = jax.random.key(seed)
    ks = jax.random.split(key, 32)
    f32 = jnp.float32
    nrm = lambda k, shape, scale: jax.random.normal(k, shape, f32) * scale
    L = DEPTH
    return {
        'x_prompt': nrm(ks[0], (BATCH, SEQ, D_MODEL), 1.0),
        'x_sample': nrm(ks[1], (DEC_BATCH, DEC_SEQ, D_MODEL), 1.0),
        'mem_prompt': nrm(ks[2], (BATCH, MEM_TOKENS, D_MODEL), 1.0),
        'mem_sample': nrm(ks[3], (DEC_BATCH, MEM_TOKENS, D_MODEL), 1.0),
        'ln_in_g': 1.0 + nrm(ks[4], (D_MODEL,), 0.02),
        'ln_in_b': nrm(ks[5], (D_MODEL,), 0.02),
        'rel_bias_table': nrm(ks[6], (N_REL_BUCKETS, N_Q_HEADS), 0.1),
        'w_in': nrm(ks[7], (L, D_MODEL, IN_WIDTH), D_MODEL ** -0.5),
        'b_in': nrm(ks[8], (L, IN_WIDTH), 0.02),
        'w_pool': nrm(ks[9], (L, N_POOL_GROUPS, POOL_GROUP_CH, POOL_GROUP_CH), POOL_GROUP_CH ** -0.5),
        'pool_scale': 1.0 + nrm(ks[10], (L, POOL_WIDTH), 0.02),
        'p_pool': nrm(ks[11], (L, POOL_WIDTH, D_MODEL), POOL_WIDTH ** -0.5),
        'sink': nrm(ks[12], (L, N_Q_HEADS), 0.5),
        'p_attn': nrm(ks[13], (L, ATTN_WIDTH, D_MODEL), ATTN_WIDTH ** -0.5),
        'w_mem_kv': nrm(ks[14], (L, D_MODEL, 2 * MEM_WIDTH), D_MODEL ** -0.5),
        'p_mem': nrm(ks[15], (L, MEM_WIDTH, D_MODEL), MEM_WIDTH ** -0.5),
        'w_out': nrm(ks[16], (L, D_MODEL, D_MODEL), BETA * D_MODEL ** -0.5),
        'ln1_g': 1.0 + nrm(ks[17], (L, D_MODEL), 0.02),
        'ln1_b': nrm(ks[18], (L, D_MODEL), 0.02),
        'w_router_group': nrm(ks[19], (L, D_MODEL, N_EXPERT_GROUPS), D_MODEL ** -0.5),
        'w_router_expert': nrm(ks[20], (L, D_MODEL, N_EXPERTS), D_MODEL ** -0.5),
        'w_gu': nrm(ks[21], (L, N_EXPERTS, D_MODEL, 2 * EXPERT_HIDDEN), D_MODEL ** -0.5),
        'w_down': nrm(ks[22], (L, N_EXPERTS, EXPERT_HIDDEN, D_MODEL), BETA * EXPERT_HIDDEN ** -0.5),
        'ln2_g': 1.0 + nrm(ks[23], (L, D_MODEL), 0.02),
        'ln2_b': nrm(ks[24], (L, D_MODEL), 0.02),
    }


def reference(x_prompt, x_sample, mem_prompt, mem_sample, ln_in_g, ln_in_b, rel_bias_table, w_in, b_in,
              w_pool, pool_scale, p_pool, sink, p_attn, w_mem_kv, p_mem, w_out, ln1_g, ln1_b,
              w_router_group, w_router_expert, w_gu, w_down, ln2_g, ln2_b):
    y_prompt = encoder_trunk(x_prompt, mem_prompt, ln_in_g, ln_in_b, rel_bias_table, w_in, b_in, w_pool,
                             pool_scale, p_pool, sink, p_attn, w_mem_kv, p_mem, w_out, ln1_g, ln1_b,
                             w_router_group, w_router_expert, w_gu, w_down, ln2_g, ln2_b)
    y_sample = encoder_trunk(x_sample, mem_sample, ln_in_g, ln_in_b, rel_bias_table, w_in, b_in, w_pool,
                             pool_scale, p_pool, sink, p_attn, w_mem_kv, p_mem, w_out, ln1_g, ln1_b,
                             w_router_group, w_router_expert, w_gu, w_down, ln2_g, ln2_b)
    return (y_prompt, y_sample)
```

```python
import functools
import math

import jax
import jax.numpy as jnp
from jax import lax
from jax.experimental import pallas as pl
from jax.experimental.pallas import tpu as pltpu

F32 = jnp.float32
BF16 = jnp.bfloat16
I32 = jnp.int32

D_MODEL = 1024
DEPTH = 1
POOL_WIDTH = 512
POOL_WINDOWS = (2, 4, 8, 16)
N_POOL_GROUPS = 4
POOL_GROUP_CH = POOL_WIDTH // N_POOL_GROUPS
N_Q_HEADS = 16
N_KV_HEADS = 4
HEAD_DIM = 64
ATTN_WIDTH = N_Q_HEADS * HEAD_DIM
KV_WIDTH = N_KV_HEADS * HEAD_DIM
WINDOW = 128
BLOCK = 128
N_REL_BUCKETS = 32
REL_MAX_DISTANCE = 128
MEM_TOKENS = 256
MEM_HEADS = 4
MEM_HEAD_DIM = 128
MEM_WIDTH = MEM_HEADS * MEM_HEAD_DIM
N_BRANCHES = 3
N_EXPERT_GROUPS = 4
EXPERTS_PER_GROUP = 8
N_EXPERTS = N_EXPERT_GROUPS * EXPERTS_PER_GROUP
TOP_K = 2
EXPERT_HIDDEN = 512
ALPHA = (2 * DEPTH) ** 0.25
LN_EPS = 1e-5
NEG_INF = -1e30

LANES = 128
SUBLANES = 8
VMEM_LIMIT_BYTES = 56 * 1024 * 1024

KV_DUP_WIDTH = N_KV_HEADS * LANES

C_U = 0
C_Q = C_U + POOL_WIDTH
C_K = C_Q + ATTN_WIDTH
C_V = C_K + KV_DUP_WIDTH
C_M = C_V + KV_DUP_WIDTH
C_G = C_M + MEM_WIDTH
C_END = C_G + N_BRANCHES * D_MODEL

TM_INPROJ = 512
TQ_ATTN = 256
TM_MIX = 512
BM_MOE = 256
TM_FINAL = 512
POOL_HALO = 8
ROUTER_ROWS = 40


def _layer_norm(x, g, b):
    mu = jnp.mean(x, axis=-1, keepdims=True)
    xc = x - mu
    var = jnp.mean(xc * xc, axis=-1, keepdims=True)
    return xc * lax.rsqrt(var + LN_EPS) * g + b


def _sigmoid(x):
    return 1.0 / (1.0 + jnp.exp(-x))


def _params(**kw):
    return pltpu.CompilerParams(vmem_limit_bytes=VMEM_LIMIT_BYTES, **kw)


def _memkv_kernel(mem_ref, w_ref, out_ref):
    out_ref[...] = jnp.dot(mem_ref[...].astype(BF16), w_ref[...],
                           preferred_element_type=F32).astype(BF16)


def _memkv_call(mem2, w_memkv):
    rows = mem2.shape[0]
    return pl.pallas_call(
        _memkv_kernel,
        grid=(rows // MEM_TOKENS,),
        in_specs=[pl.BlockSpec((MEM_TOKENS, D_MODEL), lambda i: (i, 0)),
                  pl.BlockSpec((D_MODEL, 2 * MEM_WIDTH), lambda i: (0, 0))],
        out_specs=pl.BlockSpec((MEM_TOKENS, 2 * MEM_WIDTH), lambda i: (i, 0)),
        out_shape=jax.ShapeDtypeStruct((rows, 2 * MEM_WIDTH), BF16),
        compiler_params=_params(dimension_semantics=("arbitrary",)),
        name="memkv",
    )(mem2, w_memkv)


def _inproj_kernel(x_ref, g_ref, b_ref, w_ref, bias_ref,
                   xn_ref, u_ref, q_ref, k_ref, v_ref, qm_ref, gate_ref):
    xn = _layer_norm(x_ref[...], g_ref[...], b_ref[...])
    xn_ref[...] = xn
    xb = xn.astype(BF16)

    def seg(lo, hi):
        return jnp.dot(xb, w_ref[:, lo:hi], preferred_element_type=F32) + bias_ref[:, lo:hi]

    u_ref[...] = seg(C_U, C_Q)
    q_ref[...] = (seg(C_Q, C_K) * (HEAD_DIM ** -0.5)).astype(BF16)
    k_ref[...] = seg(C_K, C_V).astype(BF16)
    v_ref[...] = seg(C_V, C_M).astype(BF16)
    qm_ref[...] = seg(C_M, C_G).astype(BF16)
    for j in range(N_BRANCHES):
        lo = C_G + j * D_MODEL
        gate_ref[:, j * D_MODEL:(j + 1) * D_MODEL] = _sigmoid(seg(lo, lo + D_MODEL)).astype(BF16)


def _inproj_call(x2, ln_g, ln_b, w_in, b_in):
    t = x2.shape[0]
    tm = TM_INPROJ
    row = lambda width: pl.BlockSpec((tm, width), lambda i: (i, 0))
    const = lambda shape: pl.BlockSpec(shape, lambda i: (0, 0))
    widths = (D_MODEL, POOL_WIDTH, ATTN_WIDTH, KV_DUP_WIDTH, KV_DUP_WIDTH, MEM_WIDTH, N_BRANCHES * D_MODEL)
    dtypes = (F32, F32, BF16, BF16, BF16, BF16, BF16)
    return pl.pallas_call(
        _inproj_kernel,
        grid=(t // tm,),
        in_specs=[row(D_MODEL), const((1, D_MODEL)), const((1, D_MODEL)),
                  const((D_MODEL, C_END)), const((1, C_END))],
        out_specs=[row(w) for w in widths],
        out_shape=[jax.ShapeDtypeStruct((t, w), d) for w, d in zip(widths, dtypes)],
        compiler_params=_params(dimension_semantics=("arbitrary",)),
        name="inproj",
    )(x2, ln_g, ln_b, w_in, b_in)


def _attn_kernel(sink_ref, q_ref, kc_ref, kp_ref, kn_ref, vc_ref, vp_ref, vn_ref,
                 bp_ref, bc_ref, bn_ref, o_ref, klo, khi, vlo, vhi):
    i = pl.program_id(1)
    n_tiles = pl.num_programs(1)
    tq = q_ref.shape[0]
    n_qb = tq // BLOCK

    lane = lax.broadcasted_iota(I32, (BLOCK, KV_DUP_WIDTH), 1)
    low = (lane & (LANES - 1)) < HEAD_DIM

    def put(dst_lo, dst_hi, r0, val):
        zero = jnp.zeros_like(val)
        dst_lo[r0:r0 + BLOCK, :] = jnp.where(low, val, zero)
        dst_hi[r0:r0 + BLOCK, :] = jnp.where(low, zero, val)

    put(klo, khi, 0, kp_ref[...])
    put(vlo, vhi, 0, vp_ref[...])
    for j in range(n_qb):
        put(klo, khi, (j + 1) * BLOCK, kc_ref[j * BLOCK:(j + 1) * BLOCK, :])
        put(vlo, vhi, (j + 1) * BLOCK, vc_ref[j * BLOCK:(j + 1) * BLOCK, :])
    put(klo, khi, (n_qb + 1) * BLOCK, kn_ref[...])
    put(vlo, vhi, (n_qb + 1) * BLOCK, vn_ref[...])

    first = (i == 0).astype(I32)
    last = (i == n_tiles - 1).astype(I32)

    def softmax_rows(s, sink):
        m = jnp.maximum(jnp.max(s, axis=-1, keepdims=True), sink)
        p = jnp.exp(s - m)
        denom = jnp.sum(p, axis=-1, keepdims=True) + jnp.exp(sink - m)
        return (p * (1.0 / denom)).astype(BF16)

    for j in range(n_qb):
        r0 = j * BLOCK
        pv = first if j == 0 else 0
        nv = last if j == n_qb - 1 else 0
        for h in range(N_KV_HEADS):
            c0 = h * 2 * LANES
            q_pairs = jnp.concatenate([q_ref[r0:r0 + BLOCK, c0:c0 + LANES],
                                       q_ref[r0:r0 + BLOCK, c0 + LANES:c0 + 2 * LANES]], axis=0)
            acc = jnp.zeros((2 * BLOCK, LANES), F32)
            for half, (kref, vref) in enumerate(((klo, vlo), (khi, vhi))):
                unit = h * 2 + half
                kx = kref[r0:r0 + 3 * BLOCK, h * LANES:(h + 1) * LANES]
                vx = vref[r0:r0 + 3 * BLOCK, h * LANES:(h + 1) * LANES]
                s = lax.dot_general(q_pairs, kx, (((1,), (1,)), ((), ())),
                                    preferred_element_type=F32)
                bias = jnp.concatenate([bp_ref[pv, unit], bc_ref[unit], bn_ref[nv, unit]], axis=1)
                s = s + bias
                p_a = softmax_rows(s[:BLOCK], sink_ref[4 * h + half])
                p_b = softmax_rows(s[BLOCK:], sink_ref[4 * h + 2 + half])
                p = jnp.concatenate([p_a, p_b], axis=0)
                acc = acc + jnp.dot(p, vx, preferred_element_type=F32)
            o_ref[r0:r0 + BLOCK, c0:c0 + LANES] = acc[:BLOCK].astype(BF16)
            o_ref[r0:r0 + BLOCK, c0 + LANES:c0 + 2 * LANES] = acc[BLOCK:].astype(BF16)


def _attn_call(q, kk, vv, sink, bias_prev, bias_cur, bias_next, batch, seq):
    tq = TQ_ATTN
    n_qb = tq // BLOCK
    nb = seq // BLOCK
    n_tiles = seq // tq
    t = batch * seq

    cur = lambda width: pl.BlockSpec((tq, width), lambda b, i: (b * n_tiles + i, 0))
    prev = pl.BlockSpec((BLOCK, KV_DUP_WIDTH), lambda b, i: (b * nb + jnp.maximum(i * n_qb - 1, 0), 0))
    nxt = pl.BlockSpec((BLOCK, KV_DUP_WIDTH), lambda b, i: (b * nb + jnp.minimum((i + 1) * n_qb, nb - 1), 0))
    full = lambda a: pl.BlockSpec(a.shape, lambda b, i: (0,) * a.ndim)
    ext = (n_qb + 2) * BLOCK
    return pl.pallas_call(
        _attn_kernel,
        grid=(batch, n_tiles),
        in_specs=[pl.BlockSpec(memory_space=pltpu.SMEM),
                  cur(ATTN_WIDTH), cur(KV_DUP_WIDTH), prev, nxt, cur(KV_DUP_WIDTH), prev, nxt,
                  full(bias_prev), full(bias_cur), full(bias_next)],
        out_specs=cur(ATTN_WIDTH),
        out_shape=jax.ShapeDtypeStruct((t, ATTN_WIDTH), BF16),
        scratch_shapes=[pltpu.VMEM((ext, KV_DUP_WIDTH), BF16) for _ in range(4)],
        compiler_params=_params(dimension_semantics=("arbitrary", "arbitrary")),
        name="attn",
    )(sink, q, kk, kk, kk, vv, vv, vv, bias_prev, bias_cur, bias_next)


def _mix_kernel(xn_ref, u_ref, up_ref, un_ref, o_ref, qm_ref, gate_ref, memkv_ref,
                wpool_ref, pscale_ref, ppool_ref, pattn_ref, pmem_ref, wout_ref,
                g1_ref, b1_ref, wr_ref,
                x1_ref, eid_ref, wts_ref, uext, *, seq):
    i = pl.program_id(1)
    n_tiles = pl.num_programs(1)
    tm = xn_ref.shape[0]

    halo = POOL_HALO
    uext[0:halo, :] = jnp.where(i > 0, up_ref[...], 0.0)
    uext[halo:halo + tm, :] = u_ref[...]
    uext[halo + tm:halo + tm + halo, :] = jnp.where(i < n_tiles - 1, un_ref[...], 0.0)
    pos = i * tm + lax.broadcasted_iota(I32, (tm, 1), 0)
    mixed = []
    for gi, win in enumerate(POOL_WINDOWS):
        c0 = gi * POOL_GROUP_CH
        half = win // 2
        total = jnp.zeros((tm, POOL_GROUP_CH), F32)
        for off in range(-half, half):
            total = total + uext[halo + off:halo + off + tm, c0:c0 + POOL_GROUP_CH]
        cnt = (jnp.minimum(pos + half, seq) - jnp.maximum(pos - half, 0)).astype(F32)
        pooled = total / cnt - u_ref[:, c0:c0 + POOL_GROUP_CH]
        mixed.append(jnp.dot(pooled.astype(BF16), wpool_ref[gi], preferred_element_type=F32)
                     * pscale_ref[:, c0:c0 + POOL_GROUP_CH])
    mixed = jnp.concatenate(mixed, axis=1).astype(BF16)
    merged = gate_ref[:, 0:D_MODEL].astype(F32) * jnp.dot(
        mixed, ppool_ref[...], preferred_element_type=F32)

    merged = merged + gate_ref[:, D_MODEL:2 * D_MODEL].astype(F32) * jnp.dot(
        o_ref[...], pattn_ref[...], preferred_element_type=F32)

    heads = []
    for h in range(MEM_HEADS):
        c0 = h * MEM_HEAD_DIM
        km = memkv_ref[:, c0:c0 + MEM_HEAD_DIM]
        vm = memkv_ref[:, MEM_WIDTH + c0:MEM_WIDTH + c0 + MEM_HEAD_DIM]
        s = lax.dot_general(qm_ref[:, c0:c0 + MEM_HEAD_DIM], km, (((1,), (1,)), ((), ())),
                            preferred_element_type=F32) * (MEM_HEAD_DIM ** -0.5)
        p = jnp.exp(s - jnp.max(s, axis=-1, keepdims=True))
        p = (p / jnp.sum(p, axis=-1, keepdims=True)).astype(BF16)
        heads.append(jnp.dot(p, vm, preferred_element_type=F32).astype(BF16))
    om = jnp.concatenate(heads, axis=1)
    merged = merged + gate_ref[:, 2 * D_MODEL:3 * D_MODEL].astype(F32) * jnp.dot(
        om, pmem_ref[...], preferred_element_type=F32)

    y = jnp.dot(merged.astype(BF16), wout_ref[...], preferred_element_type=F32)
    x1 = _layer_norm(ALPHA * xn_ref[...] + y, g1_ref[...], b1_ref[...])
    x1_ref[...] = x1

    lt = lax.dot_general(wr_ref[...], x1.astype(BF16), (((1,), (1,)), ((), ())),
                         preferred_element_type=F32)
    gl = [lt[r:r + 1, :] for r in range(N_EXPERT_GROUPS)]
    gmax = gl[0]
    grp = jnp.zeros((1, tm), I32)
    for r in range(1, N_EXPERT_GROUPS):
        better = gl[r] > gmax
        grp = jnp.where(better, r, grp)
        gmax = jnp.where(better, gl[r], gmax)
    gsum = gl[0] * 0.0
    for r in range(N_EXPERT_GROUPS):
        gsum = gsum + jnp.exp(gl[r] - gmax)
    gp = 1.0 / gsum
    sel = jnp.zeros((EXPERTS_PER_GROUP, tm), F32)
    for r in range(N_EXPERT_GROUPS):
        rows = lt[SUBLANES + r * EXPERTS_PER_GROUP:SUBLANES + (r + 1) * EXPERTS_PER_GROUP, :]
        sel = jnp.where(grp == r, rows, sel)
    ridx = lax.broadcasted_iota(I32, (EXPERTS_PER_GROUP, tm), 0)
    top1 = jnp.max(sel, axis=0, keepdims=True)
    i1 = jnp.min(jnp.where(sel == top1, ridx, EXPERTS_PER_GROUP), axis=0, keepdims=True)
    rest = jnp.where(ridx == i1, -jnp.inf, sel)
    top2 = jnp.max(rest, axis=0, keepdims=True)
    i2 = jnp.min(jnp.where(rest == top2, ridx, EXPERTS_PER_GROUP), axis=0, keepdims=True)
    e2 = jnp.exp(top2 - top1)
    inv = gp / (1.0 + e2)
    eid_ref[0:1, :] = grp * EXPERTS_PER_GROUP + i1
    eid_ref[1:2, :] = grp * EXPERTS_PER_GROUP + i2
    wts_ref[0:1, :] = inv
    wts_ref[1:2, :] = e2 * inv


def _mix_call(xn, u, o, qm, gate, memkv, p, batch, seq):
    tm = TM_MIX
    n_tiles = seq // tm
    t = batch * seq
    hb = tm // POOL_HALO

    row = lambda width: pl.BlockSpec((tm, width), lambda b, i: (b * n_tiles + i, 0))
    prev = pl.BlockSpec((POOL_HALO, POOL_WIDTH),
                        lambda b, i: (jnp.maximum((b * n_tiles + i) * hb - 1, 0), 0))
    nxt = pl.BlockSpec((POOL_HALO, POOL_WIDTH),
                       lambda b, i: (jnp.minimum((b * n_tiles + i + 1) * hb, t // POOL_HALO - 1), 0))
    full = lambda a: pl.BlockSpec(a.shape, lambda b, i: (0,) * a.ndim)
    lane_row = pl.BlockSpec((TOP_K, tm), lambda b, i: (0, b * n_tiles + i))
    weights = (p["w_pool"], p["pool_scale"], p["p_pool"], p["p_attn"], p["p_mem"], p["w_out"],
               p["ln1_g"], p["ln1_b"], p["w_router"])
    return pl.pallas_call(
        functools.partial(_mix_kernel, seq=seq),
        grid=(batch, n_tiles),
        in_specs=[row(D_MODEL), row(POOL_WIDTH), prev, nxt, row(ATTN_WIDTH), row(MEM_WIDTH),
                  row(N_BRANCHES * D_MODEL),
                  pl.BlockSpec((MEM_TOKENS, 2 * MEM_WIDTH), lambda b, i: (b, 0))]
                 + [full(w) for w in weights],
        out_specs=[row(D_MODEL), lane_row, lane_row],
        out_shape=[jax.ShapeDtypeStruct((t, D_MODEL), F32),
                   jax.ShapeDtypeStruct((TOP_K, t), I32),
                   jax.ShapeDtypeStruct((TOP_K, t), F32)],
        scratch_shapes=[pltpu.VMEM((tm + 2 * POOL_HALO, POOL_WIDTH), F32)],
        compiler_params=_params(dimension_semantics=("arbitrary", "arbitrary")),
        name="mix",
    )(xn, u, u, u, o, qm, gate, memkv, *weights)


def _moe_kernel(sa_ref, iblk_ref, ie_ref, ilo_ref, ihi_ref,
                x1_hbm, ws_ref, wgu_ref, wd_ref, y2_hbm,
                xbuf, acc, gsem, ssem):
    i = pl.program_id(0)
    bm = xbuf.shape[0]
    lo = ilo_ref[i]
    hi = ihi_ref[i]
    base = iblk_ref[i] * bm
    nonempty = hi > lo

    def row_gather(r):
        tok = lax.shift_right_logical(sa_ref[base + r], 1)
        return pltpu.make_async_copy(x1_hbm.at[pl.ds(tok, 1)], xbuf.at[pl.ds(r, 1)], gsem)

    def row_scatter(r):
        return pltpu.make_async_copy(acc.at[pl.ds(r, 1)], y2_hbm.at[pl.ds(sa_ref[base + r], 1)], ssem)

    @pl.when(jnp.logical_and(nonempty, lo == 0))
    def _():
        def start(r, c):
            row_gather(r).start()
            return c
        lax.fori_loop(0, bm, start, 0)

        def wait(r, c):
            row_gather(r).wait()
            return c
        lax.fori_loop(0, bm, wait, 0)
        acc[...] = jnp.zeros_like(acc)

    @pl.when(nonempty)
    def _():
        rows = lax.broadcasted_iota(I32, (bm, 1), 0)
        mine = jnp.logical_and(rows >= lo, rows < hi)
        gu = jnp.dot(xbuf[...].astype(BF16), wgu_ref[0], preferred_element_type=F32)
        gate = gu[:, :EXPERT_HIDDEN]
        hid = (gate * _sigmoid(gate)) * gu[:, EXPERT_HIDDEN:]
        y = jnp.dot(hid.astype(BF16), wd_ref[0], preferred_element_type=F32)
        acc[...] += jnp.where(mine, y, 0.0)

    @pl.when(jnp.logical_and(nonempty, hi == bm))
    def _():
        acc[...] = acc[...] * ws_ref[...]

        def start(r, c):
            row_scatter(r).start()
            return c
        lax.fori_loop(0, bm, start, 0)

        def wait(r, c):
            row_scatter(r).wait()
            return c
        lax.fori_loop(0, bm, wait, 0)


def _moe_call(x1, sorted_a, wsorted, item_blk, item_e, item_lo, item_hi, w_gu, w_down):
    t = x1.shape[0]
    a = sorted_a.shape[0]
    bm = BM_MOE
    n_items = item_blk.shape[0]
    grid_spec = pltpu.PrefetchScalarGridSpec(
        num_scalar_prefetch=5,
        grid=(n_items,),
        in_specs=[pl.BlockSpec(memory_space=pl.ANY),
                  pl.BlockSpec((bm, 1), lambda i, sa, blk, e, lo, hi: (blk[i], 0)),
                  pl.BlockSpec((1, D_MODEL, 2 * EXPERT_HIDDEN), lambda i, sa, blk, e, lo, hi: (e[i], 0, 0)),
                  pl.BlockSpec((1, EXPERT_HIDDEN, D_MODEL), lambda i, sa, blk, e, lo, hi: (e[i], 0, 0))],
        out_specs=pl.BlockSpec(memory_space=pl.ANY),
        scratch_shapes=[pltpu.VMEM((bm, D_MODEL), F32), pltpu.VMEM((bm, D_MODEL), F32),
                        pltpu.SemaphoreType.DMA, pltpu.SemaphoreType.DMA],
    )
    return pl.pallas_call(
        _moe_kernel,
        grid_spec=grid_spec,
        out_shape=jax.ShapeDtypeStruct((a, D_MODEL), F32),
        compiler_params=_params(dimension_semantics=("arbitrary",)),
        name="moe",
    )(sorted_a, item_blk, item_e, item_lo, item_hi, x1, wsorted, w_gu, w_down)


def _final_kernel(x1_ref, y2_ref, g_ref, b_ref, out_ref):
    moe = y2_ref[:, :D_MODEL] + y2_ref[:, D_MODEL:]
    out_ref[...] = _layer_norm(ALPHA * x1_ref[...] + moe, g_ref[...], b_ref[...])


def _final_call(x1, y2pairs, ln_g, ln_b):
    t = x1.shape[0]
    tm = TM_FINAL
    return pl.pallas_call(
        _final_kernel,
        grid=(t // tm,),
        in_specs=[pl.BlockSpec((tm, D_MODEL), lambda i: (i, 0)),
                  pl.BlockSpec((tm, TOP_K * D_MODEL), lambda i: (i, 0)),
                  pl.BlockSpec((1, D_MODEL), lambda i: (0, 0)),
                  pl.BlockSpec((1, D_MODEL), lambda i: (0, 0))],
        out_specs=pl.BlockSpec((tm, D_MODEL), lambda i: (i, 0)),
        out_shape=jax.ShapeDtypeStruct((t, D_MODEL), F32),
        compiler_params=_params(dimension_semantics=("arbitrary",)),
        name="final",
    )(x1, y2pairs, ln_g, ln_b)


def _t5_bucket(rel):
    nb = N_REL_BUCKETS // 2
    max_exact = nb // 2
    ret = jnp.where(rel > 0, nb, 0)
    n = jnp.abs(rel)
    nf = jnp.maximum(n, 1).astype(F32)
    large = max_exact + (jnp.log(nf / max_exact) / math.log(REL_MAX_DISTANCE / max_exact)
                         * (nb - max_exact)).astype(I32)
    large = jnp.minimum(large, nb - 1)
    return ret + jnp.where(n < max_exact, n, large)


def _bias_tables(rel_table):
    tq = jnp.arange(BLOCK)[:, None]
    j = jnp.arange(3 * BLOCK)[None, :]
    rel = j - BLOCK - tq
    bias = rel_table[_t5_bucket(rel)].astype(F32)
    bias = jnp.where((jnp.abs(rel) <= WINDOW)[:, :, None], bias, NEG_INF)
    bias = jnp.transpose(bias, (2, 0, 1))
    head_a = jnp.array([4 * h + half for h in range(N_KV_HEADS) for half in range(2)])
    units = jnp.concatenate([bias[head_a], bias[head_a + 2]], axis=1)
    masked = jnp.full((2 * N_KV_HEADS, 2 * BLOCK, BLOCK), NEG_INF, F32)
    prev = jnp.stack([units[:, :, :BLOCK], masked])
    cur = units[:, :, BLOCK:2 * BLOCK]
    nxt = jnp.stack([units[:, :, 2 * BLOCK:], masked])
    return prev, cur, nxt


def _prepare(rel_bias_table, w_in, b_in, w_pool, pool_scale, p_pool, sink, p_attn, w_mem_kv, p_mem,
             w_out, ln1_g, ln1_b, w_router_group, w_router_expert, w_gu, w_down):
    l = 0
    o1 = POOL_WIDTH
    o2 = o1 + ATTN_WIDTH
    o3 = o2 + KV_WIDTH
    o4 = o3 + KV_WIDTH
    o5 = o4 + MEM_WIDTH
    dup = (jnp.arange(KV_DUP_WIDTH) // LANES) * HEAD_DIM + jnp.arange(KV_DUP_WIDTH) % HEAD_DIM
    cols = jnp.concatenate([jnp.arange(0, o2), o2 + dup, o3 + dup, jnp.arange(o4, w_in.shape[-1])])
    w_router = jnp.zeros((ROUTER_ROWS, D_MODEL), F32)
    w_router = w_router.at[:N_EXPERT_GROUPS].set(w_router_group[l].T)
    w_router = w_router.at[SUBLANES:].set(w_router_expert[l].T)
    bias_prev, bias_cur, bias_next = _bias_tables(rel_bias_table)
    return dict(
        w_in=w_in[l][:, cols].astype(BF16), b_in=b_in[l][cols][None, :],
        w_pool=w_pool[l].astype(BF16), pool_scale=pool_scale[l][None, :],
        p_pool=p_pool[l].astype(BF16), p_attn=p_attn[l].astype(BF16), p_mem=p_mem[l].astype(BF16),
        w_out=w_out[l].astype(BF16), w_memkv=w_mem_kv[l].astype(BF16),
        ln1_g=ln1_g[l][None, :], ln1_b=ln1_b[l][None, :],
        w_router=w_router.astype(BF16), sink=sink[l],
        w_gu=w_gu[l].astype(BF16), w_down=w_down[l].astype(BF16),
        bias_prev=bias_prev, bias_cur=bias_cur, bias_next=bias_next,
    )


def _dispatch_plan(eid, wts):
    t = eid.shape[1]
    a = TOP_K * t
    bm = BM_MOE
    nblk = a // bm
    eflat = eid.T.reshape(a)
    wflat = wts.T.reshape(a)
    order = jnp.argsort(eflat, stable=True).astype(I32)
    wsorted = wflat[order][:, None]
    counts = jnp.sum((eflat[:, None] == jnp.arange(N_EXPERTS, dtype=I32)[None, :]).astype(I32), axis=0)
    ends = jnp.cumsum(counts)
    starts = ends - counts
    cuts = jnp.sort(jnp.concatenate([jnp.arange(nblk, dtype=I32) * bm, starts.astype(I32),
                                     jnp.array([a], I32)]))
    lo_abs = cuts[:-1]
    hi_abs = cuts[1:]
    nonempty = hi_abs > lo_abs
    blk = jnp.minimum(lo_abs // bm, nblk - 1)
    e_raw = jnp.minimum(jnp.searchsorted(ends, lo_abs, side="right"), N_EXPERTS - 1).astype(I32)
    item_e = lax.cummax(jnp.where(nonempty, e_raw, 0))
    item_lo = jnp.where(nonempty, lo_abs - blk * bm, 0).astype(I32)
    item_hi = jnp.where(nonempty, hi_abs - blk * bm, 0).astype(I32)
    return order, wsorted, blk.astype(I32), item_e.astype(I32), item_lo, item_hi


def _trunk(x, mem, ln_in_g, ln_in_b, ln2_g, ln2_b, p):
    batch, seq, d = x.shape
    t = batch * seq
    assert seq % TM_MIX == 0 and seq % TQ_ATTN == 0 and t % TM_INPROJ == 0 and (TOP_K * t) % BM_MOE == 0
    memkv = _memkv_call(mem.reshape(batch * MEM_TOKENS, d), p["w_memkv"])
    xn, u, q, kk, vv, qm, gate = _inproj_call(x.reshape(t, d), ln_in_g[None, :], ln_in_b[None, :],
                                              p["w_in"], p["b_in"])
    o = _attn_call(q, kk, vv, p["sink"], p["bias_prev"], p["bias_cur"], p["bias_next"], batch, seq)
    x1, eid, wts = _mix_call(xn, u, o, qm, gate, memkv, p, batch, seq)
    order, wsorted, item_blk, item_e, item_lo, item_hi = _dispatch_plan(eid, wts)
    y2 = _moe_call(x1, order, wsorted, item_blk, item_e, item_lo, item_hi, p["w_gu"], p["w_down"])
    out = _final_call(x1, y2.reshape(t, TOP_K * d), ln2_g[None, :], ln2_b[None, :])
    return out.reshape(batch, seq, d)


def kernel(x_prompt, x_sample, mem_prompt, mem_sample, ln_in_g, ln_in_b, rel_bias_table, w_in, b_in,
           w_pool, pool_scale, p_pool, sink, p_attn, w_mem_kv, p_mem, w_out, ln1_g, ln1_b,
           w_router_group, w_router_expert, w_gu, w_down, ln2_g, ln2_b):
    p = _prepare(rel_bias_table, w_in, b_in, w_pool, pool_scale, p_pool, sink, p_attn, w_mem_kv, p_mem,
                 w_out, ln1_g, ln1_b, w_router_group, w_router_expert, w_gu, w_down)
    y_prompt = _trunk(x_prompt, mem_prompt, ln_in_g, ln_in_b, ln2_g[0], ln2_b[0], p)
    y_sample = _trunk(x_sample, mem_sample, ln_in_g, ln_in_b, ln2_g[0], ln2_b[0], p)
    return (y_prompt, y_sample)
```

```python
import functools
import math

import jax
import jax.numpy as jnp
from jax import lax
from jax.experimental import pallas as pl
from jax.experimental.pallas import tpu as pltpu

F32 = jnp.float32
BF16 = jnp.bfloat16
I32 = jnp.int32

D_MODEL = 1024
DEPTH = 1
POOL_WIDTH = 512
POOL_WINDOWS = (2, 4, 8, 16)
N_POOL_GROUPS = 4
POOL_GROUP_CH = POOL_WIDTH // N_POOL_GROUPS
N_Q_HEADS = 16
N_KV_HEADS = 4
HEAD_DIM = 64
ATTN_WIDTH = N_Q_HEADS * HEAD_DIM
KV_WIDTH = N_KV_HEADS * HEAD_DIM
WINDOW = 128
BLOCK = 128
N_REL_BUCKETS = 32
REL_MAX_DISTANCE = 128
MEM_TOKENS = 256
MEM_HEADS = 4
MEM_HEAD_DIM = 128
MEM_WIDTH = MEM_HEADS * MEM_HEAD_DIM
N_BRANCHES = 3
N_EXPERT_GROUPS = 4
EXPERTS_PER_GROUP = 8
N_EXPERTS = N_EXPERT_GROUPS * EXPERTS_PER_GROUP
TOP_K = 2
EXPERT_HIDDEN = 512
ALPHA = (2 * DEPTH) ** 0.25
LN_EPS = 1e-5
NEG_INF = -1e30

LANES = 128
SUBLANES = 8
VMEM_LIMIT_BYTES = 56 * 1024 * 1024

KV_DUP_WIDTH = N_KV_HEADS * LANES

C_U = 0
C_Q = C_U + POOL_WIDTH
C_K = C_Q + ATTN_WIDTH
C_V = C_K + KV_DUP_WIDTH
C_M = C_V + KV_DUP_WIDTH
C_G = C_M + MEM_WIDTH
C_END = C_G + N_BRANCHES * D_MODEL

TM_INPROJ = 512
TQ_ATTN = 256
TM_MIX = 512
BM_MOE = 256
TM_FINAL = 512
POOL_HALO = 8
ROUTER_ROWS = 40


def _layer_norm(x, g, b):
    mu = jnp.mean(x, axis=-1, keepdims=True)
    xc = x - mu
    var = jnp.mean(xc * xc, axis=-1, keepdims=True)
    return xc * lax.rsqrt(var + LN_EPS) * g + b


def _sigmoid(x):
    return 1.0 / (1.0 + jnp.exp(-x))


def _params(**kw):
    return pltpu.CompilerParams(vmem_limit_bytes=VMEM_LIMIT_BYTES, **kw)


def _memkv_kernel(mem_ref, w_ref, out_ref):
    out_ref[...] = jnp.dot(mem_ref[...].astype(BF16), w_ref[...],
                           preferred_element_type=F32).astype(BF16)


def _memkv_call(mem2, w_memkv):
    rows = mem2.shape[0]
    return pl.pallas_call(
        _memkv_kernel,
        grid=(rows // MEM_TOKENS,),
        in_specs=[pl.BlockSpec((MEM_TOKENS, D_MODEL), lambda i: (i, 0)),
                  pl.BlockSpec((D_MODEL, 2 * MEM_WIDTH), lambda i: (0, 0))],
        out_specs=pl.BlockSpec((MEM_TOKENS, 2 * MEM_WIDTH), lambda i: (i, 0)),
        out_shape=jax.ShapeDtypeStruct((rows, 2 * MEM_WIDTH), BF16),
        compiler_params=_params(dimension_semantics=("arbitrary",)),
        name="memkv",
    )(mem2, w_memkv)


def _inproj_kernel(x_ref, g_ref, b_ref, w_ref, bias_ref,
                   xn_ref, u_ref, q_ref, k_ref, v_ref, qm_ref, gate_ref):
    xn = _layer_norm(x_ref[...], g_ref[...], b_ref[...])
    xn_ref[...] = xn
    xb = xn.astype(BF16)

    def seg(lo, hi):
        return jnp.dot(xb, w_ref[:, lo:hi], preferred_element_type=F32) + bias_ref[:, lo:hi]

    u_ref[...] = seg(C_U, C_Q)
    q_ref[...] = (seg(C_Q, C_K) * (HEAD_DIM ** -0.5)).astype(BF16)
    k_ref[...] = seg(C_K, C_V).astype(BF16)
    v_ref[...] = seg(C_V, C_M).astype(BF16)
    qm_ref[...] = seg(C_M, C_G).astype(BF16)
    for j in range(N_BRANCHES):
        lo = C_G + j * D_MODEL
        gate_ref[:, j * D_MODEL:(j + 1) * D_MODEL] = _sigmoid(seg(lo, lo + D_MODEL)).astype(BF16)


def _inproj_call(x2, ln_g, ln_b, w_in, b_in):
    t = x2.shape[0]
    tm = TM_INPROJ
    row = lambda width: pl.BlockSpec((tm, width), lambda i: (i, 0))
    const = lambda shape: pl.BlockSpec(shape, lambda i: (0, 0))
    widths = (D_MODEL, POOL_WIDTH, ATTN_WIDTH, KV_DUP_WIDTH, KV_DUP_WIDTH, MEM_WIDTH, N_BRANCHES * D_MODEL)
    dtypes = (F32, F32, BF16, BF16, BF16, BF16, BF16)
    return pl.pallas_call(
        _inproj_kernel,
        grid=(t // tm,),
        in_specs=[row(D_MODEL), const((1, D_MODEL)), const((1, D_MODEL)),
                  const((D_MODEL, C_END)), const((1, C_END))],
        out_specs=[row(w) for w in widths],
        out_shape=[jax.ShapeDtypeStruct((t, w), d) for w, d in zip(widths, dtypes)],
        compiler_params=_params(dimension_semantics=("arbitrary",)),
        name="inproj",
    )(x2, ln_g, ln_b, w_in, b_in)


def _attn_kernel(sink_ref, q_ref, kc_ref, kp_ref, kn_ref, vc_ref, vp_ref, vn_ref,
                 bp_ref, bc_ref, bn_ref, o_ref, klo, khi, vlo, vhi):
    i = pl.program_id(1)
    n_tiles = pl.num_programs(1)
    tq = q_ref.shape[0]
    n_qb = tq // BLOCK

    lane = lax.broadcasted_iota(I32, (BLOCK, KV_DUP_WIDTH), 1)
    low = (lane & (LANES - 1)) < HEAD_DIM

    def put(dst_lo, dst_hi, r0, val):
        zero = jnp.zeros_like(val)
        dst_lo[r0:r0 + BLOCK, :] = jnp.where(low, val, zero)
        dst_hi[r0:r0 + BLOCK, :] = jnp.where(low, zero, val)

    put(klo, khi, 0, kp_ref[...])
    put(vlo, vhi, 0, vp_ref[...])
    for j in range(n_qb):
        put(klo, khi, (j + 1) * BLOCK, kc_ref[j * BLOCK:(j + 1) * BLOCK, :])
        put(vlo, vhi, (j + 1) * BLOCK, vc_ref[j * BLOCK:(j + 1) * BLOCK, :])
    put(klo, khi, (n_qb + 1) * BLOCK, kn_ref[...])
    put(vlo, vhi, (n_qb + 1) * BLOCK, vn_ref[...])

    first = (i == 0).astype(I32)
    last = (i == n_tiles - 1).astype(I32)

    def softmax_rows(s, sink):
        m = jnp.maximum(jnp.max(s, axis=-1, keepdims=True), sink)
        p = jnp.exp(s - m)
        denom = jnp.sum(p, axis=-1, keepdims=True) + jnp.exp(sink - m)
        return (p * (1.0 / denom)).astype(BF16)

    for j in range(n_qb):
        r0 = j * BLOCK
        pv = first if j == 0 else 0
        nv = last if j == n_qb - 1 else 0
        for h in range(N_KV_HEADS):
            c0 = h * 2 * LANES
            q_pairs = jnp.concatenate([q_ref[r0:r0 + BLOCK, c0:c0 + LANES],
                                       q_ref[r0:r0 + BLOCK, c0 + LANES:c0 + 2 * LANES]], axis=0)
            acc = jnp.zeros((2 * BLOCK, LANES), F32)
            for half, (kref, vref) in enumerate(((klo, vlo), (khi, vhi))):
                unit = h * 2 + half
                kx = kref[r0:r0 + 3 * BLOCK, h * LANES:(h + 1) * LANES]
                vx = vref[r0:r0 + 3 * BLOCK, h * LANES:(h + 1) * LANES]
                s = lax.dot_general(q_pairs, kx, (((1,), (1,)), ((), ())),
                                    preferred_element_type=F32)
                bias = jnp.concatenate([bp_ref[pv, unit], bc_ref[unit], bn_ref[nv, unit]], axis=1)
                s = s + bias
                p_a = softmax_rows(s[:BLOCK], sink_ref[4 * h + half])
                p_b = softmax_rows(s[BLOCK:], sink_ref[4 * h + 2 + half])
                p = jnp.concatenate([p_a, p_b], axis=0)
                acc = acc + jnp.dot(p, vx, preferred_element_type=F32)
            o_ref[r0:r0 + BLOCK, c0:c0 + LANES] = acc[:BLOCK].astype(BF16)
            o_ref[r0:r0 + BLOCK, c0 + LANES:c0 + 2 * LANES] = acc[BLOCK:].astype(BF16)


def _attn_call(q, kk, vv, sink, bias_prev, bias_cur, bias_next, batch, seq):
    tq = TQ_ATTN
    n_qb = tq // BLOCK
    nb = seq // BLOCK
    n_tiles = seq // tq
    t = batch * seq

    cur = lambda width: pl.BlockSpec((tq, width), lambda b, i: (b * n_tiles + i, 0))
    prev = pl.BlockSpec((BLOCK, KV_DUP_WIDTH), lambda b, i: (b * nb + jnp.maximum(i * n_qb - 1, 0), 0))
    nxt = pl.BlockSpec((BLOCK, KV_DUP_WIDTH), lambda b, i: (b * nb + jnp.minimum((i + 1) * n_qb, nb - 1), 0))
    full = lambda a: pl.BlockSpec(a.shape, lambda b, i: (0,) * a.ndim)
    ext = (n_qb + 2) * BLOCK
    return pl.pallas_call(
        _attn_kernel,
        grid=(batch, n_tiles),
        in_specs=[pl.BlockSpec(memory_space=pltpu.SMEM),
                  cur(ATTN_WIDTH), cur(KV_DUP_WIDTH), prev, nxt, cur(KV_DUP_WIDTH), prev, nxt,
                  full(bias_prev), full(bias_cur), full(bias_next)],
        out_specs=cur(ATTN_WIDTH),
        out_shape=jax.ShapeDtypeStruct((t, ATTN_WIDTH), BF16),
        scratch_shapes=[pltpu.VMEM((ext, KV_DUP_WIDTH), BF16) for _ in range(4)],
        compiler_params=_params(dimension_semantics=("arbitrary", "arbitrary")),
        name="attn",
    )(sink, q, kk, kk, kk, vv, vv, vv, bias_prev, bias_cur, bias_next)


def _mix_kernel(xn_ref, u_ref, up_ref, un_ref, o_ref, qm_ref, gate_ref, memkv_ref,
                wpool_ref, pscale_ref, ppool_ref, pattn_ref, pmem_ref, wout_ref,
                g1_ref, b1_ref, wr_ref,
                x1_ref, eid_ref, wts_ref, uext, *, seq):
    i = pl.program_id(1)
    n_tiles = pl.num_programs(1)
    tm = xn_ref.shape[0]

    halo = POOL_HALO
    uext[0:halo, :] = jnp.where(i > 0, up_ref[...], 0.0)
    uext[halo:halo + tm, :] = u_ref[...]
    uext[halo + tm:halo + tm + halo, :] = jnp.where(i < n_tiles - 1, un_ref[...], 0.0)
    pos = i * tm + lax.broadcasted_iota(I32, (tm, 1), 0)
    mixed = []
    for gi, win in enumerate(POOL_WINDOWS):
        c0 = gi * POOL_GROUP_CH
        half = win // 2
        total = jnp.zeros((tm, POOL_GROUP_CH), F32)
        for off in range(-half, half):
            total = total + uext[halo + off:halo + off + tm, c0:c0 + POOL_GROUP_CH]
        cnt = (jnp.minimum(pos + half, seq) - jnp.maximum(pos - half, 0)).astype(F32)
        pooled = total / cnt - u_ref[:, c0:c0 + POOL_GROUP_CH]
        mixed.append(jnp.dot(pooled.astype(BF16), wpool_ref[gi], preferred_element_type=F32)
                     * pscale_ref[:, c0:c0 + POOL_GROUP_CH])
    mixed = jnp.concatenate(mixed, axis=1).astype(BF16)
    merged = gate_ref[:, 0:D_MODEL].astype(F32) * jnp.dot(
        mixed, ppool_ref[...], preferred_element_type=F32)

    merged = merged + gate_ref[:, D_MODEL:2 * D_MODEL].astype(F32) * jnp.dot(
        o_ref[...], pattn_ref[...], preferred_element_type=F32)

    heads = []
    for h in range(MEM_HEADS):
        c0 = h * MEM_HEAD_DIM
        km = memkv_ref[:, c0:c0 + MEM_HEAD_DIM]
        vm = memkv_ref[:, MEM_WIDTH + c0:MEM_WIDTH + c0 + MEM_HEAD_DIM]
        s = lax.dot_general(qm_ref[:, c0:c0 + MEM_HEAD_DIM], km, (((1,), (1,)), ((), ())),
                            preferred_element_type=F32) * (MEM_HEAD_DIM ** -0.5)
        p = jnp.exp(s - jnp.max(s, axis=-1, keepdims=True))
        p = (p / jnp.sum(p, axis=-1, keepdims=True)).astype(BF16)
        heads.append(jnp.dot(p, vm, preferred_element_type=F32).astype(BF16))
    om = jnp.concatenate(heads, axis=1)
    merged = merged + gate_ref[:, 2 * D_MODEL:3 * D_MODEL].astype(F32) * jnp.dot(
        om, pmem_ref[...], preferred_element_type=F32)

    y = jnp.dot(merged.astype(BF16), wout_ref[...], preferred_element_type=F32)
    x1 = _layer_norm(ALPHA * xn_ref[...] + y, g1_ref[...], b1_ref[...])
    x1_ref[...] = x1

    lt = lax.dot_general(wr_ref[...], x1.astype(BF16), (((1,), (1,)), ((), ())),
                         preferred_element_type=F32)
    gl = [lt[r:r + 1, :] for r in range(N_EXPERT_GROUPS)]
    gmax = gl[0]
    grp = jnp.zeros((1, tm), I32)
    for r in range(1, N_EXPERT_GROUPS):
        better = gl[r] > gmax
        grp = jnp.where(better, r, grp)
        gmax = jnp.where(better, gl[r], gmax)
    gsum = gl[0] * 0.0
    for r in range(N_EXPERT_GROUPS):
        gsum = gsum + jnp.exp(gl[r] - gmax)
    gp = 1.0 / gsum
    sel = jnp.zeros((EXPERTS_PER_GROUP, tm), F32)
    for r in range(N_EXPERT_GROUPS):
        rows = lt[SUBLANES + r * EXPERTS_PER_GROUP:SUBLANES + (r + 1) * EXPERTS_PER_GROUP, :]
        sel = jnp.where(grp == r, rows, sel)
    ridx = lax.broadcasted_iota(I32, (EXPERTS_PER_GROUP, tm), 0)
    top1 = jnp.max(sel, axis=0, keepdims=True)
    i1 = jnp.min(jnp.where(sel == top1, ridx, EXPERTS_PER_GROUP), axis=0, keepdims=True)
    rest = jnp.where(ridx == i1, -jnp.inf, sel)
    top2 = jnp.max(rest, axis=0, keepdims=True)
    i2 = jnp.min(jnp.where(rest == top2, ridx, EXPERTS_PER_GROUP), axis=0, keepdims=True)
    e2 = jnp.exp(top2 - top1)
    inv = gp / (1.0 + e2)
    eid_ref[0:1, :] = grp * EXPERTS_PER_GROUP + i1
    eid_ref[1:2, :] = grp * EXPERTS_PER_GROUP + i2
    wts_ref[0:1, :] = inv
    wts_ref[1:2, :] = e2 * inv


def _mix_call(xn, u, o, qm, gate, memkv, p, batch, seq):
    tm = TM_MIX
    n_tiles = seq // tm
    t = batch * seq
    hb = tm // POOL_HALO

    row = lambda width: pl.BlockSpec((tm, width), lambda b, i: (b * n_tiles + i, 0))
    prev = pl.BlockSpec((POOL_HALO, POOL_WIDTH),
                        lambda b, i: (jnp.maximum((b * n_tiles + i) * hb - 1, 0), 0))
    nxt = pl.BlockSpec((POOL_HALO, POOL_WIDTH),
                       lambda b, i: (jnp.minimum((b * n_tiles + i + 1) * hb, t // POOL_HALO - 1), 0))
    full = lambda a: pl.BlockSpec(a.shape, lambda b, i: (0,) * a.ndim)
    lane_row = pl.BlockSpec((TOP_K, tm), lambda b, i: (0, b * n_tiles + i))
    weights = (p["w_pool"], p["pool_scale"], p["p_pool"], p["p_attn"], p["p_mem"], p["w_out"],
               p["ln1_g"], p["ln1_b"], p["w_router"])
    return pl.pallas_call(
        functools.partial(_mix_kernel, seq=seq),
        grid=(batch, n_tiles),
        in_specs=[row(D_MODEL), row(POOL_WIDTH), prev, nxt, row(ATTN_WIDTH), row(MEM_WIDTH),
                  row(N_BRANCHES * D_MODEL),
                  pl.BlockSpec((MEM_TOKENS, 2 * MEM_WIDTH), lambda b, i: (b, 0))]
                 + [full(w) for w in weights],
        out_specs=[row(D_MODEL), lane_row, lane_row],
        out_shape=[jax.ShapeDtypeStruct((t, D_MODEL), F32),
                   jax.ShapeDtypeStruct((TOP_K, t), I32),
                   jax.ShapeDtypeStruct((TOP_K, t), F32)],
        scratch_shapes=[pltpu.VMEM((tm + 2 * POOL_HALO, POOL_WIDTH), F32)],
        compiler_params=_params(dimension_semantics=("arbitrary", "arbitrary")),
        name="mix",
    )(xn, u, u, u, o, qm, gate, memkv, *weights)


def _moe_kernel(tok_ref, dst_ref, iblk_ref, ie_ref, ilo_ref, ihi_ref,
                x1_hbm, ws_ref, wgu_ref, wd_ref, y2_hbm,
                xbuf, ybuf, gsem, ssem, *, n_blocks):
    i = pl.program_id(0)
    groups = xbuf.shape[1]
    bm = groups * SUBLANES
    lo = ilo_ref[i]
    hi = ihi_ref[i]
    k = iblk_ref[i]
    slot = k & 1
    nonempty = hi > lo

    def hbm_row(ref, row):
        return ref.at[lax.shift_right_logical(row, 3), pl.ds(row & (SUBLANES - 1), 1)]

    def gather_start(blk, s):
        def body(g, c):
            for j in range(SUBLANES):
                tok = tok_ref[blk * bm + g * SUBLANES + j]
                pltpu.make_async_copy(hbm_row(x1_hbm, tok), xbuf.at[s, g, pl.ds(j, 1)], gsem.at[s]).start()
            return c
        lax.fori_loop(0, groups, body, 0)

    def gather_wait(s):
        pltpu.make_async_copy(x1_hbm.at[pl.ds(0, groups)], xbuf.at[s], gsem.at[s]).wait()

    def scatter_start(blk, s):
        def body(g, c):
            for j in range(SUBLANES):
                row = dst_ref[blk * bm + g * SUBLANES + j]
                pltpu.make_async_copy(ybuf.at[s, g, pl.ds(j, 1)], hbm_row(y2_hbm, row), ssem.at[s]).start()
            return c
        lax.fori_loop(0, groups, body, 0)

    def scatter_wait(s):
        pltpu.make_async_copy(ybuf.at[s], y2_hbm.at[pl.ds(0, groups)], ssem.at[s]).wait()

    @pl.when(jnp.logical_and(nonempty, lo == 0))
    def _():
        @pl.when(k == 0)
        def _():
            gather_start(0, 0)
        gather_wait(slot)

        @pl.when(k + 1 < n_blocks)
        def _():
            gather_start(k + 1, 1 - slot)

        @pl.when(k >= 2)
        def _():
            scatter_wait(slot)

    @pl.when(nonempty)
    def _():
        rows = lax.broadcasted_iota(I32, (bm, 1), 0)
        mine = jnp.logical_and(rows >= lo, rows < hi)
        x = xbuf[slot].reshape(bm, D_MODEL)
        gu = jnp.dot(x.astype(BF16), wgu_ref[0], preferred_element_type=F32)
        gate = gu[:, :EXPERT_HIDDEN]
        hid = (gate * _sigmoid(gate)) * gu[:, EXPERT_HIDDEN:]
        y = jnp.dot(hid.astype(BF16), wd_ref[0], preferred_element_type=F32) * ws_ref[...]

        @pl.when(lo == 0)
        def _():
            ybuf[slot] = jnp.where(mine, y, 0.0).reshape(groups, SUBLANES, D_MODEL)

        @pl.when(lo > 0)
        def _():
            old = ybuf[slot].reshape(bm, D_MODEL)
            ybuf[slot] = jnp.where(mine, y, old).reshape(groups, SUBLANES, D_MODEL)

    @pl.when(jnp.logical_and(nonempty, hi == bm))
    def _():
        scatter_start(k, slot)

        @pl.when(k == n_blocks - 1)
        def _():
            if n_blocks >= 2:
                scatter_wait(1 - slot)
            scatter_wait(slot)


def _moe_call(x1, sorted_tok, sorted_dst, wsorted, item_blk, item_e, item_lo, item_hi, w_gu, w_down):
    t = x1.shape[0]
    a = sorted_dst.shape[0]
    bm = BM_MOE
    n_items = item_blk.shape[0]
    groups = bm // SUBLANES
    imap = lambda f: (lambda i, tok, dst, blk, e, lo, hi: f(i, blk, e))
    grid_spec = pltpu.PrefetchScalarGridSpec(
        num_scalar_prefetch=6,
        grid=(n_items,),
        in_specs=[pl.BlockSpec(memory_space=pl.ANY),
                  pl.BlockSpec((bm, 1), imap(lambda i, blk, e: (blk[i], 0))),
                  pl.BlockSpec((1, D_MODEL, 2 * EXPERT_HIDDEN), imap(lambda i, blk, e: (e[i], 0, 0))),
                  pl.BlockSpec((1, EXPERT_HIDDEN, D_MODEL), imap(lambda i, blk, e: (e[i], 0, 0)))],
        out_specs=pl.BlockSpec(memory_space=pl.ANY),
        scratch_shapes=[pltpu.VMEM((2, groups, SUBLANES, D_MODEL), F32),
                        pltpu.VMEM((2, groups, SUBLANES, D_MODEL), F32),
                        pltpu.SemaphoreType.DMA((2,)), pltpu.SemaphoreType.DMA((2,))],
    )
    y2 = pl.pallas_call(
        functools.partial(_moe_kernel, n_blocks=a // bm),
        grid_spec=grid_spec,
        out_shape=jax.ShapeDtypeStruct((a // SUBLANES, SUBLANES, D_MODEL), F32),
        compiler_params=_params(dimension_semantics=("arbitrary",)),
        name="moe",
    )(sorted_tok, sorted_dst, item_blk, item_e, item_lo, item_hi,
      x1.reshape(t // SUBLANES, SUBLANES, D_MODEL), wsorted, w_gu, w_down)
    return y2.reshape(a, D_MODEL)


def _final_kernel(x1_ref, ya_ref, yb_ref, g_ref, b_ref, out_ref):
    moe = ya_ref[...] + yb_ref[...]
    out_ref[...] = _layer_norm(ALPHA * x1_ref[...] + moe, g_ref[...], b_ref[...])


def _final_call(x1, y2, ln_g, ln_b):
    t = x1.shape[0]
    tm = TM_FINAL
    n_tiles = t // tm
    return pl.pallas_call(
        _final_kernel,
        grid=(n_tiles,),
        in_specs=[pl.BlockSpec((tm, D_MODEL), lambda i: (i, 0)),
                  pl.BlockSpec((tm, D_MODEL), lambda i: (i, 0)),
                  pl.BlockSpec((tm, D_MODEL), lambda i: (n_tiles + i, 0)),
                  pl.BlockSpec((1, D_MODEL), lambda i: (0, 0)),
                  pl.BlockSpec((1, D_MODEL), lambda i: (0, 0))],
        out_specs=pl.BlockSpec((tm, D_MODEL), lambda i: (i, 0)),
        out_shape=jax.ShapeDtypeStruct((t, D_MODEL), F32),
        compiler_params=_params(dimension_semantics=("arbitrary",)),
        name="final",
    )(x1, y2, y2, ln_g, ln_b)


def _t5_bucket(rel):
    nb = N_REL_BUCKETS // 2
    max_exact = nb // 2
    ret = jnp.where(rel > 0, nb, 0)
    n = jnp.abs(rel)
    nf = jnp.maximum(n, 1).astype(F32)
    large = max_exact + (jnp.log(nf / max_exact) / math.log(REL_MAX_DISTANCE / max_exact)
                         * (nb - max_exact)).astype(I32)
    large = jnp.minimum(large, nb - 1)
    return ret + jnp.where(n < max_exact, n, large)


def _bias_tables(rel_table):
    tq = jnp.arange(BLOCK)[:, None]
    j = jnp.arange(3 * BLOCK)[None, :]
    rel = j - BLOCK - tq
    bias = rel_table[_t5_bucket(rel)].astype(F32)
    bias = jnp.where((jnp.abs(rel) <= WINDOW)[:, :, None], bias, NEG_INF)
    bias = jnp.transpose(bias, (2, 0, 1))
    head_a = jnp.array([4 * h + half for h in range(N_KV_HEADS) for half in range(2)])
    units = jnp.concatenate([bias[head_a], bias[head_a + 2]], axis=1)
    masked = jnp.full((2 * N_KV_HEADS, 2 * BLOCK, BLOCK), NEG_INF, F32)
    prev = jnp.stack([units[:, :, :BLOCK], masked])
    cur = units[:, :, BLOCK:2 * BLOCK]
    nxt = jnp.stack([units[:, :, 2 * BLOCK:], masked])
    return prev, cur, nxt


def _prepare(rel_bias_table, w_in, b_in, w_pool, pool_scale, p_pool, sink, p_attn, w_mem_kv, p_mem,
             w_out, ln1_g, ln1_b, w_router_group, w_router_expert, w_gu, w_down):
    l = 0
    o1 = POOL_WIDTH
    o2 = o1 + ATTN_WIDTH
    o3 = o2 + KV_WIDTH
    o4 = o3 + KV_WIDTH
    o5 = o4 + MEM_WIDTH
    def dup_heads(m):
        lead = m.shape[:-1]
        heads = m.reshape(lead + (N_KV_HEADS, HEAD_DIM))
        return jnp.concatenate([heads, heads], axis=-1).reshape(lead + (KV_DUP_WIDTH,))

    def arrange(m):
        return jnp.concatenate([m[..., :o2], dup_heads(m[..., o2:o3]), dup_heads(m[..., o3:o4]),
                                m[..., o4:]], axis=-1)

    w_router = jnp.zeros((ROUTER_ROWS, D_MODEL), F32)
    w_router = w_router.at[:N_EXPERT_GROUPS].set(w_router_group[l].T)
    w_router = w_router.at[SUBLANES:].set(w_router_expert[l].T)
    bias_prev, bias_cur, bias_next = _bias_tables(rel_bias_table)
    return dict(
        w_in=arrange(w_in[l].astype(BF16)), b_in=arrange(b_in[l])[None, :],
        w_pool=w_pool[l].astype(BF16), pool_scale=pool_scale[l][None, :],
        p_pool=p_pool[l].astype(BF16), p_attn=p_attn[l].astype(BF16), p_mem=p_mem[l].astype(BF16),
        w_out=w_out[l].astype(BF16), w_memkv=w_mem_kv[l].astype(BF16),
        ln1_g=ln1_g[l][None, :], ln1_b=ln1_b[l][None, :],
        w_router=w_router.astype(BF16), sink=sink[l],
        w_gu=w_gu[l].astype(BF16), w_down=w_down[l].astype(BF16),
        bias_prev=bias_prev, bias_cur=bias_cur, bias_next=bias_next,
    )


def _dispatch_plan(eid, wts):
    t = eid.shape[1]
    a = TOP_K * t
    bm = BM_MOE
    nblk = a // bm
    eflat = eid.reshape(a)
    wflat = wts.reshape(a)
    order = jnp.argsort(eflat, stable=True).astype(I32)
    wsorted = wflat[order][:, None]
    experts = jnp.arange(N_EXPERTS, dtype=I32)
    counts = jnp.sum((eflat[None, :] == experts[:, None]).astype(I32), axis=1)
    ends = jnp.cumsum(counts).astype(I32)
    starts = ends - counts
    cuts = jnp.sort(jnp.concatenate([jnp.arange(nblk, dtype=I32) * bm, starts, jnp.array([a], I32)]))
    lo_abs = cuts[:-1]
    hi_abs = cuts[1:]
    nonempty = hi_abs > lo_abs
    blk = jnp.minimum(lo_abs // bm, nblk - 1)
    probe = jnp.minimum(lo_abs, a - 1)
    item_e = jnp.sum((ends[None, :] <= probe[:, None]).astype(I32), axis=1)
    item_lo = jnp.where(nonempty, lo_abs - blk * bm, 0).astype(I32)
    item_hi = jnp.where(nonempty, hi_abs - blk * bm, 0).astype(I32)
    return order, wsorted, blk.astype(I32), item_e.astype(I32), item_lo, item_hi


def _trunk(x, mem, ln_in_g, ln_in_b, ln2_g, ln2_b, p):
    batch, seq, d = x.shape
    t = batch * seq
    assert seq % TM_MIX == 0 and seq % TQ_ATTN == 0 and t % TM_INPROJ == 0 and (TOP_K * t) % BM_MOE == 0
    memkv = _memkv_call(mem.reshape(batch * MEM_TOKENS, d), p["w_memkv"])
    xn, u, q, kk, vv, qm, gate = _inproj_call(x.reshape(t, d), ln_in_g[None, :], ln_in_b[None, :],
                                              p["w_in"], p["b_in"])
    o = _attn_call(q, kk, vv, p["sink"], p["bias_prev"], p["bias_cur"], p["bias_next"], batch, seq)
    x1, eid, wts = _mix_call(xn, u, o, qm, gate, memkv, p, batch, seq)
    order, wsorted, item_blk, item_e, item_lo, item_hi = _dispatch_plan(eid, wts)
    sorted_tok = jnp.where(order >= t, order - t, order)
    y2 = _moe_call(x1, sorted_tok, order, wsorted, item_blk, item_e, item_lo, item_hi, p["w_gu"], p["w_down"])
    out = _final_call(x1, y2, ln2_g[None, :], ln2_b[None, :])
    return out.reshape(batch, seq, d)


def kernel(x_prompt, x_sample, mem_prompt, mem_sample, ln_in_g, ln_in_b, rel_bias_table, w_in, b_in,
           w_pool, pool_scale, p_pool, sink, p_attn, w_mem_kv, p_mem, w_out, ln1_g, ln1_b,
           w_router_group, w_router_expert, w_gu, w_down, ln2_g, ln2_b):
    p = _prepare(rel_bias_table, w_in, b_in, w_pool, pool_scale, p_pool, sink, p_attn, w_mem_kv, p_mem,
                 w_out, ln1_g, ln1_b, w_router_group, w_router_expert, w_gu, w_down)
    y_prompt = _trunk(x_prompt, mem_prompt, ln_in_g, ln_in_b, ln2_g[0], ln2_b[0], p)
    y_sample = _trunk(x_sample, mem_sample, ln_in_g, ln_in_b, ln2_g[0], ln2_b[0], p)
    return (y_prompt, y_sample)
```

```python
import functools
import math

import jax
import jax.numpy as jnp
from jax import lax
from jax.experimental import pallas as pl
from jax.experimental.pallas import tpu as pltpu

F32 = jnp.float32
BF16 = jnp.bfloat16
I32 = jnp.int32

D_MODEL = 1024
DEPTH = 1
POOL_WIDTH = 512
POOL_WINDOWS = (2, 4, 8, 16)
N_POOL_GROUPS = 4
POOL_GROUP_CH = POOL_WIDTH // N_POOL_GROUPS
N_Q_HEADS = 16
N_KV_HEADS = 4
HEAD_DIM = 64
ATTN_WIDTH = N_Q_HEADS * HEAD_DIM
KV_WIDTH = N_KV_HEADS * HEAD_DIM
WINDOW = 128
BLOCK = 128
N_REL_BUCKETS = 32
REL_MAX_DISTANCE = 128
MEM_TOKENS = 256
MEM_HEADS = 4
MEM_HEAD_DIM = 128
MEM_WIDTH = MEM_HEADS * MEM_HEAD_DIM
N_BRANCHES = 3
N_EXPERT_GROUPS = 4
EXPERTS_PER_GROUP = 8
N_EXPERTS = N_EXPERT_GROUPS * EXPERTS_PER_GROUP
TOP_K = 2
EXPERT_HIDDEN = 512
ALPHA = (2 * DEPTH) ** 0.25
LN_EPS = 1e-5
NEG_INF = -1e30

LANES = 128
SUBLANES = 8
ROW_TILE = D_MODEL // LANES
VMEM_LIMIT_BYTES = 56 * 1024 * 1024

KV_DUP_WIDTH = N_KV_HEADS * LANES

C_U = 0
C_Q = C_U + POOL_WIDTH
C_K = C_Q + ATTN_WIDTH
C_V = C_K + KV_DUP_WIDTH
C_M = C_V + KV_DUP_WIDTH
C_G = C_M + MEM_WIDTH
C_END = C_G + N_BRANCHES * D_MODEL

TM_INPROJ = 512
TQ_ATTN = 256
TM_MIX = 512
BM_MOE = 256
MOE_ROW_UNROLL = 8
TM_FINAL = 512
POOL_HALO = 8
ROUTER_ROWS = 40


def _layer_norm(x, g, b):
    mu = jnp.mean(x, axis=-1, keepdims=True)
    xc = x - mu
    var = jnp.mean(xc * xc, axis=-1, keepdims=True)
    return xc * lax.rsqrt(var + LN_EPS) * g + b


def _sigmoid(x):
    return 1.0 / (1.0 + jnp.exp(-x))


def _rows_to_tiles(x):
    return pltpu.einshape("r(cl)->rcl", x, c=ROW_TILE)


def _tiles_to_rows(x):
    return pltpu.einshape("rcl->r(cl)", x)


def _params(**kw):
    return pltpu.CompilerParams(vmem_limit_bytes=VMEM_LIMIT_BYTES, **kw)


def _memkv_kernel(mem_ref, w_ref, out_ref):
    out_ref[...] = jnp.dot(mem_ref[...].astype(BF16), w_ref[...],
                           preferred_element_type=F32).astype(BF16)


def _memkv_call(mem2, w_memkv):
    rows = mem2.shape[0]
    return pl.pallas_call(
        _memkv_kernel,
        grid=(rows // MEM_TOKENS,),
        in_specs=[pl.BlockSpec((MEM_TOKENS, D_MODEL), lambda i: (i, 0)),
                  pl.BlockSpec((D_MODEL, 2 * MEM_WIDTH), lambda i: (0, 0))],
        out_specs=pl.BlockSpec((MEM_TOKENS, 2 * MEM_WIDTH), lambda i: (i, 0)),
        out_shape=jax.ShapeDtypeStruct((rows, 2 * MEM_WIDTH), BF16),
        compiler_params=_params(dimension_semantics=("arbitrary",)),
        name="memkv",
    )(mem2, w_memkv)


def _inproj_kernel(x_ref, g_ref, b_ref, w_ref, bias_ref,
                   xn_ref, u_ref, q_ref, k_ref, v_ref, qm_ref, gate_ref):
    xn = _layer_norm(x_ref[...], g_ref[...], b_ref[...])
    xn_ref[...] = xn
    xb = xn.astype(BF16)

    def seg(lo, hi):
        return jnp.dot(xb, w_ref[:, lo:hi], preferred_element_type=F32) + bias_ref[:, lo:hi]

    u_ref[...] = seg(C_U, C_Q)
    q_ref[...] = (seg(C_Q, C_K) * (HEAD_DIM ** -0.5)).astype(BF16)
    k_ref[...] = seg(C_K, C_V).astype(BF16)
    v_ref[...] = seg(C_V, C_M).astype(BF16)
    qm_ref[...] = seg(C_M, C_G).astype(BF16)
    for j in range(N_BRANCHES):
        lo = C_G + j * D_MODEL
        gate_ref[:, j * D_MODEL:(j + 1) * D_MODEL] = _sigmoid(seg(lo, lo + D_MODEL)).astype(BF16)


def _inproj_call(x2, ln_g, ln_b, w_in, b_in):
    t = x2.shape[0]
    tm = TM_INPROJ
    row = lambda width: pl.BlockSpec((tm, width), lambda i: (i, 0))
    const = lambda shape: pl.BlockSpec(shape, lambda i: (0, 0))
    widths = (D_MODEL, POOL_WIDTH, ATTN_WIDTH, KV_DUP_WIDTH, KV_DUP_WIDTH, MEM_WIDTH, N_BRANCHES * D_MODEL)
    dtypes = (F32, F32, BF16, BF16, BF16, BF16, BF16)
    return pl.pallas_call(
        _inproj_kernel,
        grid=(t // tm,),
        in_specs=[row(D_MODEL), const((1, D_MODEL)), const((1, D_MODEL)),
                  const((D_MODEL, C_END)), const((1, C_END))],
        out_specs=[row(w) for w in widths],
        out_shape=[jax.ShapeDtypeStruct((t, w), d) for w, d in zip(widths, dtypes)],
        compiler_params=_params(dimension_semantics=("arbitrary",)),
        name="inproj",
    )(x2, ln_g, ln_b, w_in, b_in)


def _attn_kernel(sink_ref, q_ref, kc_ref, kp_ref, kn_ref, vc_ref, vp_ref, vn_ref,
                 bp_ref, bc_ref, bn_ref, o_ref, klo, khi, vlo, vhi):
    i = pl.program_id(1)
    n_tiles = pl.num_programs(1)
    tq = q_ref.shape[0]
    n_qb = tq // BLOCK

    lane = lax.broadcasted_iota(I32, (BLOCK, KV_DUP_WIDTH), 1)
    low = (lane & (LANES - 1)) < HEAD_DIM

    def put(dst_lo, dst_hi, r0, val):
        zero = jnp.zeros_like(val)
        dst_lo[r0:r0 + BLOCK, :] = jnp.where(low, val, zero)
        dst_hi[r0:r0 + BLOCK, :] = jnp.where(low, zero, val)

    put(klo, khi, 0, kp_ref[...])
    put(vlo, vhi, 0, vp_ref[...])
    for j in range(n_qb):
        put(klo, khi, (j + 1) * BLOCK, kc_ref[j * BLOCK:(j + 1) * BLOCK, :])
        put(vlo, vhi, (j + 1) * BLOCK, vc_ref[j * BLOCK:(j + 1) * BLOCK, :])
    put(klo, khi, (n_qb + 1) * BLOCK, kn_ref[...])
    put(vlo, vhi, (n_qb + 1) * BLOCK, vn_ref[...])

    first = (i == 0).astype(I32)
    last = (i == n_tiles - 1).astype(I32)

    def softmax_rows(s, sink):
        m = jnp.maximum(jnp.max(s, axis=-1, keepdims=True), sink)
        p = jnp.exp(s - m)
        denom = jnp.sum(p, axis=-1, keepdims=True) + jnp.exp(sink - m)
        return (p * (1.0 / denom)).astype(BF16)

    for j in range(n_qb):
        r0 = j * BLOCK
        pv = first if j == 0 else 0
        nv = last if j == n_qb - 1 else 0
        for h in range(N_KV_HEADS):
            c0 = h * 2 * LANES
            q_pairs = jnp.concatenate([q_ref[r0:r0 + BLOCK, c0:c0 + LANES],
                                       q_ref[r0:r0 + BLOCK, c0 + LANES:c0 + 2 * LANES]], axis=0)
            acc = jnp.zeros((2 * BLOCK, LANES), F32)
            for half, (kref, vref) in enumerate(((klo, vlo), (khi, vhi))):
                unit = h * 2 + half
                kx = kref[r0:r0 + 3 * BLOCK, h * LANES:(h + 1) * LANES]
                vx = vref[r0:r0 + 3 * BLOCK, h * LANES:(h + 1) * LANES]
                s = lax.dot_general(q_pairs, kx, (((1,), (1,)), ((), ())),
                                    preferred_element_type=F32)
                bias = jnp.concatenate([bp_ref[pv, unit], bc_ref[unit], bn_ref[nv, unit]], axis=1)
                s = s + bias
                p_a = softmax_rows(s[:BLOCK], sink_ref[4 * h + half])
                p_b = softmax_rows(s[BLOCK:], sink_ref[4 * h + 2 + half])
                p = jnp.concatenate([p_a, p_b], axis=0)
                acc = acc + jnp.dot(p, vx, preferred_element_type=F32)
            o_ref[r0:r0 + BLOCK, c0:c0 + LANES] = acc[:BLOCK].astype(BF16)
            o_ref[r0:r0 + BLOCK, c0 + LANES:c0 + 2 * LANES] = acc[BLOCK:].astype(BF16)


def _attn_call(q, kk, vv, sink, bias_prev, bias_cur, bias_next, batch, seq):
    tq = TQ_ATTN
    n_qb = tq // BLOCK
    nb = seq // BLOCK
    n_tiles = seq // tq
    t = batch * seq

    cur = lambda width: pl.BlockSpec((tq, width), lambda b, i: (b * n_tiles + i, 0))
    prev = pl.BlockSpec((BLOCK, KV_DUP_WIDTH), lambda b, i: (b * nb + jnp.maximum(i * n_qb - 1, 0), 0))
    nxt = pl.BlockSpec((BLOCK, KV_DUP_WIDTH), lambda b, i: (b * nb + jnp.minimum((i + 1) * n_qb, nb - 1), 0))
    full = lambda a: pl.BlockSpec(a.shape, lambda b, i: (0,) * a.ndim)
    ext = (n_qb + 2) * BLOCK
    return pl.pallas_call(
        _attn_kernel,
        grid=(batch, n_tiles),
        in_specs=[pl.BlockSpec(memory_space=pltpu.SMEM),
                  cur(ATTN_WIDTH), cur(KV_DUP_WIDTH), prev, nxt, cur(KV_DUP_WIDTH), prev, nxt,
                  full(bias_prev), full(bias_cur), full(bias_next)],
        out_specs=cur(ATTN_WIDTH),
        out_shape=jax.ShapeDtypeStruct((t, ATTN_WIDTH), BF16),
        scratch_shapes=[pltpu.VMEM((ext, KV_DUP_WIDTH), BF16) for _ in range(4)],
        compiler_params=_params(dimension_semantics=("arbitrary", "arbitrary")),
        name="attn",
    )(sink, q, kk, kk, kk, vv, vv, vv, bias_prev, bias_cur, bias_next)


def _mix_kernel(xn_ref, u_ref, up_ref, un_ref, o_ref, qm_ref, gate_ref, memkv_ref,
                wpool_ref, pscale_ref, ppool_ref, pattn_ref, pmem_ref, wout_ref,
                g1_ref, b1_ref, wr_ref,
                x1_ref, x1r_ref, eid_ref, wts_ref, uext, *, seq):
    i = pl.program_id(1)
    n_tiles = pl.num_programs(1)
    tm = xn_ref.shape[0]

    halo = POOL_HALO
    uext[0:halo, :] = jnp.where(i > 0, up_ref[...], 0.0)
    uext[halo:halo + tm, :] = u_ref[...]
    uext[halo + tm:halo + tm + halo, :] = jnp.where(i < n_tiles - 1, un_ref[...], 0.0)
    pos = i * tm + lax.broadcasted_iota(I32, (tm, 1), 0)
    mixed = []
    for gi, win in enumerate(POOL_WINDOWS):
        c0 = gi * POOL_GROUP_CH
        half = win // 2
        total = jnp.zeros((tm, POOL_GROUP_CH), F32)
        for off in range(-half, half):
            total = total + uext[halo + off:halo + off + tm, c0:c0 + POOL_GROUP_CH]
        cnt = (jnp.minimum(pos + half, seq) - jnp.maximum(pos - half, 0)).astype(F32)
        pooled = total / cnt - u_ref[:, c0:c0 + POOL_GROUP_CH]
        mixed.append(jnp.dot(pooled.astype(BF16), wpool_ref[gi], preferred_element_type=F32)
                     * pscale_ref[:, c0:c0 + POOL_GROUP_CH])
    mixed = jnp.concatenate(mixed, axis=1).astype(BF16)
    merged = gate_ref[:, 0:D_MODEL].astype(F32) * jnp.dot(
        mixed, ppool_ref[...], preferred_element_type=F32)

    merged = merged + gate_ref[:, D_MODEL:2 * D_MODEL].astype(F32) * jnp.dot(
        o_ref[...], pattn_ref[...], preferred_element_type=F32)

    heads = []
    for h in range(MEM_HEADS):
        c0 = h * MEM_HEAD_DIM
        km = memkv_ref[:, c0:c0 + MEM_HEAD_DIM]
        vm = memkv_ref[:, MEM_WIDTH + c0:MEM_WIDTH + c0 + MEM_HEAD_DIM]
        s = lax.dot_general(qm_ref[:, c0:c0 + MEM_HEAD_DIM], km, (((1,), (1,)), ((), ())),
                            preferred_element_type=F32) * (MEM_HEAD_DIM ** -0.5)
        p = jnp.exp(s - jnp.max(s, axis=-1, keepdims=True))
        p = (p / jnp.sum(p, axis=-1, keepdims=True)).astype(BF16)
        heads.append(jnp.dot(p, vm, preferred_element_type=F32).astype(BF16))
    om = jnp.concatenate(heads, axis=1)
    merged = merged + gate_ref[:, 2 * D_MODEL:3 * D_MODEL].astype(F32) * jnp.dot(
        om, pmem_ref[...], preferred_element_type=F32)

    y = jnp.dot(merged.astype(BF16), wout_ref[...], preferred_element_type=F32)
    x1 = _layer_norm(ALPHA * xn_ref[...] + y, g1_ref[...], b1_ref[...])
    x1_ref[...] = x1
    x1r_ref[...] = _rows_to_tiles(x1)

    lt = lax.dot_general(wr_ref[...], x1.astype(BF16), (((1,), (1,)), ((), ())),
                         preferred_element_type=F32)
    gl = [lt[r:r + 1, :] for r in range(N_EXPERT_GROUPS)]
    gmax = gl[0]
    grp = jnp.zeros((1, tm), I32)
    for r in range(1, N_EXPERT_GROUPS):
        better = gl[r] > gmax
        grp = jnp.where(better, r, grp)
        gmax = jnp.where(better, gl[r], gmax)
    gsum = gl[0] * 0.0
    for r in range(N_EXPERT_GROUPS):
        gsum = gsum + jnp.exp(gl[r] - gmax)
    gp = 1.0 / gsum
    sel = jnp.zeros((EXPERTS_PER_GROUP, tm), F32)
    for r in range(N_EXPERT_GROUPS):
        rows = lt[SUBLANES + r * EXPERTS_PER_GROUP:SUBLANES + (r + 1) * EXPERTS_PER_GROUP, :]
        sel = jnp.where(grp == r, rows, sel)
    ridx = lax.broadcasted_iota(I32, (EXPERTS_PER_GROUP, tm), 0)
    top1 = jnp.max(sel, axis=0, keepdims=True)
    i1 = jnp.min(jnp.where(sel == top1, ridx, EXPERTS_PER_GROUP), axis=0, keepdims=True)
    rest = jnp.where(ridx == i1, -jnp.inf, sel)
    top2 = jnp.max(rest, axis=0, keepdims=True)
    i2 = jnp.min(jnp.where(rest == top2, ridx, EXPERTS_PER_GROUP), axis=0, keepdims=True)
    e2 = jnp.exp(top2 - top1)
    inv = gp / (1.0 + e2)
    eid_ref[0:1, :] = grp * EXPERTS_PER_GROUP + i1
    eid_ref[1:2, :] = grp * EXPERTS_PER_GROUP + i2
    wts_ref[0:1, :] = inv
    wts_ref[1:2, :] = e2 * inv


def _mix_call(xn, u, o, qm, gate, memkv, p, batch, seq):
    tm = TM_MIX
    n_tiles = seq // tm
    t = batch * seq
    hb = tm // POOL_HALO

    row = lambda width: pl.BlockSpec((tm, width), lambda b, i: (b * n_tiles + i, 0))
    prev = pl.BlockSpec((POOL_HALO, POOL_WIDTH),
                        lambda b, i: (jnp.maximum((b * n_tiles + i) * hb - 1, 0), 0))
    nxt = pl.BlockSpec((POOL_HALO, POOL_WIDTH),
                       lambda b, i: (jnp.minimum((b * n_tiles + i + 1) * hb, t // POOL_HALO - 1), 0))
    full = lambda a: pl.BlockSpec(a.shape, lambda b, i: (0,) * a.ndim)
    lane_row = pl.BlockSpec((TOP_K, tm), lambda b, i: (0, b * n_tiles + i))
    weights = (p["w_pool"], p["pool_scale"], p["p_pool"], p["p_attn"], p["p_mem"], p["w_out"],
               p["ln1_g"], p["ln1_b"], p["w_router"])
    return pl.pallas_call(
        functools.partial(_mix_kernel, seq=seq),
        grid=(batch, n_tiles),
        in_specs=[row(D_MODEL), row(POOL_WIDTH), prev, nxt, row(ATTN_WIDTH), row(MEM_WIDTH),
                  row(N_BRANCHES * D_MODEL),
                  pl.BlockSpec((MEM_TOKENS, 2 * MEM_WIDTH), lambda b, i: (b, 0))]
                 + [full(w) for w in weights],
        out_specs=[row(D_MODEL),
                   pl.BlockSpec((tm, ROW_TILE, LANES), lambda b, i: (b * n_tiles + i, 0, 0)),
                   lane_row, lane_row],
        out_shape=[jax.ShapeDtypeStruct((t, D_MODEL), F32),
                   jax.ShapeDtypeStruct((t, ROW_TILE, LANES), F32),
                   jax.ShapeDtypeStruct((TOP_K, t), I32),
                   jax.ShapeDtypeStruct((TOP_K, t), F32)],
        scratch_shapes=[pltpu.VMEM((tm + 2 * POOL_HALO, POOL_WIDTH), F32)],
        compiler_params=_params(dimension_semantics=("arbitrary", "arbitrary")),
        name="mix",
    )(xn, u, u, u, o, qm, gate, memkv, *weights)


def _moe_kernel(tok_ref, dst_ref, iblk_ref, ie_ref, ilo_ref, ihi_ref,
                x1_hbm, ws_ref, wgu_ref, wd_ref, y2_hbm,
                xbuf, ybuf, gsem, ssem, *, n_blocks):
    i = pl.program_id(0)
    bm = xbuf.shape[1]
    lo = ilo_ref[i]
    hi = ihi_ref[i]
    k = iblk_ref[i]
    slot = k & 1
    nonempty = hi > lo

    def gather_start(blk, s):
        def body(r, c):
            pltpu.make_async_copy(x1_hbm.at[tok_ref[blk * bm + r]], xbuf.at[s, r], gsem.at[s]).start()
            return c
        lax.fori_loop(0, bm, body, 0, unroll=MOE_ROW_UNROLL)

    def gather_wait(s):
        pltpu.make_async_copy(x1_hbm.at[pl.ds(0, bm)], xbuf.at[s], gsem.at[s]).wait()

    def scatter_start(blk, s):
        def body(r, c):
            pltpu.make_async_copy(ybuf.at[s, r], y2_hbm.at[dst_ref[blk * bm + r]], ssem.at[s]).start()
            return c
        lax.fori_loop(0, bm, body, 0, unroll=MOE_ROW_UNROLL)

    def scatter_wait(s):
        pltpu.make_async_copy(ybuf.at[s], y2_hbm.at[pl.ds(0, bm)], ssem.at[s]).wait()

    @pl.when(jnp.logical_and(nonempty, lo == 0))
    def _():
        @pl.when(k == 0)
        def _():
            gather_start(0, 0)
        gather_wait(slot)

        @pl.when(k + 1 < n_blocks)
        def _():
            gather_start(k + 1, 1 - slot)

        @pl.when(k >= 2)
        def _():
            scatter_wait(slot)

    @pl.when(nonempty)
    def _():
        x = _tiles_to_rows(xbuf[slot])
        gu = jnp.dot(x.astype(BF16), wgu_ref[0], preferred_element_type=F32)
        gate = gu[:, :EXPERT_HIDDEN]
        hid = (gate * _sigmoid(gate)) * gu[:, EXPERT_HIDDEN:]
        y = jnp.dot(hid.astype(BF16), wd_ref[0], preferred_element_type=F32) * ws_ref[...]
        y = _rows_to_tiles(y)
        rows = lax.broadcasted_iota(I32, y.shape, 0)
        mine = jnp.logical_and(rows >= lo, rows < hi)

        @pl.when(lo == 0)
        def _():
            ybuf[slot] = jnp.where(mine, y, 0.0)

        @pl.when(lo > 0)
        def _():
            ybuf[slot] = jnp.where(mine, y, ybuf[slot])

    @pl.when(jnp.logical_and(nonempty, hi == bm))
    def _():
        scatter_start(k, slot)

        @pl.when(k == n_blocks - 1)
        def _():
            if n_blocks >= 2:
                scatter_wait(1 - slot)
            scatter_wait(slot)


def _moe_call(x1r, sorted_tok, sorted_dst, wsorted, item_blk, item_e, item_lo, item_hi, w_gu, w_down):
    a = sorted_dst.shape[0]
    bm = BM_MOE
    n_items = item_blk.shape[0]
    imap = lambda f: (lambda i, tok, dst, blk, e, lo, hi: f(i, blk, e))
    grid_spec = pltpu.PrefetchScalarGridSpec(
        num_scalar_prefetch=6,
        grid=(n_items,),
        in_specs=[pl.BlockSpec(memory_space=pl.ANY),
                  pl.BlockSpec((bm, 1), imap(lambda i, blk, e: (blk[i], 0))),
                  pl.BlockSpec((1, D_MODEL, 2 * EXPERT_HIDDEN), imap(lambda i, blk, e: (e[i], 0, 0))),
                  pl.BlockSpec((1, EXPERT_HIDDEN, D_MODEL), imap(lambda i, blk, e: (e[i], 0, 0)))],
        out_specs=pl.BlockSpec(memory_space=pl.ANY),
        scratch_shapes=[pltpu.VMEM((2, bm, ROW_TILE, LANES), F32),
                        pltpu.VMEM((2, bm, ROW_TILE, LANES), F32),
                        pltpu.SemaphoreType.DMA((2,)), pltpu.SemaphoreType.DMA((2,))],
    )
    return pl.pallas_call(
        functools.partial(_moe_kernel, n_blocks=a // bm),
        grid_spec=grid_spec,
        out_shape=jax.ShapeDtypeStruct((a, ROW_TILE, LANES), F32),
        compiler_params=_params(dimension_semantics=("arbitrary",)),
        name="moe",
    )(sorted_tok, sorted_dst, item_blk, item_e, item_lo, item_hi, x1r, wsorted, w_gu, w_down)


def _final_kernel(x1_ref, ya_ref, yb_ref, g_ref, b_ref, out_ref):
    moe = _tiles_to_rows(ya_ref[...]) + _tiles_to_rows(yb_ref[...])
    out_ref[...] = _layer_norm(ALPHA * x1_ref[...] + moe, g_ref[...], b_ref[...])


def _final_call(x1, y2, ln_g, ln_b):
    t = x1.shape[0]
    tm = TM_FINAL
    n_tiles = t // tm
    return pl.pallas_call(
        _final_kernel,
        grid=(n_tiles,),
        in_specs=[pl.BlockSpec((tm, D_MODEL), lambda i: (i, 0)),
                  pl.BlockSpec((tm, ROW_TILE, LANES), lambda i: (i, 0, 0)),
                  pl.BlockSpec((tm, ROW_TILE, LANES), lambda i: (n_tiles + i, 0, 0)),
                  pl.BlockSpec((1, D_MODEL), lambda i: (0, 0)),
                  pl.BlockSpec((1, D_MODEL), lambda i: (0, 0))],
        out_specs=pl.BlockSpec((tm, D_MODEL), lambda i: (i, 0)),
        out_shape=jax.ShapeDtypeStruct((t, D_MODEL), F32),
        compiler_params=_params(dimension_semantics=("arbitrary",)),
        name="final",
    )(x1, y2, y2, ln_g, ln_b)


def _t5_bucket(rel):
    nb = N_REL_BUCKETS // 2
    max_exact = nb // 2
    ret = jnp.where(rel > 0, nb, 0)
    n = jnp.abs(rel)
    nf = jnp.maximum(n, 1).astype(F32)
    large = max_exact + (jnp.log(nf / max_exact) / math.log(REL_MAX_DISTANCE / max_exact)
                         * (nb - max_exact)).astype(I32)
    large = jnp.minimum(large, nb - 1)
    return ret + jnp.where(n < max_exact, n, large)


def _bias_tables(rel_table):
    n, m = BLOCK, 3 * BLOCK
    rel = jnp.arange(-(2 * BLOCK - 1), 2 * BLOCK)
    by_rel = jnp.where((jnp.abs(rel) <= WINDOW)[:, None], rel_table[_t5_bucket(rel)].astype(F32), NEG_INF)
    by_rel = jnp.pad(by_rel.T, ((0, 0), (0, 1)))
    skew = jnp.tile(by_rel, (1, n))[:, :n * (m + n - 1)].reshape(N_Q_HEADS, n, m + n - 1)
    bias = skew[:, :, n - 1:n - 1 + m]
    heads = [4 * h + half for h in range(N_KV_HEADS) for half in range(2)]
    units = jnp.concatenate([jnp.stack([bias[a] for a in heads]),
                             jnp.stack([bias[a + 2] for a in heads])], axis=1)
    masked = jnp.full((2 * N_KV_HEADS, 2 * BLOCK, BLOCK), NEG_INF, F32)
    prev = jnp.stack([units[:, :, :BLOCK], masked])
    cur = units[:, :, BLOCK:2 * BLOCK]
    nxt = jnp.stack([units[:, :, 2 * BLOCK:], masked])
    return prev, cur, nxt


def _prepare(rel_bias_table, w_in, b_in, w_pool, pool_scale, p_pool, sink, p_attn, w_mem_kv, p_mem,
             w_out, ln1_g, ln1_b, w_router_group, w_router_expert, w_gu, w_down):
    l = 0
    o1 = POOL_WIDTH
    o2 = o1 + ATTN_WIDTH
    o3 = o2 + KV_WIDTH
    o4 = o3 + KV_WIDTH
    o5 = o4 + MEM_WIDTH
    def dup_heads(m):
        lead = m.shape[:-1]
        heads = m.reshape(lead + (N_KV_HEADS, HEAD_DIM))
        return jnp.concatenate([heads, heads], axis=-1).reshape(lead + (KV_DUP_WIDTH,))

    def arrange(m):
        return jnp.concatenate([m[..., :o2], dup_heads(m[..., o2:o3]), dup_heads(m[..., o3:o4]),
                                m[..., o4:]], axis=-1)

    w_router = jnp.zeros((ROUTER_ROWS, D_MODEL), F32)
    w_router = w_router.at[:N_EXPERT_GROUPS].set(w_router_group[l].T)
    w_router = w_router.at[SUBLANES:].set(w_router_expert[l].T)
    bias_prev, bias_cur, bias_next = _bias_tables(rel_bias_table)
    return dict(
        w_in=arrange(w_in[l].astype(BF16)), b_in=arrange(b_in[l])[None, :],
        w_pool=w_pool[l].astype(BF16), pool_scale=pool_scale[l][None, :],
        p_pool=p_pool[l].astype(BF16), p_attn=p_attn[l].astype(BF16), p_mem=p_mem[l].astype(BF16),
        w_out=w_out[l].astype(BF16), w_memkv=w_mem_kv[l].astype(BF16),
        ln1_g=ln1_g[l][None, :], ln1_b=ln1_b[l][None, :],
        w_router=w_router.astype(BF16), sink=sink[l],
        w_gu=w_gu[l].astype(BF16), w_down=w_down[l].astype(BF16),
        bias_prev=bias_prev, bias_cur=bias_cur, bias_next=bias_next,
    )


def _dispatch_plan(eid, wts):
    t = eid.shape[1]
    a = TOP_K * t
    bm = BM_MOE
    nblk = a // bm
    eflat = eid.reshape(a)
    wflat = wts.reshape(a)
    order = jnp.argsort(eflat, stable=True).astype(I32)
    wsorted = wflat[order][:, None]
    experts = jnp.arange(N_EXPERTS, dtype=I32)
    counts = jnp.sum((eflat[None, :] == experts[:, None]).astype(I32), axis=1)
    ends = jnp.cumsum(counts).astype(I32)
    starts = ends - counts
    cuts = jnp.sort(jnp.concatenate([jnp.arange(nblk, dtype=I32) * bm, starts, jnp.array([a], I32)]))
    lo_abs = cuts[:-1]
    hi_abs = cuts[1:]
    nonempty = hi_abs > lo_abs
    blk = jnp.minimum(lo_abs // bm, nblk - 1)
    probe = jnp.minimum(lo_abs, a - 1)
    item_e = jnp.sum((ends[None, :] <= probe[:, None]).astype(I32), axis=1)
    item_lo = jnp.where(nonempty, lo_abs - blk * bm, 0).astype(I32)
    item_hi = jnp.where(nonempty, hi_abs - blk * bm, 0).astype(I32)
    return order, wsorted, blk.astype(I32), item_e.astype(I32), item_lo, item_hi


def _trunk(x, mem, ln_in_g, ln_in_b, ln2_g, ln2_b, p):
    batch, seq, d = x.shape
    t = batch * seq
    assert seq % TM_MIX == 0 and seq % TQ_ATTN == 0 and t % TM_INPROJ == 0 and (TOP_K * t) % BM_MOE == 0
    memkv = _memkv_call(mem.reshape(batch * MEM_TOKENS, d), p["w_memkv"])
    xn, u, q, kk, vv, qm, gate = _inproj_call(x.reshape(t, d), ln_in_g[None, :], ln_in_b[None, :],
                                              p["w_in"], p["b_in"])
    o = _attn_call(q, kk, vv, p["sink"], p["bias_prev"], p["bias_cur"], p["bias_next"], batch, seq)
    x1, x1r, eid, wts = _mix_call(xn, u, o, qm, gate, memkv, p, batch, seq)
    order, wsorted, item_blk, item_e, item_lo, item_hi = _dispatch_plan(eid, wts)
    sorted_tok = jnp.where(order >= t, order - t, order)
    y2 = _moe_call(x1r, sorted_tok, order, wsorted, item_blk, item_e, item_lo, item_hi, p["w_gu"], p["w_down"])
    out = _final_call(x1, y2, ln2_g[None, :], ln2_b[None, :])
    return out.reshape(batch, seq, d)


def kernel(x_prompt, x_sample, mem_prompt, mem_sample, ln_in_g, ln_in_b, rel_bias_table, w_in, b_in,
           w_pool, pool_scale, p_pool, sink, p_attn, w_mem_kv, p_mem, w_out, ln1_g, ln1_b,
           w_router_group, w_router_expert, w_gu, w_down, ln2_g, ln2_b):
    p = _prepare(rel_bias_table, w_in, b_in, w_pool, pool_scale, p_pool, sink, p_attn, w_mem_kv, p_mem,
                 w_out, ln1_g, ln1_b, w_router_group, w_router_expert, w_gu, w_down)
    y_prompt = _trunk(x_prompt, mem_prompt, ln_in_g, ln_in_b, ln2_g[0], ln2_b[0], p)
    y_sample = _trunk(x_sample, mem_sample, ln_in_g, ln_in_b, ln2_g[0], ln2_b[0], p)
    return (y_prompt, y_sample)
```

```python
import functools
import math

import jax
import jax.numpy as jnp
from jax import lax
from jax.experimental import pallas as pl
from jax.experimental.pallas import tpu as pltpu

F32 = jnp.float32
BF16 = jnp.bfloat16
I32 = jnp.int32

D_MODEL = 1024
DEPTH = 1
POOL_WIDTH = 512
POOL_WINDOWS = (2, 4, 8, 16)
N_POOL_GROUPS = 4
POOL_GROUP_CH = POOL_WIDTH // N_POOL_GROUPS
N_Q_HEADS = 16
N_KV_HEADS = 4
HEAD_DIM = 64
ATTN_WIDTH = N_Q_HEADS * HEAD_DIM
KV_WIDTH = N_KV_HEADS * HEAD_DIM
WINDOW = 128
BLOCK = 128
N_REL_BUCKETS = 32
REL_MAX_DISTANCE = 128
MEM_TOKENS = 256
MEM_HEADS = 4
MEM_HEAD_DIM = 128
MEM_WIDTH = MEM_HEADS * MEM_HEAD_DIM
N_BRANCHES = 3
N_EXPERT_GROUPS = 4
EXPERTS_PER_GROUP = 8
N_EXPERTS = N_EXPERT_GROUPS * EXPERTS_PER_GROUP
TOP_K = 2
EXPERT_HIDDEN = 512
ALPHA = (2 * DEPTH) ** 0.25
LN_EPS = 1e-5
NEG_INF = -1e30

LANES = 128
SUBLANES = 8
ROW_TILE = D_MODEL // LANES
VMEM_LIMIT_BYTES = 56 * 1024 * 1024

KV_DUP_WIDTH = N_KV_HEADS * LANES

C_U = 0
C_Q = C_U + POOL_WIDTH
C_K = C_Q + ATTN_WIDTH
C_V = C_K + KV_WIDTH
C_M = C_V + KV_WIDTH
C_G = C_M + MEM_WIDTH
C_END = C_G + N_BRANCHES * D_MODEL

TM_INPROJ = 512
TQ_ATTN = 256
TM_MIX = 512
BM_MOE = 256
MOE_ROW_UNROLL = 8
TM_FINAL = 512
POOL_HALO = 8
ROUTER_ROWS = 40


def _layer_norm(x, g, b):
    mu = jnp.mean(x, axis=-1, keepdims=True)
    xc = x - mu
    var = jnp.mean(xc * xc, axis=-1, keepdims=True)
    return xc * lax.rsqrt(var + LN_EPS) * g + b


def _sigmoid(x):
    return 1.0 / (1.0 + jnp.exp(-x))


def _rows_to_tiles(x):
    return pltpu.einshape("r(cl)->rcl", x, c=ROW_TILE)


def _tiles_to_rows(x):
    return pltpu.einshape("rcl->r(cl)", x)


def _params(**kw):
    return pltpu.CompilerParams(vmem_limit_bytes=VMEM_LIMIT_BYTES, **kw)


def _memkv_kernel(mem_ref, w_ref, out_ref):
    out_ref[...] = jnp.dot(mem_ref[...].astype(BF16), w_ref[...],
                           preferred_element_type=F32).astype(BF16)


def _memkv_call(mem2, w_memkv):
    rows = mem2.shape[0]
    return pl.pallas_call(
        _memkv_kernel,
        grid=(rows // MEM_TOKENS,),
        in_specs=[pl.BlockSpec((MEM_TOKENS, D_MODEL), lambda i: (i, 0)),
                  pl.BlockSpec((D_MODEL, 2 * MEM_WIDTH), lambda i: (0, 0))],
        out_specs=pl.BlockSpec((MEM_TOKENS, 2 * MEM_WIDTH), lambda i: (i, 0)),
        out_shape=jax.ShapeDtypeStruct((rows, 2 * MEM_WIDTH), BF16),
        compiler_params=_params(dimension_semantics=("arbitrary",)),
        name="memkv",
    )(mem2, w_memkv)


def _inproj_kernel(x_ref, g_ref, b_ref, w_ref, bias_ref,
                   xn_ref, u_ref, q_ref, k_ref, v_ref, qm_ref, gate_ref):
    xn = _layer_norm(x_ref[...], g_ref[...], b_ref[...])
    xn_ref[...] = xn
    xb = xn.astype(BF16)

    def seg(lo, hi):
        return jnp.dot(xb, w_ref[:, lo:hi], preferred_element_type=F32) + bias_ref[:, lo:hi]

    u_ref[...] = seg(C_U, C_Q)
    q_ref[...] = (seg(C_Q, C_K) * (HEAD_DIM ** -0.5)).astype(BF16)
    kv = seg(C_K, C_M)
    low = lax.broadcasted_iota(I32, (kv.shape[0], LANES), 1) < HEAD_DIM
    for out_ref, c_lo in ((k_ref, 0), (v_ref, KV_WIDTH)):
        for c in range(KV_WIDTH // LANES):
            blk = kv[:, c_lo + c * LANES:c_lo + (c + 1) * LANES]
            rot = pltpu.roll(blk, HEAD_DIM, 1)
            out_ref[:, 2 * c * LANES:(2 * c + 1) * LANES] = jnp.where(low, blk, rot).astype(BF16)
            out_ref[:, (2 * c + 1) * LANES:(2 * c + 2) * LANES] = jnp.where(low, rot, blk).astype(BF16)
    qm_ref[...] = seg(C_M, C_G).astype(BF16)
    for j in range(N_BRANCHES):
        lo = C_G + j * D_MODEL
        gate_ref[:, j * D_MODEL:(j + 1) * D_MODEL] = _sigmoid(seg(lo, lo + D_MODEL)).astype(BF16)


def _inproj_call(x2, ln_g, ln_b, w_in, b_in):
    t = x2.shape[0]
    tm = TM_INPROJ
    row = lambda width: pl.BlockSpec((tm, width), lambda i: (i, 0))
    const = lambda shape: pl.BlockSpec(shape, lambda i: (0, 0))
    widths = (D_MODEL, POOL_WIDTH, ATTN_WIDTH, KV_DUP_WIDTH, KV_DUP_WIDTH, MEM_WIDTH, N_BRANCHES * D_MODEL)
    dtypes = (F32, F32, BF16, BF16, BF16, BF16, BF16)
    return pl.pallas_call(
        _inproj_kernel,
        grid=(t // tm,),
        in_specs=[row(D_MODEL), const((1, D_MODEL)), const((1, D_MODEL)),
                  const((D_MODEL, C_END)), const((1, C_END))],
        out_specs=[row(w) for w in widths],
        out_shape=[jax.ShapeDtypeStruct((t, w), d) for w, d in zip(widths, dtypes)],
        compiler_params=_params(dimension_semantics=("arbitrary",)),
        name="inproj",
    )(x2, ln_g, ln_b, w_in, b_in)


def _attn_kernel(sink_ref, q_ref, kc_ref, kp_ref, kn_ref, vc_ref, vp_ref, vn_ref,
                 bp_ref, bc_ref, bn_ref, o_ref, klo, khi, vlo, vhi, s_scr, p_scr):
    i = pl.program_id(1)
    n_tiles = pl.num_programs(1)
    tq = q_ref.shape[0]
    n_qb = tq // BLOCK

    lane = lax.broadcasted_iota(I32, (BLOCK, KV_DUP_WIDTH), 1)
    low = (lane & (LANES - 1)) < HEAD_DIM

    def put(dst_lo, dst_hi, r0, val):
        zero = jnp.zeros_like(val)
        dst_lo[r0:r0 + BLOCK, :] = jnp.where(low, val, zero)
        dst_hi[r0:r0 + BLOCK, :] = jnp.where(low, zero, val)

    put(klo, khi, 0, kp_ref[...])
    put(vlo, vhi, 0, vp_ref[...])
    for j in range(n_qb):
        put(klo, khi, (j + 1) * BLOCK, kc_ref[j * BLOCK:(j + 1) * BLOCK, :])
        put(vlo, vhi, (j + 1) * BLOCK, vc_ref[j * BLOCK:(j + 1) * BLOCK, :])
    put(klo, khi, (n_qb + 1) * BLOCK, kn_ref[...])
    put(vlo, vhi, (n_qb + 1) * BLOCK, vn_ref[...])

    first = (i == 0).astype(I32)
    last = (i == n_tiles - 1).astype(I32)

    def softmax_rows(s, sink):
        m = jnp.maximum(jnp.max(s, axis=-1, keepdims=True), sink)
        p = jnp.exp(s - m)
        denom = jnp.sum(p, axis=-1, keepdims=True) + jnp.exp(sink - m)
        return (p * (1.0 / denom)).astype(BF16)

    units = [(j, h, half) for j in range(n_qb) for h in range(N_KV_HEADS) for half in range(2)]
    always = i >= 0

    @pl.when(always)
    def _scores():
        for u, (j, h, half) in enumerate(units):
            r0 = j * BLOCK
            c0 = h * 2 * LANES
            pv = first if j == 0 else 0
            nv = last if j == n_qb - 1 else 0
            unit = h * 2 + half
            q_pairs = jnp.concatenate([q_ref[r0:r0 + BLOCK, c0:c0 + LANES],
                                       q_ref[r0:r0 + BLOCK, c0 + LANES:c0 + 2 * LANES]], axis=0)
            kx = (klo, khi)[half][r0:r0 + 3 * BLOCK, h * LANES:(h + 1) * LANES]
            s = lax.dot_general(q_pairs, kx, (((1,), (1,)), ((), ())), preferred_element_type=F32)
            bias = jnp.concatenate([bp_ref[pv, unit], bc_ref[unit], bn_ref[nv, unit]], axis=1)
            s_scr[u] = s + bias

    @pl.when(always)
    def _softmax():
        for u, (j, h, half) in enumerate(units):
            p_scr[u, :BLOCK] = softmax_rows(s_scr[u, :BLOCK], sink_ref[4 * h + half])
            p_scr[u, BLOCK:] = softmax_rows(s_scr[u, BLOCK:], sink_ref[4 * h + 2 + half])

    @pl.when(always)
    def _values():
        for j in range(n_qb):
            r0 = j * BLOCK
            for h in range(N_KV_HEADS):
                c0 = h * 2 * LANES
                acc = jnp.zeros((2 * BLOCK, LANES), F32)
                for half, vref in enumerate((vlo, vhi)):
                    u = units.index((j, h, half))
                    vx = vref[r0:r0 + 3 * BLOCK, h * LANES:(h + 1) * LANES]
                    acc = acc + jnp.dot(p_scr[u], vx, preferred_element_type=F32)
                o_ref[r0:r0 + BLOCK, c0:c0 + LANES] = acc[:BLOCK].astype(BF16)
                o_ref[r0:r0 + BLOCK, c0 + LANES:c0 + 2 * LANES] = acc[BLOCK:].astype(BF16)


def _attn_call(q, kk, vv, sink, bias_prev, bias_cur, bias_next, batch, seq):
    tq = TQ_ATTN
    n_qb = tq // BLOCK
    nb = seq // BLOCK
    n_tiles = seq // tq
    t = batch * seq

    cur = lambda width: pl.BlockSpec((tq, width), lambda b, i: (b * n_tiles + i, 0))
    prev = pl.BlockSpec((BLOCK, KV_DUP_WIDTH), lambda b, i: (b * nb + jnp.maximum(i * n_qb - 1, 0), 0))
    nxt = pl.BlockSpec((BLOCK, KV_DUP_WIDTH), lambda b, i: (b * nb + jnp.minimum((i + 1) * n_qb, nb - 1), 0))
    full = lambda a: pl.BlockSpec(a.shape, lambda b, i: (0,) * a.ndim)
    ext = (n_qb + 2) * BLOCK
    n_units = n_qb * N_KV_HEADS * 2
    return pl.pallas_call(
        _attn_kernel,
        grid=(batch, n_tiles),
        in_specs=[pl.BlockSpec(memory_space=pltpu.SMEM),
                  cur(ATTN_WIDTH), cur(KV_DUP_WIDTH), prev, nxt, cur(KV_DUP_WIDTH), prev, nxt,
                  full(bias_prev), full(bias_cur), full(bias_next)],
        out_specs=cur(ATTN_WIDTH),
        out_shape=jax.ShapeDtypeStruct((t, ATTN_WIDTH), BF16),
        scratch_shapes=[pltpu.VMEM((ext, KV_DUP_WIDTH), BF16) for _ in range(4)]
                       + [pltpu.VMEM((n_units, 2 * BLOCK, 3 * BLOCK), F32),
                          pltpu.VMEM((n_units, 2 * BLOCK, 3 * BLOCK), BF16)],
        compiler_params=_params(dimension_semantics=("arbitrary", "arbitrary")),
        name="attn",
    )(sink, q, kk, kk, kk, vv, vv, vv, bias_prev, bias_cur, bias_next)


def _mix_kernel(xn_ref, u_ref, up_ref, un_ref, o_ref, qm_ref, gate_ref, memkv_ref,
                wpool_ref, pscale_ref, ppool_ref, pattn_ref, pmem_ref, wout_ref,
                g1_ref, b1_ref, wr_ref,
                x1r_ref, eid_ref, wts_ref, uext, *, seq):
    i = pl.program_id(1)
    n_tiles = pl.num_programs(1)
    tm = xn_ref.shape[0]

    halo = POOL_HALO
    uext[0:halo, :] = jnp.where(i > 0, up_ref[...], 0.0)
    uext[halo:halo + tm, :] = u_ref[...]
    uext[halo + tm:halo + tm + halo, :] = jnp.where(i < n_tiles - 1, un_ref[...], 0.0)
    pos = i * tm + lax.broadcasted_iota(I32, (tm, 1), 0)
    mixed = []
    for gi, win in enumerate(POOL_WINDOWS):
        c0 = gi * POOL_GROUP_CH
        half = win // 2
        total = jnp.zeros((tm, POOL_GROUP_CH), F32)
        for off in range(-half, half):
            total = total + uext[halo + off:halo + off + tm, c0:c0 + POOL_GROUP_CH]
        cnt = (jnp.minimum(pos + half, seq) - jnp.maximum(pos - half, 0)).astype(F32)
        pooled = total * (1.0 / cnt) - u_ref[:, c0:c0 + POOL_GROUP_CH]
        mixed.append(jnp.dot(pooled.astype(BF16), wpool_ref[gi], preferred_element_type=F32)
                     * pscale_ref[:, c0:c0 + POOL_GROUP_CH])
    mixed = jnp.concatenate(mixed, axis=1).astype(BF16)
    merged = gate_ref[:, 0:D_MODEL].astype(F32) * jnp.dot(
        mixed, ppool_ref[...], preferred_element_type=F32)

    merged = merged + gate_ref[:, D_MODEL:2 * D_MODEL].astype(F32) * jnp.dot(
        o_ref[...], pattn_ref[...], preferred_element_type=F32)

    heads = []
    for h in range(MEM_HEADS):
        c0 = h * MEM_HEAD_DIM
        km = memkv_ref[:, c0:c0 + MEM_HEAD_DIM]
        vm = memkv_ref[:, MEM_WIDTH + c0:MEM_WIDTH + c0 + MEM_HEAD_DIM]
        s = lax.dot_general(qm_ref[:, c0:c0 + MEM_HEAD_DIM], km, (((1,), (1,)), ((), ())),
                            preferred_element_type=F32) * (MEM_HEAD_DIM ** -0.5)
        p = jnp.exp(s - jnp.max(s, axis=-1, keepdims=True))
        p = (p / jnp.sum(p, axis=-1, keepdims=True)).astype(BF16)
        heads.append(jnp.dot(p, vm, preferred_element_type=F32).astype(BF16))
    om = jnp.concatenate(heads, axis=1)
    merged = merged + gate_ref[:, 2 * D_MODEL:3 * D_MODEL].astype(F32) * jnp.dot(
        om, pmem_ref[...], preferred_element_type=F32)

    y = jnp.dot(merged.astype(BF16), wout_ref[...], preferred_element_type=F32)
    x1 = _layer_norm(ALPHA * xn_ref[...] + y, g1_ref[...], b1_ref[...])
    x1r_ref[...] = _rows_to_tiles(x1)

    lt = lax.dot_general(wr_ref[...], x1.astype(BF16), (((1,), (1,)), ((), ())),
                         preferred_element_type=F32)
    gl = [lt[r:r + 1, :] for r in range(N_EXPERT_GROUPS)]
    gmax = gl[0]
    grp = jnp.zeros((1, tm), I32)
    for r in range(1, N_EXPERT_GROUPS):
        better = gl[r] > gmax
        grp = jnp.where(better, r, grp)
        gmax = jnp.where(better, gl[r], gmax)
    gsum = gl[0] * 0.0
    for r in range(N_EXPERT_GROUPS):
        gsum = gsum + jnp.exp(gl[r] - gmax)
    gp = 1.0 / gsum
    sel = jnp.zeros((EXPERTS_PER_GROUP, tm), F32)
    for r in range(N_EXPERT_GROUPS):
        rows = lt[SUBLANES + r * EXPERTS_PER_GROUP:SUBLANES + (r + 1) * EXPERTS_PER_GROUP, :]
        sel = jnp.where(grp == r, rows, sel)
    ridx = lax.broadcasted_iota(I32, (EXPERTS_PER_GROUP, tm), 0)
    top1 = jnp.max(sel, axis=0, keepdims=True)
    i1 = jnp.min(jnp.where(sel == top1, ridx, EXPERTS_PER_GROUP), axis=0, keepdims=True)
    rest = jnp.where(ridx == i1, -jnp.inf, sel)
    top2 = jnp.max(rest, axis=0, keepdims=True)
    i2 = jnp.min(jnp.where(rest == top2, ridx, EXPERTS_PER_GROUP), axis=0, keepdims=True)
    e2 = jnp.exp(top2 - top1)
    inv = gp / (1.0 + e2)
    eid_ref[0:1, :] = grp * EXPERTS_PER_GROUP + i1
    eid_ref[1:2, :] = grp * EXPERTS_PER_GROUP + i2
    wts_ref[0:1, :] = inv
    wts_ref[1:2, :] = e2 * inv


def _mix_call(xn, u, o, qm, gate, memkv, p, batch, seq):
    tm = TM_MIX
    n_tiles = seq // tm
    t = batch * seq
    hb = tm // POOL_HALO

    row = lambda width: pl.BlockSpec((tm, width), lambda b, i: (b * n_tiles + i, 0))
    prev = pl.BlockSpec((POOL_HALO, POOL_WIDTH),
                        lambda b, i: (jnp.maximum((b * n_tiles + i) * hb - 1, 0), 0))
    nxt = pl.BlockSpec((POOL_HALO, POOL_WIDTH),
                       lambda b, i: (jnp.minimum((b * n_tiles + i + 1) * hb, t // POOL_HALO - 1), 0))
    full = lambda a: pl.BlockSpec(a.shape, lambda b, i: (0,) * a.ndim)
    lane_row = pl.BlockSpec((TOP_K, tm), lambda b, i: (0, b * n_tiles + i))
    weights = (p["w_pool"], p["pool_scale"], p["p_pool"], p["p_attn"], p["p_mem"], p["w_out"],
               p["ln1_g"], p["ln1_b"], p["w_router"])
    return pl.pallas_call(
        functools.partial(_mix_kernel, seq=seq),
        grid=(batch, n_tiles),
        in_specs=[row(D_MODEL), row(POOL_WIDTH), prev, nxt, row(ATTN_WIDTH), row(MEM_WIDTH),
                  row(N_BRANCHES * D_MODEL),
                  pl.BlockSpec((MEM_TOKENS, 2 * MEM_WIDTH), lambda b, i: (b, 0))]
                 + [full(w) for w in weights],
        out_specs=[pl.BlockSpec((tm, ROW_TILE, LANES), lambda b, i: (b * n_tiles + i, 0, 0)),
                   lane_row, lane_row],
        out_shape=[jax.ShapeDtypeStruct((t, ROW_TILE, LANES), F32),
                   jax.ShapeDtypeStruct((TOP_K, t), I32),
                   jax.ShapeDtypeStruct((TOP_K, t), F32)],
        scratch_shapes=[pltpu.VMEM((tm + 2 * POOL_HALO, POOL_WIDTH), F32)],
        compiler_params=_params(dimension_semantics=("arbitrary", "arbitrary")),
        name="mix",
    )(xn, u, u, u, o, qm, gate, memkv, *weights)


def _moe_kernel(tok_ref, dst_ref, iblk_ref, ie_ref, ilo_ref, ihi_ref,
                x1_hbm, ws_ref, wgu_ref, wd_ref, y2_hbm,
                xbuf, ybuf, gsem, ssem, *, n_blocks):
    i = pl.program_id(0)
    bm = xbuf.shape[1]
    lo = ilo_ref[i]
    hi = ihi_ref[i]
    k = iblk_ref[i]
    slot = k & 1
    nonempty = hi > lo

    def gather_start(blk, s):
        def body(r, c):
            pltpu.make_async_copy(x1_hbm.at[tok_ref[blk * bm + r]], xbuf.at[s, r], gsem.at[s]).start()
            return c
        lax.fori_loop(0, bm, body, 0, unroll=MOE_ROW_UNROLL)

    def gather_wait(s):
        pltpu.make_async_copy(x1_hbm.at[pl.ds(0, bm)], xbuf.at[s], gsem.at[s]).wait()

    def scatter_start(blk, s):
        def body(r, c):
            pltpu.make_async_copy(ybuf.at[s, r], y2_hbm.at[dst_ref[blk * bm + r]], ssem.at[s]).start()
            return c
        lax.fori_loop(0, bm, body, 0, unroll=MOE_ROW_UNROLL)

    def scatter_wait(s):
        pltpu.make_async_copy(ybuf.at[s], y2_hbm.at[pl.ds(0, bm)], ssem.at[s]).wait()

    @pl.when(jnp.logical_and(nonempty, lo == 0))
    def _():
        @pl.when(k == 0)
        def _():
            gather_start(0, 0)
        gather_wait(slot)

        @pl.when(k + 1 < n_blocks)
        def _():
            gather_start(k + 1, 1 - slot)

        @pl.when(k >= 2)
        def _():
            scatter_wait(slot)

    @pl.when(nonempty)
    def _():
        x = _tiles_to_rows(xbuf[slot])
        gu = jnp.dot(x.astype(BF16), wgu_ref[0], preferred_element_type=F32)
        gate = gu[:, :EXPERT_HIDDEN]
        hid = (gate * _sigmoid(gate)) * gu[:, EXPERT_HIDDEN:]
        y = jnp.dot(hid.astype(BF16), wd_ref[0], preferred_element_type=F32) * ws_ref[...]
        y = _rows_to_tiles(y)
        rows = lax.broadcasted_iota(I32, y.shape, 0)
        mine = jnp.logical_and(rows >= lo, rows < hi)

        @pl.when(lo == 0)
        def _():
            ybuf[slot] = jnp.where(mine, y, 0.0)

        @pl.when(lo > 0)
        def _():
            ybuf[slot] = jnp.where(mine, y, ybuf[slot])

    @pl.when(jnp.logical_and(nonempty, hi == bm))
    def _():
        scatter_start(k, slot)

        @pl.when(k == n_blocks - 1)
        def _():
            if n_blocks >= 2:
                scatter_wait(1 - slot)
            scatter_wait(slot)


def _moe_call(x1r, sorted_tok, sorted_dst, wsorted, item_blk, item_e, item_lo, item_hi, w_gu, w_down):
    a = sorted_dst.shape[0]
    bm = BM_MOE
    n_items = item_blk.shape[0]
    imap = lambda f: (lambda i, tok, dst, blk, e, lo, hi: f(i, blk, e))
    grid_spec = pltpu.PrefetchScalarGridSpec(
        num_scalar_prefetch=6,
        grid=(n_items,),
        in_specs=[pl.BlockSpec(memory_space=pl.ANY),
                  pl.BlockSpec((bm, 1), imap(lambda i, blk, e: (blk[i], 0))),
                  pl.BlockSpec((1, D_MODEL, 2 * EXPERT_HIDDEN), imap(lambda i, blk, e: (e[i], 0, 0))),
                  pl.BlockSpec((1, EXPERT_HIDDEN, D_MODEL), imap(lambda i, blk, e: (e[i], 0, 0)))],
        out_specs=pl.BlockSpec(memory_space=pl.ANY),
        scratch_shapes=[pltpu.VMEM((2, bm, ROW_TILE, LANES), F32),
                        pltpu.VMEM((2, bm, ROW_TILE, LANES), F32),
                        pltpu.SemaphoreType.DMA((2,)), pltpu.SemaphoreType.DMA((2,))],
    )
    return pl.pallas_call(
        functools.partial(_moe_kernel, n_blocks=a // bm),
        grid_spec=grid_spec,
        out_shape=jax.ShapeDtypeStruct((a, ROW_TILE, LANES), F32),
        compiler_params=_params(dimension_semantics=("arbitrary",)),
        name="moe",
    )(sorted_tok, sorted_dst, item_blk, item_e, item_lo, item_hi, x1r, wsorted, w_gu, w_down)


def _final_kernel(x1_ref, ya_ref, yb_ref, g_ref, b_ref, out_ref):
    moe = ya_ref[...] + yb_ref[...]
    out_ref[...] = _layer_norm(_tiles_to_rows(ALPHA * x1_ref[...] + moe), g_ref[...], b_ref[...])


def _final_call(x1, y2, ln_g, ln_b):
    t = x1.shape[0]
    tm = TM_FINAL
    n_tiles = t // tm
    return pl.pallas_call(
        _final_kernel,
        grid=(n_tiles,),
        in_specs=[pl.BlockSpec((tm, ROW_TILE, LANES), lambda i: (i, 0, 0)),
                  pl.BlockSpec((tm, ROW_TILE, LANES), lambda i: (i, 0, 0)),
                  pl.BlockSpec((tm, ROW_TILE, LANES), lambda i: (n_tiles + i, 0, 0)),
                  pl.BlockSpec((1, D_MODEL), lambda i: (0, 0)),
                  pl.BlockSpec((1, D_MODEL), lambda i: (0, 0))],
        out_specs=pl.BlockSpec((tm, D_MODEL), lambda i: (i, 0)),
        out_shape=jax.ShapeDtypeStruct((t, D_MODEL), F32),
        compiler_params=_params(dimension_semantics=("arbitrary",)),
        name="final",
    )(x1, y2, y2, ln_g, ln_b)


def _t5_bucket(rel):
    nb = N_REL_BUCKETS // 2
    max_exact = nb // 2
    ret = jnp.where(rel > 0, nb, 0)
    n = jnp.abs(rel)
    nf = jnp.maximum(n, 1).astype(F32)
    large = max_exact + (jnp.log(nf / max_exact) / math.log(REL_MAX_DISTANCE / max_exact)
                         * (nb - max_exact)).astype(I32)
    large = jnp.minimum(large, nb - 1)
    return ret + jnp.where(n < max_exact, n, large)


def _bias_tables(rel_table):
    n, m = BLOCK, 3 * BLOCK
    rel = jnp.arange(-(2 * BLOCK - 1), 2 * BLOCK)
    by_rel = jnp.where((jnp.abs(rel) <= WINDOW)[:, None], rel_table[_t5_bucket(rel)].astype(F32), NEG_INF)
    by_rel = jnp.pad(by_rel.T, ((0, 0), (0, 1)))
    skew = jnp.tile(by_rel, (1, n))[:, :n * (m + n - 1)].reshape(N_Q_HEADS, n, m + n - 1)
    bias = skew[:, :, n - 1:n - 1 + m]
    heads = [4 * h + half for h in range(N_KV_HEADS) for half in range(2)]
    units = jnp.concatenate([jnp.stack([bias[a] for a in heads]),
                             jnp.stack([bias[a + 2] for a in heads])], axis=1)
    masked = jnp.full((2 * N_KV_HEADS, 2 * BLOCK, BLOCK), NEG_INF, F32)
    prev = jnp.stack([units[:, :, :BLOCK], masked])
    cur = units[:, :, BLOCK:2 * BLOCK]
    nxt = jnp.stack([units[:, :, 2 * BLOCK:], masked])
    return prev, cur, nxt


def _prepare(rel_bias_table, w_in, b_in, w_pool, pool_scale, p_pool, sink, p_attn, w_mem_kv, p_mem,
             w_out, ln1_g, ln1_b, w_router_group, w_router_expert, w_gu, w_down):
    l = 0
    o1 = POOL_WIDTH
    o2 = o1 + ATTN_WIDTH
    o3 = o2 + KV_WIDTH
    o4 = o3 + KV_WIDTH
    o5 = o4 + MEM_WIDTH
    w_router = jnp.zeros((ROUTER_ROWS, D_MODEL), F32)
    w_router = w_router.at[:N_EXPERT_GROUPS].set(w_router_group[l].T)
    w_router = w_router.at[SUBLANES:].set(w_router_expert[l].T)
    bias_prev, bias_cur, bias_next = _bias_tables(rel_bias_table)
    return dict(
        w_in=w_in[l].astype(BF16), b_in=b_in[l][None, :],
        w_pool=w_pool[l].astype(BF16), pool_scale=pool_scale[l][None, :],
        p_pool=p_pool[l].astype(BF16), p_attn=p_attn[l].astype(BF16), p_mem=p_mem[l].astype(BF16),
        w_out=w_out[l].astype(BF16), w_memkv=w_mem_kv[l].astype(BF16),
        ln1_g=ln1_g[l][None, :], ln1_b=ln1_b[l][None, :],
        w_router=w_router.astype(BF16), sink=sink[l],
        w_gu=w_gu[l].astype(BF16), w_down=w_down[l].astype(BF16),
        bias_prev=bias_prev, bias_cur=bias_cur, bias_next=bias_next,
    )


def _dispatch_plan(eid, wts):
    t = eid.shape[1]
    a = TOP_K * t
    bm = BM_MOE
    nblk = a // bm
    eflat = eid.reshape(a)
    wflat = wts.reshape(a)
    order = jnp.argsort(eflat, stable=True).astype(I32)
    wsorted = wflat[order][:, None]
    experts = jnp.arange(N_EXPERTS, dtype=I32)
    counts = jnp.sum((eflat[None, :] == experts[:, None]).astype(I32), axis=1)
    ends = jnp.cumsum(counts).astype(I32)
    starts = ends - counts
    cuts = jnp.sort(jnp.concatenate([jnp.arange(nblk, dtype=I32) * bm, starts, jnp.array([a], I32)]))
    lo_abs = cuts[:-1]
    hi_abs = cuts[1:]
    nonempty = hi_abs > lo_abs
    blk = jnp.minimum(lo_abs // bm, nblk - 1)
    probe = jnp.minimum(lo_abs, a - 1)
    item_e = jnp.sum((ends[None, :] <= probe[:, None]).astype(I32), axis=1)
    item_lo = jnp.where(nonempty, lo_abs - blk * bm, 0).astype(I32)
    item_hi = jnp.where(nonempty, hi_abs - blk * bm, 0).astype(I32)
    return order, wsorted, blk.astype(I32), item_e.astype(I32), item_lo, item_hi


def _trunk(x, mem, ln_in_g, ln_in_b, ln2_g, ln2_b, p):
    batch, seq, d = x.shape
    t = batch * seq
    assert seq % TM_MIX == 0 and seq % TQ_ATTN == 0 and t % TM_INPROJ == 0 and (TOP_K * t) % BM_MOE == 0
    memkv = _memkv_call(mem.reshape(batch * MEM_TOKENS, d), p["w_memkv"])
    xn, u, q, kk, vv, qm, gate = _inproj_call(x.reshape(t, d), ln_in_g[None, :], ln_in_b[None, :],
                                              p["w_in"], p["b_in"])
    o = _attn_call(q, kk, vv, p["sink"], p["bias_prev"], p["bias_cur"], p["bias_next"], batch, seq)
    x1r, eid, wts = _mix_call(xn, u, o, qm, gate, memkv, p, batch, seq)
    order, wsorted, item_blk, item_e, item_lo, item_hi = _dispatch_plan(eid, wts)
    sorted_tok = jnp.where(order >= t, order - t, order)
    y2 = _moe_call(x1r, sorted_tok, order, wsorted, item_blk, item_e, item_lo, item_hi, p["w_gu"], p["w_down"])
    out = _final_call(x1r, y2, ln2_g[None, :], ln2_b[None, :])
    return out.reshape(batch, seq, d)


def kernel(x_prompt, x_sample, mem_prompt, mem_sample, ln_in_g, ln_in_b, rel_bias_table, w_in, b_in,
           w_pool, pool_scale, p_pool, sink, p_attn, w_mem_kv, p_mem, w_out, ln1_g, ln1_b,
           w_router_group, w_router_expert, w_gu, w_down, ln2_g, ln2_b):
    p = _prepare(rel_bias_table, w_in, b_in, w_pool, pool_scale, p_pool, sink, p_attn, w_mem_kv, p_mem,
                 w_out, ln1_g, ln1_b, w_router_group, w_router_expert, w_gu, w_down)
    y_prompt = _trunk(x_prompt, mem_prompt, ln_in_g, ln_in_b, ln2_g[0], ln2_b[0], p)
    y_sample = _trunk(x_sample, mem_sample, ln_in_g, ln_in_b, ln2_g[0], ln2_b[0], p)
    return (y_prompt, y_sample)
```

```python
import functools
import math

import jax
import jax.numpy as jnp
from jax import lax
from jax.experimental import pallas as pl
from jax.experimental.pallas import tpu as pltpu

F32 = jnp.float32
BF16 = jnp.bfloat16
I32 = jnp.int32

D_MODEL = 1024
DEPTH = 1
POOL_WIDTH = 512
POOL_WINDOWS = (2, 4, 8, 16)
N_POOL_GROUPS = 4
POOL_GROUP_CH = POOL_WIDTH // N_POOL_GROUPS
N_Q_HEADS = 16
N_KV_HEADS = 4
HEAD_DIM = 64
ATTN_WIDTH = N_Q_HEADS * HEAD_DIM
KV_WIDTH = N_KV_HEADS * HEAD_DIM
WINDOW = 128
BLOCK = 128
N_REL_BUCKETS = 32
REL_MAX_DISTANCE = 128
MEM_TOKENS = 256
MEM_HEADS = 4
MEM_HEAD_DIM = 128
MEM_WIDTH = MEM_HEADS * MEM_HEAD_DIM
N_BRANCHES = 3
N_EXPERT_GROUPS = 4
EXPERTS_PER_GROUP = 8
N_EXPERTS = N_EXPERT_GROUPS * EXPERTS_PER_GROUP
TOP_K = 2
EXPERT_HIDDEN = 512
ALPHA = (2 * DEPTH) ** 0.25
LN_EPS = 1e-5
NEG_INF = -1e30

LANES = 128
SUBLANES = 8
ROW_TILE = D_MODEL // LANES
VMEM_LIMIT_BYTES = 56 * 1024 * 1024

KV_DUP_WIDTH = N_KV_HEADS * LANES

C_U = 0
C_Q = C_U + POOL_WIDTH
C_K = C_Q + ATTN_WIDTH
C_V = C_K + KV_WIDTH
C_M = C_V + KV_WIDTH
C_G = C_M + MEM_WIDTH
C_END = C_G + N_BRANCHES * D_MODEL

TM_INPROJ = 512
TQ_ATTN = 256
TM_MIX = 512
BM_MOE = 256
MOE_ROW_UNROLL = 8
TM_FINAL = 512
POOL_HALO = 8
ROUTER_ROWS = 40


def _layer_norm(x, g, b):
    mu = jnp.mean(x, axis=-1, keepdims=True)
    xc = x - mu
    var = jnp.mean(xc * xc, axis=-1, keepdims=True)
    return xc * lax.rsqrt(var + LN_EPS) * g + b


def _sigmoid(x):
    return 1.0 / (1.0 + jnp.exp(-x))


def _rows_to_tiles(x):
    return pltpu.einshape("r(cl)->rcl", x, c=ROW_TILE)


def _tiles_to_rows(x):
    return pltpu.einshape("rcl->r(cl)", x)


def _params(**kw):
    return pltpu.CompilerParams(vmem_limit_bytes=VMEM_LIMIT_BYTES, **kw)


def _memkv_kernel(mem_ref, w_ref, out_ref):
    out_ref[...] = jnp.dot(mem_ref[...].astype(BF16), w_ref[...],
                           preferred_element_type=F32).astype(BF16)


def _memkv_call(mem2, w_memkv):
    rows = mem2.shape[0]
    return pl.pallas_call(
        _memkv_kernel,
        grid=(rows // MEM_TOKENS,),
        in_specs=[pl.BlockSpec((MEM_TOKENS, D_MODEL), lambda i: (i, 0)),
                  pl.BlockSpec((D_MODEL, 2 * MEM_WIDTH), lambda i: (0, 0))],
        out_specs=pl.BlockSpec((MEM_TOKENS, 2 * MEM_WIDTH), lambda i: (i, 0)),
        out_shape=jax.ShapeDtypeStruct((rows, 2 * MEM_WIDTH), BF16),
        compiler_params=_params(dimension_semantics=("arbitrary",)),
        name="memkv",
    )(mem2, w_memkv)


def _inproj_kernel(x_ref, g_ref, b_ref, w_ref, bias_ref,
                   xn_ref, u_ref, q_ref, k_ref, v_ref, qm_ref, gate_ref):
    xn = _layer_norm(x_ref[...], g_ref[...], b_ref[...])
    xn_ref[...] = xn
    xb = xn.astype(BF16)

    def seg(lo, hi):
        return jnp.dot(xb, w_ref[:, lo:hi], preferred_element_type=F32) + bias_ref[:, lo:hi]

    u_ref[...] = seg(C_U, C_Q)
    q_ref[...] = (seg(C_Q, C_K) * (HEAD_DIM ** -0.5)).astype(BF16)
    kv = seg(C_K, C_M)
    low = lax.broadcasted_iota(I32, (kv.shape[0], LANES), 1) < HEAD_DIM
    for out_ref, c_lo in ((k_ref, 0), (v_ref, KV_WIDTH)):
        for c in range(KV_WIDTH // LANES):
            blk = kv[:, c_lo + c * LANES:c_lo + (c + 1) * LANES]
            rot = pltpu.roll(blk, HEAD_DIM, 1)
            out_ref[:, 2 * c * LANES:(2 * c + 1) * LANES] = jnp.where(low, blk, rot).astype(BF16)
            out_ref[:, (2 * c + 1) * LANES:(2 * c + 2) * LANES] = jnp.where(low, rot, blk).astype(BF16)
    qm_ref[...] = seg(C_M, C_G).astype(BF16)
    for j in range(N_BRANCHES):
        lo = C_G + j * D_MODEL
        gate_ref[:, j * D_MODEL:(j + 1) * D_MODEL] = _sigmoid(seg(lo, lo + D_MODEL)).astype(BF16)


def _inproj_call(x2, ln_g, ln_b, w_in, b_in):
    t = x2.shape[0]
    tm = TM_INPROJ
    row = lambda width: pl.BlockSpec((tm, width), lambda i: (i, 0))
    const = lambda shape: pl.BlockSpec(shape, lambda i: (0, 0))
    widths = (D_MODEL, POOL_WIDTH, ATTN_WIDTH, KV_DUP_WIDTH, KV_DUP_WIDTH, MEM_WIDTH, N_BRANCHES * D_MODEL)
    dtypes = (F32, F32, BF16, BF16, BF16, BF16, BF16)
    return pl.pallas_call(
        _inproj_kernel,
        grid=(t // tm,),
        in_specs=[row(D_MODEL), const((1, D_MODEL)), const((1, D_MODEL)),
                  const((D_MODEL, C_END)), const((1, C_END))],
        out_specs=[row(w) for w in widths],
        out_shape=[jax.ShapeDtypeStruct((t, w), d) for w, d in zip(widths, dtypes)],
        compiler_params=_params(dimension_semantics=("arbitrary",)),
        name="inproj",
    )(x2, ln_g, ln_b, w_in, b_in)


def _attn_kernel(sink_ref, q_ref, kc_ref, kp_ref, kn_ref, vc_ref, vp_ref, vn_ref,
                 bp_ref, bc_ref, bn_ref, o_ref, klo, khi, vlo, vhi, s_scr, p_scr):
    i = pl.program_id(1)
    n_tiles = pl.num_programs(1)
    tq = q_ref.shape[0]
    n_qb = tq // BLOCK

    lane = lax.broadcasted_iota(I32, (BLOCK, KV_DUP_WIDTH), 1)
    low = (lane & (LANES - 1)) < HEAD_DIM

    def put(dst_lo, dst_hi, r0, val):
        zero = jnp.zeros_like(val)
        dst_lo[r0:r0 + BLOCK, :] = jnp.where(low, val, zero)
        dst_hi[r0:r0 + BLOCK, :] = jnp.where(low, zero, val)

    put(klo, khi, 0, kp_ref[...])
    put(vlo, vhi, 0, vp_ref[...])
    for j in range(n_qb):
        put(klo, khi, (j + 1) * BLOCK, kc_ref[j * BLOCK:(j + 1) * BLOCK, :])
        put(vlo, vhi, (j + 1) * BLOCK, vc_ref[j * BLOCK:(j + 1) * BLOCK, :])
    put(klo, khi, (n_qb + 1) * BLOCK, kn_ref[...])
    put(vlo, vhi, (n_qb + 1) * BLOCK, vn_ref[...])

    first = (i == 0).astype(I32)
    last = (i == n_tiles - 1).astype(I32)

    def softmax_rows(s, sink):
        m = jnp.maximum(jnp.max(s, axis=-1, keepdims=True), sink)
        p = jnp.exp(s - m)
        denom = jnp.sum(p, axis=-1, keepdims=True) + jnp.exp(sink - m)
        return (p * (1.0 / denom)).astype(BF16)

    units = [(j, h, half) for j in range(n_qb) for h in range(N_KV_HEADS) for half in range(2)]
    always = i >= 0

    @pl.when(always)
    def _scores():
        for u, (j, h, half) in enumerate(units):
            r0 = j * BLOCK
            c0 = h * 2 * LANES
            pv = first if j == 0 else 0
            nv = last if j == n_qb - 1 else 0
            unit = h * 2 + half
            q_pairs = jnp.concatenate([q_ref[r0:r0 + BLOCK, c0:c0 + LANES],
                                       q_ref[r0:r0 + BLOCK, c0 + LANES:c0 + 2 * LANES]], axis=0)
            kx = (klo, khi)[half][r0:r0 + 3 * BLOCK, h * LANES:(h + 1) * LANES]
            s = lax.dot_general(q_pairs, kx, (((1,), (1,)), ((), ())), preferred_element_type=F32)
            bias = jnp.concatenate([bp_ref[pv, unit], bc_ref[unit], bn_ref[nv, unit]], axis=1)
            s_scr[u] = s + bias

    @pl.when(always)
    def _softmax():
        for u, (j, h, half) in enumerate(units):
            p_scr[u, :BLOCK] = softmax_rows(s_scr[u, :BLOCK], sink_ref[4 * h + half])
            p_scr[u, BLOCK:] = softmax_rows(s_scr[u, BLOCK:], sink_ref[4 * h + 2 + half])

    @pl.when(always)
    def _values():
        for j in range(n_qb):
            r0 = j * BLOCK
            for h in range(N_KV_HEADS):
                c0 = h * 2 * LANES
                acc = jnp.zeros((2 * BLOCK, LANES), F32)
                for half, vref in enumerate((vlo, vhi)):
                    u = units.index((j, h, half))
                    vx = vref[r0:r0 + 3 * BLOCK, h * LANES:(h + 1) * LANES]
                    acc = acc + jnp.dot(p_scr[u], vx, preferred_element_type=F32)
                o_ref[r0:r0 + BLOCK, c0:c0 + LANES] = acc[:BLOCK].astype(BF16)
                o_ref[r0:r0 + BLOCK, c0 + LANES:c0 + 2 * LANES] = acc[BLOCK:].astype(BF16)


def _attn_call(q, kk, vv, sink, bias_prev, bias_cur, bias_next, batch, seq):
    tq = TQ_ATTN
    n_qb = tq // BLOCK
    nb = seq // BLOCK
    n_tiles = seq // tq
    t = batch * seq

    cur = lambda width: pl.BlockSpec((tq, width), lambda b, i: (b * n_tiles + i, 0))
    prev = pl.BlockSpec((BLOCK, KV_DUP_WIDTH), lambda b, i: (b * nb + jnp.maximum(i * n_qb - 1, 0), 0))
    nxt = pl.BlockSpec((BLOCK, KV_DUP_WIDTH), lambda b, i: (b * nb + jnp.minimum((i + 1) * n_qb, nb - 1), 0))
    full = lambda a: pl.BlockSpec(a.shape, lambda b, i: (0,) * a.ndim)
    ext = (n_qb + 2) * BLOCK
    n_units = n_qb * N_KV_HEADS * 2
    return pl.pallas_call(
        _attn_kernel,
        grid=(batch, n_tiles),
        in_specs=[pl.BlockSpec(memory_space=pltpu.SMEM),
                  cur(ATTN_WIDTH), cur(KV_DUP_WIDTH), prev, nxt, cur(KV_DUP_WIDTH), prev, nxt,
                  full(bias_prev), full(bias_cur), full(bias_next)],
        out_specs=cur(ATTN_WIDTH),
        out_shape=jax.ShapeDtypeStruct((t, ATTN_WIDTH), BF16),
        scratch_shapes=[pltpu.VMEM((ext, KV_DUP_WIDTH), BF16) for _ in range(4)]
                       + [pltpu.VMEM((n_units, 2 * BLOCK, 3 * BLOCK), F32),
                          pltpu.VMEM((n_units, 2 * BLOCK, 3 * BLOCK), BF16)],
        compiler_params=_params(dimension_semantics=("arbitrary", "arbitrary")),
        name="attn",
    )(sink, q, kk, kk, kk, vv, vv, vv, bias_prev, bias_cur, bias_next)


def _mix_kernel(xn_ref, u_ref, up_ref, un_ref, o_ref, qm_ref, gate_ref, memkv_ref,
                wpool_ref, pscale_ref, ppool_ref, pattn_ref, pmem_ref, wout_ref,
                g1_ref, b1_ref, wr_ref,
                x1r_ref, eid_ref, wts_ref, uext, *, seq):
    i = pl.program_id(1)
    n_tiles = pl.num_programs(1)
    tm = xn_ref.shape[0]

    halo = POOL_HALO
    uext[0:halo, :] = jnp.where(i > 0, up_ref[...], 0.0)
    uext[halo:halo + tm, :] = u_ref[...]
    uext[halo + tm:halo + tm + halo, :] = jnp.where(i < n_tiles - 1, un_ref[...], 0.0)
    pos = i * tm + lax.broadcasted_iota(I32, (tm, 1), 0)
    mixed = []
    for gi, win in enumerate(POOL_WINDOWS):
        c0 = gi * POOL_GROUP_CH
        half = win // 2
        total = jnp.zeros((tm, POOL_GROUP_CH), F32)
        for off in range(-half, half):
            total = total + uext[halo + off:halo + off + tm, c0:c0 + POOL_GROUP_CH]
        cnt = (jnp.minimum(pos + half, seq) - jnp.maximum(pos - half, 0)).astype(F32)
        pooled = total * (1.0 / cnt) - u_ref[:, c0:c0 + POOL_GROUP_CH]
        mixed.append(jnp.dot(pooled.astype(BF16), wpool_ref[gi], preferred_element_type=F32)
                     * pscale_ref[:, c0:c0 + POOL_GROUP_CH])
    mixed = jnp.concatenate(mixed, axis=1).astype(BF16)
    merged = gate_ref[:, 0:D_MODEL].astype(F32) * jnp.dot(
        mixed, ppool_ref[...], preferred_element_type=F32)

    merged = merged + gate_ref[:, D_MODEL:2 * D_MODEL].astype(F32) * jnp.dot(
        o_ref[...], pattn_ref[...], preferred_element_type=F32)

    heads = []
    for h in range(MEM_HEADS):
        c0 = h * MEM_HEAD_DIM
        km = memkv_ref[:, c0:c0 + MEM_HEAD_DIM]
        vm = memkv_ref[:, MEM_WIDTH + c0:MEM_WIDTH + c0 + MEM_HEAD_DIM]
        s = lax.dot_general(qm_ref[:, c0:c0 + MEM_HEAD_DIM], km, (((1,), (1,)), ((), ())),
                            preferred_element_type=F32) * (MEM_HEAD_DIM ** -0.5)
        p = jnp.exp(s - jnp.max(s, axis=-1, keepdims=True))
        p = (p / jnp.sum(p, axis=-1, keepdims=True)).astype(BF16)
        heads.append(jnp.dot(p, vm, preferred_element_type=F32).astype(BF16))
    om = jnp.concatenate(heads, axis=1)
    merged = merged + gate_ref[:, 2 * D_MODEL:3 * D_MODEL].astype(F32) * jnp.dot(
        om, pmem_ref[...], preferred_element_type=F32)

    y = jnp.dot(merged.astype(BF16), wout_ref[...], preferred_element_type=F32)
    x1 = _layer_norm(ALPHA * xn_ref[...] + y, g1_ref[...], b1_ref[...])
    x1r_ref[...] = _rows_to_tiles(x1)

    lt = lax.dot_general(wr_ref[...], x1.astype(BF16), (((1,), (1,)), ((), ())),
                         preferred_element_type=F32)
    gl = [lt[r:r + 1, :] for r in range(N_EXPERT_GROUPS)]
    gmax = gl[0]
    grp = jnp.zeros((1, tm), I32)
    for r in range(1, N_EXPERT_GROUPS):
        better = gl[r] > gmax
        grp = jnp.where(better, r, grp)
        gmax = jnp.where(better, gl[r], gmax)
    gsum = gl[0] * 0.0
    for r in range(N_EXPERT_GROUPS):
        gsum = gsum + jnp.exp(gl[r] - gmax)
    gp = 1.0 / gsum
    sel = jnp.zeros((EXPERTS_PER_GROUP, tm), F32)
    for r in range(N_EXPERT_GROUPS):
        rows = lt[SUBLANES + r * EXPERTS_PER_GROUP:SUBLANES + (r + 1) * EXPERTS_PER_GROUP, :]
        sel = jnp.where(grp == r, rows, sel)
    ridx = lax.broadcasted_iota(I32, (EXPERTS_PER_GROUP, tm), 0)
    top1 = jnp.max(sel, axis=0, keepdims=True)
    i1 = jnp.min(jnp.where(sel == top1, ridx, EXPERTS_PER_GROUP), axis=0, keepdims=True)
    rest = jnp.where(ridx == i1, -jnp.inf, sel)
    top2 = jnp.max(rest, axis=0, keepdims=True)
    i2 = jnp.min(jnp.where(rest == top2, ridx, EXPERTS_PER_GROUP), axis=0, keepdims=True)
    e2 = jnp.exp(top2 - top1)
    inv = gp / (1.0 + e2)
    eid_ref[0:1, :] = grp * EXPERTS_PER_GROUP + i1
    eid_ref[1:2, :] = grp * EXPERTS_PER_GROUP + i2
    wts_ref[0:1, :] = inv
    wts_ref[1:2, :] = e2 * inv


def _mix_call(xn, u, o, qm, gate, memkv, p, batch, seq):
    tm = TM_MIX
    n_tiles = seq // tm
    t = batch * seq
    hb = tm // POOL_HALO

    row = lambda width: pl.BlockSpec((tm, width), lambda b, i: (b * n_tiles + i, 0))
    prev = pl.BlockSpec((POOL_HALO, POOL_WIDTH),
                        lambda b, i: (jnp.maximum((b * n_tiles + i) * hb - 1, 0), 0))
    nxt = pl.BlockSpec((POOL_HALO, POOL_WIDTH),
                       lambda b, i: (jnp.minimum((b * n_tiles + i + 1) * hb, t // POOL_HALO - 1), 0))
    full = lambda a: pl.BlockSpec(a.shape, lambda b, i: (0,) * a.ndim)
    lane_row = pl.BlockSpec((TOP_K, tm), lambda b, i: (0, b * n_tiles + i))
    weights = (p["w_pool"], p["pool_scale"], p["p_pool"], p["p_attn"], p["p_mem"], p["w_out"],
               p["ln1_g"], p["ln1_b"], p["w_router"])
    return pl.pallas_call(
        functools.partial(_mix_kernel, seq=seq),
        grid=(batch, n_tiles),
        in_specs=[row(D_MODEL), row(POOL_WIDTH), prev, nxt, row(ATTN_WIDTH), row(MEM_WIDTH),
                  row(N_BRANCHES * D_MODEL),
                  pl.BlockSpec((MEM_TOKENS, 2 * MEM_WIDTH), lambda b, i: (b, 0))]
                 + [full(w) for w in weights],
        out_specs=[pl.BlockSpec((tm, ROW_TILE, LANES), lambda b, i: (b * n_tiles + i, 0, 0)),
                   lane_row, lane_row],
        out_shape=[jax.ShapeDtypeStruct((t, ROW_TILE, LANES), F32),
                   jax.ShapeDtypeStruct((TOP_K, t), I32),
                   jax.ShapeDtypeStruct((TOP_K, t), F32)],
        scratch_shapes=[pltpu.VMEM((tm + 2 * POOL_HALO, POOL_WIDTH), F32)],
        compiler_params=_params(dimension_semantics=("arbitrary", "arbitrary")),
        name="mix",
    )(xn, u, u, u, o, qm, gate, memkv, *weights)


def _moe_kernel(tok_ref, dst_ref, iblk_ref, ie_ref, ilo_ref, ihi_ref,
                x1_hbm, ws_ref, wgu_ref, wd_ref, y2_hbm,
                xbuf, ybuf, x_bf, h_bf, wgu_bf, wd_bf, gsem, ssem, *, n_blocks):
    i = pl.program_id(0)
    bm = xbuf.shape[1]
    lo = ilo_ref[i]
    hi = ihi_ref[i]
    k = iblk_ref[i]
    slot = k & 1
    nonempty = hi > lo

    def gather_start(blk, s):
        def body(r, c):
            pltpu.make_async_copy(x1_hbm.at[tok_ref[blk * bm + r]], xbuf.at[s, r], gsem.at[s]).start()
            return c
        lax.fori_loop(0, bm, body, 0, unroll=MOE_ROW_UNROLL)

    def gather_wait(s):
        pltpu.make_async_copy(x1_hbm.at[pl.ds(0, bm)], xbuf.at[s], gsem.at[s]).wait()

    def scatter_start(blk, s):
        def body(r, c):
            pltpu.make_async_copy(ybuf.at[s, r], y2_hbm.at[dst_ref[blk * bm + r]], ssem.at[s]).start()
            return c
        lax.fori_loop(0, bm, body, 0, unroll=MOE_ROW_UNROLL)

    def scatter_wait(s):
        pltpu.make_async_copy(ybuf.at[s], y2_hbm.at[pl.ds(0, bm)], ssem.at[s]).wait()

    @pl.when(jnp.logical_and(nonempty, lo == 0))
    def _():
        @pl.when(k == 0)
        def _():
            gather_start(0, 0)
        gather_wait(slot)

        @pl.when(k + 1 < n_blocks)
        def _():
            gather_start(k + 1, 1 - slot)

        @pl.when(k >= 2)
        def _():
            scatter_wait(slot)

    @pl.when(jnp.logical_or(i == 0, ie_ref[i] != ie_ref[jnp.maximum(i - 1, 0)]))
    def _():
        wgu_bf[...] = wgu_ref[0].astype(BF16)
        wd_bf[...] = wd_ref[0].astype(BF16)

    @pl.when(nonempty)
    def _():
        x_bf[...] = _tiles_to_rows(xbuf[slot]).astype(BF16)

    @pl.when(nonempty)
    def _():
        gu = jnp.dot(x_bf[...], wgu_bf[...], preferred_element_type=F32)
        gate = gu[:, :EXPERT_HIDDEN]
        h_bf[...] = ((gate * _sigmoid(gate)) * gu[:, EXPERT_HIDDEN:]).astype(BF16)

    @pl.when(nonempty)
    def _():
        rows = lax.broadcasted_iota(I32, (bm, 1), 0)
        mine = jnp.logical_and(rows >= lo, rows < hi)
        y = jnp.dot(h_bf[...], wd_bf[...], preferred_element_type=F32) * ws_ref[...]
        y = _rows_to_tiles(jnp.where(mine, y, 0.0))

        @pl.when(lo == 0)
        def _():
            ybuf[slot] = y

        @pl.when(lo > 0)
        def _():
            ybuf[slot] = ybuf[slot] + y

    @pl.when(jnp.logical_and(nonempty, hi == bm))
    def _():
        scatter_start(k, slot)

        @pl.when(k == n_blocks - 1)
        def _():
            if n_blocks >= 2:
                scatter_wait(1 - slot)
            scatter_wait(slot)


def _moe_call(x1r, sorted_tok, sorted_dst, wsorted, item_blk, item_e, item_lo, item_hi, w_gu, w_down):
    a = sorted_dst.shape[0]
    bm = BM_MOE
    n_items = item_blk.shape[0]
    imap = lambda f: (lambda i, tok, dst, blk, e, lo, hi: f(i, blk, e))
    grid_spec = pltpu.PrefetchScalarGridSpec(
        num_scalar_prefetch=6,
        grid=(n_items,),
        in_specs=[pl.BlockSpec(memory_space=pl.ANY),
                  pl.BlockSpec((bm, 1), imap(lambda i, blk, e: (blk[i], 0))),
                  pl.BlockSpec((1, D_MODEL, 2 * EXPERT_HIDDEN), imap(lambda i, blk, e: (e[i], 0, 0))),
                  pl.BlockSpec((1, EXPERT_HIDDEN, D_MODEL), imap(lambda i, blk, e: (e[i], 0, 0)))],
        out_specs=pl.BlockSpec(memory_space=pl.ANY),
        scratch_shapes=[pltpu.VMEM((2, bm, ROW_TILE, LANES), F32),
                        pltpu.VMEM((2, bm, ROW_TILE, LANES), F32),
                        pltpu.VMEM((bm, D_MODEL), BF16),
                        pltpu.VMEM((bm, EXPERT_HIDDEN), BF16),
                        pltpu.VMEM((D_MODEL, 2 * EXPERT_HIDDEN), BF16),
                        pltpu.VMEM((EXPERT_HIDDEN, D_MODEL), BF16),
                        pltpu.SemaphoreType.DMA((2,)), pltpu.SemaphoreType.DMA((2,))],
    )
    return pl.pallas_call(
        functools.partial(_moe_kernel, n_blocks=a // bm),
        grid_spec=grid_spec,
        out_shape=jax.ShapeDtypeStruct((a, ROW_TILE, LANES), F32),
        compiler_params=_params(dimension_semantics=("arbitrary",)),
        name="moe",
    )(sorted_tok, sorted_dst, item_blk, item_e, item_lo, item_hi, x1r, wsorted, w_gu, w_down)


def _final_kernel(x1_ref, ya_ref, yb_ref, g_ref, b_ref, out_ref):
    moe = ya_ref[...] + yb_ref[...]
    out_ref[...] = _layer_norm(_tiles_to_rows(ALPHA * x1_ref[...] + moe), g_ref[...], b_ref[...])


def _final_call(x1, y2, ln_g, ln_b):
    t = x1.shape[0]
    tm = TM_FINAL
    n_tiles = t // tm
    return pl.pallas_call(
        _final_kernel,
        grid=(n_tiles,),
        in_specs=[pl.BlockSpec((tm, ROW_TILE, LANES), lambda i: (i, 0, 0)),
                  pl.BlockSpec((tm, ROW_TILE, LANES), lambda i: (i, 0, 0)),
                  pl.BlockSpec((tm, ROW_TILE, LANES), lambda i: (n_tiles + i, 0, 0)),
                  pl.BlockSpec((1, D_MODEL), lambda i: (0, 0)),
                  pl.BlockSpec((1, D_MODEL), lambda i: (0, 0))],
        out_specs=pl.BlockSpec((tm, D_MODEL), lambda i: (i, 0)),
        out_shape=jax.ShapeDtypeStruct((t, D_MODEL), F32),
        compiler_params=_params(dimension_semantics=("arbitrary",)),
        name="final",
    )(x1, y2, y2, ln_g, ln_b)


def _t5_bucket(rel):
    nb = N_REL_BUCKETS // 2
    max_exact = nb // 2
    ret = jnp.where(rel > 0, nb, 0)
    n = jnp.abs(rel)
    nf = jnp.maximum(n, 1).astype(F32)
    large = max_exact + (jnp.log(nf / max_exact) / math.log(REL_MAX_DISTANCE / max_exact)
                         * (nb - max_exact)).astype(I32)
    large = jnp.minimum(large, nb - 1)
    return ret + jnp.where(n < max_exact, n, large)


def _bias_tables(rel_table):
    n, m = BLOCK, 3 * BLOCK
    rel = jnp.arange(-(2 * BLOCK - 1), 2 * BLOCK)
    by_rel = jnp.where((jnp.abs(rel) <= WINDOW)[:, None], rel_table[_t5_bucket(rel)].astype(F32), NEG_INF)
    by_rel = jnp.pad(by_rel.T, ((0, 0), (0, 1)))
    skew = jnp.tile(by_rel, (1, n))[:, :n * (m + n - 1)].reshape(N_Q_HEADS, n, m + n - 1)
    bias = skew[:, :, n - 1:n - 1 + m]
    heads = [4 * h + half for h in range(N_KV_HEADS) for half in range(2)]
    units = jnp.concatenate([jnp.stack([bias[a] for a in heads]),
                             jnp.stack([bias[a + 2] for a in heads])], axis=1)
    masked = jnp.full((2 * N_KV_HEADS, 2 * BLOCK, BLOCK), NEG_INF, F32)
    prev = jnp.stack([units[:, :, :BLOCK], masked])
    cur = units[:, :, BLOCK:2 * BLOCK]
    nxt = jnp.stack([units[:, :, 2 * BLOCK:], masked])
    return prev, cur, nxt


def _prepare(rel_bias_table, w_in, b_in, w_pool, pool_scale, p_pool, sink, p_attn, w_mem_kv, p_mem,
             w_out, ln1_g, ln1_b, w_router_group, w_router_expert, w_gu, w_down):
    l = 0
    o1 = POOL_WIDTH
    o2 = o1 + ATTN_WIDTH
    o3 = o2 + KV_WIDTH
    o4 = o3 + KV_WIDTH
    o5 = o4 + MEM_WIDTH
    w_router = jnp.zeros((ROUTER_ROWS, D_MODEL), F32)
    w_router = w_router.at[:N_EXPERT_GROUPS].set(w_router_group[l].T)
    w_router = w_router.at[SUBLANES:].set(w_router_expert[l].T)
    bias_prev, bias_cur, bias_next = _bias_tables(rel_bias_table)
    return dict(
        w_in=w_in[l].astype(BF16), b_in=b_in[l][None, :],
        w_pool=w_pool[l].astype(BF16), pool_scale=pool_scale[l][None, :],
        p_pool=p_pool[l].astype(BF16), p_attn=p_attn[l].astype(BF16), p_mem=p_mem[l].astype(BF16),
        w_out=w_out[l].astype(BF16), w_memkv=w_mem_kv[l].astype(BF16),
        ln1_g=ln1_g[l][None, :], ln1_b=ln1_b[l][None, :],
        w_router=w_router.astype(BF16), sink=sink[l],
        w_gu=w_gu[l], w_down=w_down[l],
        bias_prev=bias_prev, bias_cur=bias_cur, bias_next=bias_next,
    )


def _dispatch_plan(eid, wts):
    t = eid.shape[1]
    a = TOP_K * t
    bm = BM_MOE
    nblk = a // bm
    eflat = eid.reshape(a)
    wflat = wts.reshape(a)
    order = jnp.argsort(eflat, stable=True).astype(I32)
    wsorted = wflat[order][:, None]
    experts = jnp.arange(N_EXPERTS, dtype=I32)
    counts = jnp.sum((eflat[None, :] == experts[:, None]).astype(I32), axis=1)
    ends = jnp.cumsum(counts).astype(I32)
    starts = ends - counts
    cuts = jnp.sort(jnp.concatenate([jnp.arange(nblk, dtype=I32) * bm, starts, jnp.array([a], I32)]))
    lo_abs = cuts[:-1]
    hi_abs = cuts[1:]
    nonempty = hi_abs > lo_abs
    blk = jnp.minimum(lo_abs // bm, nblk - 1)
    probe = jnp.minimum(lo_abs, a - 1)
    item_e = jnp.sum((ends[None, :] <= probe[:, None]).astype(I32), axis=1)
    item_lo = jnp.where(nonempty, lo_abs - blk * bm, 0).astype(I32)
    item_hi = jnp.where(nonempty, hi_abs - blk * bm, 0).astype(I32)
    return order, wsorted, blk.astype(I32), item_e.astype(I32), item_lo, item_hi


def _trunk(x, mem, ln_in_g, ln_in_b, ln2_g, ln2_b, p):
    batch, seq, d = x.shape
    t = batch * seq
    assert seq % TM_MIX == 0 and seq % TQ_ATTN == 0 and t % TM_INPROJ == 0 and (TOP_K * t) % BM_MOE == 0
    memkv = _memkv_call(mem.reshape(batch * MEM_TOKENS, d), p["w_memkv"])
    xn, u, q, kk, vv, qm, gate = _inproj_call(x.reshape(t, d), ln_in_g[None, :], ln_in_b[None, :],
                                              p["w_in"], p["b_in"])
    o = _attn_call(q, kk, vv, p["sink"], p["bias_prev"], p["bias_cur"], p["bias_next"], batch, seq)
    x1r, eid, wts = _mix_call(xn, u, o, qm, gate, memkv, p, batch, seq)
    order, wsorted, item_blk, item_e, item_lo, item_hi = _dispatch_plan(eid, wts)
    sorted_tok = jnp.where(order >= t, order - t, order)
    y2 = _moe_call(x1r, sorted_tok, order, wsorted, item_blk, item_e, item_lo, item_hi, p["w_gu"], p["w_down"])
    out = _final_call(x1r, y2, ln2_g[None, :], ln2_b[None, :])
    return out.reshape(batch, seq, d)


def kernel(x_prompt, x_sample, mem_prompt, mem_sample, ln_in_g, ln_in_b, rel_bias_table, w_in, b_in,
           w_pool, pool_scale, p_pool, sink, p_attn, w_mem_kv, p_mem, w_out, ln1_g, ln1_b,
           w_router_group, w_router_expert, w_gu, w_down, ln2_g, ln2_b):
    p = _prepare(rel_bias_table, w_in, b_in, w_pool, pool_scale, p_pool, sink, p_attn, w_mem_kv, p_mem,
                 w_out, ln1_g, ln1_b, w_router_group, w_router_expert, w_gu, w_down)
    y_prompt = _trunk(x_prompt, mem_prompt, ln_in_g, ln_in_b, ln2_g[0], ln2_b[0], p)
    y_sample = _trunk(x_sample, mem_sample, ln_in_g, ln_in_b, ln2_g[0], ln2_b[0], p)
    return (y_prompt, y_sample)
```

```python
import functools
import math

import jax
import jax.numpy as jnp
from jax import lax
from jax.experimental import pallas as pl
from jax.experimental.pallas import tpu as pltpu

F32 = jnp.float32
BF16 = jnp.bfloat16
I32 = jnp.int32

D_MODEL = 1024
DEPTH = 1
POOL_WIDTH = 512
POOL_WINDOWS = (2, 4, 8, 16)
N_POOL_GROUPS = 4
POOL_GROUP_CH = POOL_WIDTH // N_POOL_GROUPS
N_Q_HEADS = 16
N_KV_HEADS = 4
HEAD_DIM = 64
ATTN_WIDTH = N_Q_HEADS * HEAD_DIM
KV_WIDTH = N_KV_HEADS * HEAD_DIM
WINDOW = 128
BLOCK = 128
N_REL_BUCKETS = 32
REL_MAX_DISTANCE = 128
MEM_TOKENS = 256
MEM_HEADS = 4
MEM_HEAD_DIM = 128
MEM_WIDTH = MEM_HEADS * MEM_HEAD_DIM
N_BRANCHES = 3
N_EXPERT_GROUPS = 4
EXPERTS_PER_GROUP = 8
N_EXPERTS = N_EXPERT_GROUPS * EXPERTS_PER_GROUP
TOP_K = 2
EXPERT_HIDDEN = 512
ALPHA = (2 * DEPTH) ** 0.25
LN_EPS = 1e-5
NEG_INF = -1e30

LANES = 128
SUBLANES = 8
ROW_TILE = D_MODEL // LANES
VMEM_LIMIT_BYTES = 56 * 1024 * 1024

KV_DUP_WIDTH = N_KV_HEADS * LANES

C_U = 0
C_Q = C_U + POOL_WIDTH
C_K = C_Q + ATTN_WIDTH
C_V = C_K + KV_WIDTH
C_M = C_V + KV_WIDTH
C_G = C_M + MEM_WIDTH
C_END = C_G + N_BRANCHES * D_MODEL

TM_INPROJ = 512
TQ_ATTN = 256
TM_MIX = 512
BM_MOE = 256
MOE_ROW_UNROLL = 8
TM_FINAL = 512
POOL_HALO = 8
ROUTER_ROWS = 40


def _layer_norm(x, g, b):
    mu = jnp.mean(x, axis=-1, keepdims=True)
    xc = x - mu
    var = jnp.mean(xc * xc, axis=-1, keepdims=True)
    return xc * lax.rsqrt(var + LN_EPS) * g + b


def _sigmoid(x):
    return 1.0 / (1.0 + jnp.exp(-x))


def _rows_to_tiles(x):
    return pltpu.einshape("r(cl)->rcl", x, c=ROW_TILE)


def _tiles_to_rows(x):
    return pltpu.einshape("rcl->r(cl)", x)


def _params(**kw):
    return pltpu.CompilerParams(vmem_limit_bytes=VMEM_LIMIT_BYTES, **kw)


def _memkv_kernel(mem_ref, w_ref, out_ref):
    out_ref[...] = jnp.dot(mem_ref[...].astype(BF16), w_ref[...],
                           preferred_element_type=F32).astype(BF16)


def _memkv_call(mem2, w_memkv):
    rows = mem2.shape[0]
    return pl.pallas_call(
        _memkv_kernel,
        grid=(rows // MEM_TOKENS,),
        in_specs=[pl.BlockSpec((MEM_TOKENS, D_MODEL), lambda i: (i, 0)),
                  pl.BlockSpec((D_MODEL, 2 * MEM_WIDTH), lambda i: (0, 0))],
        out_specs=pl.BlockSpec((MEM_TOKENS, 2 * MEM_WIDTH), lambda i: (i, 0)),
        out_shape=jax.ShapeDtypeStruct((rows, 2 * MEM_WIDTH), BF16),
        compiler_params=_params(dimension_semantics=("arbitrary",)),
        name="memkv",
    )(mem2, w_memkv)


def _inproj_kernel(x_ref, g_ref, b_ref, w_ref, bias_ref,
                   xn_ref, u_ref, q_ref, k_ref, v_ref, qm_ref, gate_ref):
    xn = _layer_norm(x_ref[...], g_ref[...], b_ref[...])
    xn_ref[...] = xn
    xb = xn.astype(BF16)

    def seg(lo, hi):
        return jnp.dot(xb, w_ref[:, lo:hi], preferred_element_type=F32) + bias_ref[:, lo:hi]

    u_ref[...] = seg(C_U, C_Q)
    q_ref[...] = (seg(C_Q, C_K) * (HEAD_DIM ** -0.5)).astype(BF16)
    kv = seg(C_K, C_M)
    low = lax.broadcasted_iota(I32, (kv.shape[0], LANES), 1) < HEAD_DIM
    for out_ref, c_lo in ((k_ref, 0), (v_ref, KV_WIDTH)):
        for c in range(KV_WIDTH // LANES):
            blk = kv[:, c_lo + c * LANES:c_lo + (c + 1) * LANES]
            rot = pltpu.roll(blk, HEAD_DIM, 1)
            out_ref[:, 2 * c * LANES:(2 * c + 1) * LANES] = jnp.where(low, blk, rot).astype(BF16)
            out_ref[:, (2 * c + 1) * LANES:(2 * c + 2) * LANES] = jnp.where(low, rot, blk).astype(BF16)
    qm_ref[...] = seg(C_M, C_G).astype(BF16)
    for j in range(N_BRANCHES):
        lo = C_G + j * D_MODEL
        gate_ref[:, j * D_MODEL:(j + 1) * D_MODEL] = _sigmoid(seg(lo, lo + D_MODEL)).astype(BF16)


def _inproj_call(x2, ln_g, ln_b, w_in, b_in):
    t = x2.shape[0]
    tm = TM_INPROJ
    row = lambda width: pl.BlockSpec((tm, width), lambda i: (i, 0))
    const = lambda shape: pl.BlockSpec(shape, lambda i: (0, 0))
    widths = (D_MODEL, POOL_WIDTH, ATTN_WIDTH, KV_DUP_WIDTH, KV_DUP_WIDTH, MEM_WIDTH, N_BRANCHES * D_MODEL)
    dtypes = (F32, F32, BF16, BF16, BF16, BF16, BF16)
    return pl.pallas_call(
        _inproj_kernel,
        grid=(t // tm,),
        in_specs=[row(D_MODEL), const((1, D_MODEL)), const((1, D_MODEL)),
                  const((D_MODEL, C_END)), const((1, C_END))],
        out_specs=[row(w) for w in widths],
        out_shape=[jax.ShapeDtypeStruct((t, w), d) for w, d in zip(widths, dtypes)],
        compiler_params=_params(dimension_semantics=("arbitrary",)),
        name="inproj",
    )(x2, ln_g, ln_b, w_in, b_in)


def _attn_kernel(sink_ref, q_ref, kc_ref, kp_ref, kn_ref, vc_ref, vp_ref, vn_ref,
                 bp_ref, bc_ref, bn_ref, o_ref, klo, khi, vlo, vhi, s_scr, p_scr):
    i = pl.program_id(1)
    n_tiles = pl.num_programs(1)
    tq = q_ref.shape[0]
    n_qb = tq // BLOCK

    lane = lax.broadcasted_iota(I32, (BLOCK, KV_DUP_WIDTH), 1)
    low = (lane & (LANES - 1)) < HEAD_DIM

    def put(dst_lo, dst_hi, r0, val):
        zero = jnp.zeros_like(val)
        dst_lo[r0:r0 + BLOCK, :] = jnp.where(low, val, zero)
        dst_hi[r0:r0 + BLOCK, :] = jnp.where(low, zero, val)

    put(klo, khi, 0, kp_ref[...])
    put(vlo, vhi, 0, vp_ref[...])
    for j in range(n_qb):
        put(klo, khi, (j + 1) * BLOCK, kc_ref[j * BLOCK:(j + 1) * BLOCK, :])
        put(vlo, vhi, (j + 1) * BLOCK, vc_ref[j * BLOCK:(j + 1) * BLOCK, :])
    put(klo, khi, (n_qb + 1) * BLOCK, kn_ref[...])
    put(vlo, vhi, (n_qb + 1) * BLOCK, vn_ref[...])

    first = (i == 0).astype(I32)
    last = (i == n_tiles - 1).astype(I32)

    def softmax_rows(s, sink):
        m = jnp.maximum(jnp.max(s, axis=-1, keepdims=True), sink)
        p = jnp.exp(s - m)
        denom = jnp.sum(p, axis=-1, keepdims=True) + jnp.exp(sink - m)
        return (p * (1.0 / denom)).astype(BF16)

    units = [(j, h, half) for j in range(n_qb) for h in range(N_KV_HEADS) for half in range(2)]
    always = i >= 0

    @pl.when(always)
    def _scores():
        for u, (j, h, half) in enumerate(units):
            r0 = j * BLOCK
            c0 = h * 2 * LANES
            pv = first if j == 0 else 0
            nv = last if j == n_qb - 1 else 0
            unit = h * 2 + half
            q_pairs = jnp.concatenate([q_ref[r0:r0 + BLOCK, c0:c0 + LANES],
                                       q_ref[r0:r0 + BLOCK, c0 + LANES:c0 + 2 * LANES]], axis=0)
            kx = (klo, khi)[half][r0:r0 + 3 * BLOCK, h * LANES:(h + 1) * LANES]
            s = lax.dot_general(q_pairs, kx, (((1,), (1,)), ((), ())), preferred_element_type=F32)
            bias = jnp.concatenate([bp_ref[pv, unit], bc_ref[unit], bn_ref[nv, unit]], axis=1)
            s_scr[u] = s + bias

    @pl.when(always)
    def _softmax():
        for u, (j, h, half) in enumerate(units):
            p_scr[u, :BLOCK] = softmax_rows(s_scr[u, :BLOCK], sink_ref[4 * h + half])
            p_scr[u, BLOCK:] = softmax_rows(s_scr[u, BLOCK:], sink_ref[4 * h + 2 + half])

    @pl.when(always)
    def _values():
        for j in range(n_qb):
            r0 = j * BLOCK
            for h in range(N_KV_HEADS):
                c0 = h * 2 * LANES
                acc = jnp.zeros((2 * BLOCK, LANES), F32)
                for half, vref in enumerate((vlo, vhi)):
                    u = units.index((j, h, half))
                    vx = vref[r0:r0 + 3 * BLOCK, h * LANES:(h + 1) * LANES]
                    acc = acc + jnp.dot(p_scr[u], vx, preferred_element_type=F32)
                o_ref[r0:r0 + BLOCK, c0:c0 + LANES] = acc[:BLOCK].astype(BF16)
                o_ref[r0:r0 + BLOCK, c0 + LANES:c0 + 2 * LANES] = acc[BLOCK:].astype(BF16)


def _attn_call(q, kk, vv, sink, bias_prev, bias_cur, bias_next, batch, seq):
    tq = TQ_ATTN
    n_qb = tq // BLOCK
    nb = seq // BLOCK
    n_tiles = seq // tq
    t = batch * seq

    cur = lambda width: pl.BlockSpec((tq, width), lambda b, i: (b * n_tiles + i, 0))
    prev = pl.BlockSpec((BLOCK, KV_DUP_WIDTH), lambda b, i: (b * nb + jnp.maximum(i * n_qb - 1, 0), 0))
    nxt = pl.BlockSpec((BLOCK, KV_DUP_WIDTH), lambda b, i: (b * nb + jnp.minimum((i + 1) * n_qb, nb - 1), 0))
    full = lambda a: pl.BlockSpec(a.shape, lambda b, i: (0,) * a.ndim)
    ext = (n_qb + 2) * BLOCK
    n_units = n_qb * N_KV_HEADS * 2
    return pl.pallas_call(
        _attn_kernel,
        grid=(batch, n_tiles),
        in_specs=[pl.BlockSpec(memory_space=pltpu.SMEM),
                  cur(ATTN_WIDTH), cur(KV_DUP_WIDTH), prev, nxt, cur(KV_DUP_WIDTH), prev, nxt,
                  full(bias_prev), full(bias_cur), full(bias_next)],
        out_specs=cur(ATTN_WIDTH),
        out_shape=jax.ShapeDtypeStruct((t, ATTN_WIDTH), BF16),
        scratch_shapes=[pltpu.VMEM((ext, KV_DUP_WIDTH), BF16) for _ in range(4)]
                       + [pltpu.VMEM((n_units, 2 * BLOCK, 3 * BLOCK), F32),
                          pltpu.VMEM((n_units, 2 * BLOCK, 3 * BLOCK), BF16)],
        compiler_params=_params(dimension_semantics=("arbitrary", "arbitrary")),
        name="attn",
    )(sink, q, kk, kk, kk, vv, vv, vv, bias_prev, bias_cur, bias_next)


def _mix_kernel(xn_ref, u_ref, up_ref, un_ref, o_ref, qm_ref, gate_ref, memkv_ref,
                wpool_ref, pscale_ref, ppool_ref, pattn_ref, pmem_ref, wout_ref,
                g1_ref, b1_ref, wr_ref,
                x1r_ref, eid_ref, wts_ref, uext, *, seq):
    i = pl.program_id(1)
    n_tiles = pl.num_programs(1)
    tm = xn_ref.shape[0]

    halo = POOL_HALO
    uext[0:halo, :] = jnp.where(i > 0, up_ref[...], 0.0)
    uext[halo:halo + tm, :] = u_ref[...]
    uext[halo + tm:halo + tm + halo, :] = jnp.where(i < n_tiles - 1, un_ref[...], 0.0)
    pos = i * tm + lax.broadcasted_iota(I32, (tm, 1), 0)
    mixed = []
    for gi, win in enumerate(POOL_WINDOWS):
        c0 = gi * POOL_GROUP_CH
        half = win // 2
        total = jnp.zeros((tm, POOL_GROUP_CH), F32)
        for off in range(-half, half):
            total = total + uext[halo + off:halo + off + tm, c0:c0 + POOL_GROUP_CH]
        cnt = (jnp.minimum(pos + half, seq) - jnp.maximum(pos - half, 0)).astype(F32)
        pooled = total * (1.0 / cnt) - u_ref[:, c0:c0 + POOL_GROUP_CH]
        mixed.append(jnp.dot(pooled.astype(BF16), wpool_ref[gi], preferred_element_type=F32)
                     * pscale_ref[:, c0:c0 + POOL_GROUP_CH])
    mixed = jnp.concatenate(mixed, axis=1).astype(BF16)
    merged = gate_ref[:, 0:D_MODEL].astype(F32) * jnp.dot(
        mixed, ppool_ref[...], preferred_element_type=F32)

    merged = merged + gate_ref[:, D_MODEL:2 * D_MODEL].astype(F32) * jnp.dot(
        o_ref[...], pattn_ref[...], preferred_element_type=F32)

    heads = []
    for h in range(MEM_HEADS):
        c0 = h * MEM_HEAD_DIM
        km = memkv_ref[:, c0:c0 + MEM_HEAD_DIM]
        vm = memkv_ref[:, MEM_WIDTH + c0:MEM_WIDTH + c0 + MEM_HEAD_DIM]
        s = lax.dot_general(qm_ref[:, c0:c0 + MEM_HEAD_DIM], km, (((1,), (1,)), ((), ())),
                            preferred_element_type=F32) * (MEM_HEAD_DIM ** -0.5)
        p = jnp.exp(s - jnp.max(s, axis=-1, keepdims=True))
        p = (p / jnp.sum(p, axis=-1, keepdims=True)).astype(BF16)
        heads.append(jnp.dot(p, vm, preferred_element_type=F32).astype(BF16))
    om = jnp.concatenate(heads, axis=1)
    merged = merged + gate_ref[:, 2 * D_MODEL:3 * D_MODEL].astype(F32) * jnp.dot(
        om, pmem_ref[...], preferred_element_type=F32)

    y = jnp.dot(merged.astype(BF16), wout_ref[...], preferred_element_type=F32)
    x1 = _layer_norm(ALPHA * xn_ref[...] + y, g1_ref[...], b1_ref[...])
    x1r_ref[...] = _rows_to_tiles(x1)

    lt = lax.dot_general(wr_ref[...], x1.astype(BF16), (((1,), (1,)), ((), ())),
                         preferred_element_type=F32)
    gl = [lt[r:r + 1, :] for r in range(N_EXPERT_GROUPS)]
    gmax = gl[0]
    grp = jnp.zeros((1, tm), I32)
    for r in range(1, N_EXPERT_GROUPS):
        better = gl[r] > gmax
        grp = jnp.where(better, r, grp)
        gmax = jnp.where(better, gl[r], gmax)
    gsum = gl[0] * 0.0
    for r in range(N_EXPERT_GROUPS):
        gsum = gsum + jnp.exp(gl[r] - gmax)
    gp = 1.0 / gsum
    sel = jnp.zeros((EXPERTS_PER_GROUP, tm), F32)
    for r in range(N_EXPERT_GROUPS):
        rows = lt[SUBLANES + r * EXPERTS_PER_GROUP:SUBLANES + (r + 1) * EXPERTS_PER_GROUP, :]
        sel = jnp.where(grp == r, rows, sel)
    ridx = lax.broadcasted_iota(I32, (EXPERTS_PER_GROUP, tm), 0)
    top1 = jnp.max(sel, axis=0, keepdims=True)
    i1 = jnp.min(jnp.where(sel == top1, ridx, EXPERTS_PER_GROUP), axis=0, keepdims=True)
    rest = jnp.where(ridx == i1, -jnp.inf, sel)
    top2 = jnp.max(rest, axis=0, keepdims=True)
    i2 = jnp.min(jnp.where(rest == top2, ridx, EXPERTS_PER_GROUP), axis=0, keepdims=True)
    e2 = jnp.exp(top2 - top1)
    inv = gp / (1.0 + e2)
    eid_ref[0:1, :] = grp * EXPERTS_PER_GROUP + i1
    eid_ref[1:2, :] = grp * EXPERTS_PER_GROUP + i2
    wts_ref[0:1, :] = inv
    wts_ref[1:2, :] = e2 * inv


def _mix_call(xn, u, o, qm, gate, memkv, p, batch, seq):
    tm = TM_MIX
    n_tiles = seq // tm
    t = batch * seq
    hb = tm // POOL_HALO

    row = lambda width: pl.BlockSpec((tm, width), lambda b, i: (b * n_tiles + i, 0))
    prev = pl.BlockSpec((POOL_HALO, POOL_WIDTH),
                        lambda b, i: (jnp.maximum((b * n_tiles + i) * hb - 1, 0), 0))
    nxt = pl.BlockSpec((POOL_HALO, POOL_WIDTH),
                       lambda b, i: (jnp.minimum((b * n_tiles + i + 1) * hb, t // POOL_HALO - 1), 0))
    full = lambda a: pl.BlockSpec(a.shape, lambda b, i: (0,) * a.ndim)
    lane_row = pl.BlockSpec((TOP_K, tm), lambda b, i: (0, b * n_tiles + i))
    weights = (p["w_pool"], p["pool_scale"], p["p_pool"], p["p_attn"], p["p_mem"], p["w_out"],
               p["ln1_g"], p["ln1_b"], p["w_router"])
    return pl.pallas_call(
        functools.partial(_mix_kernel, seq=seq),
        grid=(batch, n_tiles),
        in_specs=[row(D_MODEL), row(POOL_WIDTH), prev, nxt, row(ATTN_WIDTH), row(MEM_WIDTH),
                  row(N_BRANCHES * D_MODEL),
                  pl.BlockSpec((MEM_TOKENS, 2 * MEM_WIDTH), lambda b, i: (b, 0))]
                 + [full(w) for w in weights],
        out_specs=[pl.BlockSpec((tm, ROW_TILE, LANES), lambda b, i: (b * n_tiles + i, 0, 0)),
                   lane_row, lane_row],
        out_shape=[jax.ShapeDtypeStruct((t, ROW_TILE, LANES), F32),
                   jax.ShapeDtypeStruct((TOP_K, t), I32),
                   jax.ShapeDtypeStruct((TOP_K, t), F32)],
        scratch_shapes=[pltpu.VMEM((tm + 2 * POOL_HALO, POOL_WIDTH), F32)],
        compiler_params=_params(dimension_semantics=("arbitrary", "arbitrary")),
        name="mix",
    )(xn, u, u, u, o, qm, gate, memkv, *weights)


def _moe_kernel(tok_ref, dst_ref, iblk_ref, ie_ref, inext_ref, irun_ref, ilo_ref, ihi_ref,
                x1_hbm, ws_ref, wgu_hbm, wd_hbm, y2_hbm,
                xbuf, ybuf, x_bf, h_bf, wgu_f32, wd_f32, wgu_bf, wd_bf, gsem, ssem, wsem, *, n_blocks):
    i = pl.program_id(0)
    bm = xbuf.shape[1]
    lo = ilo_ref[i]
    hi = ihi_ref[i]
    k = iblk_ref[i]
    slot = k & 1
    nonempty = hi > lo

    def gather_start(blk, s):
        def body(r, c):
            pltpu.make_async_copy(x1_hbm.at[tok_ref[blk * bm + r]], xbuf.at[s, r], gsem.at[s]).start()
            return c
        lax.fori_loop(0, bm, body, 0, unroll=MOE_ROW_UNROLL)

    def gather_wait(s):
        pltpu.make_async_copy(x1_hbm.at[pl.ds(0, bm)], xbuf.at[s], gsem.at[s]).wait()

    def scatter_start(blk, s):
        def body(r, c):
            pltpu.make_async_copy(ybuf.at[s, r], y2_hbm.at[dst_ref[blk * bm + r]], ssem.at[s]).start()
            return c
        lax.fori_loop(0, bm, body, 0, unroll=MOE_ROW_UNROLL)

    def scatter_wait(s):
        pltpu.make_async_copy(ybuf.at[s], y2_hbm.at[pl.ds(0, bm)], ssem.at[s]).wait()

    @pl.when(jnp.logical_and(nonempty, lo == 0))
    def _():
        @pl.when(k == 0)
        def _():
            gather_start(0, 0)
        gather_wait(slot)

        @pl.when(k + 1 < n_blocks)
        def _():
            gather_start(k + 1, 1 - slot)

        @pl.when(k >= 2)
        def _():
            scatter_wait(slot)

    def weight_copies(e, ws):
        return (pltpu.make_async_copy(wgu_hbm.at[e], wgu_f32.at[ws], wsem.at[ws]),
                pltpu.make_async_copy(wd_hbm.at[e], wd_f32.at[ws], wsem.at[ws]))

    @pl.when(jnp.logical_or(i == 0, ie_ref[i] != ie_ref[jnp.maximum(i - 1, 0)]))
    def _():
        e = ie_ref[i]
        wslot = irun_ref[i] & 1

        @pl.when(i == 0)
        def _():
            for c in weight_copies(e, wslot):
                c.start()
        for c in weight_copies(e, wslot):
            c.wait()

        @pl.when(inext_ref[i] >= 0)
        def _():
            for c in weight_copies(inext_ref[i], 1 - wslot):
                c.start()
        wgu_bf[...] = wgu_f32[wslot].astype(BF16)
        wd_bf[...] = wd_f32[wslot].astype(BF16)

    @pl.when(nonempty)
    def _():
        x_bf[...] = _tiles_to_rows(xbuf[slot]).astype(BF16)

    @pl.when(nonempty)
    def _():
        gu = jnp.dot(x_bf[...], wgu_bf[...], preferred_element_type=F32)
        gate = gu[:, :EXPERT_HIDDEN]
        h_bf[...] = ((gate * _sigmoid(gate)) * gu[:, EXPERT_HIDDEN:]).astype(BF16)

    @pl.when(nonempty)
    def _():
        rows = lax.broadcasted_iota(I32, (bm, 1), 0)
        mine = jnp.logical_and(rows >= lo, rows < hi)
        y = jnp.dot(h_bf[...], wd_bf[...], preferred_element_type=F32) * ws_ref[...]
        y = _rows_to_tiles(jnp.where(mine, y, 0.0))

        @pl.when(lo == 0)
        def _():
            ybuf[slot] = y

        @pl.when(lo > 0)
        def _():
            ybuf[slot] = ybuf[slot] + y

    @pl.when(jnp.logical_and(nonempty, hi == bm))
    def _():
        scatter_start(k, slot)

        @pl.when(k == n_blocks - 1)
        def _():
            if n_blocks >= 2:
                scatter_wait(1 - slot)
            scatter_wait(slot)


def _moe_call(x1r, sorted_tok, sorted_dst, wsorted, item_blk, item_e, item_lo, item_hi, w_gu, w_down):
    a = sorted_dst.shape[0]
    bm = BM_MOE
    n_items = item_blk.shape[0]
    change = jnp.concatenate([jnp.zeros((1,), I32), (item_e[1:] != item_e[:-1]).astype(I32)])
    item_run = jnp.cumsum(change).astype(I32)
    first_later = jnp.sum((item_e[None, :] <= item_e[:, None]).astype(I32), axis=1)
    item_next = jnp.where(first_later < n_items, item_e[jnp.minimum(first_later, n_items - 1)], -1).astype(I32)
    grid_spec = pltpu.PrefetchScalarGridSpec(
        num_scalar_prefetch=8,
        grid=(n_items,),
        in_specs=[pl.BlockSpec(memory_space=pl.ANY),
                  pl.BlockSpec((bm, 1), lambda i, tok, dst, blk, *_: (blk[i], 0)),
                  pl.BlockSpec(memory_space=pl.ANY),
                  pl.BlockSpec(memory_space=pl.ANY)],
        out_specs=pl.BlockSpec(memory_space=pl.ANY),
        scratch_shapes=[pltpu.VMEM((2, bm, ROW_TILE, LANES), F32),
                        pltpu.VMEM((2, bm, ROW_TILE, LANES), F32),
                        pltpu.VMEM((bm, D_MODEL), BF16),
                        pltpu.VMEM((bm, EXPERT_HIDDEN), BF16),
                        pltpu.VMEM((2, D_MODEL, 2 * EXPERT_HIDDEN), F32),
                        pltpu.VMEM((2, EXPERT_HIDDEN, D_MODEL), F32),
                        pltpu.VMEM((D_MODEL, 2 * EXPERT_HIDDEN), BF16),
                        pltpu.VMEM((EXPERT_HIDDEN, D_MODEL), BF16),
                        pltpu.SemaphoreType.DMA((2,)), pltpu.SemaphoreType.DMA((2,)),
                        pltpu.SemaphoreType.DMA((2,))],
    )
    return pl.pallas_call(
        functools.partial(_moe_kernel, n_blocks=a // bm),
        grid_spec=grid_spec,
        out_shape=jax.ShapeDtypeStruct((a, ROW_TILE, LANES), F32),
        compiler_params=_params(dimension_semantics=("arbitrary",)),
        name="moe",
    )(sorted_tok, sorted_dst, item_blk, item_e, item_next, item_run, item_lo, item_hi,
      x1r, wsorted, w_gu, w_down)


def _final_kernel(x1_ref, ya_ref, yb_ref, g_ref, b_ref, out_ref):
    moe = ya_ref[...] + yb_ref[...]
    out_ref[...] = _layer_norm(_tiles_to_rows(ALPHA * x1_ref[...] + moe), g_ref[...], b_ref[...])


def _final_call(x1, y2, ln_g, ln_b):
    t = x1.shape[0]
    tm = TM_FINAL
    n_tiles = t // tm
    return pl.pallas_call(
        _final_kernel,
        grid=(n_tiles,),
        in_specs=[pl.BlockSpec((tm, ROW_TILE, LANES), lambda i: (i, 0, 0)),
                  pl.BlockSpec((tm, ROW_TILE, LANES), lambda i: (i, 0, 0)),
                  pl.BlockSpec((tm, ROW_TILE, LANES), lambda i: (n_tiles + i, 0, 0)),
                  pl.BlockSpec((1, D_MODEL), lambda i: (0, 0)),
                  pl.BlockSpec((1, D_MODEL), lambda i: (0, 0))],
        out_specs=pl.BlockSpec((tm, D_MODEL), lambda i: (i, 0)),
        out_shape=jax.ShapeDtypeStruct((t, D_MODEL), F32),
        compiler_params=_params(dimension_semantics=("arbitrary",)),
        name="final",
    )(x1, y2, y2, ln_g, ln_b)


def _t5_bucket(rel):
    nb = N_REL_BUCKETS // 2
    max_exact = nb // 2
    ret = jnp.where(rel > 0, nb, 0)
    n = jnp.abs(rel)
    nf = jnp.maximum(n, 1).astype(F32)
    large = max_exact + (jnp.log(nf / max_exact) / math.log(REL_MAX_DISTANCE / max_exact)
                         * (nb - max_exact)).astype(I32)
    large = jnp.minimum(large, nb - 1)
    return ret + jnp.where(n < max_exact, n, large)


def _bias_tables(rel_table):
    n, m = BLOCK, 3 * BLOCK
    rel = jnp.arange(-(2 * BLOCK - 1), 2 * BLOCK)
    by_rel = jnp.where((jnp.abs(rel) <= WINDOW)[:, None], rel_table[_t5_bucket(rel)].astype(F32), NEG_INF)
    by_rel = jnp.pad(by_rel.T, ((0, 0), (0, 1)))
    skew = jnp.tile(by_rel, (1, n))[:, :n * (m + n - 1)].reshape(N_Q_HEADS, n, m + n - 1)
    bias = skew[:, :, n - 1:n - 1 + m]
    heads = [4 * h + half for h in range(N_KV_HEADS) for half in range(2)]
    units = jnp.concatenate([jnp.stack([bias[a] for a in heads]),
                             jnp.stack([bias[a + 2] for a in heads])], axis=1)
    masked = jnp.full((2 * N_KV_HEADS, 2 * BLOCK, BLOCK), NEG_INF, F32)
    prev = jnp.stack([units[:, :, :BLOCK], masked])
    cur = units[:, :, BLOCK:2 * BLOCK]
    nxt = jnp.stack([units[:, :, 2 * BLOCK:], masked])
    return prev, cur, nxt


def _prepare(rel_bias_table, w_in, b_in, w_pool, pool_scale, p_pool, sink, p_attn, w_mem_kv, p_mem,
             w_out, ln1_g, ln1_b, w_router_group, w_router_expert, w_gu, w_down):
    l = 0
    o1 = POOL_WIDTH
    o2 = o1 + ATTN_WIDTH
    o3 = o2 + KV_WIDTH
    o4 = o3 + KV_WIDTH
    o5 = o4 + MEM_WIDTH
    w_router = jnp.zeros((ROUTER_ROWS, D_MODEL), F32)
    w_router = w_router.at[:N_EXPERT_GROUPS].set(w_router_group[l].T)
    w_router = w_router.at[SUBLANES:].set(w_router_expert[l].T)
    bias_prev, bias_cur, bias_next = _bias_tables(rel_bias_table)
    return dict(
        w_in=w_in[l].astype(BF16), b_in=b_in[l][None, :],
        w_pool=w_pool[l].astype(BF16), pool_scale=pool_scale[l][None, :],
        p_pool=p_pool[l].astype(BF16), p_attn=p_attn[l].astype(BF16), p_mem=p_mem[l].astype(BF16),
        w_out=w_out[l].astype(BF16), w_memkv=w_mem_kv[l].astype(BF16),
        ln1_g=ln1_g[l][None, :], ln1_b=ln1_b[l][None, :],
        w_router=w_router.astype(BF16), sink=sink[l],
        w_gu=w_gu[l], w_down=w_down[l],
        bias_prev=bias_prev, bias_cur=bias_cur, bias_next=bias_next,
    )


def _dispatch_plan(eid, wts):
    t = eid.shape[1]
    a = TOP_K * t
    bm = BM_MOE
    nblk = a // bm
    eflat = eid.reshape(a)
    wflat = wts.reshape(a)
    order = jnp.argsort(eflat, stable=True).astype(I32)
    wsorted = wflat[order][:, None]
    experts = jnp.arange(N_EXPERTS, dtype=I32)
    counts = jnp.sum((eflat[None, :] == experts[:, None]).astype(I32), axis=1)
    ends = jnp.cumsum(counts).astype(I32)
    starts = ends - counts
    cuts = jnp.sort(jnp.concatenate([jnp.arange(nblk, dtype=I32) * bm, starts, jnp.array([a], I32)]))
    lo_abs = cuts[:-1]
    hi_abs = cuts[1:]
    nonempty = hi_abs > lo_abs
    blk = jnp.minimum(lo_abs // bm, nblk - 1)
    probe = jnp.minimum(lo_abs, a - 1)
    item_e = jnp.sum((ends[None, :] <= probe[:, None]).astype(I32), axis=1)
    item_lo = jnp.where(nonempty, lo_abs - blk * bm, 0).astype(I32)
    item_hi = jnp.where(nonempty, hi_abs - blk * bm, 0).astype(I32)
    return order, wsorted, blk.astype(I32), item_e.astype(I32), item_lo, item_hi


def _trunk(x, mem, ln_in_g, ln_in_b, ln2_g, ln2_b, p):
    batch, seq, d = x.shape
    t = batch * seq
    assert seq % TM_MIX == 0 and seq % TQ_ATTN == 0 and t % TM_INPROJ == 0 and (TOP_K * t) % BM_MOE == 0
    memkv = _memkv_call(mem.reshape(batch * MEM_TOKENS, d), p["w_memkv"])
    xn, u, q, kk, vv, qm, gate = _inproj_call(x.reshape(t, d), ln_in_g[None, :], ln_in_b[None, :],
                                              p["w_in"], p["b_in"])
    o = _attn_call(q, kk, vv, p["sink"], p["bias_prev"], p["bias_cur"], p["bias_next"], batch, seq)
    x1r, eid, wts = _mix_call(xn, u, o, qm, gate, memkv, p, batch, seq)
    order, wsorted, item_blk, item_e, item_lo, item_hi = _dispatch_plan(eid, wts)
    sorted_tok = jnp.where(order >= t, order - t, order)
    y2 = _moe_call(x1r, sorted_tok, order, wsorted, item_blk, item_e, item_lo, item_hi, p["w_gu"], p["w_down"])
    out = _final_call(x1r, y2, ln2_g[None, :], ln2_b[None, :])
    return out.reshape(batch, seq, d)


def kernel(x_prompt, x_sample, mem_prompt, mem_sample, ln_in_g, ln_in_b, rel_bias_table, w_in, b_in,
           w_pool, pool_scale, p_pool, sink, p_attn, w_mem_kv, p_mem, w_out, ln1_g, ln1_b,
           w_router_group, w_router_expert, w_gu, w_down, ln2_g, ln2_b):
    p = _prepare(rel_bias_table, w_in, b_in, w_pool, pool_scale, p_pool, sink, p_attn, w_mem_kv, p_mem,
                 w_out, ln1_g, ln1_b, w_router_group, w_router_expert, w_gu, w_down)
    y_prompt = _trunk(x_prompt, mem_prompt, ln_in_g, ln_in_b, ln2_g[0], ln2_b[0], p)
    y_sample = _trunk(x_sample, mem_sample, ln_in_g, ln_in_b, ln2_g[0], ln2_b[0], p)
    return (y_prompt, y_sample)
```

```python
import functools
import math

import jax
import jax.numpy as jnp
from jax import lax
from jax.experimental import pallas as pl
from jax.experimental.pallas import tpu as pltpu

F32 = jnp.float32
BF16 = jnp.bfloat16
I32 = jnp.int32

D_MODEL = 1024
DEPTH = 1
POOL_WIDTH = 512
POOL_WINDOWS = (2, 4, 8, 16)
N_POOL_GROUPS = 4
POOL_GROUP_CH = POOL_WIDTH // N_POOL_GROUPS
N_Q_HEADS = 16
N_KV_HEADS = 4
HEAD_DIM = 64
ATTN_WIDTH = N_Q_HEADS * HEAD_DIM
KV_WIDTH = N_KV_HEADS * HEAD_DIM
WINDOW = 128
BLOCK = 128
N_REL_BUCKETS = 32
REL_MAX_DISTANCE = 128
MEM_TOKENS = 256
MEM_HEADS = 4
MEM_HEAD_DIM = 128
MEM_WIDTH = MEM_HEADS * MEM_HEAD_DIM
N_BRANCHES = 3
N_EXPERT_GROUPS = 4
EXPERTS_PER_GROUP = 8
N_EXPERTS = N_EXPERT_GROUPS * EXPERTS_PER_GROUP
TOP_K = 2
EXPERT_HIDDEN = 512
ALPHA = (2 * DEPTH) ** 0.25
LN_EPS = 1e-5
NEG_INF = -1e30

LANES = 128
SUBLANES = 8
ROW_TILE = D_MODEL // LANES
VMEM_LIMIT_BYTES = 56 * 1024 * 1024

KV_DUP_WIDTH = N_KV_HEADS * LANES

C_U = 0
C_Q = C_U + POOL_WIDTH
C_K = C_Q + ATTN_WIDTH
C_V = C_K + KV_WIDTH
C_M = C_V + KV_WIDTH
C_G = C_M + MEM_WIDTH
C_END = C_G + N_BRANCHES * D_MODEL

TM_INPROJ = 512
TQ_ATTN = 256
TM_MIX = 512
BM_MOE = 256
MOE_ROW_UNROLL = 8
TM_FINAL = 512
POOL_HALO = 8
ROUTER_ROWS = 40


def _layer_norm(x, g, b):
    mu = jnp.mean(x, axis=-1, keepdims=True)
    xc = x - mu
    var = jnp.mean(xc * xc, axis=-1, keepdims=True)
    return xc * lax.rsqrt(var + LN_EPS) * g + b


def _sigmoid(x):
    return 1.0 / (1.0 + jnp.exp(-x))


def _rows_to_tiles(x):
    return pltpu.einshape("r(cl)->rcl", x, c=ROW_TILE)


def _tiles_to_rows(x):
    return pltpu.einshape("rcl->r(cl)", x)


def _params(**kw):
    return pltpu.CompilerParams(vmem_limit_bytes=VMEM_LIMIT_BYTES, **kw)


def _memkv_kernel(mem_ref, w_ref, out_ref):
    out_ref[...] = jnp.dot(mem_ref[...].astype(BF16), w_ref[...],
                           preferred_element_type=F32).astype(BF16)


def _memkv_call(mem2, w_memkv):
    rows = mem2.shape[0]
    return pl.pallas_call(
        _memkv_kernel,
        grid=(rows // MEM_TOKENS,),
        in_specs=[pl.BlockSpec((MEM_TOKENS, D_MODEL), lambda i: (i, 0)),
                  pl.BlockSpec((D_MODEL, 2 * MEM_WIDTH), lambda i: (0, 0))],
        out_specs=pl.BlockSpec((MEM_TOKENS, 2 * MEM_WIDTH), lambda i: (i, 0)),
        out_shape=jax.ShapeDtypeStruct((rows, 2 * MEM_WIDTH), BF16),
        compiler_params=_params(dimension_semantics=("arbitrary",)),
        name="memkv",
    )(mem2, w_memkv)


def _inproj_kernel(x_ref, g_ref, b_ref, w_ref, bias_ref,
                   xn_ref, u_ref, q_ref, k_ref, v_ref, qm_ref, gate_ref):
    xn = _layer_norm(x_ref[...], g_ref[...], b_ref[...])
    xn_ref[...] = xn
    xb = xn.astype(BF16)

    def seg(lo, hi):
        return jnp.dot(xb, w_ref[:, lo:hi], preferred_element_type=F32) + bias_ref[:, lo:hi]

    u_ref[...] = seg(C_U, C_Q)
    q_ref[...] = (seg(C_Q, C_K) * (HEAD_DIM ** -0.5)).astype(BF16)
    kv = seg(C_K, C_M)
    low = lax.broadcasted_iota(I32, (kv.shape[0], LANES), 1) < HEAD_DIM
    for out_ref, c_lo in ((k_ref, 0), (v_ref, KV_WIDTH)):
        for c in range(KV_WIDTH // LANES):
            blk = kv[:, c_lo + c * LANES:c_lo + (c + 1) * LANES]
            rot = pltpu.roll(blk, HEAD_DIM, 1)
            out_ref[:, 2 * c * LANES:(2 * c + 1) * LANES] = jnp.where(low, blk, rot).astype(BF16)
            out_ref[:, (2 * c + 1) * LANES:(2 * c + 2) * LANES] = jnp.where(low, rot, blk).astype(BF16)
    qm_ref[...] = seg(C_M, C_G).astype(BF16)
    for j in range(N_BRANCHES):
        lo = C_G + j * D_MODEL
        gate_ref[:, j * D_MODEL:(j + 1) * D_MODEL] = _sigmoid(seg(lo, lo + D_MODEL)).astype(BF16)


def _inproj_call(x2, ln_g, ln_b, w_in, b_in):
    t = x2.shape[0]
    tm = TM_INPROJ
    row = lambda width: pl.BlockSpec((tm, width), lambda i: (i, 0))
    const = lambda shape: pl.BlockSpec(shape, lambda i: (0, 0))
    widths = (D_MODEL, POOL_WIDTH, ATTN_WIDTH, KV_DUP_WIDTH, KV_DUP_WIDTH, MEM_WIDTH, N_BRANCHES * D_MODEL)
    dtypes = (F32, F32, BF16, BF16, BF16, BF16, BF16)
    return pl.pallas_call(
        _inproj_kernel,
        grid=(t // tm,),
        in_specs=[row(D_MODEL), const((1, D_MODEL)), const((1, D_MODEL)),
                  const((D_MODEL, C_END)), const((1, C_END))],
        out_specs=[row(w) for w in widths],
        out_shape=[jax.ShapeDtypeStruct((t, w), d) for w, d in zip(widths, dtypes)],
        compiler_params=_params(dimension_semantics=("arbitrary",)),
        name="inproj",
    )(x2, ln_g, ln_b, w_in, b_in)


def _attn_kernel(sink_ref, q_ref, kc_ref, kp_ref, kn_ref, vc_ref, vp_ref, vn_ref,
                 bp_ref, bc_ref, bn_ref, o_ref, klo, khi, vlo, vhi, s_scr, p_scr):
    i = pl.program_id(1)
    n_tiles = pl.num_programs(1)
    tq = q_ref.shape[0]
    n_qb = tq // BLOCK

    lane = lax.broadcasted_iota(I32, (BLOCK, KV_DUP_WIDTH), 1)
    low = (lane & (LANES - 1)) < HEAD_DIM

    def put(dst_lo, dst_hi, r0, val):
        zero = jnp.zeros_like(val)
        dst_lo[r0:r0 + BLOCK, :] = jnp.where(low, val, zero)
        dst_hi[r0:r0 + BLOCK, :] = jnp.where(low, zero, val)

    put(klo, khi, 0, kp_ref[...])
    put(vlo, vhi, 0, vp_ref[...])
    for j in range(n_qb):
        put(klo, khi, (j + 1) * BLOCK, kc_ref[j * BLOCK:(j + 1) * BLOCK, :])
        put(vlo, vhi, (j + 1) * BLOCK, vc_ref[j * BLOCK:(j + 1) * BLOCK, :])
    put(klo, khi, (n_qb + 1) * BLOCK, kn_ref[...])
    put(vlo, vhi, (n_qb + 1) * BLOCK, vn_ref[...])

    first = (i == 0).astype(I32)
    last = (i == n_tiles - 1).astype(I32)

    def softmax_rows(s, sink):
        m = jnp.maximum(jnp.max(s, axis=-1, keepdims=True), sink)
        p = jnp.exp(s - m)
        denom = jnp.sum(p, axis=-1, keepdims=True) + jnp.exp(sink - m)
        return (p * (1.0 / denom)).astype(BF16)

    units = [(j, h, half) for j in range(n_qb) for h in range(N_KV_HEADS) for half in range(2)]
    always = i >= 0

    @pl.when(always)
    def _scores():
        for u, (j, h, half) in enumerate(units):
            r0 = j * BLOCK
            c0 = h * 2 * LANES
            pv = first if j == 0 else 0
            nv = last if j == n_qb - 1 else 0
            unit = h * 2 + half
            q_pairs = jnp.concatenate([q_ref[r0:r0 + BLOCK, c0:c0 + LANES],
                                       q_ref[r0:r0 + BLOCK, c0 + LANES:c0 + 2 * LANES]], axis=0)
            kx = (klo, khi)[half][r0:r0 + 3 * BLOCK, h * LANES:(h + 1) * LANES]
            s = lax.dot_general(q_pairs, kx, (((1,), (1,)), ((), ())), preferred_element_type=F32)
            bias = jnp.concatenate([bp_ref[pv, unit], bc_ref[unit], bn_ref[nv, unit]], axis=1)
            s_scr[u] = s + bias

    @pl.when(always)
    def _softmax():
        for u, (j, h, half) in enumerate(units):
            p_scr[u, :BLOCK] = softmax_rows(s_scr[u, :BLOCK], sink_ref[4 * h + half])
            p_scr[u, BLOCK:] = softmax_rows(s_scr[u, BLOCK:], sink_ref[4 * h + 2 + half])

    @pl.when(always)
    def _values():
        for j in range(n_qb):
            r0 = j * BLOCK
            for h in range(N_KV_HEADS):
                c0 = h * 2 * LANES
                acc = jnp.zeros((2 * BLOCK, LANES), F32)
                for half, vref in enumerate((vlo, vhi)):
                    u = units.index((j, h, half))
                    vx = vref[r0:r0 + 3 * BLOCK, h * LANES:(h + 1) * LANES]
                    acc = acc + jnp.dot(p_scr[u], vx, preferred_element_type=F32)
                o_ref[r0:r0 + BLOCK, c0:c0 + LANES] = acc[:BLOCK].astype(BF16)
                o_ref[r0:r0 + BLOCK, c0 + LANES:c0 + 2 * LANES] = acc[BLOCK:].astype(BF16)


def _attn_call(q, kk, vv, sink, bias_prev, bias_cur, bias_next, batch, seq):
    tq = TQ_ATTN
    n_qb = tq // BLOCK
    nb = seq // BLOCK
    n_tiles = seq // tq
    t = batch * seq

    cur = lambda width: pl.BlockSpec((tq, width), lambda b, i: (b * n_tiles + i, 0))
    prev = pl.BlockSpec((BLOCK, KV_DUP_WIDTH), lambda b, i: (b * nb + jnp.maximum(i * n_qb - 1, 0), 0))
    nxt = pl.BlockSpec((BLOCK, KV_DUP_WIDTH), lambda b, i: (b * nb + jnp.minimum((i + 1) * n_qb, nb - 1), 0))
    full = lambda a: pl.BlockSpec(a.shape, lambda b, i: (0,) * a.ndim)
    ext = (n_qb + 2) * BLOCK
    n_units = n_qb * N_KV_HEADS * 2
    return pl.pallas_call(
        _attn_kernel,
        grid=(batch, n_tiles),
        in_specs=[pl.BlockSpec(memory_space=pltpu.SMEM),
                  cur(ATTN_WIDTH), cur(KV_DUP_WIDTH), prev, nxt, cur(KV_DUP_WIDTH), prev, nxt,
                  full(bias_prev), full(bias_cur), full(bias_next)],
        out_specs=cur(ATTN_WIDTH),
        out_shape=jax.ShapeDtypeStruct((t, ATTN_WIDTH), BF16),
        scratch_shapes=[pltpu.VMEM((ext, KV_DUP_WIDTH), BF16) for _ in range(4)]
                       + [pltpu.VMEM((n_units, 2 * BLOCK, 3 * BLOCK), F32),
                          pltpu.VMEM((n_units, 2 * BLOCK, 3 * BLOCK), BF16)],
        compiler_params=_params(dimension_semantics=("arbitrary", "arbitrary")),
        name="attn",
    )(sink, q, kk, kk, kk, vv, vv, vv, bias_prev, bias_cur, bias_next)


def _mix_kernel(xn_ref, u_ref, up_ref, un_ref, o_ref, qm_ref, gate_ref, memkv_ref,
                wpool_ref, pscale_ref, ppool_ref, pattn_ref, pmem_ref, wout_ref,
                g1_ref, b1_ref, wr_ref,
                x1r_ref, eid_ref, wts_ref, uext, *, seq):
    i = pl.program_id(1)
    n_tiles = pl.num_programs(1)
    tm = xn_ref.shape[0]

    halo = POOL_HALO
    uext[0:halo, :] = jnp.where(i > 0, up_ref[...], 0.0)
    uext[halo:halo + tm, :] = u_ref[...]
    uext[halo + tm:halo + tm + halo, :] = jnp.where(i < n_tiles - 1, un_ref[...], 0.0)
    pos = i * tm + lax.broadcasted_iota(I32, (tm, 1), 0)
    mixed = []
    for gi, win in enumerate(POOL_WINDOWS):
        c0 = gi * POOL_GROUP_CH
        half = win // 2
        total = jnp.zeros((tm, POOL_GROUP_CH), F32)
        for off in range(-half, half):
            total = total + uext[halo + off:halo + off + tm, c0:c0 + POOL_GROUP_CH]
        cnt = (jnp.minimum(pos + half, seq) - jnp.maximum(pos - half, 0)).astype(F32)
        pooled = total * (1.0 / cnt) - u_ref[:, c0:c0 + POOL_GROUP_CH]
        mixed.append(jnp.dot(pooled.astype(BF16), wpool_ref[gi], preferred_element_type=F32)
                     * pscale_ref[:, c0:c0 + POOL_GROUP_CH])
    mixed = jnp.concatenate(mixed, axis=1).astype(BF16)
    merged = gate_ref[:, 0:D_MODEL].astype(F32) * jnp.dot(
        mixed, ppool_ref[...], preferred_element_type=F32)

    merged = merged + gate_ref[:, D_MODEL:2 * D_MODEL].astype(F32) * jnp.dot(
        o_ref[...], pattn_ref[...], preferred_element_type=F32)

    heads = []
    for h in range(MEM_HEADS):
        c0 = h * MEM_HEAD_DIM
        km = memkv_ref[:, c0:c0 + MEM_HEAD_DIM]
        vm = memkv_ref[:, MEM_WIDTH + c0:MEM_WIDTH + c0 + MEM_HEAD_DIM]
        s = lax.dot_general(qm_ref[:, c0:c0 + MEM_HEAD_DIM], km, (((1,), (1,)), ((), ())),
                            preferred_element_type=F32) * (MEM_HEAD_DIM ** -0.5)
        p = jnp.exp(s - jnp.max(s, axis=-1, keepdims=True))
        p = (p / jnp.sum(p, axis=-1, keepdims=True)).astype(BF16)
        heads.append(jnp.dot(p, vm, preferred_element_type=F32).astype(BF16))
    om = jnp.concatenate(heads, axis=1)
    merged = merged + gate_ref[:, 2 * D_MODEL:3 * D_MODEL].astype(F32) * jnp.dot(
        om, pmem_ref[...], preferred_element_type=F32)

    y = jnp.dot(merged.astype(BF16), wout_ref[...], preferred_element_type=F32)
    x1 = _layer_norm(ALPHA * xn_ref[...] + y, g1_ref[...], b1_ref[...])
    x1r_ref[...] = _rows_to_tiles(x1)

    lt = lax.dot_general(wr_ref[...], x1.astype(BF16), (((1,), (1,)), ((), ())),
                         preferred_element_type=F32)
    gl = [lt[r:r + 1, :] for r in range(N_EXPERT_GROUPS)]
    gmax = gl[0]
    grp = jnp.zeros((1, tm), I32)
    for r in range(1, N_EXPERT_GROUPS):
        better = gl[r] > gmax
        grp = jnp.where(better, r, grp)
        gmax = jnp.where(better, gl[r], gmax)
    gsum = gl[0] * 0.0
    for r in range(N_EXPERT_GROUPS):
        gsum = gsum + jnp.exp(gl[r] - gmax)
    gp = 1.0 / gsum
    sel = jnp.zeros((EXPERTS_PER_GROUP, tm), F32)
    for r in range(N_EXPERT_GROUPS):
        rows = lt[SUBLANES + r * EXPERTS_PER_GROUP:SUBLANES + (r + 1) * EXPERTS_PER_GROUP, :]
        sel = jnp.where(grp == r, rows, sel)
    ridx = lax.broadcasted_iota(I32, (EXPERTS_PER_GROUP, tm), 0)
    top1 = jnp.max(sel, axis=0, keepdims=True)
    i1 = jnp.min(jnp.where(sel == top1, ridx, EXPERTS_PER_GROUP), axis=0, keepdims=True)
    rest = jnp.where(ridx == i1, -jnp.inf, sel)
    top2 = jnp.max(rest, axis=0, keepdims=True)
    i2 = jnp.min(jnp.where(rest == top2, ridx, EXPERTS_PER_GROUP), axis=0, keepdims=True)
    e2 = jnp.exp(top2 - top1)
    inv = gp / (1.0 + e2)
    eid_ref[0:1, :] = grp * EXPERTS_PER_GROUP + i1
    eid_ref[1:2, :] = grp * EXPERTS_PER_GROUP + i2
    wts_ref[0:1, :] = inv
    wts_ref[1:2, :] = e2 * inv


def _mix_call(xn, u, o, qm, gate, memkv, p, batch, seq):
    tm = TM_MIX
    n_tiles = seq // tm
    t = batch * seq
    hb = tm // POOL_HALO

    row = lambda width: pl.BlockSpec((tm, width), lambda b, i: (b * n_tiles + i, 0))
    prev = pl.BlockSpec((POOL_HALO, POOL_WIDTH),
                        lambda b, i: (jnp.maximum((b * n_tiles + i) * hb - 1, 0), 0))
    nxt = pl.BlockSpec((POOL_HALO, POOL_WIDTH),
                       lambda b, i: (jnp.minimum((b * n_tiles + i + 1) * hb, t // POOL_HALO - 1), 0))
    full = lambda a: pl.BlockSpec(a.shape, lambda b, i: (0,) * a.ndim)
    lane_row = pl.BlockSpec((TOP_K, tm), lambda b, i: (0, b * n_tiles + i))
    weights = (p["w_pool"], p["pool_scale"], p["p_pool"], p["p_attn"], p["p_mem"], p["w_out"],
               p["ln1_g"], p["ln1_b"], p["w_router"])
    return pl.pallas_call(
        functools.partial(_mix_kernel, seq=seq),
        grid=(batch, n_tiles),
        in_specs=[row(D_MODEL), row(POOL_WIDTH), prev, nxt, row(ATTN_WIDTH), row(MEM_WIDTH),
                  row(N_BRANCHES * D_MODEL),
                  pl.BlockSpec((MEM_TOKENS, 2 * MEM_WIDTH), lambda b, i: (b, 0))]
                 + [full(w) for w in weights],
        out_specs=[pl.BlockSpec((tm, ROW_TILE, LANES), lambda b, i: (b * n_tiles + i, 0, 0)),
                   lane_row, lane_row],
        out_shape=[jax.ShapeDtypeStruct((t, ROW_TILE, LANES), F32),
                   jax.ShapeDtypeStruct((TOP_K, t), I32),
                   jax.ShapeDtypeStruct((TOP_K, t), F32)],
        scratch_shapes=[pltpu.VMEM((tm + 2 * POOL_HALO, POOL_WIDTH), F32)],
        compiler_params=_params(dimension_semantics=("arbitrary", "arbitrary")),
        name="mix",
    )(xn, u, u, u, o, qm, gate, memkv, *weights)


def _moe_kernel(tok_ref, dst_ref, iblk_ref, ie_ref, inext_ref, irun_ref, ilo_ref, ihi_ref,
                x1_hbm, ws_ref, wgu_hbm, wd_hbm, y2_hbm,
                xbuf, ybuf, x_bf, h_bf, wgu_f32, wd_f32, wgu_bf, wd_bf, gsem, ssem, wsem, *, n_blocks):
    i = pl.program_id(0)
    bm = xbuf.shape[1]
    lo = ilo_ref[i]
    hi = ihi_ref[i]
    k = iblk_ref[i]
    slot = k & 1
    nonempty = hi > lo

    def gather_start(blk, s):
        def body(r2, c):
            for prio in range(2):
                r = 2 * r2 + prio
                pltpu.make_async_copy(x1_hbm.at[tok_ref[blk * bm + r]], xbuf.at[s, r],
                                      gsem.at[s]).start(priority=prio)
            return c
        lax.fori_loop(0, bm // 2, body, 0, unroll=MOE_ROW_UNROLL // 2)

    def gather_wait(s):
        pltpu.make_async_copy(x1_hbm.at[pl.ds(0, bm)], xbuf.at[s], gsem.at[s]).wait()

    def scatter_start(blk, s):
        def body(r2, c):
            for prio in range(2):
                r = 2 * r2 + prio
                pltpu.make_async_copy(ybuf.at[s, r], y2_hbm.at[dst_ref[blk * bm + r]],
                                      ssem.at[s]).start(priority=prio)
            return c
        lax.fori_loop(0, bm // 2, body, 0, unroll=MOE_ROW_UNROLL // 2)

    def scatter_wait(s):
        pltpu.make_async_copy(ybuf.at[s], y2_hbm.at[pl.ds(0, bm)], ssem.at[s]).wait()

    @pl.when(jnp.logical_and(nonempty, lo == 0))
    def _():
        @pl.when(k == 0)
        def _():
            gather_start(0, 0)
        gather_wait(slot)

        @pl.when(k + 1 < n_blocks)
        def _():
            gather_start(k + 1, 1 - slot)

        @pl.when(k >= 2)
        def _():
            scatter_wait(slot)

    def weight_copies(e, ws):
        return (pltpu.make_async_copy(wgu_hbm.at[e], wgu_f32.at[ws], wsem.at[ws]),
                pltpu.make_async_copy(wd_hbm.at[e], wd_f32.at[ws], wsem.at[ws]))

    @pl.when(jnp.logical_or(i == 0, ie_ref[i] != ie_ref[jnp.maximum(i - 1, 0)]))
    def _():
        e = ie_ref[i]
        wslot = irun_ref[i] & 1

        @pl.when(i == 0)
        def _():
            for c in weight_copies(e, wslot):
                c.start()
        for c in weight_copies(e, wslot):
            c.wait()

        @pl.when(inext_ref[i] >= 0)
        def _():
            for c in weight_copies(inext_ref[i], 1 - wslot):
                c.start()
        wgu_bf[...] = wgu_f32[wslot].astype(BF16)
        wd_bf[...] = wd_f32[wslot].astype(BF16)

    @pl.when(nonempty)
    def _():
        x_bf[...] = _tiles_to_rows(xbuf[slot]).astype(BF16)

    @pl.when(nonempty)
    def _():
        gu = jnp.dot(x_bf[...], wgu_bf[...], preferred_element_type=F32)
        gate = gu[:, :EXPERT_HIDDEN]
        h_bf[...] = ((gate * _sigmoid(gate)) * gu[:, EXPERT_HIDDEN:]).astype(BF16)

    @pl.when(nonempty)
    def _():
        rows = lax.broadcasted_iota(I32, (bm, 1), 0)
        mine = jnp.logical_and(rows >= lo, rows < hi)
        y = jnp.dot(h_bf[...], wd_bf[...], preferred_element_type=F32) * ws_ref[...]
        y = _rows_to_tiles(jnp.where(mine, y, 0.0))

        @pl.when(lo == 0)
        def _():
            ybuf[slot] = y

        @pl.when(lo > 0)
        def _():
            ybuf[slot] = ybuf[slot] + y

    @pl.when(jnp.logical_and(nonempty, hi == bm))
    def _():
        scatter_start(k, slot)

        @pl.when(k == n_blocks - 1)
        def _():
            if n_blocks >= 2:
                scatter_wait(1 - slot)
            scatter_wait(slot)


def _moe_call(x1r, sorted_tok, sorted_dst, wsorted, item_blk, item_e, item_lo, item_hi, w_gu, w_down):
    a = sorted_dst.shape[0]
    bm = BM_MOE
    n_items = item_blk.shape[0]
    change = jnp.concatenate([jnp.zeros((1,), I32), (item_e[1:] != item_e[:-1]).astype(I32)])
    item_run = jnp.cumsum(change).astype(I32)
    first_later = jnp.sum((item_e[None, :] <= item_e[:, None]).astype(I32), axis=1)
    item_next = jnp.where(first_later < n_items, item_e[jnp.minimum(first_later, n_items - 1)], -1).astype(I32)
    grid_spec = pltpu.PrefetchScalarGridSpec(
        num_scalar_prefetch=8,
        grid=(n_items,),
        in_specs=[pl.BlockSpec(memory_space=pl.ANY),
                  pl.BlockSpec((bm, 1), lambda i, tok, dst, blk, *_: (blk[i], 0)),
                  pl.BlockSpec(memory_space=pl.ANY),
                  pl.BlockSpec(memory_space=pl.ANY)],
        out_specs=pl.BlockSpec(memory_space=pl.ANY),
        scratch_shapes=[pltpu.VMEM((2, bm, ROW_TILE, LANES), F32),
                        pltpu.VMEM((2, bm, ROW_TILE, LANES), F32),
                        pltpu.VMEM((bm, D_MODEL), BF16),
                        pltpu.VMEM((bm, EXPERT_HIDDEN), BF16),
                        pltpu.VMEM((2, D_MODEL, 2 * EXPERT_HIDDEN), F32),
                        pltpu.VMEM((2, EXPERT_HIDDEN, D_MODEL), F32),
                        pltpu.VMEM((D_MODEL, 2 * EXPERT_HIDDEN), BF16),
                        pltpu.VMEM((EXPERT_HIDDEN, D_MODEL), BF16),
                        pltpu.SemaphoreType.DMA((2,)), pltpu.SemaphoreType.DMA((2,)),
                        pltpu.SemaphoreType.DMA((2,))],
    )
    return pl.pallas_call(
        functools.partial(_moe_kernel, n_blocks=a // bm),
        grid_spec=grid_spec,
        out_shape=jax.ShapeDtypeStruct((a, ROW_TILE, LANES), F32),
        compiler_params=_params(dimension_semantics=("arbitrary",)),
        name="moe",
    )(sorted_tok, sorted_dst, item_blk, item_e, item_next, item_run, item_lo, item_hi,
      x1r, wsorted, w_gu, w_down)


def _final_kernel(x1_ref, ya_ref, yb_ref, g_ref, b_ref, out_ref):
    moe = ya_ref[...] + yb_ref[...]
    out_ref[...] = _layer_norm(_tiles_to_rows(ALPHA * x1_ref[...] + moe), g_ref[...], b_ref[...])


def _final_call(x1, y2, ln_g, ln_b):
    t = x1.shape[0]
    tm = TM_FINAL
    n_tiles = t // tm
    return pl.pallas_call(
        _final_kernel,
        grid=(n_tiles,),
        in_specs=[pl.BlockSpec((tm, ROW_TILE, LANES), lambda i: (i, 0, 0)),
                  pl.BlockSpec((tm, ROW_TILE, LANES), lambda i: (i, 0, 0)),
                  pl.BlockSpec((tm, ROW_TILE, LANES), lambda i: (n_tiles + i, 0, 0)),
                  pl.BlockSpec((1, D_MODEL), lambda i: (0, 0)),
                  pl.BlockSpec((1, D_MODEL), lambda i: (0, 0))],
        out_specs=pl.BlockSpec((tm, D_MODEL), lambda i: (i, 0)),
        out_shape=jax.ShapeDtypeStruct((t, D_MODEL), F32),
        compiler_params=_params(dimension_semantics=("arbitrary",)),
        name="final",
    )(x1, y2, y2, ln_g, ln_b)


def _t5_bucket(rel):
    nb = N_REL_BUCKETS // 2
    max_exact = nb // 2
    ret = jnp.where(rel > 0, nb, 0)
    n = jnp.abs(rel)
    nf = jnp.maximum(n, 1).astype(F32)
    large = max_exact + (jnp.log(nf / max_exact) / math.log(REL_MAX_DISTANCE / max_exact)
                         * (nb - max_exact)).astype(I32)
    large = jnp.minimum(large, nb - 1)
    return ret + jnp.where(n < max_exact, n, large)


def _bias_tables(rel_table):
    n, m = BLOCK, 3 * BLOCK
    rel = jnp.arange(-(2 * BLOCK - 1), 2 * BLOCK)
    by_rel = jnp.where((jnp.abs(rel) <= WINDOW)[:, None], rel_table[_t5_bucket(rel)].astype(F32), NEG_INF)
    by_rel = jnp.pad(by_rel.T, ((0, 0), (0, 1)))
    skew = jnp.tile(by_rel, (1, n))[:, :n * (m + n - 1)].reshape(N_Q_HEADS, n, m + n - 1)
    bias = skew[:, :, n - 1:n - 1 + m]
    heads = [4 * h + half for h in range(N_KV_HEADS) for half in range(2)]
    units = jnp.concatenate([jnp.stack([bias[a] for a in heads]),
                             jnp.stack([bias[a + 2] for a in heads])], axis=1)
    masked = jnp.full((2 * N_KV_HEADS, 2 * BLOCK, BLOCK), NEG_INF, F32)
    prev = jnp.stack([units[:, :, :BLOCK], masked])
    cur = units[:, :, BLOCK:2 * BLOCK]
    nxt = jnp.stack([units[:, :, 2 * BLOCK:], masked])
    return prev, cur, nxt


def _prepare(rel_bias_table, w_in, b_in, w_pool, pool_scale, p_pool, sink, p_attn, w_mem_kv, p_mem,
             w_out, ln1_g, ln1_b, w_router_group, w_router_expert, w_gu, w_down):
    l = 0
    o1 = POOL_WIDTH
    o2 = o1 + ATTN_WIDTH
    o3 = o2 + KV_WIDTH
    o4 = o3 + KV_WIDTH
    o5 = o4 + MEM_WIDTH
    w_router = jnp.zeros((ROUTER_ROWS, D_MODEL), F32)
    w_router = w_router.at[:N_EXPERT_GROUPS].set(w_router_group[l].T)
    w_router = w_router.at[SUBLANES:].set(w_router_expert[l].T)
    bias_prev, bias_cur, bias_next = _bias_tables(rel_bias_table)
    return dict(
        w_in=w_in[l].astype(BF16), b_in=b_in[l][None, :],
        w_pool=w_pool[l].astype(BF16), pool_scale=pool_scale[l][None, :],
        p_pool=p_pool[l].astype(BF16), p_attn=p_attn[l].astype(BF16), p_mem=p_mem[l].astype(BF16),
        w_out=w_out[l].astype(BF16), w_memkv=w_mem_kv[l].astype(BF16),
        ln1_g=ln1_g[l][None, :], ln1_b=ln1_b[l][None, :],
        w_router=w_router.astype(BF16), sink=sink[l],
        w_gu=w_gu[l], w_down=w_down[l],
        bias_prev=bias_prev, bias_cur=bias_cur, bias_next=bias_next,
    )


def _dispatch_plan(eid, wts):
    t = eid.shape[1]
    a = TOP_K * t
    bm = BM_MOE
    nblk = a // bm
    eflat = eid.reshape(a)
    wflat = wts.reshape(a)
    order = jnp.argsort(eflat, stable=True).astype(I32)
    wsorted = wflat[order][:, None]
    experts = jnp.arange(N_EXPERTS, dtype=I32)
    counts = jnp.sum((eflat[None, :] == experts[:, None]).astype(I32), axis=1)
    ends = jnp.cumsum(counts).astype(I32)
    starts = ends - counts
    cuts = jnp.sort(jnp.concatenate([jnp.arange(nblk, dtype=I32) * bm, starts, jnp.array([a], I32)]))
    lo_abs = cuts[:-1]
    hi_abs = cuts[1:]
    nonempty = hi_abs > lo_abs
    blk = jnp.minimum(lo_abs // bm, nblk - 1)
    probe = jnp.minimum(lo_abs, a - 1)
    item_e = jnp.sum((ends[None, :] <= probe[:, None]).astype(I32), axis=1)
    item_lo = jnp.where(nonempty, lo_abs - blk * bm, 0).astype(I32)
    item_hi = jnp.where(nonempty, hi_abs - blk * bm, 0).astype(I32)
    return order, wsorted, blk.astype(I32), item_e.astype(I32), item_lo, item_hi


def _trunk(x, mem, ln_in_g, ln_in_b, ln2_g, ln2_b, p):
    batch, seq, d = x.shape
    t = batch * seq
    assert seq % TM_MIX == 0 and seq % TQ_ATTN == 0 and t % TM_INPROJ == 0 and (TOP_K * t) % BM_MOE == 0
    memkv = _memkv_call(mem.reshape(batch * MEM_TOKENS, d), p["w_memkv"])
    xn, u, q, kk, vv, qm, gate = _inproj_call(x.reshape(t, d), ln_in_g[None, :], ln_in_b[None, :],
                                              p["w_in"], p["b_in"])
    o = _attn_call(q, kk, vv, p["sink"], p["bias_prev"], p["bias_cur"], p["bias_next"], batch, seq)
    x1r, eid, wts = _mix_call(xn, u, o, qm, gate, memkv, p, batch, seq)
    order, wsorted, item_blk, item_e, item_lo, item_hi = _dispatch_plan(eid, wts)
    sorted_tok = jnp.where(order >= t, order - t, order)
    y2 = _moe_call(x1r, sorted_tok, order, wsorted, item_blk, item_e, item_lo, item_hi, p["w_gu"], p["w_down"])
    out = _final_call(x1r, y2, ln2_g[None, :], ln2_b[None, :])
    return out.reshape(batch, seq, d)


def kernel(x_prompt, x_sample, mem_prompt, mem_sample, ln_in_g, ln_in_b, rel_bias_table, w_in, b_in,
           w_pool, pool_scale, p_pool, sink, p_attn, w_mem_kv, p_mem, w_out, ln1_g, ln1_b,
           w_router_group, w_router_expert, w_gu, w_down, ln2_g, ln2_b):
    p = _prepare(rel_bias_table, w_in, b_in, w_pool, pool_scale, p_pool, sink, p_attn, w_mem_kv, p_mem,
                 w_out, ln1_g, ln1_b, w_router_group, w_router_expert, w_gu, w_down)
    y_prompt = _trunk(x_prompt, mem_prompt, ln_in_g, ln_in_b, ln2_g[0], ln2_b[0], p)
    y_sample = _trunk(x_sample, mem_sample, ln_in_g, ln_in_b, ln2_g[0], ln2_b[0], p)
    return (y_prompt, y_sample)
```

```python
import functools
import math

import jax
import jax.numpy as jnp
from jax import lax
from jax.experimental import pallas as pl
from jax.experimental.pallas import tpu as pltpu

F32 = jnp.float32
BF16 = jnp.bfloat16
I32 = jnp.int32

D_MODEL = 1024
DEPTH = 1
POOL_WIDTH = 512
POOL_WINDOWS = (2, 4, 8, 16)
N_POOL_GROUPS = 4
POOL_GROUP_CH = POOL_WIDTH // N_POOL_GROUPS
N_Q_HEADS = 16
N_KV_HEADS = 4
HEAD_DIM = 64
ATTN_WIDTH = N_Q_HEADS * HEAD_DIM
KV_WIDTH = N_KV_HEADS * HEAD_DIM
WINDOW = 128
BLOCK = 128
N_REL_BUCKETS = 32
REL_MAX_DISTANCE = 128
MEM_TOKENS = 256
MEM_HEADS = 4
MEM_HEAD_DIM = 128
MEM_WIDTH = MEM_HEADS * MEM_HEAD_DIM
N_BRANCHES = 3
N_EXPERT_GROUPS = 4
EXPERTS_PER_GROUP = 8
N_EXPERTS = N_EXPERT_GROUPS * EXPERTS_PER_GROUP
TOP_K = 2
EXPERT_HIDDEN = 512
ALPHA = (2 * DEPTH) ** 0.25
LN_EPS = 1e-5
NEG_INF = -1e30

LANES = 128
SUBLANES = 8
ROW_TILE = D_MODEL // LANES
VMEM_LIMIT_BYTES = 56 * 1024 * 1024

KV_DUP_WIDTH = N_KV_HEADS * LANES

C_U = 0
C_Q = C_U + POOL_WIDTH
C_K = C_Q + ATTN_WIDTH
C_V = C_K + KV_WIDTH
C_M = C_V + KV_WIDTH
C_G = C_M + MEM_WIDTH
C_END = C_G + N_BRANCHES * D_MODEL

TM_INPROJ = 512
TQ_ATTN = 256
TM_MIX = 512
BM_MOE = 512
MOE_ROW_UNROLL = 8
TM_FINAL = 512
POOL_HALO = 8
ROUTER_ROWS = 40


def _layer_norm(x, g, b):
    mu = jnp.mean(x, axis=-1, keepdims=True)
    xc = x - mu
    var = jnp.mean(xc * xc, axis=-1, keepdims=True)
    return xc * lax.rsqrt(var + LN_EPS) * g + b


def _sigmoid(x):
    return 1.0 / (1.0 + jnp.exp(-x))


def _rows_to_tiles(x):
    return pltpu.einshape("r(cl)->rcl", x, c=ROW_TILE)


def _tiles_to_rows(x):
    return pltpu.einshape("rcl->r(cl)", x)


def _params(**kw):
    return pltpu.CompilerParams(vmem_limit_bytes=VMEM_LIMIT_BYTES, **kw)


def _memkv_kernel(mem_ref, w_ref, out_ref):
    out_ref[...] = jnp.dot(mem_ref[...].astype(BF16), w_ref[...],
                           preferred_element_type=F32).astype(BF16)


def _memkv_call(mem2, w_memkv):
    rows = mem2.shape[0]
    return pl.pallas_call(
        _memkv_kernel,
        grid=(rows // MEM_TOKENS,),
        in_specs=[pl.BlockSpec((MEM_TOKENS, D_MODEL), lambda i: (i, 0)),
                  pl.BlockSpec((D_MODEL, 2 * MEM_WIDTH), lambda i: (0, 0))],
        out_specs=pl.BlockSpec((MEM_TOKENS, 2 * MEM_WIDTH), lambda i: (i, 0)),
        out_shape=jax.ShapeDtypeStruct((rows, 2 * MEM_WIDTH), BF16),
        compiler_params=_params(dimension_semantics=("arbitrary",)),
        name="memkv",
    )(mem2, w_memkv)


def _inproj_kernel(x_ref, g_ref, b_ref, w_ref, bias_ref,
                   xn_ref, u_ref, q_ref, k_ref, v_ref, qm_ref, gate_ref):
    xn = _layer_norm(x_ref[...], g_ref[...], b_ref[...])
    xn_ref[...] = xn
    xb = xn.astype(BF16)

    def seg(lo, hi):
        return jnp.dot(xb, w_ref[:, lo:hi], preferred_element_type=F32) + bias_ref[:, lo:hi]

    u_ref[...] = seg(C_U, C_Q)
    q_ref[...] = (seg(C_Q, C_K) * (HEAD_DIM ** -0.5)).astype(BF16)
    kv = seg(C_K, C_M)
    low = lax.broadcasted_iota(I32, (kv.shape[0], LANES), 1) < HEAD_DIM
    for out_ref, c_lo in ((k_ref, 0), (v_ref, KV_WIDTH)):
        for c in range(KV_WIDTH // LANES):
            blk = kv[:, c_lo + c * LANES:c_lo + (c + 1) * LANES]
            rot = pltpu.roll(blk, HEAD_DIM, 1)
            out_ref[:, 2 * c * LANES:(2 * c + 1) * LANES] = jnp.where(low, blk, rot).astype(BF16)
            out_ref[:, (2 * c + 1) * LANES:(2 * c + 2) * LANES] = jnp.where(low, rot, blk).astype(BF16)
    qm_ref[...] = seg(C_M, C_G).astype(BF16)
    for j in range(N_BRANCHES):
        lo = C_G + j * D_MODEL
        gate_ref[:, j * D_MODEL:(j + 1) * D_MODEL] = _sigmoid(seg(lo, lo + D_MODEL)).astype(BF16)


def _inproj_call(x2, ln_g, ln_b, w_in, b_in):
    t = x2.shape[0]
    tm = TM_INPROJ
    row = lambda width: pl.BlockSpec((tm, width), lambda i: (i, 0))
    const = lambda shape: pl.BlockSpec(shape, lambda i: (0, 0))
    widths = (D_MODEL, POOL_WIDTH, ATTN_WIDTH, KV_DUP_WIDTH, KV_DUP_WIDTH, MEM_WIDTH, N_BRANCHES * D_MODEL)
    dtypes = (F32, F32, BF16, BF16, BF16, BF16, BF16)
    return pl.pallas_call(
        _inproj_kernel,
        grid=(t // tm,),
        in_specs=[row(D_MODEL), const((1, D_MODEL)), const((1, D_MODEL)),
                  const((D_MODEL, C_END)), const((1, C_END))],
        out_specs=[row(w) for w in widths],
        out_shape=[jax.ShapeDtypeStruct((t, w), d) for w, d in zip(widths, dtypes)],
        compiler_params=_params(dimension_semantics=("arbitrary",)),
        name="inproj",
    )(x2, ln_g, ln_b, w_in, b_in)


def _attn_kernel(sink_ref, q_ref, kc_ref, kp_ref, kn_ref, vc_ref, vp_ref, vn_ref,
                 bp_ref, bc_ref, bn_ref, o_ref, klo, khi, vlo, vhi, s_scr, p_scr):
    i = pl.program_id(1)
    n_tiles = pl.num_programs(1)
    tq = q_ref.shape[0]
    n_qb = tq // BLOCK

    lane = lax.broadcasted_iota(I32, (BLOCK, KV_DUP_WIDTH), 1)
    low = (lane & (LANES - 1)) < HEAD_DIM

    def put(dst_lo, dst_hi, r0, val):
        zero = jnp.zeros_like(val)
        dst_lo[r0:r0 + BLOCK, :] = jnp.where(low, val, zero)
        dst_hi[r0:r0 + BLOCK, :] = jnp.where(low, zero, val)

    put(klo, khi, 0, kp_ref[...])
    put(vlo, vhi, 0, vp_ref[...])
    for j in range(n_qb):
        put(klo, khi, (j + 1) * BLOCK, kc_ref[j * BLOCK:(j + 1) * BLOCK, :])
        put(vlo, vhi, (j + 1) * BLOCK, vc_ref[j * BLOCK:(j + 1) * BLOCK, :])
    put(klo, khi, (n_qb + 1) * BLOCK, kn_ref[...])
    put(vlo, vhi, (n_qb + 1) * BLOCK, vn_ref[...])

    first = (i == 0).astype(I32)
    last = (i == n_tiles - 1).astype(I32)

    def softmax_rows(s, sink):
        m = jnp.maximum(jnp.max(s, axis=-1, keepdims=True), sink)
        p = jnp.exp(s - m)
        denom = jnp.sum(p, axis=-1, keepdims=True) + jnp.exp(sink - m)
        return (p * (1.0 / denom)).astype(BF16)

    units = [(j, h, half) for j in range(n_qb) for h in range(N_KV_HEADS) for half in range(2)]
    always = i >= 0

    @pl.when(always)
    def _scores():
        for u, (j, h, half) in enumerate(units):
            r0 = j * BLOCK
            c0 = h * 2 * LANES
            pv = first if j == 0 else 0
            nv = last if j == n_qb - 1 else 0
            unit = h * 2 + half
            q_pairs = jnp.concatenate([q_ref[r0:r0 + BLOCK, c0:c0 + LANES],
                                       q_ref[r0:r0 + BLOCK, c0 + LANES:c0 + 2 * LANES]], axis=0)
            kx = (klo, khi)[half][r0:r0 + 3 * BLOCK, h * LANES:(h + 1) * LANES]
            s = lax.dot_general(q_pairs, kx, (((1,), (1,)), ((), ())), preferred_element_type=F32)
            bias = jnp.concatenate([bp_ref[pv, unit], bc_ref[unit], bn_ref[nv, unit]], axis=1)
            s_scr[u] = s + bias

    @pl.when(always)
    def _softmax():
        for u, (j, h, half) in enumerate(units):
            p_scr[u, :BLOCK] = softmax_rows(s_scr[u, :BLOCK], sink_ref[4 * h + half])
            p_scr[u, BLOCK:] = softmax_rows(s_scr[u, BLOCK:], sink_ref[4 * h + 2 + half])

    @pl.when(always)
    def _values():
        for j in range(n_qb):
            r0 = j * BLOCK
            for h in range(N_KV_HEADS):
                c0 = h * 2 * LANES
                acc = jnp.zeros((2 * BLOCK, LANES), F32)
                for half, vref in enumerate((vlo, vhi)):
                    u = units.index((j, h, half))
                    vx = vref[r0:r0 + 3 * BLOCK, h * LANES:(h + 1) * LANES]
                    acc = acc + jnp.dot(p_scr[u], vx, preferred_element_type=F32)
                o_ref[r0:r0 + BLOCK, c0:c0 + LANES] = acc[:BLOCK].astype(BF16)
                o_ref[r0:r0 + BLOCK, c0 + LANES:c0 + 2 * LANES] = acc[BLOCK:].astype(BF16)


def _attn_call(q, kk, vv, sink, bias_prev, bias_cur, bias_next, batch, seq):
    tq = TQ_ATTN
    n_qb = tq // BLOCK
    nb = seq // BLOCK
    n_tiles = seq // tq
    t = batch * seq

    cur = lambda width: pl.BlockSpec((tq, width), lambda b, i: (b * n_tiles + i, 0))
    prev = pl.BlockSpec((BLOCK, KV_DUP_WIDTH), lambda b, i: (b * nb + jnp.maximum(i * n_qb - 1, 0), 0))
    nxt = pl.BlockSpec((BLOCK, KV_DUP_WIDTH), lambda b, i: (b * nb + jnp.minimum((i + 1) * n_qb, nb - 1), 0))
    full = lambda a: pl.BlockSpec(a.shape, lambda b, i: (0,) * a.ndim)
    ext = (n_qb + 2) * BLOCK
    n_units = n_qb * N_KV_HEADS * 2
    return pl.pallas_call(
        _attn_kernel,
        grid=(batch, n_tiles),
        in_specs=[pl.BlockSpec(memory_space=pltpu.SMEM),
                  cur(ATTN_WIDTH), cur(KV_DUP_WIDTH), prev, nxt, cur(KV_DUP_WIDTH), prev, nxt,
                  full(bias_prev), full(bias_cur), full(bias_next)],
        out_specs=cur(ATTN_WIDTH),
        out_shape=jax.ShapeDtypeStruct((t, ATTN_WIDTH), BF16),
        scratch_shapes=[pltpu.VMEM((ext, KV_DUP_WIDTH), BF16) for _ in range(4)]
                       + [pltpu.VMEM((n_units, 2 * BLOCK, 3 * BLOCK), F32),
                          pltpu.VMEM((n_units, 2 * BLOCK, 3 * BLOCK), BF16)],
        compiler_params=_params(dimension_semantics=("arbitrary", "arbitrary")),
        name="attn",
    )(sink, q, kk, kk, kk, vv, vv, vv, bias_prev, bias_cur, bias_next)


def _mix_kernel(xn_ref, u_ref, up_ref, un_ref, o_ref, qm_ref, gate_ref, memkv_ref,
                wpool_ref, pscale_ref, ppool_ref, pattn_ref, pmem_ref, wout_ref,
                g1_ref, b1_ref, wr_ref,
                x1r_ref, eid_ref, wts_ref, uext, *, seq):
    i = pl.program_id(1)
    n_tiles = pl.num_programs(1)
    tm = xn_ref.shape[0]

    halo = POOL_HALO
    uext[0:halo, :] = jnp.where(i > 0, up_ref[...], 0.0)
    uext[halo:halo + tm, :] = u_ref[...]
    uext[halo + tm:halo + tm + halo, :] = jnp.where(i < n_tiles - 1, un_ref[...], 0.0)
    pos = i * tm + lax.broadcasted_iota(I32, (tm, 1), 0)
    mixed = []
    for gi, win in enumerate(POOL_WINDOWS):
        c0 = gi * POOL_GROUP_CH
        half = win // 2
        total = jnp.zeros((tm, POOL_GROUP_CH), F32)
        for off in range(-half, half):
            total = total + uext[halo + off:halo + off + tm, c0:c0 + POOL_GROUP_CH]
        cnt = (jnp.minimum(pos + half, seq) - jnp.maximum(pos - half, 0)).astype(F32)
        pooled = total * (1.0 / cnt) - u_ref[:, c0:c0 + POOL_GROUP_CH]
        mixed.append(jnp.dot(pooled.astype(BF16), wpool_ref[gi], preferred_element_type=F32)
                     * pscale_ref[:, c0:c0 + POOL_GROUP_CH])
    mixed = jnp.concatenate(mixed, axis=1).astype(BF16)
    merged = gate_ref[:, 0:D_MODEL].astype(F32) * jnp.dot(
        mixed, ppool_ref[...], preferred_element_type=F32)

    merged = merged + gate_ref[:, D_MODEL:2 * D_MODEL].astype(F32) * jnp.dot(
        o_ref[...], pattn_ref[...], preferred_element_type=F32)

    heads = []
    for h in range(MEM_HEADS):
        c0 = h * MEM_HEAD_DIM
        km = memkv_ref[:, c0:c0 + MEM_HEAD_DIM]
        vm = memkv_ref[:, MEM_WIDTH + c0:MEM_WIDTH + c0 + MEM_HEAD_DIM]
        s = lax.dot_general(qm_ref[:, c0:c0 + MEM_HEAD_DIM], km, (((1,), (1,)), ((), ())),
                            preferred_element_type=F32) * (MEM_HEAD_DIM ** -0.5)
        p = jnp.exp(s - jnp.max(s, axis=-1, keepdims=True))
        p = (p / jnp.sum(p, axis=-1, keepdims=True)).astype(BF16)
        heads.append(jnp.dot(p, vm, preferred_element_type=F32).astype(BF16))
    om = jnp.concatenate(heads, axis=1)
    merged = merged + gate_ref[:, 2 * D_MODEL:3 * D_MODEL].astype(F32) * jnp.dot(
        om, pmem_ref[...], preferred_element_type=F32)

    y = jnp.dot(merged.astype(BF16), wout_ref[...], preferred_element_type=F32)
    x1 = _layer_norm(ALPHA * xn_ref[...] + y, g1_ref[...], b1_ref[...])
    x1r_ref[...] = _rows_to_tiles(x1)

    lt = lax.dot_general(wr_ref[...], x1.astype(BF16), (((1,), (1,)), ((), ())),
                         preferred_element_type=F32)
    gl = [lt[r:r + 1, :] for r in range(N_EXPERT_GROUPS)]
    gmax = gl[0]
    grp = jnp.zeros((1, tm), I32)
    for r in range(1, N_EXPERT_GROUPS):
        better = gl[r] > gmax
        grp = jnp.where(better, r, grp)
        gmax = jnp.where(better, gl[r], gmax)
    gsum = gl[0] * 0.0
    for r in range(N_EXPERT_GROUPS):
        gsum = gsum + jnp.exp(gl[r] - gmax)
    gp = 1.0 / gsum
    sel = jnp.zeros((EXPERTS_PER_GROUP, tm), F32)
    for r in range(N_EXPERT_GROUPS):
        rows = lt[SUBLANES + r * EXPERTS_PER_GROUP:SUBLANES + (r + 1) * EXPERTS_PER_GROUP, :]
        sel = jnp.where(grp == r, rows, sel)
    ridx = lax.broadcasted_iota(I32, (EXPERTS_PER_GROUP, tm), 0)
    top1 = jnp.max(sel, axis=0, keepdims=True)
    i1 = jnp.min(jnp.where(sel == top1, ridx, EXPERTS_PER_GROUP), axis=0, keepdims=True)
    rest = jnp.where(ridx == i1, -jnp.inf, sel)
    top2 = jnp.max(rest, axis=0, keepdims=True)
    i2 = jnp.min(jnp.where(rest == top2, ridx, EXPERTS_PER_GROUP), axis=0, keepdims=True)
    e2 = jnp.exp(top2 - top1)
    inv = gp / (1.0 + e2)
    eid_ref[0:1, :] = grp * EXPERTS_PER_GROUP + i1
    eid_ref[1:2, :] = grp * EXPERTS_PER_GROUP + i2
    wts_ref[0:1, :] = inv
    wts_ref[1:2, :] = e2 * inv


def _mix_call(xn, u, o, qm, gate, memkv, p, batch, seq):
    tm = TM_MIX
    n_tiles = seq // tm
    t = batch * seq
    hb = tm // POOL_HALO

    row = lambda width: pl.BlockSpec((tm, width), lambda b, i: (b * n_tiles + i, 0))
    prev = pl.BlockSpec((POOL_HALO, POOL_WIDTH),
                        lambda b, i: (jnp.maximum((b * n_tiles + i) * hb - 1, 0), 0))
    nxt = pl.BlockSpec((POOL_HALO, POOL_WIDTH),
                       lambda b, i: (jnp.minimum((b * n_tiles + i + 1) * hb, t // POOL_HALO - 1), 0))
    full = lambda a: pl.BlockSpec(a.shape, lambda b, i: (0,) * a.ndim)
    lane_row = pl.BlockSpec((TOP_K, tm), lambda b, i: (0, b * n_tiles + i))
    weights = (p["w_pool"], p["pool_scale"], p["p_pool"], p["p_attn"], p["p_mem"], p["w_out"],
               p["ln1_g"], p["ln1_b"], p["w_router"])
    return pl.pallas_call(
        functools.partial(_mix_kernel, seq=seq),
        grid=(batch, n_tiles),
        in_specs=[row(D_MODEL), row(POOL_WIDTH), prev, nxt, row(ATTN_WIDTH), row(MEM_WIDTH),
                  row(N_BRANCHES * D_MODEL),
                  pl.BlockSpec((MEM_TOKENS, 2 * MEM_WIDTH), lambda b, i: (b, 0))]
                 + [full(w) for w in weights],
        out_specs=[pl.BlockSpec((tm, ROW_TILE, LANES), lambda b, i: (b * n_tiles + i, 0, 0)),
                   lane_row, lane_row],
        out_shape=[jax.ShapeDtypeStruct((t, ROW_TILE, LANES), F32),
                   jax.ShapeDtypeStruct((TOP_K, t), I32),
                   jax.ShapeDtypeStruct((TOP_K, t), F32)],
        scratch_shapes=[pltpu.VMEM((tm + 2 * POOL_HALO, POOL_WIDTH), F32)],
        compiler_params=_params(dimension_semantics=("arbitrary", "arbitrary")),
        name="mix",
    )(xn, u, u, u, o, qm, gate, memkv, *weights)


def _moe_kernel(tok_ref, dst_ref, iblk_ref, ie_ref, inext_ref, irun_ref, ilo_ref, ihi_ref,
                x1_hbm, ws_ref, wgu_hbm, wd_hbm, y2_hbm,
                xbuf, ybuf, x_bf, h_bf, wgu_f32, wd_f32, wgu_bf, wd_bf, gsem, ssem, wsem, *, n_blocks):
    i = pl.program_id(0)
    bm = xbuf.shape[1]
    lo = ilo_ref[i]
    hi = ihi_ref[i]
    k = iblk_ref[i]
    slot = k & 1
    nonempty = hi > lo

    def gather_start(blk, s):
        def body(r2, c):
            for prio in range(2):
                r = 2 * r2 + prio
                pltpu.make_async_copy(x1_hbm.at[tok_ref[blk * bm + r]], xbuf.at[s, r],
                                      gsem.at[s]).start(priority=prio)
            return c
        lax.fori_loop(0, bm // 2, body, 0, unroll=MOE_ROW_UNROLL // 2)

    def gather_wait(s):
        pltpu.make_async_copy(x1_hbm.at[pl.ds(0, bm)], xbuf.at[s], gsem.at[s]).wait()

    def scatter_start(blk, s):
        def body(r2, c):
            for prio in range(2):
                r = 2 * r2 + prio
                pltpu.make_async_copy(ybuf.at[s, r], y2_hbm.at[dst_ref[blk * bm + r]],
                                      ssem.at[s]).start(priority=prio)
            return c
        lax.fori_loop(0, bm // 2, body, 0, unroll=MOE_ROW_UNROLL // 2)

    def scatter_wait(s):
        pltpu.make_async_copy(ybuf.at[s], y2_hbm.at[pl.ds(0, bm)], ssem.at[s]).wait()

    @pl.when(jnp.logical_and(nonempty, lo == 0))
    def _():
        @pl.when(k == 0)
        def _():
            gather_start(0, 0)
        gather_wait(slot)

        @pl.when(k + 1 < n_blocks)
        def _():
            gather_start(k + 1, 1 - slot)

        @pl.when(k >= 2)
        def _():
            scatter_wait(slot)

    def weight_copies(e, ws):
        return (pltpu.make_async_copy(wgu_hbm.at[e], wgu_f32.at[ws], wsem.at[ws]),
                pltpu.make_async_copy(wd_hbm.at[e], wd_f32.at[ws], wsem.at[ws]))

    @pl.when(jnp.logical_or(i == 0, ie_ref[i] != ie_ref[jnp.maximum(i - 1, 0)]))
    def _():
        e = ie_ref[i]
        wslot = irun_ref[i] & 1

        @pl.when(i == 0)
        def _():
            for c in weight_copies(e, wslot):
                c.start()
        for c in weight_copies(e, wslot):
            c.wait()

        @pl.when(inext_ref[i] >= 0)
        def _():
            for c in weight_copies(inext_ref[i], 1 - wslot):
                c.start()
        wgu_bf[...] = wgu_f32[wslot].astype(BF16)
        wd_bf[...] = wd_f32[wslot].astype(BF16)

    @pl.when(nonempty)
    def _():
        x_bf[...] = _tiles_to_rows(xbuf[slot]).astype(BF16)

    @pl.when(nonempty)
    def _():
        gu = jnp.dot(x_bf[...], wgu_bf[...], preferred_element_type=F32)
        gate = gu[:, :EXPERT_HIDDEN]
        h_bf[...] = ((gate * _sigmoid(gate)) * gu[:, EXPERT_HIDDEN:]).astype(BF16)

    @pl.when(nonempty)
    def _():
        rows = lax.broadcasted_iota(I32, (bm, 1), 0)
        mine = jnp.logical_and(rows >= lo, rows < hi)
        y = jnp.dot(h_bf[...], wd_bf[...], preferred_element_type=F32) * ws_ref[...]
        y = _rows_to_tiles(jnp.where(mine, y, 0.0))

        @pl.when(lo == 0)
        def _():
            ybuf[slot] = y

        @pl.when(lo > 0)
        def _():
            ybuf[slot] = ybuf[slot] + y

    @pl.when(jnp.logical_and(nonempty, hi == bm))
    def _():
        scatter_start(k, slot)

        @pl.when(k == n_blocks - 1)
        def _():
            if n_blocks >= 2:
                scatter_wait(1 - slot)
            scatter_wait(slot)


def _moe_call(x1r, sorted_tok, sorted_dst, wsorted, item_blk, item_e, item_lo, item_hi, w_gu, w_down):
    a = sorted_dst.shape[0]
    bm = BM_MOE
    n_items = item_blk.shape[0]
    change = jnp.concatenate([jnp.zeros((1,), I32), (item_e[1:] != item_e[:-1]).astype(I32)])
    item_run = jnp.cumsum(change).astype(I32)
    first_later = jnp.sum((item_e[None, :] <= item_e[:, None]).astype(I32), axis=1)
    item_next = jnp.where(first_later < n_items, item_e[jnp.minimum(first_later, n_items - 1)], -1).astype(I32)
    grid_spec = pltpu.PrefetchScalarGridSpec(
        num_scalar_prefetch=8,
        grid=(n_items,),
        in_specs=[pl.BlockSpec(memory_space=pl.ANY),
                  pl.BlockSpec((bm, 1), lambda i, tok, dst, blk, *_: (blk[i], 0)),
                  pl.BlockSpec(memory_space=pl.ANY),
                  pl.BlockSpec(memory_space=pl.ANY)],
        out_specs=pl.BlockSpec(memory_space=pl.ANY),
        scratch_shapes=[pltpu.VMEM((2, bm, ROW_TILE, LANES), F32),
                        pltpu.VMEM((2, bm, ROW_TILE, LANES), F32),
                        pltpu.VMEM((bm, D_MODEL), BF16),
                        pltpu.VMEM((bm, EXPERT_HIDDEN), BF16),
                        pltpu.VMEM((2, D_MODEL, 2 * EXPERT_HIDDEN), F32),
                        pltpu.VMEM((2, EXPERT_HIDDEN, D_MODEL), F32),
                        pltpu.VMEM((D_MODEL, 2 * EXPERT_HIDDEN), BF16),
                        pltpu.VMEM((EXPERT_HIDDEN, D_MODEL), BF16),
                        pltpu.SemaphoreType.DMA((2,)), pltpu.SemaphoreType.DMA((2,)),
                        pltpu.SemaphoreType.DMA((2,))],
    )
    return pl.pallas_call(
        functools.partial(_moe_kernel, n_blocks=a // bm),
        grid_spec=grid_spec,
        out_shape=jax.ShapeDtypeStruct((a, ROW_TILE, LANES), F32),
        compiler_params=_params(dimension_semantics=("arbitrary",)),
        name="moe",
    )(sorted_tok, sorted_dst, item_blk, item_e, item_next, item_run, item_lo, item_hi,
      x1r, wsorted, w_gu, w_down)


def _final_kernel(x1_ref, ya_ref, yb_ref, g_ref, b_ref, out_ref):
    moe = ya_ref[...] + yb_ref[...]
    out_ref[...] = _layer_norm(_tiles_to_rows(ALPHA * x1_ref[...] + moe), g_ref[...], b_ref[...])


def _final_call(x1, y2, ln_g, ln_b):
    t = x1.shape[0]
    tm = TM_FINAL
    n_tiles = t // tm
    return pl.pallas_call(
        _final_kernel,
        grid=(n_tiles,),
        in_specs=[pl.BlockSpec((tm, ROW_TILE, LANES), lambda i: (i, 0, 0)),
                  pl.BlockSpec((tm, ROW_TILE, LANES), lambda i: (i, 0, 0)),
                  pl.BlockSpec((tm, ROW_TILE, LANES), lambda i: (n_tiles + i, 0, 0)),
                  pl.BlockSpec((1, D_MODEL), lambda i: (0, 0)),
                  pl.BlockSpec((1, D_MODEL), lambda i: (0, 0))],
        out_specs=pl.BlockSpec((tm, D_MODEL), lambda i: (i, 0)),
        out_shape=jax.ShapeDtypeStruct((t, D_MODEL), F32),
        compiler_params=_params(dimension_semantics=("arbitrary",)),
        name="final",
    )(x1, y2, y2, ln_g, ln_b)


def _t5_bucket(rel):
    nb = N_REL_BUCKETS // 2
    max_exact = nb // 2
    ret = jnp.where(rel > 0, nb, 0)
    n = jnp.abs(rel)
    nf = jnp.maximum(n, 1).astype(F32)
    large = max_exact + (jnp.log(nf / max_exact) / math.log(REL_MAX_DISTANCE / max_exact)
                         * (nb - max_exact)).astype(I32)
    large = jnp.minimum(large, nb - 1)
    return ret + jnp.where(n < max_exact, n, large)


def _bias_tables(rel_table):
    n, m = BLOCK, 3 * BLOCK
    rel = jnp.arange(-(2 * BLOCK - 1), 2 * BLOCK)
    by_rel = jnp.where((jnp.abs(rel) <= WINDOW)[:, None], rel_table[_t5_bucket(rel)].astype(F32), NEG_INF)
    by_rel = jnp.pad(by_rel.T, ((0, 0), (0, 1)))
    skew = jnp.tile(by_rel, (1, n))[:, :n * (m + n - 1)].reshape(N_Q_HEADS, n, m + n - 1)
    bias = skew[:, :, n - 1:n - 1 + m]
    heads = [4 * h + half for h in range(N_KV_HEADS) for half in range(2)]
    units = jnp.concatenate([jnp.stack([bias[a] for a in heads]),
                             jnp.stack([bias[a + 2] for a in heads])], axis=1)
    masked = jnp.full((2 * N_KV_HEADS, 2 * BLOCK, BLOCK), NEG_INF, F32)
    prev = jnp.stack([units[:, :, :BLOCK], masked])
    cur = units[:, :, BLOCK:2 * BLOCK]
    nxt = jnp.stack([units[:, :, 2 * BLOCK:], masked])
    return prev, cur, nxt


def _prepare(rel_bias_table, w_in, b_in, w_pool, pool_scale, p_pool, sink, p_attn, w_mem_kv, p_mem,
             w_out, ln1_g, ln1_b, w_router_group, w_router_expert, w_gu, w_down):
    l = 0
    o1 = POOL_WIDTH
    o2 = o1 + ATTN_WIDTH
    o3 = o2 + KV_WIDTH
    o4 = o3 + KV_WIDTH
    o5 = o4 + MEM_WIDTH
    w_router = jnp.zeros((ROUTER_ROWS, D_MODEL), F32)
    w_router = w_router.at[:N_EXPERT_GROUPS].set(w_router_group[l].T)
    w_router = w_router.at[SUBLANES:].set(w_router_expert[l].T)
    bias_prev, bias_cur, bias_next = _bias_tables(rel_bias_table)
    return dict(
        w_in=w_in[l].astype(BF16), b_in=b_in[l][None, :],
        w_pool=w_pool[l].astype(BF16), pool_scale=pool_scale[l][None, :],
        p_pool=p_pool[l].astype(BF16), p_attn=p_attn[l].astype(BF16), p_mem=p_mem[l].astype(BF16),
        w_out=w_out[l].astype(BF16), w_memkv=w_mem_kv[l].astype(BF16),
        ln1_g=ln1_g[l][None, :], ln1_b=ln1_b[l][None, :],
        w_router=w_router.astype(BF16), sink=sink[l],
        w_gu=w_gu[l], w_down=w_down[l],
        bias_prev=bias_prev, bias_cur=bias_cur, bias_next=bias_next,
    )


def _dispatch_plan(eid, wts):
    t = eid.shape[1]
    a = TOP_K * t
    bm = BM_MOE
    nblk = a // bm
    eflat = eid.reshape(a)
    wflat = wts.reshape(a)
    order = jnp.argsort(eflat, stable=True).astype(I32)
    wsorted = wflat[order][:, None]
    experts = jnp.arange(N_EXPERTS, dtype=I32)
    counts = jnp.sum((eflat[None, :] == experts[:, None]).astype(I32), axis=1)
    ends = jnp.cumsum(counts).astype(I32)
    starts = ends - counts
    cuts = jnp.sort(jnp.concatenate([jnp.arange(nblk, dtype=I32) * bm, starts, jnp.array([a], I32)]))
    lo_abs = cuts[:-1]
    hi_abs = cuts[1:]
    nonempty = hi_abs > lo_abs
    blk = jnp.minimum(lo_abs // bm, nblk - 1)
    probe = jnp.minimum(lo_abs, a - 1)
    item_e = jnp.sum((ends[None, :] <= probe[:, None]).astype(I32), axis=1)
    item_lo = jnp.where(nonempty, lo_abs - blk * bm, 0).astype(I32)
    item_hi = jnp.where(nonempty, hi_abs - blk * bm, 0).astype(I32)
    return order, wsorted, blk.astype(I32), item_e.astype(I32), item_lo, item_hi


def _trunk(x, mem, ln_in_g, ln_in_b, ln2_g, ln2_b, p):
    batch, seq, d = x.shape
    t = batch * seq
    assert seq % TM_MIX == 0 and seq % TQ_ATTN == 0 and t % TM_INPROJ == 0 and (TOP_K * t) % BM_MOE == 0
    memkv = _memkv_call(mem.reshape(batch * MEM_TOKENS, d), p["w_memkv"])
    xn, u, q, kk, vv, qm, gate = _inproj_call(x.reshape(t, d), ln_in_g[None, :], ln_in_b[None, :],
                                              p["w_in"], p["b_in"])
    o = _attn_call(q, kk, vv, p["sink"], p["bias_prev"], p["bias_cur"], p["bias_next"], batch, seq)
    x1r, eid, wts = _mix_call(xn, u, o, qm, gate, memkv, p, batch, seq)
    order, wsorted, item_blk, item_e, item_lo, item_hi = _dispatch_plan(eid, wts)
    sorted_tok = jnp.where(order >= t, order - t, order)
    y2 = _moe_call(x1r, sorted_tok, order, wsorted, item_blk, item_e, item_lo, item_hi, p["w_gu"], p["w_down"])
    out = _final_call(x1r, y2, ln2_g[None, :], ln2_b[None, :])
    return out.reshape(batch, seq, d)


def kernel(x_prompt, x_sample, mem_prompt, mem_sample, ln_in_g, ln_in_b, rel_bias_table, w_in, b_in,
           w_pool, pool_scale, p_pool, sink, p_attn, w_mem_kv, p_mem, w_out, ln1_g, ln1_b,
           w_router_group, w_router_expert, w_gu, w_down, ln2_g, ln2_b):
    p = _prepare(rel_bias_table, w_in, b_in, w_pool, pool_scale, p_pool, sink, p_attn, w_mem_kv, p_mem,
                 w_out, ln1_g, ln1_b, w_router_group, w_router_expert, w_gu, w_down)
    y_prompt = _trunk(x_prompt, mem_prompt, ln_in_g, ln_in_b, ln2_g[0], ln2_b[0], p)
    y_sample = _trunk(x_sample, mem_sample, ln_in_g, ln_in_b, ln2_g[0], ln2_b[0], p)
    return (y_prompt, y_sample)
```

```python
import functools
import math

import jax
import jax.numpy as jnp
from jax import lax
from jax.experimental import pallas as pl
from jax.experimental.pallas import tpu as pltpu
from jax.experimental.pallas import tpu_sc as plsc

F32 = jnp.float32
BF16 = jnp.bfloat16
I32 = jnp.int32

D_MODEL = 1024
DEPTH = 1
POOL_WIDTH = 512
POOL_WINDOWS = (2, 4, 8, 16)
N_POOL_GROUPS = 4
POOL_GROUP_CH = POOL_WIDTH // N_POOL_GROUPS
N_Q_HEADS = 16
N_KV_HEADS = 4
HEAD_DIM = 64
ATTN_WIDTH = N_Q_HEADS * HEAD_DIM
KV_WIDTH = N_KV_HEADS * HEAD_DIM
WINDOW = 128
BLOCK = 128
N_REL_BUCKETS = 32
REL_MAX_DISTANCE = 128
MEM_TOKENS = 256
MEM_HEADS = 4
MEM_HEAD_DIM = 128
MEM_WIDTH = MEM_HEADS * MEM_HEAD_DIM
N_BRANCHES = 3
N_EXPERT_GROUPS = 4
EXPERTS_PER_GROUP = 8
N_EXPERTS = N_EXPERT_GROUPS * EXPERTS_PER_GROUP
TOP_K = 2
EXPERT_HIDDEN = 512
ALPHA = (2 * DEPTH) ** 0.25
LN_EPS = 1e-5
NEG_INF = -1e30

LANES = 128
SUBLANES = 8
ROW_TILE = D_MODEL // LANES
VMEM_LIMIT_BYTES = 56 * 1024 * 1024

KV_DUP_WIDTH = N_KV_HEADS * LANES

C_U = 0
C_Q = C_U + POOL_WIDTH
C_K = C_Q + ATTN_WIDTH
C_V = C_K + KV_WIDTH
C_M = C_V + KV_WIDTH
C_G = C_M + MEM_WIDTH
C_END = C_G + N_BRANCHES * D_MODEL

TM_INPROJ = 512
TQ_ATTN = 256
TM_MIX = 512
BM_MOE = 256
SC_ROW_WINDOW = 32
TM_FINAL = 512
POOL_HALO = 8
ROUTER_ROWS = 40


def _layer_norm(x, g, b):
    mu = jnp.mean(x, axis=-1, keepdims=True)
    xc = x - mu
    var = jnp.mean(xc * xc, axis=-1, keepdims=True)
    return xc * lax.rsqrt(var + LN_EPS) * g + b


def _sigmoid(x):
    return 1.0 / (1.0 + jnp.exp(-x))


def _rows_to_tiles(x):
    return pltpu.einshape("r(cl)->rcl", x, c=ROW_TILE)


def _tiles_to_rows(x):
    return pltpu.einshape("rcl->r(cl)", x)


def _params(**kw):
    return pltpu.CompilerParams(vmem_limit_bytes=VMEM_LIMIT_BYTES, **kw)


def _memkv_kernel(mem_ref, w_ref, out_ref):
    out_ref[...] = jnp.dot(mem_ref[...].astype(BF16), w_ref[...],
                           preferred_element_type=F32).astype(BF16)


def _memkv_call(mem2, w_memkv):
    rows = mem2.shape[0]
    return pl.pallas_call(
        _memkv_kernel,
        grid=(rows // MEM_TOKENS,),
        in_specs=[pl.BlockSpec((MEM_TOKENS, D_MODEL), lambda i: (i, 0)),
                  pl.BlockSpec((D_MODEL, 2 * MEM_WIDTH), lambda i: (0, 0))],
        out_specs=pl.BlockSpec((MEM_TOKENS, 2 * MEM_WIDTH), lambda i: (i, 0)),
        out_shape=jax.ShapeDtypeStruct((rows, 2 * MEM_WIDTH), BF16),
        compiler_params=_params(dimension_semantics=("arbitrary",)),
        name="memkv",
    )(mem2, w_memkv)


def _inproj_kernel(x_ref, g_ref, b_ref, w_ref, bias_ref,
                   xn_ref, u_ref, q_ref, k_ref, v_ref, qm_ref, gate_ref):
    xn = _layer_norm(x_ref[...], g_ref[...], b_ref[...])
    xn_ref[...] = xn
    xb = xn.astype(BF16)

    def seg(lo, hi):
        return jnp.dot(xb, w_ref[:, lo:hi], preferred_element_type=F32) + bias_ref[:, lo:hi]

    u_ref[...] = seg(C_U, C_Q)
    q_ref[...] = (seg(C_Q, C_K) * (HEAD_DIM ** -0.5)).astype(BF16)
    kv = seg(C_K, C_M)
    low = lax.broadcasted_iota(I32, (kv.shape[0], LANES), 1) < HEAD_DIM
    for out_ref, c_lo in ((k_ref, 0), (v_ref, KV_WIDTH)):
        for c in range(KV_WIDTH // LANES):
            blk = kv[:, c_lo + c * LANES:c_lo + (c + 1) * LANES]
            rot = pltpu.roll(blk, HEAD_DIM, 1)
            out_ref[:, 2 * c * LANES:(2 * c + 1) * LANES] = jnp.where(low, blk, rot).astype(BF16)
            out_ref[:, (2 * c + 1) * LANES:(2 * c + 2) * LANES] = jnp.where(low, rot, blk).astype(BF16)
    qm_ref[...] = seg(C_M, C_G).astype(BF16)
    for j in range(N_BRANCHES):
        lo = C_G + j * D_MODEL
        gate_ref[:, j * D_MODEL:(j + 1) * D_MODEL] = _sigmoid(seg(lo, lo + D_MODEL)).astype(BF16)


def _inproj_call(x2, ln_g, ln_b, w_in, b_in):
    t = x2.shape[0]
    tm = TM_INPROJ
    row = lambda width: pl.BlockSpec((tm, width), lambda i: (i, 0))
    const = lambda shape: pl.BlockSpec(shape, lambda i: (0, 0))
    widths = (D_MODEL, POOL_WIDTH, ATTN_WIDTH, KV_DUP_WIDTH, KV_DUP_WIDTH, MEM_WIDTH, N_BRANCHES * D_MODEL)
    dtypes = (F32, F32, BF16, BF16, BF16, BF16, BF16)
    return pl.pallas_call(
        _inproj_kernel,
        grid=(t // tm,),
        in_specs=[row(D_MODEL), const((1, D_MODEL)), const((1, D_MODEL)),
                  const((D_MODEL, C_END)), const((1, C_END))],
        out_specs=[row(w) for w in widths],
        out_shape=[jax.ShapeDtypeStruct((t, w), d) for w, d in zip(widths, dtypes)],
        compiler_params=_params(dimension_semantics=("arbitrary",)),
        name="inproj",
    )(x2, ln_g, ln_b, w_in, b_in)


def _attn_kernel(sink_ref, q_ref, kc_ref, kp_ref, kn_ref, vc_ref, vp_ref, vn_ref,
                 bp_ref, bc_ref, bn_ref, o_ref, klo, khi, vlo, vhi, s_scr, p_scr):
    i = pl.program_id(1)
    n_tiles = pl.num_programs(1)
    tq = q_ref.shape[0]
    n_qb = tq // BLOCK

    lane = lax.broadcasted_iota(I32, (BLOCK, KV_DUP_WIDTH), 1)
    low = (lane & (LANES - 1)) < HEAD_DIM

    def put(dst_lo, dst_hi, r0, val):
        zero = jnp.zeros_like(val)
        dst_lo[r0:r0 + BLOCK, :] = jnp.where(low, val, zero)
        dst_hi[r0:r0 + BLOCK, :] = jnp.where(low, zero, val)

    put(klo, khi, 0, kp_ref[...])
    put(vlo, vhi, 0, vp_ref[...])
    for j in range(n_qb):
        put(klo, khi, (j + 1) * BLOCK, kc_ref[j * BLOCK:(j + 1) * BLOCK, :])
        put(vlo, vhi, (j + 1) * BLOCK, vc_ref[j * BLOCK:(j + 1) * BLOCK, :])
    put(klo, khi, (n_qb + 1) * BLOCK, kn_ref[...])
    put(vlo, vhi, (n_qb + 1) * BLOCK, vn_ref[...])

    first = (i == 0).astype(I32)
    last = (i == n_tiles - 1).astype(I32)

    def softmax_rows(s, sink):
        m = jnp.maximum(jnp.max(s, axis=-1, keepdims=True), sink)
        p = jnp.exp(s - m)
        denom = jnp.sum(p, axis=-1, keepdims=True) + jnp.exp(sink - m)
        return (p * (1.0 / denom)).astype(BF16)

    units = [(j, h, half) for j in range(n_qb) for h in range(N_KV_HEADS) for half in range(2)]
    always = i >= 0

    @pl.when(always)
    def _scores():
        for u, (j, h, half) in enumerate(units):
            r0 = j * BLOCK
            c0 = h * 2 * LANES
            pv = first if j == 0 else 0
            nv = last if j == n_qb - 1 else 0
            unit = h * 2 + half
            q_pairs = jnp.concatenate([q_ref[r0:r0 + BLOCK, c0:c0 + LANES],
                                       q_ref[r0:r0 + BLOCK, c0 + LANES:c0 + 2 * LANES]], axis=0)
            kx = (klo, khi)[half][r0:r0 + 3 * BLOCK, h * LANES:(h + 1) * LANES]
            s = lax.dot_general(q_pairs, kx, (((1,), (1,)), ((), ())), preferred_element_type=F32)
            bias = jnp.concatenate([bp_ref[pv, unit], bc_ref[unit], bn_ref[nv, unit]], axis=1)
            s_scr[u] = s + bias

    @pl.when(always)
    def _softmax():
        for u, (j, h, half) in enumerate(units):
            p_scr[u, :BLOCK] = softmax_rows(s_scr[u, :BLOCK], sink_ref[4 * h + half])
            p_scr[u, BLOCK:] = softmax_rows(s_scr[u, BLOCK:], sink_ref[4 * h + 2 + half])

    @pl.when(always)
    def _values():
        for j in range(n_qb):
            r0 = j * BLOCK
            for h in range(N_KV_HEADS):
                c0 = h * 2 * LANES
                acc = jnp.zeros((2 * BLOCK, LANES), F32)
                for half, vref in enumerate((vlo, vhi)):
                    u = units.index((j, h, half))
                    vx = vref[r0:r0 + 3 * BLOCK, h * LANES:(h + 1) * LANES]
                    acc = acc + jnp.dot(p_scr[u], vx, preferred_element_type=F32)
                o_ref[r0:r0 + BLOCK, c0:c0 + LANES] = acc[:BLOCK].astype(BF16)
                o_ref[r0:r0 + BLOCK, c0 + LANES:c0 + 2 * LANES] = acc[BLOCK:].astype(BF16)


def _attn_call(q, kk, vv, sink, bias_prev, bias_cur, bias_next, batch, seq):
    tq = TQ_ATTN
    n_qb = tq // BLOCK
    nb = seq // BLOCK
    n_tiles = seq // tq
    t = batch * seq

    cur = lambda width: pl.BlockSpec((tq, width), lambda b, i: (b * n_tiles + i, 0))
    prev = pl.BlockSpec((BLOCK, KV_DUP_WIDTH), lambda b, i: (b * nb + jnp.maximum(i * n_qb - 1, 0), 0))
    nxt = pl.BlockSpec((BLOCK, KV_DUP_WIDTH), lambda b, i: (b * nb + jnp.minimum((i + 1) * n_qb, nb - 1), 0))
    full = lambda a: pl.BlockSpec(a.shape, lambda b, i: (0,) * a.ndim)
    ext = (n_qb + 2) * BLOCK
    n_units = n_qb * N_KV_HEADS * 2
    return pl.pallas_call(
        _attn_kernel,
        grid=(batch, n_tiles),
        in_specs=[pl.BlockSpec(memory_space=pltpu.SMEM),
                  cur(ATTN_WIDTH), cur(KV_DUP_WIDTH), prev, nxt, cur(KV_DUP_WIDTH), prev, nxt,
                  full(bias_prev), full(bias_cur), full(bias_next)],
        out_specs=cur(ATTN_WIDTH),
        out_shape=jax.ShapeDtypeStruct((t, ATTN_WIDTH), BF16),
        scratch_shapes=[pltpu.VMEM((ext, KV_DUP_WIDTH), BF16) for _ in range(4)]
                       + [pltpu.VMEM((n_units, 2 * BLOCK, 3 * BLOCK), F32),
                          pltpu.VMEM((n_units, 2 * BLOCK, 3 * BLOCK), BF16)],
        compiler_params=_params(dimension_semantics=("arbitrary", "arbitrary")),
        name="attn",
    )(sink, q, kk, kk, kk, vv, vv, vv, bias_prev, bias_cur, bias_next)


def _mix_kernel(xn_ref, u_ref, up_ref, un_ref, o_ref, qm_ref, gate_ref, memkv_ref,
                wpool_ref, pscale_ref, ppool_ref, pattn_ref, pmem_ref, wout_ref,
                g1_ref, b1_ref, wr_ref,
                x1r_ref, eid_ref, wts_ref, uext, *, seq):
    i = pl.program_id(1)
    n_tiles = pl.num_programs(1)
    tm = xn_ref.shape[0]

    halo = POOL_HALO
    uext[0:halo, :] = jnp.where(i > 0, up_ref[...], 0.0)
    uext[halo:halo + tm, :] = u_ref[...]
    uext[halo + tm:halo + tm + halo, :] = jnp.where(i < n_tiles - 1, un_ref[...], 0.0)
    pos = i * tm + lax.broadcasted_iota(I32, (tm, 1), 0)
    mixed = []
    for gi, win in enumerate(POOL_WINDOWS):
        c0 = gi * POOL_GROUP_CH
        half = win // 2
        total = jnp.zeros((tm, POOL_GROUP_CH), F32)
        for off in range(-half, half):
            total = total + uext[halo + off:halo + off + tm, c0:c0 + POOL_GROUP_CH]
        cnt = (jnp.minimum(pos + half, seq) - jnp.maximum(pos - half, 0)).astype(F32)
        pooled = total * (1.0 / cnt) - u_ref[:, c0:c0 + POOL_GROUP_CH]
        mixed.append(jnp.dot(pooled.astype(BF16), wpool_ref[gi], preferred_element_type=F32)
                     * pscale_ref[:, c0:c0 + POOL_GROUP_CH])
    mixed = jnp.concatenate(mixed, axis=1).astype(BF16)
    merged = gate_ref[:, 0:D_MODEL].astype(F32) * jnp.dot(
        mixed, ppool_ref[...], preferred_element_type=F32)

    merged = merged + gate_ref[:, D_MODEL:2 * D_MODEL].astype(F32) * jnp.dot(
        o_ref[...], pattn_ref[...], preferred_element_type=F32)

    heads = []
    for h in range(MEM_HEADS):
        c0 = h * MEM_HEAD_DIM
        km = memkv_ref[:, c0:c0 + MEM_HEAD_DIM]
        vm = memkv_ref[:, MEM_WIDTH + c0:MEM_WIDTH + c0 + MEM_HEAD_DIM]
        s = lax.dot_general(qm_ref[:, c0:c0 + MEM_HEAD_DIM], km, (((1,), (1,)), ((), ())),
                            preferred_element_type=F32) * (MEM_HEAD_DIM ** -0.5)
        p = jnp.exp(s - jnp.max(s, axis=-1, keepdims=True))
        p = (p / jnp.sum(p, axis=-1, keepdims=True)).astype(BF16)
        heads.append(jnp.dot(p, vm, preferred_element_type=F32).astype(BF16))
    om = jnp.concatenate(heads, axis=1)
    merged = merged + gate_ref[:, 2 * D_MODEL:3 * D_MODEL].astype(F32) * jnp.dot(
        om, pmem_ref[...], preferred_element_type=F32)

    y = jnp.dot(merged.astype(BF16), wout_ref[...], preferred_element_type=F32)
    x1 = _layer_norm(ALPHA * xn_ref[...] + y, g1_ref[...], b1_ref[...])
    x1r_ref[...] = _rows_to_tiles(x1)

    lt = lax.dot_general(wr_ref[...], x1.astype(BF16), (((1,), (1,)), ((), ())),
                         preferred_element_type=F32)
    gl = [lt[r:r + 1, :] for r in range(N_EXPERT_GROUPS)]
    gmax = gl[0]
    grp = jnp.zeros((1, tm), I32)
    for r in range(1, N_EXPERT_GROUPS):
        better = gl[r] > gmax
        grp = jnp.where(better, r, grp)
        gmax = jnp.where(better, gl[r], gmax)
    gsum = gl[0] * 0.0
    for r in range(N_EXPERT_GROUPS):
        gsum = gsum + jnp.exp(gl[r] - gmax)
    gp = 1.0 / gsum
    sel = jnp.zeros((EXPERTS_PER_GROUP, tm), F32)
    for r in range(N_EXPERT_GROUPS):
        rows = lt[SUBLANES + r * EXPERTS_PER_GROUP:SUBLANES + (r + 1) * EXPERTS_PER_GROUP, :]
        sel = jnp.where(grp == r, rows, sel)
    ridx = lax.broadcasted_iota(I32, (EXPERTS_PER_GROUP, tm), 0)
    top1 = jnp.max(sel, axis=0, keepdims=True)
    i1 = jnp.min(jnp.where(sel == top1, ridx, EXPERTS_PER_GROUP), axis=0, keepdims=True)
    rest = jnp.where(ridx == i1, -jnp.inf, sel)
    top2 = jnp.max(rest, axis=0, keepdims=True)
    i2 = jnp.min(jnp.where(rest == top2, ridx, EXPERTS_PER_GROUP), axis=0, keepdims=True)
    e2 = jnp.exp(top2 - top1)
    inv = gp / (1.0 + e2)
    eid_ref[0:1, :] = grp * EXPERTS_PER_GROUP + i1
    eid_ref[1:2, :] = grp * EXPERTS_PER_GROUP + i2
    wts_ref[0:1, :] = inv
    wts_ref[1:2, :] = e2 * inv


def _mix_call(xn, u, o, qm, gate, memkv, p, batch, seq):
    tm = TM_MIX
    n_tiles = seq // tm
    t = batch * seq
    hb = tm // POOL_HALO

    row = lambda width: pl.BlockSpec((tm, width), lambda b, i: (b * n_tiles + i, 0))
    prev = pl.BlockSpec((POOL_HALO, POOL_WIDTH),
                        lambda b, i: (jnp.maximum((b * n_tiles + i) * hb - 1, 0), 0))
    nxt = pl.BlockSpec((POOL_HALO, POOL_WIDTH),
                       lambda b, i: (jnp.minimum((b * n_tiles + i + 1) * hb, t // POOL_HALO - 1), 0))
    full = lambda a: pl.BlockSpec(a.shape, lambda b, i: (0,) * a.ndim)
    lane_row = pl.BlockSpec((TOP_K, tm), lambda b, i: (0, b * n_tiles + i))
    weights = (p["w_pool"], p["pool_scale"], p["p_pool"], p["p_attn"], p["p_mem"], p["w_out"],
               p["ln1_g"], p["ln1_b"], p["w_router"])
    return pl.pallas_call(
        functools.partial(_mix_kernel, seq=seq),
        grid=(batch, n_tiles),
        in_specs=[row(D_MODEL), row(POOL_WIDTH), prev, nxt, row(ATTN_WIDTH), row(MEM_WIDTH),
                  row(N_BRANCHES * D_MODEL),
                  pl.BlockSpec((MEM_TOKENS, 2 * MEM_WIDTH), lambda b, i: (b, 0))]
                 + [full(w) for w in weights],
        out_specs=[pl.BlockSpec((tm, ROW_TILE, LANES), lambda b, i: (b * n_tiles + i, 0, 0)),
                   lane_row, lane_row],
        out_shape=[jax.ShapeDtypeStruct((t, ROW_TILE, LANES), F32),
                   jax.ShapeDtypeStruct((TOP_K, t), I32),
                   jax.ShapeDtypeStruct((TOP_K, t), F32)],
        scratch_shapes=[pltpu.VMEM((tm + 2 * POOL_HALO, POOL_WIDTH), F32)],
        compiler_params=_params(dimension_semantics=("arbitrary", "arbitrary")),
        name="mix",
    )(xn, u, u, u, o, qm, gate, memkv, *weights)


def _moe_kernel(iblk_ref, ie_ref, inext_ref, irun_ref, ilo_ref, ihi_ref,
                x_ref, ws_ref, wgu_hbm, wd_hbm, y_ref,
                x_bf, h_bf, wgu_f32, wd_f32, wgu_bf, wd_bf, wsem):
    i = pl.program_id(0)
    bm = x_ref.shape[0]
    lo = ilo_ref[i]
    hi = ihi_ref[i]
    nonempty = hi > lo

    def weight_copies(e, ws):
        return (pltpu.make_async_copy(wgu_hbm.at[e], wgu_f32.at[ws], wsem.at[ws]),
                pltpu.make_async_copy(wd_hbm.at[e], wd_f32.at[ws], wsem.at[ws]))

    @pl.when(jnp.logical_or(i == 0, ie_ref[i] != ie_ref[jnp.maximum(i - 1, 0)]))
    def _():
        e = ie_ref[i]
        wslot = irun_ref[i] & 1

        @pl.when(i == 0)
        def _():
            for c in weight_copies(e, wslot):
                c.start()
        for c in weight_copies(e, wslot):
            c.wait()

        @pl.when(inext_ref[i] >= 0)
        def _():
            for c in weight_copies(inext_ref[i], 1 - wslot):
                c.start()
        wgu_bf[...] = wgu_f32[wslot].astype(BF16)
        wd_bf[...] = wd_f32[wslot].astype(BF16)

    @pl.when(nonempty)
    def _():
        x_bf[...] = _tiles_to_rows(x_ref[...]).astype(BF16)

    @pl.when(nonempty)
    def _():
        gu = jnp.dot(x_bf[...], wgu_bf[...], preferred_element_type=F32)
        gate = gu[:, :EXPERT_HIDDEN]
        h_bf[...] = ((gate * _sigmoid(gate)) * gu[:, EXPERT_HIDDEN:]).astype(BF16)

    @pl.when(nonempty)
    def _():
        rows = lax.broadcasted_iota(I32, (bm, 1), 0)
        mine = jnp.logical_and(rows >= lo, rows < hi)
        y = jnp.dot(h_bf[...], wd_bf[...], preferred_element_type=F32) * ws_ref[...]
        y = _rows_to_tiles(jnp.where(mine, y, 0.0))

        @pl.when(lo == 0)
        def _():
            y_ref[...] = y

        @pl.when(lo > 0)
        def _():
            y_ref[...] = y_ref[...] + y


def _moe_call(xs, wsorted, item_blk, item_e, item_lo, item_hi, w_gu, w_down):
    a = xs.shape[0]
    bm = BM_MOE
    n_items = item_blk.shape[0]
    change = jnp.concatenate([jnp.zeros((1,), I32), (item_e[1:] != item_e[:-1]).astype(I32)])
    item_run = jnp.cumsum(change).astype(I32)
    first_later = jnp.sum((item_e[None, :] <= item_e[:, None]).astype(I32), axis=1)
    item_next = jnp.where(first_later < n_items, item_e[jnp.minimum(first_later, n_items - 1)], -1).astype(I32)
    grid_spec = pltpu.PrefetchScalarGridSpec(
        num_scalar_prefetch=6,
        grid=(n_items,),
        in_specs=[pl.BlockSpec((bm, ROW_TILE, LANES), lambda i, blk, *_: (blk[i], 0, 0)),
                  pl.BlockSpec((bm, 1), lambda i, blk, *_: (blk[i], 0)),
                  pl.BlockSpec(memory_space=pl.ANY),
                  pl.BlockSpec(memory_space=pl.ANY)],
        out_specs=pl.BlockSpec((bm, ROW_TILE, LANES), lambda i, blk, *_: (blk[i], 0, 0)),
        scratch_shapes=[pltpu.VMEM((bm, D_MODEL), BF16),
                        pltpu.VMEM((bm, EXPERT_HIDDEN), BF16),
                        pltpu.VMEM((2, D_MODEL, 2 * EXPERT_HIDDEN), F32),
                        pltpu.VMEM((2, EXPERT_HIDDEN, D_MODEL), F32),
                        pltpu.VMEM((D_MODEL, 2 * EXPERT_HIDDEN), BF16),
                        pltpu.VMEM((EXPERT_HIDDEN, D_MODEL), BF16),
                        pltpu.SemaphoreType.DMA((2,))],
    )
    return pl.pallas_call(
        _moe_kernel,
        grid_spec=grid_spec,
        out_shape=jax.ShapeDtypeStruct((a, ROW_TILE, LANES), F32),
        compiler_params=_params(dimension_semantics=("arbitrary",)),
        name="moe",
    )(item_blk, item_e, item_next, item_run, item_lo, item_hi, xs, wsorted, w_gu, w_down)


def _sc_mesh():
    return plsc.VectorSubcoreMesh(core_axis_name="core", subcore_axis_name="subcore")


def _sc_gather_rows(rows, idx):
    m = idx.shape[0]
    w = SC_ROW_WINDOW

    @pl.kernel(out_type=jax.ShapeDtypeStruct((m,) + rows.shape[1:], rows.dtype), mesh=_sc_mesh(),
               scratch_types=[])
    def gather(rows_hbm, idx_hbm, out_hbm):
        def body(idx_vmem, out_vmem):
            pltpu.sync_copy(rows_hbm.at[idx_vmem.at[0]], out_vmem)

        pltpu.emit_pipeline(
            body,
            grid=(m // w,),
            in_specs=[pl.BlockSpec((1, w), index_map=lambda i: (i, 0))],
            out_specs=[pl.BlockSpec((w,) + rows.shape[1:], index_map=lambda i: (i, 0, 0))],
            core_axis_name=("core", "subcore"),
            dimension_semantics=(pltpu.PARALLEL,),
        )(idx_hbm, out_hbm)

    return gather(rows, idx.reshape(m // w, w))


def _sc_scatter_rows(rows, idx):
    m = idx.shape[0]
    w = SC_ROW_WINDOW

    @pl.kernel(out_type=jax.ShapeDtypeStruct(rows.shape, rows.dtype), mesh=_sc_mesh(), scratch_types=[])
    def scatter(rows_hbm, idx_hbm, out_hbm):
        def body(rows_vmem, idx_vmem):
            pltpu.sync_copy(rows_vmem, out_hbm.at[idx_vmem.at[0]])

        pltpu.emit_pipeline(
            body,
            grid=(m // w,),
            in_specs=[pl.BlockSpec((w,) + rows.shape[1:], index_map=lambda i: (i, 0, 0)),
                      pl.BlockSpec((1, w), index_map=lambda i: (i, 0))],
            out_specs=[],
            core_axis_name=("core", "subcore"),
            dimension_semantics=(pltpu.PARALLEL,),
        )(rows_hbm, idx_hbm)

    return scatter(rows, idx.reshape(m // w, w))


def _final_kernel(x1_ref, ya_ref, yb_ref, g_ref, b_ref, out_ref):
    moe = ya_ref[...] + yb_ref[...]
    out_ref[...] = _layer_norm(_tiles_to_rows(ALPHA * x1_ref[...] + moe), g_ref[...], b_ref[...])


def _final_call(x1, y2, ln_g, ln_b):
    t = x1.shape[0]
    tm = TM_FINAL
    n_tiles = t // tm
    return pl.pallas_call(
        _final_kernel,
        grid=(n_tiles,),
        in_specs=[pl.BlockSpec((tm, ROW_TILE, LANES), lambda i: (i, 0, 0)),
                  pl.BlockSpec((tm, ROW_TILE, LANES), lambda i: (i, 0, 0)),
                  pl.BlockSpec((tm, ROW_TILE, LANES), lambda i: (n_tiles + i, 0, 0)),
                  pl.BlockSpec((1, D_MODEL), lambda i: (0, 0)),
                  pl.BlockSpec((1, D_MODEL), lambda i: (0, 0))],
        out_specs=pl.BlockSpec((tm, D_MODEL), lambda i: (i, 0)),
        out_shape=jax.ShapeDtypeStruct((t, D_MODEL), F32),
        compiler_params=_params(dimension_semantics=("arbitrary",)),
        name="final",
    )(x1, y2, y2, ln_g, ln_b)


def _t5_bucket(rel):
    nb = N_REL_BUCKETS // 2
    max_exact = nb // 2
    ret = jnp.where(rel > 0, nb, 0)
    n = jnp.abs(rel)
    nf = jnp.maximum(n, 1).astype(F32)
    large = max_exact + (jnp.log(nf / max_exact) / math.log(REL_MAX_DISTANCE / max_exact)
                         * (nb - max_exact)).astype(I32)
    large = jnp.minimum(large, nb - 1)
    return ret + jnp.where(n < max_exact, n, large)


def _bias_tables(rel_table):
    n, m = BLOCK, 3 * BLOCK
    rel = jnp.arange(-(2 * BLOCK - 1), 2 * BLOCK)
    by_rel = jnp.where((jnp.abs(rel) <= WINDOW)[:, None], rel_table[_t5_bucket(rel)].astype(F32), NEG_INF)
    by_rel = jnp.pad(by_rel.T, ((0, 0), (0, 1)))
    skew = jnp.tile(by_rel, (1, n))[:, :n * (m + n - 1)].reshape(N_Q_HEADS, n, m + n - 1)
    bias = skew[:, :, n - 1:n - 1 + m]
    heads = [4 * h + half for h in range(N_KV_HEADS) for half in range(2)]
    units = jnp.concatenate([jnp.stack([bias[a] for a in heads]),
                             jnp.stack([bias[a + 2] for a in heads])], axis=1)
    masked = jnp.full((2 * N_KV_HEADS, 2 * BLOCK, BLOCK), NEG_INF, F32)
    prev = jnp.stack([units[:, :, :BLOCK], masked])
    cur = units[:, :, BLOCK:2 * BLOCK]
    nxt = jnp.stack([units[:, :, 2 * BLOCK:], masked])
    return prev, cur, nxt


def _prepare(rel_bias_table, w_in, b_in, w_pool, pool_scale, p_pool, sink, p_attn, w_mem_kv, p_mem,
             w_out, ln1_g, ln1_b, w_router_group, w_router_expert, w_gu, w_down):
    l = 0
    w_router = jnp.zeros((ROUTER_ROWS, D_MODEL), F32)
    w_router = w_router.at[:N_EXPERT_GROUPS].set(w_router_group[l].T)
    w_router = w_router.at[SUBLANES:].set(w_router_expert[l].T)
    bias_prev, bias_cur, bias_next = _bias_tables(rel_bias_table)
    return dict(
        w_in=w_in[l].astype(BF16), b_in=b_in[l][None, :],
        w_pool=w_pool[l].astype(BF16), pool_scale=pool_scale[l][None, :],
        p_pool=p_pool[l].astype(BF16), p_attn=p_attn[l].astype(BF16), p_mem=p_mem[l].astype(BF16),
        w_out=w_out[l].astype(BF16), w_memkv=w_mem_kv[l].astype(BF16),
        ln1_g=ln1_g[l][None, :], ln1_b=ln1_b[l][None, :],
        w_router=w_router.astype(BF16), sink=sink[l],
        w_gu=w_gu[l], w_down=w_down[l],
        bias_prev=bias_prev, bias_cur=bias_cur, bias_next=bias_next,
    )


def _dispatch_plan(eid, wts):
    t = eid.shape[1]
    a = TOP_K * t
    bm = BM_MOE
    nblk = a // bm
    eflat = eid.reshape(a)
    wflat = wts.reshape(a)
    order = jnp.argsort(eflat, stable=True).astype(I32)
    wsorted = wflat[order][:, None]
    experts = jnp.arange(N_EXPERTS, dtype=I32)
    counts = jnp.sum((eflat[None, :] == experts[:, None]).astype(I32), axis=1)
    ends = jnp.cumsum(counts).astype(I32)
    starts = ends - counts
    cuts = jnp.sort(jnp.concatenate([jnp.arange(nblk, dtype=I32) * bm, starts, jnp.array([a], I32)]))
    lo_abs = cuts[:-1]
    hi_abs = cuts[1:]
    nonempty = hi_abs > lo_abs
    blk = jnp.minimum(lo_abs // bm, nblk - 1)
    probe = jnp.minimum(lo_abs, a - 1)
    item_e = jnp.sum((ends[None, :] <= probe[:, None]).astype(I32), axis=1)
    item_lo = jnp.where(nonempty, lo_abs - blk * bm, 0).astype(I32)
    item_hi = jnp.where(nonempty, hi_abs - blk * bm, 0).astype(I32)
    return order, wsorted, blk.astype(I32), item_e.astype(I32), item_lo, item_hi


def _trunk(x, mem, ln_in_g, ln_in_b, ln2_g, ln2_b, p):
    batch, seq, d = x.shape
    t = batch * seq
    assert seq % TM_MIX == 0 and seq % TQ_ATTN == 0 and t % TM_INPROJ == 0 and (TOP_K * t) % BM_MOE == 0
    memkv = _memkv_call(mem.reshape(batch * MEM_TOKENS, d), p["w_memkv"])
    xn, u, q, kk, vv, qm, gate = _inproj_call(x.reshape(t, d), ln_in_g[None, :], ln_in_b[None, :],
                                              p["w_in"], p["b_in"])
    o = _attn_call(q, kk, vv, p["sink"], p["bias_prev"], p["bias_cur"], p["bias_next"], batch, seq)
    x1r, eid, wts = _mix_call(xn, u, o, qm, gate, memkv, p, batch, seq)
    order, wsorted, item_blk, item_e, item_lo, item_hi = _dispatch_plan(eid, wts)
    sorted_tok = jnp.where(order >= t, order - t, order)
    xs = _sc_gather_rows(x1r, sorted_tok)
    ys = _moe_call(xs, wsorted, item_blk, item_e, item_lo, item_hi, p["w_gu"], p["w_down"])
    y2 = _sc_scatter_rows(ys, order)
    out = _final_call(x1r, y2, ln2_g[None, :], ln2_b[None, :])
    return out.reshape(batch, seq, d)


def kernel(x_prompt, x_sample, mem_prompt, mem_sample, ln_in_g, ln_in_b, rel_bias_table, w_in, b_in,
           w_pool, pool_scale, p_pool, sink, p_attn, w_mem_kv, p_mem, w_out, ln1_g, ln1_b,
           w_router_group, w_router_expert, w_gu, w_down, ln2_g, ln2_b):
    p = _prepare(rel_bias_table, w_in, b_in, w_pool, pool_scale, p_pool, sink, p_attn, w_mem_kv, p_mem,
                 w_out, ln1_g, ln1_b, w_router_group, w_router_expert, w_gu, w_down)
    y_prompt = _trunk(x_prompt, mem_prompt, ln_in_g, ln_in_b, ln2_g[0], ln2_b[0], p)
    y_sample = _trunk(x_sample, mem_sample, ln_in_g, ln_in_b, ln2_g[0], ln2_b[0], p)
    return (y_prompt, y_sample)
```

```python
import functools
import math

import jax
import jax.numpy as jnp
from jax import lax
from jax.experimental import pallas as pl
from jax.experimental.pallas import tpu as pltpu
from jax.experimental.pallas import tpu_sc as plsc

F32 = jnp.float32
BF16 = jnp.bfloat16
I32 = jnp.int32

D_MODEL = 1024
DEPTH = 1
POOL_WIDTH = 512
POOL_WINDOWS = (2, 4, 8, 16)
N_POOL_GROUPS = 4
POOL_GROUP_CH = POOL_WIDTH // N_POOL_GROUPS
N_Q_HEADS = 16
N_KV_HEADS = 4
HEAD_DIM = 64
ATTN_WIDTH = N_Q_HEADS * HEAD_DIM
KV_WIDTH = N_KV_HEADS * HEAD_DIM
WINDOW = 128
BLOCK = 128
N_REL_BUCKETS = 32
REL_MAX_DISTANCE = 128
MEM_TOKENS = 256
MEM_HEADS = 4
MEM_HEAD_DIM = 128
MEM_WIDTH = MEM_HEADS * MEM_HEAD_DIM
N_BRANCHES = 3
N_EXPERT_GROUPS = 4
EXPERTS_PER_GROUP = 8
N_EXPERTS = N_EXPERT_GROUPS * EXPERTS_PER_GROUP
TOP_K = 2
EXPERT_HIDDEN = 512
ALPHA = (2 * DEPTH) ** 0.25
LN_EPS = 1e-5
NEG_INF = -1e30

LANES = 128
SUBLANES = 8
ROW_TILE = D_MODEL // LANES
VMEM_LIMIT_BYTES = 56 * 1024 * 1024

KV_DUP_WIDTH = N_KV_HEADS * LANES

C_U = 0
C_Q = C_U + POOL_WIDTH
C_K = C_Q + ATTN_WIDTH
C_V = C_K + KV_WIDTH
C_M = C_V + KV_WIDTH
C_G = C_M + MEM_WIDTH
C_END = C_G + N_BRANCHES * D_MODEL

TM_INPROJ = 512
TQ_ATTN = 256
TM_MIX = 512
BM_MOE = 512
SC_ROW_WINDOW = 32
TM_FINAL = 512
POOL_HALO = 8
ROUTER_ROWS = 40


def _layer_norm(x, g, b):
    mu = jnp.mean(x, axis=-1, keepdims=True)
    xc = x - mu
    var = jnp.mean(xc * xc, axis=-1, keepdims=True)
    return xc * lax.rsqrt(var + LN_EPS) * g + b


def _sigmoid(x):
    return 1.0 / (1.0 + jnp.exp(-x))


def _rows_to_tiles(x):
    return pltpu.einshape("r(cl)->rcl", x, c=ROW_TILE)


def _tiles_to_rows(x):
    return pltpu.einshape("rcl->r(cl)", x)


def _params(**kw):
    return pltpu.CompilerParams(vmem_limit_bytes=VMEM_LIMIT_BYTES, **kw)


def _memkv_kernel(mem_ref, w_ref, out_ref):
    out_ref[...] = jnp.dot(mem_ref[...].astype(BF16), w_ref[...],
                           preferred_element_type=F32).astype(BF16)


def _memkv_call(mem2, w_memkv):
    rows = mem2.shape[0]
    return pl.pallas_call(
        _memkv_kernel,
        grid=(rows // MEM_TOKENS,),
        in_specs=[pl.BlockSpec((MEM_TOKENS, D_MODEL), lambda i: (i, 0)),
                  pl.BlockSpec((D_MODEL, 2 * MEM_WIDTH), lambda i: (0, 0))],
        out_specs=pl.BlockSpec((MEM_TOKENS, 2 * MEM_WIDTH), lambda i: (i, 0)),
        out_shape=jax.ShapeDtypeStruct((rows, 2 * MEM_WIDTH), BF16),
        compiler_params=_params(dimension_semantics=("arbitrary",)),
        name="memkv",
    )(mem2, w_memkv)


def _inproj_kernel(x_ref, g_ref, b_ref, w_ref, bias_ref,
                   xn_ref, u_ref, q_ref, k_ref, v_ref, qm_ref, gate_ref):
    xn = _layer_norm(x_ref[...], g_ref[...], b_ref[...])
    xn_ref[...] = xn
    xb = xn.astype(BF16)

    def seg(lo, hi):
        return jnp.dot(xb, w_ref[:, lo:hi], preferred_element_type=F32) + bias_ref[:, lo:hi]

    u_ref[...] = seg(C_U, C_Q)
    q_ref[...] = (seg(C_Q, C_K) * (HEAD_DIM ** -0.5)).astype(BF16)
    kv = seg(C_K, C_M)
    low = lax.broadcasted_iota(I32, (kv.shape[0], LANES), 1) < HEAD_DIM
    for out_ref, c_lo in ((k_ref, 0), (v_ref, KV_WIDTH)):
        for c in range(KV_WIDTH // LANES):
            blk = kv[:, c_lo + c * LANES:c_lo + (c + 1) * LANES]
            rot = pltpu.roll(blk, HEAD_DIM, 1)
            out_ref[:, 2 * c * LANES:(2 * c + 1) * LANES] = jnp.where(low, blk, rot).astype(BF16)
            out_ref[:, (2 * c + 1) * LANES:(2 * c + 2) * LANES] = jnp.where(low, rot, blk).astype(BF16)
    qm_ref[...] = seg(C_M, C_G).astype(BF16)
    for j in range(N_BRANCHES):
        lo = C_G + j * D_MODEL
        gate_ref[:, j * D_MODEL:(j + 1) * D_MODEL] = _sigmoid(seg(lo, lo + D_MODEL)).astype(BF16)


def _inproj_call(x2, ln_g, ln_b, w_in, b_in):
    t = x2.shape[0]
    tm = TM_INPROJ
    row = lambda width: pl.BlockSpec((tm, width), lambda i: (i, 0))
    const = lambda shape: pl.BlockSpec(shape, lambda i: (0, 0))
    widths = (D_MODEL, POOL_WIDTH, ATTN_WIDTH, KV_DUP_WIDTH, KV_DUP_WIDTH, MEM_WIDTH, N_BRANCHES * D_MODEL)
    dtypes = (F32, F32, BF16, BF16, BF16, BF16, BF16)
    return pl.pallas_call(
        _inproj_kernel,
        grid=(t // tm,),
        in_specs=[row(D_MODEL), const((1, D_MODEL)), const((1, D_MODEL)),
                  const((D_MODEL, C_END)), const((1, C_END))],
        out_specs=[row(w) for w in widths],
        out_shape=[jax.ShapeDtypeStruct((t, w), d) for w, d in zip(widths, dtypes)],
        compiler_params=_params(dimension_semantics=("arbitrary",)),
        name="inproj",
    )(x2, ln_g, ln_b, w_in, b_in)


def _attn_kernel(sink_ref, q_ref, kc_ref, kp_ref, kn_ref, vc_ref, vp_ref, vn_ref,
                 bp_ref, bc_ref, bn_ref, o_ref, klo, khi, vlo, vhi, s_scr, p_scr):
    i = pl.program_id(1)
    n_tiles = pl.num_programs(1)
    tq = q_ref.shape[0]
    n_qb = tq // BLOCK

    lane = lax.broadcasted_iota(I32, (BLOCK, KV_DUP_WIDTH), 1)
    low = (lane & (LANES - 1)) < HEAD_DIM

    def put(dst_lo, dst_hi, r0, val):
        zero = jnp.zeros_like(val)
        dst_lo[r0:r0 + BLOCK, :] = jnp.where(low, val, zero)
        dst_hi[r0:r0 + BLOCK, :] = jnp.where(low, zero, val)

    put(klo, khi, 0, kp_ref[...])
    put(vlo, vhi, 0, vp_ref[...])
    for j in range(n_qb):
        put(klo, khi, (j + 1) * BLOCK, kc_ref[j * BLOCK:(j + 1) * BLOCK, :])
        put(vlo, vhi, (j + 1) * BLOCK, vc_ref[j * BLOCK:(j + 1) * BLOCK, :])
    put(klo, khi, (n_qb + 1) * BLOCK, kn_ref[...])
    put(vlo, vhi, (n_qb + 1) * BLOCK, vn_ref[...])

    first = (i == 0).astype(I32)
    last = (i == n_tiles - 1).astype(I32)

    def softmax_rows(s, sink):
        m = jnp.maximum(jnp.max(s, axis=-1, keepdims=True), sink)
        p = jnp.exp(s - m)
        denom = jnp.sum(p, axis=-1, keepdims=True) + jnp.exp(sink - m)
        return (p * (1.0 / denom)).astype(BF16)

    units = [(j, h, half) for j in range(n_qb) for h in range(N_KV_HEADS) for half in range(2)]
    always = i >= 0

    @pl.when(always)
    def _scores():
        for u, (j, h, half) in enumerate(units):
            r0 = j * BLOCK
            c0 = h * 2 * LANES
            pv = first if j == 0 else 0
            nv = last if j == n_qb - 1 else 0
            unit = h * 2 + half
            q_pairs = jnp.concatenate([q_ref[r0:r0 + BLOCK, c0:c0 + LANES],
                                       q_ref[r0:r0 + BLOCK, c0 + LANES:c0 + 2 * LANES]], axis=0)
            kx = (klo, khi)[half][r0:r0 + 3 * BLOCK, h * LANES:(h + 1) * LANES]
            s = lax.dot_general(q_pairs, kx, (((1,), (1,)), ((), ())), preferred_element_type=F32)
            bias = jnp.concatenate([bp_ref[pv, unit], bc_ref[unit], bn_ref[nv, unit]], axis=1)
            s_scr[u] = s + bias

    @pl.when(always)
    def _softmax():
        for u, (j, h, half) in enumerate(units):
            p_scr[u, :BLOCK] = softmax_rows(s_scr[u, :BLOCK], sink_ref[4 * h + half])
            p_scr[u, BLOCK:] = softmax_rows(s_scr[u, BLOCK:], sink_ref[4 * h + 2 + half])

    @pl.when(always)
    def _values():
        for j in range(n_qb):
            r0 = j * BLOCK
            for h in range(N_KV_HEADS):
                c0 = h * 2 * LANES
                acc = jnp.zeros((2 * BLOCK, LANES), F32)
                for half, vref in enumerate((vlo, vhi)):
                    u = units.index((j, h, half))
                    vx = vref[r0:r0 + 3 * BLOCK, h * LANES:(h + 1) * LANES]
                    acc = acc + jnp.dot(p_scr[u], vx, preferred_element_type=F32)
                o_ref[r0:r0 + BLOCK, c0:c0 + LANES] = acc[:BLOCK].astype(BF16)
                o_ref[r0:r0 + BLOCK, c0 + LANES:c0 + 2 * LANES] = acc[BLOCK:].astype(BF16)


def _attn_call(q, kk, vv, sink, bias_prev, bias_cur, bias_next, batch, seq):
    tq = TQ_ATTN
    n_qb = tq // BLOCK
    nb = seq // BLOCK
    n_tiles = seq // tq
    t = batch * seq

    cur = lambda width: pl.BlockSpec((tq, width), lambda b, i: (b * n_tiles + i, 0))
    prev = pl.BlockSpec((BLOCK, KV_DUP_WIDTH), lambda b, i: (b * nb + jnp.maximum(i * n_qb - 1, 0), 0))
    nxt = pl.BlockSpec((BLOCK, KV_DUP_WIDTH), lambda b, i: (b * nb + jnp.minimum((i + 1) * n_qb, nb - 1), 0))
    full = lambda a: pl.BlockSpec(a.shape, lambda b, i: (0,) * a.ndim)
    ext = (n_qb + 2) * BLOCK
    n_units = n_qb * N_KV_HEADS * 2
    return pl.pallas_call(
        _attn_kernel,
        grid=(batch, n_tiles),
        in_specs=[pl.BlockSpec(memory_space=pltpu.SMEM),
                  cur(ATTN_WIDTH), cur(KV_DUP_WIDTH), prev, nxt, cur(KV_DUP_WIDTH), prev, nxt,
                  full(bias_prev), full(bias_cur), full(bias_next)],
        out_specs=cur(ATTN_WIDTH),
        out_shape=jax.ShapeDtypeStruct((t, ATTN_WIDTH), BF16),
        scratch_shapes=[pltpu.VMEM((ext, KV_DUP_WIDTH), BF16) for _ in range(4)]
                       + [pltpu.VMEM((n_units, 2 * BLOCK, 3 * BLOCK), F32),
                          pltpu.VMEM((n_units, 2 * BLOCK, 3 * BLOCK), BF16)],
        compiler_params=_params(dimension_semantics=("arbitrary", "arbitrary")),
        name="attn",
    )(sink, q, kk, kk, kk, vv, vv, vv, bias_prev, bias_cur, bias_next)


def _mix_kernel(xn_ref, u_ref, up_ref, un_ref, o_ref, qm_ref, gate_ref, memkv_ref,
                wpool_ref, pscale_ref, ppool_ref, pattn_ref, pmem_ref, wout_ref,
                g1_ref, b1_ref, wr_ref,
                x1r_ref, eid_ref, wts_ref, uext, *, seq):
    i = pl.program_id(1)
    n_tiles = pl.num_programs(1)
    tm = xn_ref.shape[0]

    halo = POOL_HALO
    uext[0:halo, :] = jnp.where(i > 0, up_ref[...], 0.0)
    uext[halo:halo + tm, :] = u_ref[...]
    uext[halo + tm:halo + tm + halo, :] = jnp.where(i < n_tiles - 1, un_ref[...], 0.0)
    pos = i * tm + lax.broadcasted_iota(I32, (tm, 1), 0)
    mixed = []
    for gi, win in enumerate(POOL_WINDOWS):
        c0 = gi * POOL_GROUP_CH
        half = win // 2
        total = jnp.zeros((tm, POOL_GROUP_CH), F32)
        for off in range(-half, half):
            total = total + uext[halo + off:halo + off + tm, c0:c0 + POOL_GROUP_CH]
        cnt = (jnp.minimum(pos + half, seq) - jnp.maximum(pos - half, 0)).astype(F32)
        pooled = total * (1.0 / cnt) - u_ref[:, c0:c0 + POOL_GROUP_CH]
        mixed.append(jnp.dot(pooled.astype(BF16), wpool_ref[gi], preferred_element_type=F32)
                     * pscale_ref[:, c0:c0 + POOL_GROUP_CH])
    mixed = jnp.concatenate(mixed, axis=1).astype(BF16)
    merged = gate_ref[:, 0:D_MODEL].astype(F32) * jnp.dot(
        mixed, ppool_ref[...], preferred_element_type=F32)

    merged = merged + gate_ref[:, D_MODEL:2 * D_MODEL].astype(F32) * jnp.dot(
        o_ref[...], pattn_ref[...], preferred_element_type=F32)

    heads = []
    for h in range(MEM_HEADS):
        c0 = h * MEM_HEAD_DIM
        km = memkv_ref[:, c0:c0 + MEM_HEAD_DIM]
        vm = memkv_ref[:, MEM_WIDTH + c0:MEM_WIDTH + c0 + MEM_HEAD_DIM]
        s = lax.dot_general(qm_ref[:, c0:c0 + MEM_HEAD_DIM], km, (((1,), (1,)), ((), ())),
                            preferred_element_type=F32) * (MEM_HEAD_DIM ** -0.5)
        p = jnp.exp(s - jnp.max(s, axis=-1, keepdims=True))
        p = (p / jnp.sum(p, axis=-1, keepdims=True)).astype(BF16)
        heads.append(jnp.dot(p, vm, preferred_element_type=F32).astype(BF16))
    om = jnp.concatenate(heads, axis=1)
    merged = merged + gate_ref[:, 2 * D_MODEL:3 * D_MODEL].astype(F32) * jnp.dot(
        om, pmem_ref[...], preferred_element_type=F32)

    y = jnp.dot(merged.astype(BF16), wout_ref[...], preferred_element_type=F32)
    x1 = _layer_norm(ALPHA * xn_ref[...] + y, g1_ref[...], b1_ref[...])
    x1r_ref[...] = _rows_to_tiles(x1)

    lt = lax.dot_general(wr_ref[...], x1.astype(BF16), (((1,), (1,)), ((), ())),
                         preferred_element_type=F32)
    gl = [lt[r:r + 1, :] for r in range(N_EXPERT_GROUPS)]
    gmax = gl[0]
    grp = jnp.zeros((1, tm), I32)
    for r in range(1, N_EXPERT_GROUPS):
        better = gl[r] > gmax
        grp = jnp.where(better, r, grp)
        gmax = jnp.where(better, gl[r], gmax)
    gsum = gl[0] * 0.0
    for r in range(N_EXPERT_GROUPS):
        gsum = gsum + jnp.exp(gl[r] - gmax)
    gp = 1.0 / gsum
    sel = jnp.zeros((EXPERTS_PER_GROUP, tm), F32)
    for r in range(N_EXPERT_GROUPS):
        rows = lt[SUBLANES + r * EXPERTS_PER_GROUP:SUBLANES + (r + 1) * EXPERTS_PER_GROUP, :]
        sel = jnp.where(grp == r, rows, sel)
    ridx = lax.broadcasted_iota(I32, (EXPERTS_PER_GROUP, tm), 0)
    top1 = jnp.max(sel, axis=0, keepdims=True)
    i1 = jnp.min(jnp.where(sel == top1, ridx, EXPERTS_PER_GROUP), axis=0, keepdims=True)
    rest = jnp.where(ridx == i1, -jnp.inf, sel)
    top2 = jnp.max(rest, axis=0, keepdims=True)
    i2 = jnp.min(jnp.where(rest == top2, ridx, EXPERTS_PER_GROUP), axis=0, keepdims=True)
    e2 = jnp.exp(top2 - top1)
    inv = gp / (1.0 + e2)
    eid_ref[0:1, :] = grp * EXPERTS_PER_GROUP + i1
    eid_ref[1:2, :] = grp * EXPERTS_PER_GROUP + i2
    wts_ref[0:1, :] = inv
    wts_ref[1:2, :] = e2 * inv


def _mix_call(xn, u, o, qm, gate, memkv, p, batch, seq):
    tm = TM_MIX
    n_tiles = seq // tm
    t = batch * seq
    hb = tm // POOL_HALO

    row = lambda width: pl.BlockSpec((tm, width), lambda b, i: (b * n_tiles + i, 0))
    prev = pl.BlockSpec((POOL_HALO, POOL_WIDTH),
                        lambda b, i: (jnp.maximum((b * n_tiles + i) * hb - 1, 0), 0))
    nxt = pl.BlockSpec((POOL_HALO, POOL_WIDTH),
                       lambda b, i: (jnp.minimum((b * n_tiles + i + 1) * hb, t // POOL_HALO - 1), 0))
    full = lambda a: pl.BlockSpec(a.shape, lambda b, i: (0,) * a.ndim)
    lane_row = pl.BlockSpec((TOP_K, tm), lambda b, i: (0, b * n_tiles + i))
    weights = (p["w_pool"], p["pool_scale"], p["p_pool"], p["p_attn"], p["p_mem"], p["w_out"],
               p["ln1_g"], p["ln1_b"], p["w_router"])
    return pl.pallas_call(
        functools.partial(_mix_kernel, seq=seq),
        grid=(batch, n_tiles),
        in_specs=[row(D_MODEL), row(POOL_WIDTH), prev, nxt, row(ATTN_WIDTH), row(MEM_WIDTH),
                  row(N_BRANCHES * D_MODEL),
                  pl.BlockSpec((MEM_TOKENS, 2 * MEM_WIDTH), lambda b, i: (b, 0))]
                 + [full(w) for w in weights],
        out_specs=[pl.BlockSpec((tm, ROW_TILE, LANES), lambda b, i: (b * n_tiles + i, 0, 0)),
                   lane_row, lane_row],
        out_shape=[jax.ShapeDtypeStruct((t, ROW_TILE, LANES), F32),
                   jax.ShapeDtypeStruct((TOP_K, t), I32),
                   jax.ShapeDtypeStruct((TOP_K, t), F32)],
        scratch_shapes=[pltpu.VMEM((tm + 2 * POOL_HALO, POOL_WIDTH), F32)],
        compiler_params=_params(dimension_semantics=("arbitrary", "arbitrary")),
        name="mix",
    )(xn, u, u, u, o, qm, gate, memkv, *weights)


def _moe_kernel(iblk_ref, ie_ref, inext_ref, irun_ref, ilo_ref, ihi_ref,
                x_ref, ws_ref, wgu_hbm, wd_hbm, y_ref,
                x_bf, h_bf, wgu_f32, wd_f32, wgu_bf, wd_bf, wsem):
    i = pl.program_id(0)
    bm = x_ref.shape[0]
    lo = ilo_ref[i]
    hi = ihi_ref[i]
    nonempty = hi > lo

    def weight_copies(e, ws):
        return (pltpu.make_async_copy(wgu_hbm.at[e], wgu_f32.at[ws], wsem.at[ws]),
                pltpu.make_async_copy(wd_hbm.at[e], wd_f32.at[ws], wsem.at[ws]))

    @pl.when(jnp.logical_or(i == 0, ie_ref[i] != ie_ref[jnp.maximum(i - 1, 0)]))
    def _():
        e = ie_ref[i]
        wslot = irun_ref[i] & 1

        @pl.when(i == 0)
        def _():
            for c in weight_copies(e, wslot):
                c.start()
        for c in weight_copies(e, wslot):
            c.wait()

        @pl.when(inext_ref[i] >= 0)
        def _():
            for c in weight_copies(inext_ref[i], 1 - wslot):
                c.start()
        wgu_bf[...] = wgu_f32[wslot].astype(BF16)
        wd_bf[...] = wd_f32[wslot].astype(BF16)

    @pl.when(nonempty)
    def _():
        x_bf[...] = _tiles_to_rows(x_ref[...]).astype(BF16)

    @pl.when(nonempty)
    def _():
        gu = jnp.dot(x_bf[...], wgu_bf[...], preferred_element_type=F32)
        gate = gu[:, :EXPERT_HIDDEN]
        h_bf[...] = ((gate * _sigmoid(gate)) * gu[:, EXPERT_HIDDEN:]).astype(BF16)

    @pl.when(nonempty)
    def _():
        rows = lax.broadcasted_iota(I32, (bm, 1), 0)
        mine = jnp.logical_and(rows >= lo, rows < hi)
        y = jnp.dot(h_bf[...], wd_bf[...], preferred_element_type=F32) * ws_ref[...]
        y = _rows_to_tiles(jnp.where(mine, y, 0.0))

        @pl.when(lo == 0)
        def _():
            y_ref[...] = y

        @pl.when(lo > 0)
        def _():
            y_ref[...] = y_ref[...] + y


def _moe_call(xs, wsorted, item_blk, item_e, item_lo, item_hi, w_gu, w_down):
    a = xs.shape[0]
    bm = BM_MOE
    n_items = item_blk.shape[0]
    change = jnp.concatenate([jnp.zeros((1,), I32), (item_e[1:] != item_e[:-1]).astype(I32)])
    item_run = jnp.cumsum(change).astype(I32)
    first_later = jnp.sum((item_e[None, :] <= item_e[:, None]).astype(I32), axis=1)
    item_next = jnp.where(first_later < n_items, item_e[jnp.minimum(first_later, n_items - 1)], -1).astype(I32)
    grid_spec = pltpu.PrefetchScalarGridSpec(
        num_scalar_prefetch=6,
        grid=(n_items,),
        in_specs=[pl.BlockSpec((bm, ROW_TILE, LANES), lambda i, blk, *_: (blk[i], 0, 0)),
                  pl.BlockSpec((bm, 1), lambda i, blk, *_: (blk[i], 0)),
                  pl.BlockSpec(memory_space=pl.ANY),
                  pl.BlockSpec(memory_space=pl.ANY)],
        out_specs=pl.BlockSpec((bm, ROW_TILE, LANES), lambda i, blk, *_: (blk[i], 0, 0)),
        scratch_shapes=[pltpu.VMEM((bm, D_MODEL), BF16),
                        pltpu.VMEM((bm, EXPERT_HIDDEN), BF16),
                        pltpu.VMEM((2, D_MODEL, 2 * EXPERT_HIDDEN), F32),
                        pltpu.VMEM((2, EXPERT_HIDDEN, D_MODEL), F32),
                        pltpu.VMEM((D_MODEL, 2 * EXPERT_HIDDEN), BF16),
                        pltpu.VMEM((EXPERT_HIDDEN, D_MODEL), BF16),
                        pltpu.SemaphoreType.DMA((2,))],
    )
    return pl.pallas_call(
        _moe_kernel,
        grid_spec=grid_spec,
        out_shape=jax.ShapeDtypeStruct((a, ROW_TILE, LANES), F32),
        compiler_params=_params(dimension_semantics=("arbitrary",)),
        name="moe",
    )(item_blk, item_e, item_next, item_run, item_lo, item_hi, xs, wsorted, w_gu, w_down)


def _sc_mesh():
    return plsc.VectorSubcoreMesh(core_axis_name="core", subcore_axis_name="subcore")


def _sc_gather_rows(rows, idx):
    m = idx.shape[0]
    w = SC_ROW_WINDOW

    @pl.kernel(out_type=jax.ShapeDtypeStruct((m,) + rows.shape[1:], rows.dtype), mesh=_sc_mesh(),
               scratch_types=[])
    def gather(rows_hbm, idx_hbm, out_hbm):
        def body(idx_vmem, out_vmem):
            pltpu.sync_copy(rows_hbm.at[idx_vmem.at[0]], out_vmem)

        pltpu.emit_pipeline(
            body,
            grid=(m // w,),
            in_specs=[pl.BlockSpec((1, w), index_map=lambda i: (i, 0))],
            out_specs=[pl.BlockSpec((w,) + rows.shape[1:], index_map=lambda i: (i, 0, 0))],
            core_axis_name=("core", "subcore"),
            dimension_semantics=(pltpu.PARALLEL,),
        )(idx_hbm, out_hbm)

    return gather(rows, idx.reshape(m // w, w))


def _sc_scatter_rows(rows, idx):
    m = idx.shape[0]
    w = SC_ROW_WINDOW

    @pl.kernel(out_type=jax.ShapeDtypeStruct(rows.shape, rows.dtype), mesh=_sc_mesh(), scratch_types=[])
    def scatter(rows_hbm, idx_hbm, out_hbm):
        def body(rows_vmem, idx_vmem):
            pltpu.sync_copy(rows_vmem, out_hbm.at[idx_vmem.at[0]])

        pltpu.emit_pipeline(
            body,
            grid=(m // w,),
            in_specs=[pl.BlockSpec((w,) + rows.shape[1:], index_map=lambda i: (i, 0, 0)),
                      pl.BlockSpec((1, w), index_map=lambda i: (i, 0))],
            out_specs=[],
            core_axis_name=("core", "subcore"),
            dimension_semantics=(pltpu.PARALLEL,),
        )(rows_hbm, idx_hbm)

    return scatter(rows, idx.reshape(m // w, w))


def _final_kernel(x1_ref, ya_ref, yb_ref, g_ref, b_ref, out_ref):
    moe = ya_ref[...] + yb_ref[...]
    out_ref[...] = _layer_norm(_tiles_to_rows(ALPHA * x1_ref[...] + moe), g_ref[...], b_ref[...])


def _final_call(x1, y2, ln_g, ln_b):
    t = x1.shape[0]
    tm = TM_FINAL
    n_tiles = t // tm
    return pl.pallas_call(
        _final_kernel,
        grid=(n_tiles,),
        in_specs=[pl.BlockSpec((tm, ROW_TILE, LANES), lambda i: (i, 0, 0)),
                  pl.BlockSpec((tm, ROW_TILE, LANES), lambda i: (i, 0, 0)),
                  pl.BlockSpec((tm, ROW_TILE, LANES), lambda i: (n_tiles + i, 0, 0)),
                  pl.BlockSpec((1, D_MODEL), lambda i: (0, 0)),
                  pl.BlockSpec((1, D_MODEL), lambda i: (0, 0))],
        out_specs=pl.BlockSpec((tm, D_MODEL), lambda i: (i, 0)),
        out_shape=jax.ShapeDtypeStruct((t, D_MODEL), F32),
        compiler_params=_params(dimension_semantics=("arbitrary",)),
        name="final",
    )(x1, y2, y2, ln_g, ln_b)


def _t5_bucket(rel):
    nb = N_REL_BUCKETS // 2
    max_exact = nb // 2
    ret = jnp.where(rel > 0, nb, 0)
    n = jnp.abs(rel)
    nf = jnp.maximum(n, 1).astype(F32)
    large = max_exact + (jnp.log(nf / max_exact) / math.log(REL_MAX_DISTANCE / max_exact)
                         * (nb - max_exact)).astype(I32)
    large = jnp.minimum(large, nb - 1)
    return ret + jnp.where(n < max_exact, n, large)


def _bias_tables(rel_table):
    n, m = BLOCK, 3 * BLOCK
    rel = jnp.arange(-(2 * BLOCK - 1), 2 * BLOCK)
    by_rel = jnp.where((jnp.abs(rel) <= WINDOW)[:, None], rel_table[_t5_bucket(rel)].astype(F32), NEG_INF)
    by_rel = jnp.pad(by_rel.T, ((0, 0), (0, 1)))
    skew = jnp.tile(by_rel, (1, n))[:, :n * (m + n - 1)].reshape(N_Q_HEADS, n, m + n - 1)
    bias = skew[:, :, n - 1:n - 1 + m]
    heads = [4 * h + half for h in range(N_KV_HEADS) for half in range(2)]
    units = jnp.concatenate([jnp.stack([bias[a] for a in heads]),
                             jnp.stack([bias[a + 2] for a in heads])], axis=1)
    masked = jnp.full((2 * N_KV_HEADS, 2 * BLOCK, BLOCK), NEG_INF, F32)
    prev = jnp.stack([units[:, :, :BLOCK], masked])
    cur = units[:, :, BLOCK:2 * BLOCK]
    nxt = jnp.stack([units[:, :, 2 * BLOCK:], masked])
    return prev, cur, nxt


def _prepare(rel_bias_table, w_in, b_in, w_pool, pool_scale, p_pool, sink, p_attn, w_mem_kv, p_mem,
             w_out, ln1_g, ln1_b, w_router_group, w_router_expert, w_gu, w_down):
    l = 0
    w_router = jnp.zeros((ROUTER_ROWS, D_MODEL), F32)
    w_router = w_router.at[:N_EXPERT_GROUPS].set(w_router_group[l].T)
    w_router = w_router.at[SUBLANES:].set(w_router_expert[l].T)
    bias_prev, bias_cur, bias_next = _bias_tables(rel_bias_table)
    return dict(
        w_in=w_in[l].astype(BF16), b_in=b_in[l][None, :],
        w_pool=w_pool[l].astype(BF16), pool_scale=pool_scale[l][None, :],
        p_pool=p_pool[l].astype(BF16), p_attn=p_attn[l].astype(BF16), p_mem=p_mem[l].astype(BF16),
        w_out=w_out[l].astype(BF16), w_memkv=w_mem_kv[l].astype(BF16),
        ln1_g=ln1_g[l][None, :], ln1_b=ln1_b[l][None, :],
        w_router=w_router.astype(BF16), sink=sink[l],
        w_gu=w_gu[l], w_down=w_down[l],
        bias_prev=bias_prev, bias_cur=bias_cur, bias_next=bias_next,
    )


def _dispatch_plan(eid, wts):
    t = eid.shape[1]
    a = TOP_K * t
    bm = BM_MOE
    nblk = a // bm
    eflat = eid.reshape(a)
    wflat = wts.reshape(a)
    order = jnp.argsort(eflat, stable=True).astype(I32)
    wsorted = wflat[order][:, None]
    experts = jnp.arange(N_EXPERTS, dtype=I32)
    counts = jnp.sum((eflat[None, :] == experts[:, None]).astype(I32), axis=1)
    ends = jnp.cumsum(counts).astype(I32)
    starts = ends - counts
    cuts = jnp.sort(jnp.concatenate([jnp.arange(nblk, dtype=I32) * bm, starts, jnp.array([a], I32)]))
    lo_abs = cuts[:-1]
    hi_abs = cuts[1:]
    nonempty = hi_abs > lo_abs
    blk = jnp.minimum(lo_abs // bm, nblk - 1)
    probe = jnp.minimum(lo_abs, a - 1)
    item_e = jnp.sum((ends[None, :] <= probe[:, None]).astype(I32), axis=1)
    item_lo = jnp.where(nonempty, lo_abs - blk * bm, 0).astype(I32)
    item_hi = jnp.where(nonempty, hi_abs - blk * bm, 0).astype(I32)
    return order, wsorted, blk.astype(I32), item_e.astype(I32), item_lo, item_hi


def _trunk(x, mem, ln_in_g, ln_in_b, ln2_g, ln2_b, p):
    batch, seq, d = x.shape
    t = batch * seq
    assert seq % TM_MIX == 0 and seq % TQ_ATTN == 0 and t % TM_INPROJ == 0 and (TOP_K * t) % BM_MOE == 0
    memkv = _memkv_call(mem.reshape(batch * MEM_TOKENS, d), p["w_memkv"])
    xn, u, q, kk, vv, qm, gate = _inproj_call(x.reshape(t, d), ln_in_g[None, :], ln_in_b[None, :],
                                              p["w_in"], p["b_in"])
    o = _attn_call(q, kk, vv, p["sink"], p["bias_prev"], p["bias_cur"], p["bias_next"], batch, seq)
    x1r, eid, wts = _mix_call(xn, u, o, qm, gate, memkv, p, batch, seq)
    order, wsorted, item_blk, item_e, item_lo, item_hi = _dispatch_plan(eid, wts)
    sorted_tok = jnp.where(order >= t, order - t, order)
    xs = _sc_gather_rows(x1r, sorted_tok)
    ys = _moe_call(xs, wsorted, item_blk, item_e, item_lo, item_hi, p["w_gu"], p["w_down"])
    y2 = _sc_scatter_rows(ys, order)
    out = _final_call(x1r, y2, ln2_g[None, :], ln2_b[None, :])
    return out.reshape(batch, seq, d)


def kernel(x_prompt, x_sample, mem_prompt, mem_sample, ln_in_g, ln_in_b, rel_bias_table, w_in, b_in,
           w_pool, pool_scale, p_pool, sink, p_attn, w_mem_kv, p_mem, w_out, ln1_g, ln1_b,
           w_router_group, w_router_expert, w_gu, w_down, ln2_g, ln2_b):
    p = _prepare(rel_bias_table, w_in, b_in, w_pool, pool_scale, p_pool, sink, p_attn, w_mem_kv, p_mem,
                 w_out, ln1_g, ln1_b, w_router_group, w_router_expert, w_gu, w_down)
    y_prompt = _trunk(x_prompt, mem_prompt, ln_in_g, ln_in_b, ln2_g[0], ln2_b[0], p)
    y_sample = _trunk(x_sample, mem_sample, ln_in_g, ln_in_b, ln2_g[0], ln2_b[0], p)
    return (y_prompt, y_sample)
```

```python
import functools
import math

import jax
import jax.numpy as jnp
from jax import lax
from jax.experimental import pallas as pl
from jax.experimental.pallas import tpu as pltpu
from jax.experimental.pallas import tpu_sc as plsc

F32 = jnp.float32
BF16 = jnp.bfloat16
I32 = jnp.int32

D_MODEL = 1024
DEPTH = 1
POOL_WIDTH = 512
POOL_WINDOWS = (2, 4, 8, 16)
N_POOL_GROUPS = 4
POOL_GROUP_CH = POOL_WIDTH // N_POOL_GROUPS
N_Q_HEADS = 16
N_KV_HEADS = 4
HEAD_DIM = 64
ATTN_WIDTH = N_Q_HEADS * HEAD_DIM
KV_WIDTH = N_KV_HEADS * HEAD_DIM
WINDOW = 128
BLOCK = 128
N_REL_BUCKETS = 32
REL_MAX_DISTANCE = 128
MEM_TOKENS = 256
MEM_HEADS = 4
MEM_HEAD_DIM = 128
MEM_WIDTH = MEM_HEADS * MEM_HEAD_DIM
N_BRANCHES = 3
N_EXPERT_GROUPS = 4
EXPERTS_PER_GROUP = 8
N_EXPERTS = N_EXPERT_GROUPS * EXPERTS_PER_GROUP
TOP_K = 2
EXPERT_HIDDEN = 512
ALPHA = (2 * DEPTH) ** 0.25
LN_EPS = 1e-5
NEG_INF = -1e30

LANES = 128
SUBLANES = 8
ROW_TILE = D_MODEL // LANES
VMEM_LIMIT_BYTES = 56 * 1024 * 1024

KV_DUP_WIDTH = N_KV_HEADS * LANES

C_U = 0
C_Q = C_U + POOL_WIDTH
C_K = C_Q + ATTN_WIDTH
C_V = C_K + KV_WIDTH
C_M = C_V + KV_WIDTH
C_G = C_M + MEM_WIDTH
C_END = C_G + N_BRANCHES * D_MODEL

TM_INPROJ = 512
TQ_ATTN = 256
TM_MIX = 512
BM_MOE = 512
SC_ROW_WINDOW = 32
TM_FINAL = 512
POOL_HALO = 8
ROUTER_ROWS = 40


def _layer_norm(x, g, b):
    mu = jnp.mean(x, axis=-1, keepdims=True)
    xc = x - mu
    var = jnp.mean(xc * xc, axis=-1, keepdims=True)
    return xc * lax.rsqrt(var + LN_EPS) * g + b


def _sigmoid(x):
    return 1.0 / (1.0 + jnp.exp(-x))


def _rows_to_tiles(x):
    return pltpu.einshape("r(cl)->rcl", x, c=ROW_TILE)


def _tiles_to_rows(x):
    return pltpu.einshape("rcl->r(cl)", x)


def _params(**kw):
    return pltpu.CompilerParams(vmem_limit_bytes=VMEM_LIMIT_BYTES, **kw)


def _memkv_kernel(mem_ref, w_ref, out_ref):
    out_ref[...] = jnp.dot(mem_ref[...].astype(BF16), w_ref[...],
                           preferred_element_type=F32).astype(BF16)


def _memkv_call(mem2, w_memkv):
    rows = mem2.shape[0]
    return pl.pallas_call(
        _memkv_kernel,
        grid=(rows // MEM_TOKENS,),
        in_specs=[pl.BlockSpec((MEM_TOKENS, D_MODEL), lambda i: (i, 0)),
                  pl.BlockSpec((D_MODEL, 2 * MEM_WIDTH), lambda i: (0, 0))],
        out_specs=pl.BlockSpec((MEM_TOKENS, 2 * MEM_WIDTH), lambda i: (i, 0)),
        out_shape=jax.ShapeDtypeStruct((rows, 2 * MEM_WIDTH), BF16),
        compiler_params=_params(dimension_semantics=("arbitrary",)),
        name="memkv",
    )(mem2, w_memkv)


def _inproj_kernel(x_ref, g_ref, b_ref, w_ref, bias_ref,
                   xn_ref, u_ref, q_ref, k_ref, v_ref, qm_ref, gate_ref):
    xn = _layer_norm(x_ref[...], g_ref[...], b_ref[...])
    xn_ref[...] = xn
    xb = xn.astype(BF16)

    def seg(lo, hi):
        return jnp.dot(xb, w_ref[:, lo:hi], preferred_element_type=F32) + bias_ref[:, lo:hi]

    u_ref[...] = seg(C_U, C_Q)
    q_ref[...] = (seg(C_Q, C_K) * (HEAD_DIM ** -0.5)).astype(BF16)
    kv = seg(C_K, C_M)
    low = lax.broadcasted_iota(I32, (kv.shape[0], LANES), 1) < HEAD_DIM
    for out_ref, c_lo in ((k_ref, 0), (v_ref, KV_WIDTH)):
        for c in range(KV_WIDTH // LANES):
            blk = kv[:, c_lo + c * LANES:c_lo + (c + 1) * LANES]
            rot = pltpu.roll(blk, HEAD_DIM, 1)
            out_ref[:, 2 * c * LANES:(2 * c + 1) * LANES] = jnp.where(low, blk, rot).astype(BF16)
            out_ref[:, (2 * c + 1) * LANES:(2 * c + 2) * LANES] = jnp.where(low, rot, blk).astype(BF16)
    qm_ref[...] = seg(C_M, C_G).astype(BF16)
    for j in range(N_BRANCHES):
        lo = C_G + j * D_MODEL
        gate_ref[:, j * D_MODEL:(j + 1) * D_MODEL] = _sigmoid(seg(lo, lo + D_MODEL)).astype(BF16)


def _inproj_call(x2, ln_g, ln_b, w_in, b_in):
    t = x2.shape[0]
    tm = TM_INPROJ
    row = lambda width: pl.BlockSpec((tm, width), lambda i: (i, 0))
    const = lambda shape: pl.BlockSpec(shape, lambda i: (0, 0))
    widths = (D_MODEL, POOL_WIDTH, ATTN_WIDTH, KV_DUP_WIDTH, KV_DUP_WIDTH, MEM_WIDTH, N_BRANCHES * D_MODEL)
    dtypes = (F32, F32, BF16, BF16, BF16, BF16, BF16)
    return pl.pallas_call(
        _inproj_kernel,
        grid=(t // tm,),
        in_specs=[row(D_MODEL), const((1, D_MODEL)), const((1, D_MODEL)),
                  const((D_MODEL, C_END)), const((1, C_END))],
        out_specs=[row(w) for w in widths],
        out_shape=[jax.ShapeDtypeStruct((t, w), d) for w, d in zip(widths, dtypes)],
        compiler_params=_params(dimension_semantics=("arbitrary",)),
        name="inproj",
    )(x2, ln_g, ln_b, w_in, b_in)


def _attn_kernel(sink_ref, q_ref, kc_ref, kp_ref, kn_ref, vc_ref, vp_ref, vn_ref,
                 bp_ref, bc_ref, bn_ref, o_ref, klo, khi, vlo, vhi, s_scr, p_scr):
    i = pl.program_id(1)
    n_tiles = pl.num_programs(1)
    tq = q_ref.shape[0]
    n_qb = tq // BLOCK

    lane = lax.broadcasted_iota(I32, (BLOCK, KV_DUP_WIDTH), 1)
    low = (lane & (LANES - 1)) < HEAD_DIM

    def put(dst_lo, dst_hi, r0, val):
        zero = jnp.zeros_like(val)
        dst_lo[r0:r0 + BLOCK, :] = jnp.where(low, val, zero)
        dst_hi[r0:r0 + BLOCK, :] = jnp.where(low, zero, val)

    put(klo, khi, 0, kp_ref[...])
    put(vlo, vhi, 0, vp_ref[...])
    for j in range(n_qb):
        put(klo, khi, (j + 1) * BLOCK, kc_ref[j * BLOCK:(j + 1) * BLOCK, :])
        put(vlo, vhi, (j + 1) * BLOCK, vc_ref[j * BLOCK:(j + 1) * BLOCK, :])
    put(klo, khi, (n_qb + 1) * BLOCK, kn_ref[...])
    put(vlo, vhi, (n_qb + 1) * BLOCK, vn_ref[...])

    first = (i == 0).astype(I32)
    last = (i == n_tiles - 1).astype(I32)

    def softmax_rows(s, sink):
        m = jnp.maximum(jnp.max(s, axis=-1, keepdims=True), sink)
        p = jnp.exp(s - m)
        denom = jnp.sum(p, axis=-1, keepdims=True) + jnp.exp(sink - m)
        return (p * (1.0 / denom)).astype(BF16)

    units = [(j, h, half) for j in range(n_qb) for h in range(N_KV_HEADS) for half in range(2)]
    always = i >= 0

    @pl.when(always)
    def _scores():
        for u, (j, h, half) in enumerate(units):
            r0 = j * BLOCK
            c0 = h * 2 * LANES
            pv = first if j == 0 else 0
            nv = last if j == n_qb - 1 else 0
            unit = h * 2 + half
            q_pairs = jnp.concatenate([q_ref[r0:r0 + BLOCK, c0:c0 + LANES],
                                       q_ref[r0:r0 + BLOCK, c0 + LANES:c0 + 2 * LANES]], axis=0)
            kx = (klo, khi)[half][r0:r0 + 3 * BLOCK, h * LANES:(h + 1) * LANES]
            s = lax.dot_general(q_pairs, kx, (((1,), (1,)), ((), ())), preferred_element_type=F32)
            bias = jnp.concatenate([bp_ref[pv, unit], bc_ref[unit], bn_ref[nv, unit]], axis=1)
            s_scr[u] = s + bias

    @pl.when(always)
    def _softmax():
        for u, (j, h, half) in enumerate(units):
            p_scr[u, :BLOCK] = softmax_rows(s_scr[u, :BLOCK], sink_ref[4 * h + half])
            p_scr[u, BLOCK:] = softmax_rows(s_scr[u, BLOCK:], sink_ref[4 * h + 2 + half])

    @pl.when(always)
    def _values():
        for j in range(n_qb):
            r0 = j * BLOCK
            for h in range(N_KV_HEADS):
                c0 = h * 2 * LANES
                acc = jnp.zeros((2 * BLOCK, LANES), F32)
                for half, vref in enumerate((vlo, vhi)):
                    u = units.index((j, h, half))
                    vx = vref[r0:r0 + 3 * BLOCK, h * LANES:(h + 1) * LANES]
                    acc = acc + jnp.dot(p_scr[u], vx, preferred_element_type=F32)
                o_ref[r0:r0 + BLOCK, c0:c0 + LANES] = acc[:BLOCK].astype(BF16)
                o_ref[r0:r0 + BLOCK, c0 + LANES:c0 + 2 * LANES] = acc[BLOCK:].astype(BF16)


def _attn_call(q, kk, vv, sink, bias_prev, bias_cur, bias_next, batch, seq):
    tq = TQ_ATTN
    n_qb = tq // BLOCK
    nb = seq // BLOCK
    n_tiles = seq // tq
    t = batch * seq

    cur = lambda width: pl.BlockSpec((tq, width), lambda b, i: (b * n_tiles + i, 0))
    prev = pl.BlockSpec((BLOCK, KV_DUP_WIDTH), lambda b, i: (b * nb + jnp.maximum(i * n_qb - 1, 0), 0))
    nxt = pl.BlockSpec((BLOCK, KV_DUP_WIDTH), lambda b, i: (b * nb + jnp.minimum((i + 1) * n_qb, nb - 1), 0))
    full = lambda a: pl.BlockSpec(a.shape, lambda b, i: (0,) * a.ndim)
    ext = (n_qb + 2) * BLOCK
    n_units = n_qb * N_KV_HEADS * 2
    return pl.pallas_call(
        _attn_kernel,
        grid=(batch, n_tiles),
        in_specs=[pl.BlockSpec(memory_space=pltpu.SMEM),
                  cur(ATTN_WIDTH), cur(KV_DUP_WIDTH), prev, nxt, cur(KV_DUP_WIDTH), prev, nxt,
                  full(bias_prev), full(bias_cur), full(bias_next)],
        out_specs=cur(ATTN_WIDTH),
        out_shape=jax.ShapeDtypeStruct((t, ATTN_WIDTH), BF16),
        scratch_shapes=[pltpu.VMEM((ext, KV_DUP_WIDTH), BF16) for _ in range(4)]
                       + [pltpu.VMEM((n_units, 2 * BLOCK, 3 * BLOCK), F32),
                          pltpu.VMEM((n_units, 2 * BLOCK, 3 * BLOCK), BF16)],
        compiler_params=_params(dimension_semantics=("arbitrary", "arbitrary")),
        name="attn",
    )(sink, q, kk, kk, kk, vv, vv, vv, bias_prev, bias_cur, bias_next)


def _mix_kernel(xn_ref, u_ref, up_ref, un_ref, o_ref, qm_ref, gate_ref, memkv_ref,
                wpool_ref, pscale_ref, ppool_ref, pattn_ref, pmem_ref, wout_ref,
                g1_ref, b1_ref, wr_ref,
                x1r_ref, eid_ref, wts_ref, uext, *, seq):
    i = pl.program_id(1)
    n_tiles = pl.num_programs(1)
    tm = xn_ref.shape[0]

    halo = POOL_HALO
    uext[0:halo, :] = jnp.where(i > 0, up_ref[...], 0.0)
    uext[halo:halo + tm, :] = u_ref[...]
    uext[halo + tm:halo + tm + halo, :] = jnp.where(i < n_tiles - 1, un_ref[...], 0.0)
    pos = i * tm + lax.broadcasted_iota(I32, (tm, 1), 0)
    mixed = []
    for gi, win in enumerate(POOL_WINDOWS):
        c0 = gi * POOL_GROUP_CH
        half = win // 2
        total = jnp.zeros((tm, POOL_GROUP_CH), F32)
        for off in range(-half, half):
            total = total + uext[halo + off:halo + off + tm, c0:c0 + POOL_GROUP_CH]
        cnt = (jnp.minimum(pos + half, seq) - jnp.maximum(pos - half, 0)).astype(F32)
        pooled = total * (1.0 / cnt) - u_ref[:, c0:c0 + POOL_GROUP_CH]
        mixed.append(jnp.dot(pooled.astype(BF16), wpool_ref[gi], preferred_element_type=F32)
                     * pscale_ref[:, c0:c0 + POOL_GROUP_CH])
    mixed = jnp.concatenate(mixed, axis=1).astype(BF16)
    merged = gate_ref[:, 0:D_MODEL].astype(F32) * jnp.dot(
        mixed, ppool_ref[...], preferred_element_type=F32)

    merged = merged + gate_ref[:, D_MODEL:2 * D_MODEL].astype(F32) * jnp.dot(
        o_ref[...], pattn_ref[...], preferred_element_type=F32)

    heads = []
    for h in range(MEM_HEADS):
        c0 = h * MEM_HEAD_DIM
        km = memkv_ref[:, c0:c0 + MEM_HEAD_DIM]
        vm = memkv_ref[:, MEM_WIDTH + c0:MEM_WIDTH + c0 + MEM_HEAD_DIM]
        s = lax.dot_general(qm_ref[:, c0:c0 + MEM_HEAD_DIM], km, (((1,), (1,)), ((), ())),
                            preferred_element_type=F32) * (MEM_HEAD_DIM ** -0.5)
        p = jnp.exp(s - jnp.max(s, axis=-1, keepdims=True))
        p = (p / jnp.sum(p, axis=-1, keepdims=True)).astype(BF16)
        heads.append(jnp.dot(p, vm, preferred_element_type=F32).astype(BF16))
    om = jnp.concatenate(heads, axis=1)
    merged = merged + gate_ref[:, 2 * D_MODEL:3 * D_MODEL].astype(F32) * jnp.dot(
        om, pmem_ref[...], preferred_element_type=F32)

    y = jnp.dot(merged.astype(BF16), wout_ref[...], preferred_element_type=F32)
    x1 = _layer_norm(ALPHA * xn_ref[...] + y, g1_ref[...], b1_ref[...])
    x1r_ref[...] = _rows_to_tiles(x1)

    lt = lax.dot_general(wr_ref[...], x1.astype(BF16), (((1,), (1,)), ((), ())),
                         preferred_element_type=F32)
    gl = [lt[r:r + 1, :] for r in range(N_EXPERT_GROUPS)]
    gmax = gl[0]
    grp = jnp.zeros((1, tm), I32)
    for r in range(1, N_EXPERT_GROUPS):
        better = gl[r] > gmax
        grp = jnp.where(better, r, grp)
        gmax = jnp.where(better, gl[r], gmax)
    gsum = gl[0] * 0.0
    for r in range(N_EXPERT_GROUPS):
        gsum = gsum + jnp.exp(gl[r] - gmax)
    gp = 1.0 / gsum
    sel = jnp.zeros((EXPERTS_PER_GROUP, tm), F32)
    for r in range(N_EXPERT_GROUPS):
        rows = lt[SUBLANES + r * EXPERTS_PER_GROUP:SUBLANES + (r + 1) * EXPERTS_PER_GROUP, :]
        sel = jnp.where(grp == r, rows, sel)
    ridx = lax.broadcasted_iota(I32, (EXPERTS_PER_GROUP, tm), 0)
    top1 = jnp.max(sel, axis=0, keepdims=True)
    i1 = jnp.min(jnp.where(sel == top1, ridx, EXPERTS_PER_GROUP), axis=0, keepdims=True)
    rest = jnp.where(ridx == i1, -jnp.inf, sel)
    top2 = jnp.max(rest, axis=0, keepdims=True)
    i2 = jnp.min(jnp.where(rest == top2, ridx, EXPERTS_PER_GROUP), axis=0, keepdims=True)
    e2 = jnp.exp(top2 - top1)
    inv = gp / (1.0 + e2)
    eid_ref[0:1, :] = grp * EXPERTS_PER_GROUP + i1
    eid_ref[1:2, :] = grp * EXPERTS_PER_GROUP + i2
    wts_ref[0:1, :] = inv
    wts_ref[1:2, :] = e2 * inv


def _mix_call(xn, u, o, qm, gate, memkv, p, batch, seq):
    tm = TM_MIX
    n_tiles = seq // tm
    t = batch * seq
    hb = tm // POOL_HALO

    row = lambda width: pl.BlockSpec((tm, width), lambda b, i: (b * n_tiles + i, 0))
    prev = pl.BlockSpec((POOL_HALO, POOL_WIDTH),
                        lambda b, i: (jnp.maximum((b * n_tiles + i) * hb - 1, 0), 0))
    nxt = pl.BlockSpec((POOL_HALO, POOL_WIDTH),
                       lambda b, i: (jnp.minimum((b * n_tiles + i + 1) * hb, t // POOL_HALO - 1), 0))
    full = lambda a: pl.BlockSpec(a.shape, lambda b, i: (0,) * a.ndim)
    lane_row = pl.BlockSpec((TOP_K, tm), lambda b, i: (0, b * n_tiles + i))
    weights = (p["w_pool"], p["pool_scale"], p["p_pool"], p["p_attn"], p["p_mem"], p["w_out"],
               p["ln1_g"], p["ln1_b"], p["w_router"])
    return pl.pallas_call(
        functools.partial(_mix_kernel, seq=seq),
        grid=(batch, n_tiles),
        in_specs=[row(D_MODEL), row(POOL_WIDTH), prev, nxt, row(ATTN_WIDTH), row(MEM_WIDTH),
                  row(N_BRANCHES * D_MODEL),
                  pl.BlockSpec((MEM_TOKENS, 2 * MEM_WIDTH), lambda b, i: (b, 0))]
                 + [full(w) for w in weights],
        out_specs=[pl.BlockSpec((tm, ROW_TILE, LANES), lambda b, i: (b * n_tiles + i, 0, 0)),
                   lane_row, lane_row],
        out_shape=[jax.ShapeDtypeStruct((t, ROW_TILE, LANES), F32),
                   jax.ShapeDtypeStruct((TOP_K, t), I32),
                   jax.ShapeDtypeStruct((TOP_K, t), F32)],
        scratch_shapes=[pltpu.VMEM((tm + 2 * POOL_HALO, POOL_WIDTH), F32)],
        compiler_params=_params(dimension_semantics=("arbitrary", "arbitrary")),
        name="mix",
    )(xn, u, u, u, o, qm, gate, memkv, *weights)


def _moe_kernel(iblk_ref, ie_ref, inext_ref, irun_ref, ilo_ref, ihi_ref,
                x_ref, ws_ref, wgu_hbm, wd_hbm, y_ref,
                x_bf, h_bf, wgu_f32, wd_f32, wgu_bf, wd_bf, wsem):
    i = pl.program_id(0)
    bm = x_ref.shape[0]
    lo = ilo_ref[i]
    hi = ihi_ref[i]
    nonempty = hi > lo

    def weight_copies(e, ws):
        return (pltpu.make_async_copy(wgu_hbm.at[e], wgu_f32.at[ws], wsem.at[ws]),
                pltpu.make_async_copy(wd_hbm.at[e], wd_f32.at[ws], wsem.at[ws]))

    @pl.when(jnp.logical_or(i == 0, ie_ref[i] != ie_ref[jnp.maximum(i - 1, 0)]))
    def _():
        e = ie_ref[i]
        wslot = irun_ref[i] & 1

        @pl.when(i == 0)
        def _():
            for c in weight_copies(e, wslot):
                c.start()
        for c in weight_copies(e, wslot):
            c.wait()

        @pl.when(inext_ref[i] >= 0)
        def _():
            for c in weight_copies(inext_ref[i], 1 - wslot):
                c.start()
        wgu_bf[...] = wgu_f32[wslot].astype(BF16)
        wd_bf[...] = wd_f32[wslot].astype(BF16)

    @pl.when(nonempty)
    def _():
        x_bf[...] = _tiles_to_rows(x_ref[...]).astype(BF16)

    @pl.when(nonempty)
    def _():
        gu = jnp.dot(x_bf[...], wgu_bf[...], preferred_element_type=F32)
        gate = gu[:, :EXPERT_HIDDEN]
        h_bf[...] = ((gate * _sigmoid(gate)) * gu[:, EXPERT_HIDDEN:]).astype(BF16)

    @pl.when(nonempty)
    def _():
        rows = lax.broadcasted_iota(I32, (bm, 1), 0)
        mine = jnp.logical_and(rows >= lo, rows < hi)
        y = jnp.dot(h_bf[...], wd_bf[...], preferred_element_type=F32) * ws_ref[...]
        y = _rows_to_tiles(jnp.where(mine, y, 0.0))

        @pl.when(lo == 0)
        def _():
            y_ref[...] = y

        @pl.when(lo > 0)
        def _():
            y_ref[...] = y_ref[...] + y


def _moe_call(xs, wsorted, item_blk, item_e, item_lo, item_hi, w_gu, w_down):
    a = xs.shape[0]
    bm = BM_MOE
    n_items = item_blk.shape[0]
    change = jnp.concatenate([jnp.zeros((1,), I32), (item_e[1:] != item_e[:-1]).astype(I32)])
    item_run = jnp.cumsum(change).astype(I32)
    first_later = jnp.sum((item_e[None, :] <= item_e[:, None]).astype(I32), axis=1)
    item_next = jnp.where(first_later < n_items, item_e[jnp.minimum(first_later, n_items - 1)], -1).astype(I32)
    grid_spec = pltpu.PrefetchScalarGridSpec(
        num_scalar_prefetch=6,
        grid=(n_items,),
        in_specs=[pl.BlockSpec((bm, ROW_TILE, LANES), lambda i, blk, *_: (blk[i], 0, 0)),
                  pl.BlockSpec((bm, 1), lambda i, blk, *_: (blk[i], 0)),
                  pl.BlockSpec(memory_space=pl.ANY),
                  pl.BlockSpec(memory_space=pl.ANY)],
        out_specs=pl.BlockSpec((bm, ROW_TILE, LANES), lambda i, blk, *_: (blk[i], 0, 0)),
        scratch_shapes=[pltpu.VMEM((bm, D_MODEL), BF16),
                        pltpu.VMEM((bm, EXPERT_HIDDEN), BF16),
                        pltpu.VMEM((2, D_MODEL, 2 * EXPERT_HIDDEN), F32),
                        pltpu.VMEM((2, EXPERT_HIDDEN, D_MODEL), F32),
                        pltpu.VMEM((D_MODEL, 2 * EXPERT_HIDDEN), BF16),
                        pltpu.VMEM((EXPERT_HIDDEN, D_MODEL), BF16),
                        pltpu.SemaphoreType.DMA((2,))],
    )
    return pl.pallas_call(
        _moe_kernel,
        grid_spec=grid_spec,
        out_shape=jax.ShapeDtypeStruct((a, ROW_TILE, LANES), F32),
        compiler_params=_params(dimension_semantics=("arbitrary",)),
        name="moe",
    )(item_blk, item_e, item_next, item_run, item_lo, item_hi, xs, wsorted, w_gu, w_down)


def _sc_mesh():
    return plsc.VectorSubcoreMesh(core_axis_name="core", subcore_axis_name="subcore")


def _sc_gather_rows(rows, idx):
    m = idx.shape[0]
    w = SC_ROW_WINDOW

    @pl.kernel(out_type=jax.ShapeDtypeStruct((m,) + rows.shape[1:], rows.dtype), mesh=_sc_mesh(),
               scratch_types=[])
    def gather(rows_hbm, idx_hbm, out_hbm):
        def body(idx_vmem, out_vmem):
            pltpu.sync_copy(rows_hbm.at[idx_vmem.at[0]], out_vmem)

        pltpu.emit_pipeline(
            body,
            grid=(m // w,),
            in_specs=[pl.BlockSpec((1, w), index_map=lambda i: (i, 0))],
            out_specs=[pl.BlockSpec((w,) + rows.shape[1:], index_map=lambda i: (i, 0, 0))],
            core_axis_name=("core", "subcore"),
            dimension_semantics=(pltpu.PARALLEL,),
        )(idx_hbm, out_hbm)

    return gather(rows, idx.reshape(m // w, w))


def _sc_scatter_rows(rows, idx):
    m = idx.shape[0]
    w = SC_ROW_WINDOW

    @pl.kernel(out_type=jax.ShapeDtypeStruct(rows.shape, rows.dtype), mesh=_sc_mesh(), scratch_types=[])
    def scatter(rows_hbm, idx_hbm, out_hbm):
        def body(rows_vmem, idx_vmem):
            pltpu.sync_copy(rows_vmem, out_hbm.at[idx_vmem.at[0]])

        pltpu.emit_pipeline(
            body,
            grid=(m // w,),
            in_specs=[pl.BlockSpec((w,) + rows.shape[1:], index_map=lambda i: (i, 0, 0)),
                      pl.BlockSpec((1, w), index_map=lambda i: (i, 0))],
            out_specs=[],
            core_axis_name=("core", "subcore"),
            dimension_semantics=(pltpu.PARALLEL,),
        )(rows_hbm, idx_hbm)

    return scatter(rows, idx.reshape(m // w, w))


def _final_kernel(x1_ref, ya_ref, yb_ref, g_ref, b_ref, out_ref):
    moe = ya_ref[...] + yb_ref[...]
    out_ref[...] = _layer_norm(_tiles_to_rows(ALPHA * x1_ref[...] + moe), g_ref[...], b_ref[...])


def _final_call(x1, y2, ln_g, ln_b):
    t = x1.shape[0]
    tm = TM_FINAL
    n_tiles = t // tm
    return pl.pallas_call(
        _final_kernel,
        grid=(n_tiles,),
        in_specs=[pl.BlockSpec((tm, ROW_TILE, LANES), lambda i: (i, 0, 0)),
                  pl.BlockSpec((tm, ROW_TILE, LANES), lambda i: (i, 0, 0)),
                  pl.BlockSpec((tm, ROW_TILE, LANES), lambda i: (n_tiles + i, 0, 0)),
                  pl.BlockSpec((1, D_MODEL), lambda i: (0, 0)),
                  pl.BlockSpec((1, D_MODEL), lambda i: (0, 0))],
        out_specs=pl.BlockSpec((tm, D_MODEL), lambda i: (i, 0)),
        out_shape=jax.ShapeDtypeStruct((t, D_MODEL), F32),
        compiler_params=_params(dimension_semantics=("arbitrary",)),
        name="final",
    )(x1, y2, y2, ln_g, ln_b)


def _t5_bucket(rel):
    nb = N_REL_BUCKETS // 2
    max_exact = nb // 2
    ret = jnp.where(rel > 0, nb, 0)
    n = jnp.abs(rel)
    nf = jnp.maximum(n, 1).astype(F32)
    large = max_exact + (jnp.log(nf / max_exact) / math.log(REL_MAX_DISTANCE / max_exact)
                         * (nb - max_exact)).astype(I32)
    large = jnp.minimum(large, nb - 1)
    return ret + jnp.where(n < max_exact, n, large)


def _bias_tables(rel_table):
    n, m = BLOCK, 3 * BLOCK
    rel = jnp.arange(-(2 * BLOCK - 1), 2 * BLOCK)
    by_rel = jnp.where((jnp.abs(rel) <= WINDOW)[:, None], rel_table[_t5_bucket(rel)].astype(F32), NEG_INF)
    by_rel = jnp.pad(by_rel.T, ((0, 0), (0, 1)))
    skew = jnp.tile(by_rel, (1, n))[:, :n * (m + n - 1)].reshape(N_Q_HEADS, n, m + n - 1)
    bias = skew[:, :, n - 1:n - 1 + m]
    heads = [4 * h + half for h in range(N_KV_HEADS) for half in range(2)]
    units = jnp.concatenate([jnp.stack([bias[a] for a in heads]),
                             jnp.stack([bias[a + 2] for a in heads])], axis=1)
    masked = jnp.full((2 * N_KV_HEADS, 2 * BLOCK, BLOCK), NEG_INF, F32)
    prev = jnp.stack([units[:, :, :BLOCK], masked])
    cur = units[:, :, BLOCK:2 * BLOCK]
    nxt = jnp.stack([units[:, :, 2 * BLOCK:], masked])
    return prev, cur, nxt


def _prepare(rel_bias_table, w_in, b_in, w_pool, pool_scale, p_pool, sink, p_attn, w_mem_kv, p_mem,
             w_out, ln1_g, ln1_b, w_router_group, w_router_expert, w_gu, w_down):
    l = 0
    w_router = jnp.zeros((ROUTER_ROWS, D_MODEL), F32)
    w_router = w_router.at[:N_EXPERT_GROUPS].set(w_router_group[l].T)
    w_router = w_router.at[SUBLANES:].set(w_router_expert[l].T)
    bias_prev, bias_cur, bias_next = _bias_tables(rel_bias_table)
    return dict(
        w_in=w_in[l].astype(BF16), b_in=b_in[l][None, :],
        w_pool=w_pool[l].astype(BF16), pool_scale=pool_scale[l][None, :],
        p_pool=p_pool[l].astype(BF16), p_attn=p_attn[l].astype(BF16), p_mem=p_mem[l].astype(BF16),
        w_out=w_out[l].astype(BF16), w_memkv=w_mem_kv[l].astype(BF16),
        ln1_g=ln1_g[l][None, :], ln1_b=ln1_b[l][None, :],
        w_router=w_router.astype(BF16), sink=sink[l],
        w_gu=w_gu[l], w_down=w_down[l],
        bias_prev=bias_prev, bias_cur=bias_cur, bias_next=bias_next,
    )


def _dispatch_plan(eid, wts):
    t = eid.shape[1]
    a = TOP_K * t
    bm = BM_MOE
    nblk = a // bm
    eflat = eid.reshape(a)
    wflat = wts.reshape(a)
    order = jnp.argsort(eflat, stable=True).astype(I32)
    wsorted = wflat[order][:, None]
    experts = jnp.arange(N_EXPERTS, dtype=I32)
    counts = jnp.sum((eflat[None, :] == experts[:, None]).astype(I32), axis=1)
    ends = jnp.cumsum(counts).astype(I32)
    starts = ends - counts
    cuts = jnp.sort(jnp.concatenate([jnp.arange(nblk, dtype=I32) * bm, starts, jnp.array([a], I32)]))
    lo_abs = cuts[:-1]
    hi_abs = cuts[1:]
    nonempty = hi_abs > lo_abs
    blk = jnp.minimum(lo_abs // bm, nblk - 1)
    probe = jnp.minimum(lo_abs, a - 1)
    item_e = jnp.sum((ends[None, :] <= probe[:, None]).astype(I32), axis=1)
    item_lo = jnp.where(nonempty, lo_abs - blk * bm, 0).astype(I32)
    item_hi = jnp.where(nonempty, hi_abs - blk * bm, 0).astype(I32)
    return order, wsorted, blk.astype(I32), item_e.astype(I32), item_lo, item_hi


class _Trunk:
    def __init__(self, x, mem, ln_in_g, ln_in_b, ln2_g, ln2_b, p):
        self.batch, self.seq, self.d = x.shape
        self.t = self.batch * self.seq
        assert (self.seq % TM_MIX == 0 and self.seq % TQ_ATTN == 0 and self.t % TM_INPROJ == 0
                and (TOP_K * self.t) % BM_MOE == 0)
        self.x, self.mem, self.p = x, mem, p
        self.ln_in = (ln_in_g[None, :], ln_in_b[None, :])
        self.ln2 = (ln2_g[None, :], ln2_b[None, :])

    def project(self):
        self.memkv = _memkv_call(self.mem.reshape(self.batch * MEM_TOKENS, self.d), self.p["w_memkv"])
        self.h = _inproj_call(self.x.reshape(self.t, self.d), *self.ln_in, self.p["w_in"], self.p["b_in"])

    def attend(self):
        p = self.p
        _, _, q, kk, vv, _, _ = self.h
        self.o = _attn_call(q, kk, vv, p["sink"], p["bias_prev"], p["bias_cur"], p["bias_next"],
                            self.batch, self.seq)

    def mix(self):
        xn, u, _, _, _, qm, gate = self.h
        self.x1r, eid, wts = _mix_call(xn, u, self.o, qm, gate, self.memkv, self.p, self.batch, self.seq)
        self.order, self.wsorted, *self.items = _dispatch_plan(eid, wts)

    def dispatch(self):
        sorted_tok = jnp.where(self.order >= self.t, self.order - self.t, self.order)
        self.xs = _sc_gather_rows(self.x1r, sorted_tok)

    def experts(self):
        self.ys = _moe_call(self.xs, self.wsorted, *self.items, self.p["w_gu"], self.p["w_down"])

    def combine(self):
        self.y2 = _sc_scatter_rows(self.ys, self.order)

    def finish(self):
        out = _final_call(self.x1r, self.y2, *self.ln2)
        return out.reshape(self.batch, self.seq, self.d)


def kernel(x_prompt, x_sample, mem_prompt, mem_sample, ln_in_g, ln_in_b, rel_bias_table, w_in, b_in,
           w_pool, pool_scale, p_pool, sink, p_attn, w_mem_kv, p_mem, w_out, ln1_g, ln1_b,
           w_router_group, w_router_expert, w_gu, w_down, ln2_g, ln2_b):
    p = _prepare(rel_bias_table, w_in, b_in, w_pool, pool_scale, p_pool, sink, p_attn, w_mem_kv, p_mem,
                 w_out, ln1_g, ln1_b, w_router_group, w_router_expert, w_gu, w_down)
    a = _Trunk(x_prompt, mem_prompt, ln_in_g, ln_in_b, ln2_g[0], ln2_b[0], p)
    b = _Trunk(x_sample, mem_sample, ln_in_g, ln_in_b, ln2_g[0], ln2_b[0], p)
    a.project(); a.attend(); b.project(); a.mix()
    a.dispatch(); b.attend(); b.mix()
    b.dispatch(); a.experts()
    a.combine(); b.experts()
    b.combine(); y_prompt = a.finish()
    y_sample = b.finish()
    return (y_prompt, y_sample)
```

```python
import functools
import math

import jax
import jax.numpy as jnp
from jax import lax
from jax.experimental import pallas as pl
from jax.experimental.pallas import tpu as pltpu
from jax.experimental.pallas import tpu_sc as plsc

F32 = jnp.float32
BF16 = jnp.bfloat16
I32 = jnp.int32

D_MODEL = 1024
DEPTH = 1
POOL_WIDTH = 512
POOL_WINDOWS = (2, 4, 8, 16)
N_POOL_GROUPS = 4
POOL_GROUP_CH = POOL_WIDTH // N_POOL_GROUPS
N_Q_HEADS = 16
N_KV_HEADS = 4
HEAD_DIM = 64
ATTN_WIDTH = N_Q_HEADS * HEAD_DIM
KV_WIDTH = N_KV_HEADS * HEAD_DIM
WINDOW = 128
BLOCK = 128
N_REL_BUCKETS = 32
REL_MAX_DISTANCE = 128
MEM_TOKENS = 256
MEM_HEADS = 4
MEM_HEAD_DIM = 128
MEM_WIDTH = MEM_HEADS * MEM_HEAD_DIM
N_BRANCHES = 3
N_EXPERT_GROUPS = 4
EXPERTS_PER_GROUP = 8
N_EXPERTS = N_EXPERT_GROUPS * EXPERTS_PER_GROUP
TOP_K = 2
EXPERT_HIDDEN = 512
ALPHA = (2 * DEPTH) ** 0.25
LN_EPS = 1e-5
NEG_INF = -1e30

LANES = 128
SUBLANES = 8
ROW_TILE = D_MODEL // LANES
VMEM_LIMIT_BYTES = 56 * 1024 * 1024

KV_DUP_WIDTH = N_KV_HEADS * LANES

C_U = 0
C_Q = C_U + POOL_WIDTH
C_K = C_Q + ATTN_WIDTH
C_V = C_K + KV_WIDTH
C_M = C_V + KV_WIDTH
C_G = C_M + MEM_WIDTH
C_END = C_G + N_BRANCHES * D_MODEL

TM_INPROJ = 512
TQ_ATTN = 256
TM_MIX = 512
BM_MOE = 512
SC_ROW_WINDOW = 32
TM_FINAL = 512
POOL_HALO = 8
ROUTER_ROWS = 40


def _layer_norm(x, g, b):
    mu = jnp.mean(x, axis=-1, keepdims=True)
    xc = x - mu
    var = jnp.mean(xc * xc, axis=-1, keepdims=True)
    return xc * lax.rsqrt(var + LN_EPS) * g + b


def _sigmoid(x):
    return 1.0 / (1.0 + jnp.exp(-x))


def _rows_to_tiles(x):
    return pltpu.einshape("r(cl)->rcl", x, c=ROW_TILE)


def _tiles_to_rows(x):
    return pltpu.einshape("rcl->r(cl)", x)


def _params(**kw):
    return pltpu.CompilerParams(vmem_limit_bytes=VMEM_LIMIT_BYTES, **kw)


def _memkv_kernel(mem_ref, w_ref, out_ref):
    out_ref[...] = jnp.dot(mem_ref[...].astype(BF16), w_ref[...],
                           preferred_element_type=F32).astype(BF16)


def _memkv_call(mem2, w_memkv):
    rows = mem2.shape[0]
    return pl.pallas_call(
        _memkv_kernel,
        grid=(rows // MEM_TOKENS,),
        in_specs=[pl.BlockSpec((MEM_TOKENS, D_MODEL), lambda i: (i, 0)),
                  pl.BlockSpec((D_MODEL, 2 * MEM_WIDTH), lambda i: (0, 0))],
        out_specs=pl.BlockSpec((MEM_TOKENS, 2 * MEM_WIDTH), lambda i: (i, 0)),
        out_shape=jax.ShapeDtypeStruct((rows, 2 * MEM_WIDTH), BF16),
        compiler_params=_params(dimension_semantics=("arbitrary",)),
        name="memkv",
    )(mem2, w_memkv)


def _inproj_kernel(x_ref, g_ref, b_ref, w_ref, bias_ref,
                   xn_ref, u_ref, q_ref, k_ref, v_ref, qm_ref, gate_ref):
    xn = _layer_norm(x_ref[...], g_ref[...], b_ref[...])
    xn_ref[...] = xn
    xb = xn.astype(BF16)

    def seg(lo, hi):
        return jnp.dot(xb, w_ref[:, lo:hi], preferred_element_type=F32) + bias_ref[:, lo:hi]

    u_ref[...] = seg(C_U, C_Q)
    q_ref[...] = (seg(C_Q, C_K) * (HEAD_DIM ** -0.5)).astype(BF16)
    kv = seg(C_K, C_M)
    low = lax.broadcasted_iota(I32, (kv.shape[0], LANES), 1) < HEAD_DIM
    for out_ref, c_lo in ((k_ref, 0), (v_ref, KV_WIDTH)):
        for c in range(KV_WIDTH // LANES):
            blk = kv[:, c_lo + c * LANES:c_lo + (c + 1) * LANES]
            rot = pltpu.roll(blk, HEAD_DIM, 1)
            out_ref[:, 2 * c * LANES:(2 * c + 1) * LANES] = jnp.where(low, blk, rot).astype(BF16)
            out_ref[:, (2 * c + 1) * LANES:(2 * c + 2) * LANES] = jnp.where(low, rot, blk).astype(BF16)
    qm_ref[...] = seg(C_M, C_G).astype(BF16)
    for j in range(N_BRANCHES):
        lo = C_G + j * D_MODEL
        gate_ref[:, j * D_MODEL:(j + 1) * D_MODEL] = _sigmoid(seg(lo, lo + D_MODEL)).astype(BF16)


def _inproj_call(x2, ln_g, ln_b, w_in, b_in):
    t = x2.shape[0]
    tm = TM_INPROJ
    row = lambda width: pl.BlockSpec((tm, width), lambda i: (i, 0))
    const = lambda shape: pl.BlockSpec(shape, lambda i: (0, 0))
    widths = (D_MODEL, POOL_WIDTH, ATTN_WIDTH, KV_DUP_WIDTH, KV_DUP_WIDTH, MEM_WIDTH, N_BRANCHES * D_MODEL)
    dtypes = (F32, F32, BF16, BF16, BF16, BF16, BF16)
    return pl.pallas_call(
        _inproj_kernel,
        grid=(t // tm,),
        in_specs=[row(D_MODEL), const((1, D_MODEL)), const((1, D_MODEL)),
                  const((D_MODEL, C_END)), const((1, C_END))],
        out_specs=[row(w) for w in widths],
        out_shape=[jax.ShapeDtypeStruct((t, w), d) for w, d in zip(widths, dtypes)],
        compiler_params=_params(dimension_semantics=("arbitrary",)),
        name="inproj",
    )(x2, ln_g, ln_b, w_in, b_in)


def _attn_kernel(sink_ref, q_ref, kc_ref, kp_ref, kn_ref, vc_ref, vp_ref, vn_ref,
                 bp_ref, bc_ref, bn_ref, o_ref, klo, khi, vlo, vhi, s_scr, p_scr):
    i = pl.program_id(1)
    n_tiles = pl.num_programs(1)
    tq = q_ref.shape[0]
    n_qb = tq // BLOCK

    lane = lax.broadcasted_iota(I32, (BLOCK, KV_DUP_WIDTH), 1)
    low = (lane & (LANES - 1)) < HEAD_DIM

    def put(dst_lo, dst_hi, r0, val):
        zero = jnp.zeros_like(val)
        dst_lo[r0:r0 + BLOCK, :] = jnp.where(low, val, zero)
        dst_hi[r0:r0 + BLOCK, :] = jnp.where(low, zero, val)

    put(klo, khi, 0, kp_ref[...])
    put(vlo, vhi, 0, vp_ref[...])
    for j in range(n_qb):
        put(klo, khi, (j + 1) * BLOCK, kc_ref[j * BLOCK:(j + 1) * BLOCK, :])
        put(vlo, vhi, (j + 1) * BLOCK, vc_ref[j * BLOCK:(j + 1) * BLOCK, :])
    put(klo, khi, (n_qb + 1) * BLOCK, kn_ref[...])
    put(vlo, vhi, (n_qb + 1) * BLOCK, vn_ref[...])

    first = (i == 0).astype(I32)
    last = (i == n_tiles - 1).astype(I32)

    def softmax_rows(s, sink):
        m = jnp.maximum(jnp.max(s, axis=-1, keepdims=True), sink)
        p = jnp.exp(s - m)
        denom = jnp.sum(p, axis=-1, keepdims=True) + jnp.exp(sink - m)
        return (p * (1.0 / denom)).astype(BF16)

    units = [(j, h, half) for j in range(n_qb) for h in range(N_KV_HEADS) for half in range(2)]
    always = i >= 0

    @pl.when(always)
    def _scores():
        for u, (j, h, half) in enumerate(units):
            r0 = j * BLOCK
            c0 = h * 2 * LANES
            pv = first if j == 0 else 0
            nv = last if j == n_qb - 1 else 0
            unit = h * 2 + half
            q_pairs = jnp.concatenate([q_ref[r0:r0 + BLOCK, c0:c0 + LANES],
                                       q_ref[r0:r0 + BLOCK, c0 + LANES:c0 + 2 * LANES]], axis=0)
            kx = (klo, khi)[half][r0:r0 + 3 * BLOCK, h * LANES:(h + 1) * LANES]
            s = lax.dot_general(q_pairs, kx, (((1,), (1,)), ((), ())), preferred_element_type=F32)
            bias = jnp.concatenate([bp_ref[pv, unit], bc_ref[unit], bn_ref[nv, unit]], axis=1)
            s_scr[u] = s + bias

    @pl.when(always)
    def _softmax():
        for u, (j, h, half) in enumerate(units):
            p_scr[u, :BLOCK] = softmax_rows(s_scr[u, :BLOCK], sink_ref[4 * h + half])
            p_scr[u, BLOCK:] = softmax_rows(s_scr[u, BLOCK:], sink_ref[4 * h + 2 + half])

    @pl.when(always)
    def _values():
        for j in range(n_qb):
            r0 = j * BLOCK
            for h in range(N_KV_HEADS):
                c0 = h * 2 * LANES
                acc = jnp.zeros((2 * BLOCK, LANES), F32)
                for half, vref in enumerate((vlo, vhi)):
                    u = units.index((j, h, half))
                    vx = vref[r0:r0 + 3 * BLOCK, h * LANES:(h + 1) * LANES]
                    acc = acc + jnp.dot(p_scr[u], vx, preferred_element_type=F32)
                o_ref[r0:r0 + BLOCK, c0:c0 + LANES] = acc[:BLOCK].astype(BF16)
                o_ref[r0:r0 + BLOCK, c0 + LANES:c0 + 2 * LANES] = acc[BLOCK:].astype(BF16)


def _attn_call(q, kk, vv, sink, bias_prev, bias_cur, bias_next, batch, seq):
    tq = TQ_ATTN
    n_qb = tq // BLOCK
    nb = seq // BLOCK
    n_tiles = seq // tq
    t = batch * seq

    cur = lambda width: pl.BlockSpec((tq, width), lambda b, i: (b * n_tiles + i, 0))
    prev = pl.BlockSpec((BLOCK, KV_DUP_WIDTH), lambda b, i: (b * nb + jnp.maximum(i * n_qb - 1, 0), 0))
    nxt = pl.BlockSpec((BLOCK, KV_DUP_WIDTH), lambda b, i: (b * nb + jnp.minimum((i + 1) * n_qb, nb - 1), 0))
    full = lambda a: pl.BlockSpec(a.shape, lambda b, i: (0,) * a.ndim)
    ext = (n_qb + 2) * BLOCK
    n_units = n_qb * N_KV_HEADS * 2
    return pl.pallas_call(
        _attn_kernel,
        grid=(batch, n_tiles),
        in_specs=[pl.BlockSpec(memory_space=pltpu.SMEM),
                  cur(ATTN_WIDTH), cur(KV_DUP_WIDTH), prev, nxt, cur(KV_DUP_WIDTH), prev, nxt,
                  full(bias_prev), full(bias_cur), full(bias_next)],
        out_specs=cur(ATTN_WIDTH),
        out_shape=jax.ShapeDtypeStruct((t, ATTN_WIDTH), BF16),
        scratch_shapes=[pltpu.VMEM((ext, KV_DUP_WIDTH), BF16) for _ in range(4)]
                       + [pltpu.VMEM((n_units, 2 * BLOCK, 3 * BLOCK), F32),
                          pltpu.VMEM((n_units, 2 * BLOCK, 3 * BLOCK), BF16)],
        compiler_params=_params(dimension_semantics=("arbitrary", "arbitrary")),
        name="attn",
    )(sink, q, kk, kk, kk, vv, vv, vv, bias_prev, bias_cur, bias_next)


def _mix_kernel(xn_ref, u_ref, up_ref, un_ref, o_ref, qm_ref, gate_ref, memkv_ref,
                wpool_ref, pscale_ref, ppool_ref, pattn_ref, pmem_ref, wout_ref,
                g1_ref, b1_ref, wr_ref, after_ref,
                x1r_ref, eid_ref, wts_ref, uext, *, seq):
    i = pl.program_id(1)
    n_tiles = pl.num_programs(1)
    tm = xn_ref.shape[0]

    halo = POOL_HALO
    uext[0:halo, :] = jnp.where(i > 0, up_ref[...], 0.0)
    uext[halo:halo + tm, :] = u_ref[...]
    uext[halo + tm:halo + tm + halo, :] = jnp.where(i < n_tiles - 1, un_ref[...], 0.0)
    pos = i * tm + lax.broadcasted_iota(I32, (tm, 1), 0)
    mixed = []
    for gi, win in enumerate(POOL_WINDOWS):
        c0 = gi * POOL_GROUP_CH
        half = win // 2
        total = jnp.zeros((tm, POOL_GROUP_CH), F32)
        for off in range(-half, half):
            total = total + uext[halo + off:halo + off + tm, c0:c0 + POOL_GROUP_CH]
        cnt = (jnp.minimum(pos + half, seq) - jnp.maximum(pos - half, 0)).astype(F32)
        pooled = total * (1.0 / cnt) - u_ref[:, c0:c0 + POOL_GROUP_CH]
        mixed.append(jnp.dot(pooled.astype(BF16), wpool_ref[gi], preferred_element_type=F32)
                     * pscale_ref[:, c0:c0 + POOL_GROUP_CH])
    mixed = jnp.concatenate(mixed, axis=1).astype(BF16)
    merged = gate_ref[:, 0:D_MODEL].astype(F32) * jnp.dot(
        mixed, ppool_ref[...], preferred_element_type=F32)

    merged = merged + gate_ref[:, D_MODEL:2 * D_MODEL].astype(F32) * jnp.dot(
        o_ref[...], pattn_ref[...], preferred_element_type=F32)

    heads = []
    for h in range(MEM_HEADS):
        c0 = h * MEM_HEAD_DIM
        km = memkv_ref[:, c0:c0 + MEM_HEAD_DIM]
        vm = memkv_ref[:, MEM_WIDTH + c0:MEM_WIDTH + c0 + MEM_HEAD_DIM]
        s = lax.dot_general(qm_ref[:, c0:c0 + MEM_HEAD_DIM], km, (((1,), (1,)), ((), ())),
                            preferred_element_type=F32) * (MEM_HEAD_DIM ** -0.5)
        p = jnp.exp(s - jnp.max(s, axis=-1, keepdims=True))
        p = (p / jnp.sum(p, axis=-1, keepdims=True)).astype(BF16)
        heads.append(jnp.dot(p, vm, preferred_element_type=F32).astype(BF16))
    om = jnp.concatenate(heads, axis=1)
    merged = merged + gate_ref[:, 2 * D_MODEL:3 * D_MODEL].astype(F32) * jnp.dot(
        om, pmem_ref[...], preferred_element_type=F32)

    y = jnp.dot(merged.astype(BF16), wout_ref[...], preferred_element_type=F32)
    x1 = _layer_norm(ALPHA * xn_ref[...] + y, g1_ref[...], b1_ref[...])
    x1r_ref[...] = _rows_to_tiles(x1)

    lt = lax.dot_general(wr_ref[...], x1.astype(BF16), (((1,), (1,)), ((), ())),
                         preferred_element_type=F32)
    gl = [lt[r:r + 1, :] for r in range(N_EXPERT_GROUPS)]
    gmax = gl[0]
    grp = jnp.zeros((1, tm), I32)
    for r in range(1, N_EXPERT_GROUPS):
        better = gl[r] > gmax
        grp = jnp.where(better, r, grp)
        gmax = jnp.where(better, gl[r], gmax)
    gsum = gl[0] * 0.0
    for r in range(N_EXPERT_GROUPS):
        gsum = gsum + jnp.exp(gl[r] - gmax)
    gp = 1.0 / gsum
    sel = jnp.zeros((EXPERTS_PER_GROUP, tm), F32)
    for r in range(N_EXPERT_GROUPS):
        rows = lt[SUBLANES + r * EXPERTS_PER_GROUP:SUBLANES + (r + 1) * EXPERTS_PER_GROUP, :]
        sel = jnp.where(grp == r, rows, sel)
    ridx = lax.broadcasted_iota(I32, (EXPERTS_PER_GROUP, tm), 0)
    top1 = jnp.max(sel, axis=0, keepdims=True)
    i1 = jnp.min(jnp.where(sel == top1, ridx, EXPERTS_PER_GROUP), axis=0, keepdims=True)
    rest = jnp.where(ridx == i1, -jnp.inf, sel)
    top2 = jnp.max(rest, axis=0, keepdims=True)
    i2 = jnp.min(jnp.where(rest == top2, ridx, EXPERTS_PER_GROUP), axis=0, keepdims=True)
    e2 = jnp.exp(top2 - top1)
    inv = gp / (1.0 + e2)
    eid_ref[0:1, :] = grp * EXPERTS_PER_GROUP + i1
    eid_ref[1:2, :] = grp * EXPERTS_PER_GROUP + i2
    wts_ref[0:1, :] = inv
    wts_ref[1:2, :] = e2 * inv


def _mix_call(xn, u, o, qm, gate, memkv, p, batch, seq, after):
    tm = TM_MIX
    n_tiles = seq // tm
    t = batch * seq
    hb = tm // POOL_HALO

    row = lambda width: pl.BlockSpec((tm, width), lambda b, i: (b * n_tiles + i, 0))
    prev = pl.BlockSpec((POOL_HALO, POOL_WIDTH),
                        lambda b, i: (jnp.maximum((b * n_tiles + i) * hb - 1, 0), 0))
    nxt = pl.BlockSpec((POOL_HALO, POOL_WIDTH),
                       lambda b, i: (jnp.minimum((b * n_tiles + i + 1) * hb, t // POOL_HALO - 1), 0))
    full = lambda a: pl.BlockSpec(a.shape, lambda b, i: (0,) * a.ndim)
    lane_row = pl.BlockSpec((TOP_K, tm), lambda b, i: (0, b * n_tiles + i))
    weights = (p["w_pool"], p["pool_scale"], p["p_pool"], p["p_attn"], p["p_mem"], p["w_out"],
               p["ln1_g"], p["ln1_b"], p["w_router"])
    return pl.pallas_call(
        functools.partial(_mix_kernel, seq=seq),
        grid=(batch, n_tiles),
        in_specs=[row(D_MODEL), row(POOL_WIDTH), prev, nxt, row(ATTN_WIDTH), row(MEM_WIDTH),
                  row(N_BRANCHES * D_MODEL),
                  pl.BlockSpec((MEM_TOKENS, 2 * MEM_WIDTH), lambda b, i: (b, 0))]
                 + [full(w) for w in weights]
                 + [pl.BlockSpec((SUBLANES, LANES), lambda b, i: (0, 0))],
        out_specs=[pl.BlockSpec((tm, ROW_TILE, LANES), lambda b, i: (b * n_tiles + i, 0, 0)),
                   lane_row, lane_row],
        out_shape=[jax.ShapeDtypeStruct((t, ROW_TILE, LANES), F32),
                   jax.ShapeDtypeStruct((TOP_K, t), I32),
                   jax.ShapeDtypeStruct((TOP_K, t), F32)],
        scratch_shapes=[pltpu.VMEM((tm + 2 * POOL_HALO, POOL_WIDTH), F32)],
        compiler_params=_params(dimension_semantics=("arbitrary", "arbitrary")),
        name="mix",
    )(xn, u, u, u, o, qm, gate, memkv, *weights, after)


def _moe_kernel(iblk_ref, ie_ref, inext_ref, irun_ref, ilo_ref, ihi_ref,
                x_ref, ws_ref, wgu_hbm, wd_hbm, y_ref,
                x_bf, h_bf, wgu_f32, wd_f32, wgu_bf, wd_bf, wsem):
    i = pl.program_id(0)
    bm = x_ref.shape[0]
    lo = ilo_ref[i]
    hi = ihi_ref[i]
    nonempty = hi > lo

    def weight_copies(e, ws):
        return (pltpu.make_async_copy(wgu_hbm.at[e], wgu_f32.at[ws], wsem.at[ws]),
                pltpu.make_async_copy(wd_hbm.at[e], wd_f32.at[ws], wsem.at[ws]))

    @pl.when(jnp.logical_or(i == 0, ie_ref[i] != ie_ref[jnp.maximum(i - 1, 0)]))
    def _():
        e = ie_ref[i]
        wslot = irun_ref[i] & 1

        @pl.when(i == 0)
        def _():
            for c in weight_copies(e, wslot):
                c.start()
        for c in weight_copies(e, wslot):
            c.wait()

        @pl.when(inext_ref[i] >= 0)
        def _():
            for c in weight_copies(inext_ref[i], 1 - wslot):
                c.start()
        wgu_bf[...] = wgu_f32[wslot].astype(BF16)
        wd_bf[...] = wd_f32[wslot].astype(BF16)

    @pl.when(nonempty)
    def _():
        x_bf[...] = _tiles_to_rows(x_ref[...]).astype(BF16)

    @pl.when(nonempty)
    def _():
        gu = jnp.dot(x_bf[...], wgu_bf[...], preferred_element_type=F32)
        gate = gu[:, :EXPERT_HIDDEN]
        h_bf[...] = ((gate * _sigmoid(gate)) * gu[:, EXPERT_HIDDEN:]).astype(BF16)

    @pl.when(nonempty)
    def _():
        rows = lax.broadcasted_iota(I32, (bm, 1), 0)
        mine = jnp.logical_and(rows >= lo, rows < hi)
        y = jnp.dot(h_bf[...], wd_bf[...], preferred_element_type=F32) * ws_ref[...]
        y = _rows_to_tiles(jnp.where(mine, y, 0.0))

        @pl.when(lo == 0)
        def _():
            y_ref[...] = y

        @pl.when(lo > 0)
        def _():
            y_ref[...] = y_ref[...] + y


def _moe_call(xs, wsorted, item_blk, item_e, item_lo, item_hi, w_gu, w_down):
    a = xs.shape[0]
    bm = BM_MOE
    n_items = item_blk.shape[0]
    change = jnp.concatenate([jnp.zeros((1,), I32), (item_e[1:] != item_e[:-1]).astype(I32)])
    item_run = jnp.cumsum(change).astype(I32)
    first_later = jnp.sum((item_e[None, :] <= item_e[:, None]).astype(I32), axis=1)
    item_next = jnp.where(first_later < n_items, item_e[jnp.minimum(first_later, n_items - 1)], -1).astype(I32)
    grid_spec = pltpu.PrefetchScalarGridSpec(
        num_scalar_prefetch=6,
        grid=(n_items,),
        in_specs=[pl.BlockSpec((bm, ROW_TILE, LANES), lambda i, blk, *_: (blk[i], 0, 0)),
                  pl.BlockSpec((bm, 1), lambda i, blk, *_: (blk[i], 0)),
                  pl.BlockSpec(memory_space=pl.ANY),
                  pl.BlockSpec(memory_space=pl.ANY)],
        out_specs=pl.BlockSpec((bm, ROW_TILE, LANES), lambda i, blk, *_: (blk[i], 0, 0)),
        scratch_shapes=[pltpu.VMEM((bm, D_MODEL), BF16),
                        pltpu.VMEM((bm, EXPERT_HIDDEN), BF16),
                        pltpu.VMEM((2, D_MODEL, 2 * EXPERT_HIDDEN), F32),
                        pltpu.VMEM((2, EXPERT_HIDDEN, D_MODEL), F32),
                        pltpu.VMEM((D_MODEL, 2 * EXPERT_HIDDEN), BF16),
                        pltpu.VMEM((EXPERT_HIDDEN, D_MODEL), BF16),
                        pltpu.SemaphoreType.DMA((2,))],
    )
    return pl.pallas_call(
        _moe_kernel,
        grid_spec=grid_spec,
        out_shape=jax.ShapeDtypeStruct((a, ROW_TILE, LANES), F32),
        compiler_params=_params(dimension_semantics=("arbitrary",)),
        name="moe",
    )(item_blk, item_e, item_next, item_run, item_lo, item_hi, xs, wsorted, w_gu, w_down)


def _sc_mesh():
    return plsc.VectorSubcoreMesh(core_axis_name="core", subcore_axis_name="subcore")


def _sc_gather_rows(rows, idx):
    m = idx.shape[0]
    w = SC_ROW_WINDOW

    @pl.kernel(out_type=jax.ShapeDtypeStruct((m,) + rows.shape[1:], rows.dtype), mesh=_sc_mesh(),
               scratch_types=[])
    def gather(rows_hbm, idx_hbm, out_hbm):
        def body(idx_vmem, out_vmem):
            pltpu.sync_copy(rows_hbm.at[idx_vmem.at[0]], out_vmem)

        pltpu.emit_pipeline(
            body,
            grid=(m // w,),
            in_specs=[pl.BlockSpec((1, w), index_map=lambda i: (i, 0))],
            out_specs=[pl.BlockSpec((w,) + rows.shape[1:], index_map=lambda i: (i, 0, 0))],
            core_axis_name=("core", "subcore"),
            dimension_semantics=(pltpu.PARALLEL,),
        )(idx_hbm, out_hbm)

    return gather(rows, idx.reshape(m // w, w))


def _sc_scatter_rows(rows, idx):
    m = idx.shape[0]
    w = SC_ROW_WINDOW

    @pl.kernel(out_type=jax.ShapeDtypeStruct(rows.shape, rows.dtype), mesh=_sc_mesh(), scratch_types=[])
    def scatter(rows_hbm, idx_hbm, out_hbm):
        def body(rows_vmem, idx_vmem):
            pltpu.sync_copy(rows_vmem, out_hbm.at[idx_vmem.at[0]])

        pltpu.emit_pipeline(
            body,
            grid=(m // w,),
            in_specs=[pl.BlockSpec((w,) + rows.shape[1:], index_map=lambda i: (i, 0, 0)),
                      pl.BlockSpec((1, w), index_map=lambda i: (i, 0))],
            out_specs=[],
            core_axis_name=("core", "subcore"),
            dimension_semantics=(pltpu.PARALLEL,),
        )(rows_hbm, idx_hbm)

    return scatter(rows, idx.reshape(m // w, w))


def _final_kernel(x1_ref, ya_ref, yb_ref, g_ref, b_ref, out_ref):
    moe = ya_ref[...] + yb_ref[...]
    out_ref[...] = _layer_norm(_tiles_to_rows(ALPHA * x1_ref[...] + moe), g_ref[...], b_ref[...])


def _final_call(x1, y2, ln_g, ln_b):
    t = x1.shape[0]
    tm = TM_FINAL
    n_tiles = t // tm
    return pl.pallas_call(
        _final_kernel,
        grid=(n_tiles,),
        in_specs=[pl.BlockSpec((tm, ROW_TILE, LANES), lambda i: (i, 0, 0)),
                  pl.BlockSpec((tm, ROW_TILE, LANES), lambda i: (i, 0, 0)),
                  pl.BlockSpec((tm, ROW_TILE, LANES), lambda i: (n_tiles + i, 0, 0)),
                  pl.BlockSpec((1, D_MODEL), lambda i: (0, 0)),
                  pl.BlockSpec((1, D_MODEL), lambda i: (0, 0))],
        out_specs=pl.BlockSpec((tm, D_MODEL), lambda i: (i, 0)),
        out_shape=jax.ShapeDtypeStruct((t, D_MODEL), F32),
        compiler_params=_params(dimension_semantics=("arbitrary",)),
        name="final",
    )(x1, y2, y2, ln_g, ln_b)


def _t5_bucket(rel):
    nb = N_REL_BUCKETS // 2
    max_exact = nb // 2
    ret = jnp.where(rel > 0, nb, 0)
    n = jnp.abs(rel)
    nf = jnp.maximum(n, 1).astype(F32)
    large = max_exact + (jnp.log(nf / max_exact) / math.log(REL_MAX_DISTANCE / max_exact)
                         * (nb - max_exact)).astype(I32)
    large = jnp.minimum(large, nb - 1)
    return ret + jnp.where(n < max_exact, n, large)


def _bias_tables(rel_table):
    n, m = BLOCK, 3 * BLOCK
    rel = jnp.arange(-(2 * BLOCK - 1), 2 * BLOCK)
    by_rel = jnp.where((jnp.abs(rel) <= WINDOW)[:, None], rel_table[_t5_bucket(rel)].astype(F32), NEG_INF)
    by_rel = jnp.pad(by_rel.T, ((0, 0), (0, 1)))
    skew = jnp.tile(by_rel, (1, n))[:, :n * (m + n - 1)].reshape(N_Q_HEADS, n, m + n - 1)
    bias = skew[:, :, n - 1:n - 1 + m]
    heads = [4 * h + half for h in range(N_KV_HEADS) for half in range(2)]
    units = jnp.concatenate([jnp.stack([bias[a] for a in heads]),
                             jnp.stack([bias[a + 2] for a in heads])], axis=1)
    masked = jnp.full((2 * N_KV_HEADS, 2 * BLOCK, BLOCK), NEG_INF, F32)
    prev = jnp.stack([units[:, :, :BLOCK], masked])
    cur = units[:, :, BLOCK:2 * BLOCK]
    nxt = jnp.stack([units[:, :, 2 * BLOCK:], masked])
    return prev, cur, nxt


def _prepare(rel_bias_table, w_in, b_in, w_pool, pool_scale, p_pool, sink, p_attn, w_mem_kv, p_mem,
             w_out, ln1_g, ln1_b, w_router_group, w_router_expert, w_gu, w_down):
    l = 0
    w_router = jnp.zeros((ROUTER_ROWS, D_MODEL), F32)
    w_router = w_router.at[:N_EXPERT_GROUPS].set(w_router_group[l].T)
    w_router = w_router.at[SUBLANES:].set(w_router_expert[l].T)
    bias_prev, bias_cur, bias_next = _bias_tables(rel_bias_table)
    return dict(
        w_in=w_in[l].astype(BF16), b_in=b_in[l][None, :],
        w_pool=w_pool[l].astype(BF16), pool_scale=pool_scale[l][None, :],
        p_pool=p_pool[l].astype(BF16), p_attn=p_attn[l].astype(BF16), p_mem=p_mem[l].astype(BF16),
        w_out=w_out[l].astype(BF16), w_memkv=w_mem_kv[l].astype(BF16),
        ln1_g=ln1_g[l][None, :], ln1_b=ln1_b[l][None, :],
        w_router=w_router.astype(BF16), sink=sink[l],
        w_gu=w_gu[l], w_down=w_down[l],
        bias_prev=bias_prev, bias_cur=bias_cur, bias_next=bias_next,
    )


def _dispatch_plan(eid, wts):
    t = eid.shape[1]
    a = TOP_K * t
    bm = BM_MOE
    nblk = a // bm
    eflat = eid.reshape(a)
    wflat = wts.reshape(a)
    order = jnp.argsort(eflat, stable=True).astype(I32)
    wsorted = wflat[order][:, None]
    experts = jnp.arange(N_EXPERTS, dtype=I32)
    counts = jnp.sum((eflat[None, :] == experts[:, None]).astype(I32), axis=1)
    ends = jnp.cumsum(counts).astype(I32)
    starts = ends - counts
    cuts = jnp.sort(jnp.concatenate([jnp.arange(nblk, dtype=I32) * bm, starts, jnp.array([a], I32)]))
    lo_abs = cuts[:-1]
    hi_abs = cuts[1:]
    nonempty = hi_abs > lo_abs
    blk = jnp.minimum(lo_abs // bm, nblk - 1)
    probe = jnp.minimum(lo_abs, a - 1)
    item_e = jnp.sum((ends[None, :] <= probe[:, None]).astype(I32), axis=1)
    item_lo = jnp.where(nonempty, lo_abs - blk * bm, 0).astype(I32)
    item_hi = jnp.where(nonempty, hi_abs - blk * bm, 0).astype(I32)
    return order, wsorted, blk.astype(I32), item_e.astype(I32), item_lo, item_hi


class _Trunk:
    def __init__(self, x, mem, ln_in_g, ln_in_b, ln2_g, ln2_b, p):
        self.batch, self.seq, self.d = x.shape
        self.t = self.batch * self.seq
        assert (self.seq % TM_MIX == 0 and self.seq % TQ_ATTN == 0 and self.t % TM_INPROJ == 0
                and (TOP_K * self.t) % BM_MOE == 0)
        self.x, self.mem, self.p = x, mem, p
        self.ln_in = (ln_in_g[None, :], ln_in_b[None, :])
        self.ln2 = (ln2_g[None, :], ln2_b[None, :])

    def project(self):
        self.memkv = _memkv_call(self.mem.reshape(self.batch * MEM_TOKENS, self.d), self.p["w_memkv"])
        self.h = _inproj_call(self.x.reshape(self.t, self.d), *self.ln_in, self.p["w_in"], self.p["b_in"])

    def attend(self):
        p = self.p
        _, _, q, kk, vv, _, _ = self.h
        self.o = _attn_call(q, kk, vv, p["sink"], p["bias_prev"], p["bias_cur"], p["bias_next"],
                            self.batch, self.seq)

    def mix(self, after):
        xn, u, _, _, _, qm, gate = self.h
        self.x1r, eid, wts = _mix_call(xn, u, self.o, qm, gate, self.memkv, self.p, self.batch, self.seq,
                                       after.h[0])
        self.order, self.wsorted, *self.items = _dispatch_plan(eid, wts)

    def dispatch(self):
        sorted_tok = jnp.where(self.order >= self.t, self.order - self.t, self.order)
        self.xs = _sc_gather_rows(self.x1r, sorted_tok)

    def experts(self):
        self.ys = _moe_call(self.xs, self.wsorted, *self.items, self.p["w_gu"], self.p["w_down"])

    def combine(self):
        self.y2 = _sc_scatter_rows(self.ys, self.order)

    def finish(self):
        out = _final_call(self.x1r, self.y2, *self.ln2)
        return out.reshape(self.batch, self.seq, self.d)


def kernel(x_prompt, x_sample, mem_prompt, mem_sample, ln_in_g, ln_in_b, rel_bias_table, w_in, b_in,
           w_pool, pool_scale, p_pool, sink, p_attn, w_mem_kv, p_mem, w_out, ln1_g, ln1_b,
           w_router_group, w_router_expert, w_gu, w_down, ln2_g, ln2_b):
    p = _prepare(rel_bias_table, w_in, b_in, w_pool, pool_scale, p_pool, sink, p_attn, w_mem_kv, p_mem,
                 w_out, ln1_g, ln1_b, w_router_group, w_router_expert, w_gu, w_down)
    a = _Trunk(x_prompt, mem_prompt, ln_in_g, ln_in_b, ln2_g[0], ln2_b[0], p)
    b = _Trunk(x_sample, mem_sample, ln_in_g, ln_in_b, ln2_g[0], ln2_b[0], p)
    a.project()
    a.attend()
    b.project()
    a.mix(after=b)
    a.dispatch()
    b.attend()
    b.mix(after=a)
    b.dispatch()
    a.experts()
    a.combine()
    b.experts()
    b.combine()
    return (a.finish(), b.finish())
```

```python
import functools
import math

import jax
import jax.numpy as jnp
from jax import lax
from jax.experimental import pallas as pl
from jax.experimental.pallas import tpu as pltpu
from jax.experimental.pallas import tpu_sc as plsc

F32 = jnp.float32
BF16 = jnp.bfloat16
I32 = jnp.int32

D_MODEL = 1024
DEPTH = 1
POOL_WIDTH = 512
POOL_WINDOWS = (2, 4, 8, 16)
N_POOL_GROUPS = 4
POOL_GROUP_CH = POOL_WIDTH // N_POOL_GROUPS
N_Q_HEADS = 16
N_KV_HEADS = 4
HEAD_DIM = 64
ATTN_WIDTH = N_Q_HEADS * HEAD_DIM
KV_WIDTH = N_KV_HEADS * HEAD_DIM
WINDOW = 128
BLOCK = 128
N_REL_BUCKETS = 32
REL_MAX_DISTANCE = 128
MEM_TOKENS = 256
MEM_HEADS = 4
MEM_HEAD_DIM = 128
MEM_WIDTH = MEM_HEADS * MEM_HEAD_DIM
N_BRANCHES = 3
N_EXPERT_GROUPS = 4
EXPERTS_PER_GROUP = 8
N_EXPERTS = N_EXPERT_GROUPS * EXPERTS_PER_GROUP
TOP_K = 2
EXPERT_HIDDEN = 512
ALPHA = (2 * DEPTH) ** 0.25
LN_EPS = 1e-5
NEG_INF = -1e30
LOG2_E = math.log2(math.e)

LANES = 128
SUBLANES = 8
ROW_TILE = D_MODEL // LANES
VMEM_LIMIT_BYTES = 56 * 1024 * 1024

KV_DUP_WIDTH = N_KV_HEADS * LANES

C_U = 0
C_Q = C_U + POOL_WIDTH
C_K = C_Q + ATTN_WIDTH
C_V = C_K + KV_WIDTH
C_M = C_V + KV_WIDTH
C_G = C_M + MEM_WIDTH
C_END = C_G + N_BRANCHES * D_MODEL

TM_INPROJ = 512
TQ_ATTN = 256
TM_MIX = 512
BM_MOE = 512
SC_ROW_WINDOW = 32
TM_FINAL = 512
POOL_HALO = 8
ROUTER_ROWS = 40


def _layer_norm(x, g, b):
    mu = jnp.mean(x, axis=-1, keepdims=True)
    xc = x - mu
    var = jnp.mean(xc * xc, axis=-1, keepdims=True)
    return xc * lax.rsqrt(var + LN_EPS) * g + b


def _sigmoid(x):
    return 1.0 / (1.0 + jnp.exp(-x))


def _rows_to_tiles(x):
    return pltpu.einshape("r(cl)->rcl", x, c=ROW_TILE)


def _tiles_to_rows(x):
    return pltpu.einshape("rcl->r(cl)", x)


def _params(**kw):
    return pltpu.CompilerParams(vmem_limit_bytes=VMEM_LIMIT_BYTES, **kw)


def _memkv_kernel(mem_ref, w_ref, out_ref):
    out_ref[...] = jnp.dot(mem_ref[...].astype(BF16), w_ref[...],
                           preferred_element_type=F32).astype(BF16)


def _memkv_call(mem2, w_memkv):
    rows = mem2.shape[0]
    return pl.pallas_call(
        _memkv_kernel,
        grid=(rows // MEM_TOKENS,),
        in_specs=[pl.BlockSpec((MEM_TOKENS, D_MODEL), lambda i: (i, 0)),
                  pl.BlockSpec((D_MODEL, 2 * MEM_WIDTH), lambda i: (0, 0))],
        out_specs=pl.BlockSpec((MEM_TOKENS, 2 * MEM_WIDTH), lambda i: (i, 0)),
        out_shape=jax.ShapeDtypeStruct((rows, 2 * MEM_WIDTH), BF16),
        compiler_params=_params(dimension_semantics=("arbitrary",)),
        name="memkv",
    )(mem2, w_memkv)


def _inproj_kernel(x_ref, g_ref, b_ref, w_ref, bias_ref,
                   xn_ref, u_ref, q_ref, k_ref, v_ref, qm_ref, gate_ref):
    xn = _layer_norm(x_ref[...], g_ref[...], b_ref[...])
    xn_ref[...] = xn
    xb = xn.astype(BF16)

    def seg(lo, hi):
        return jnp.dot(xb, w_ref[:, lo:hi], preferred_element_type=F32) + bias_ref[:, lo:hi]

    u_ref[...] = seg(C_U, C_Q)
    q_ref[...] = (seg(C_Q, C_K) * (HEAD_DIM ** -0.5 * LOG2_E)).astype(BF16)
    kv = seg(C_K, C_M)
    low = lax.broadcasted_iota(I32, (kv.shape[0], LANES), 1) < HEAD_DIM
    for out_ref, c_lo in ((k_ref, 0), (v_ref, KV_WIDTH)):
        for c in range(KV_WIDTH // LANES):
            blk = kv[:, c_lo + c * LANES:c_lo + (c + 1) * LANES]
            rot = pltpu.roll(blk, HEAD_DIM, 1)
            out_ref[:, 2 * c * LANES:(2 * c + 1) * LANES] = jnp.where(low, blk, rot).astype(BF16)
            out_ref[:, (2 * c + 1) * LANES:(2 * c + 2) * LANES] = jnp.where(low, rot, blk).astype(BF16)
    qm_ref[...] = seg(C_M, C_G).astype(BF16)
    for j in range(N_BRANCHES):
        lo = C_G + j * D_MODEL
        gate_ref[:, j * D_MODEL:(j + 1) * D_MODEL] = _sigmoid(seg(lo, lo + D_MODEL)).astype(BF16)


def _inproj_call(x2, ln_g, ln_b, w_in, b_in):
    t = x2.shape[0]
    tm = TM_INPROJ
    row = lambda width: pl.BlockSpec((tm, width), lambda i: (i, 0))
    const = lambda shape: pl.BlockSpec(shape, lambda i: (0, 0))
    widths = (D_MODEL, POOL_WIDTH, ATTN_WIDTH, KV_DUP_WIDTH, KV_DUP_WIDTH, MEM_WIDTH, N_BRANCHES * D_MODEL)
    dtypes = (F32, F32, BF16, BF16, BF16, BF16, BF16)
    return pl.pallas_call(
        _inproj_kernel,
        grid=(t // tm,),
        in_specs=[row(D_MODEL), const((1, D_MODEL)), const((1, D_MODEL)),
                  const((D_MODEL, C_END)), const((1, C_END))],
        out_specs=[row(w) for w in widths],
        out_shape=[jax.ShapeDtypeStruct((t, w), d) for w, d in zip(widths, dtypes)],
        compiler_params=_params(dimension_semantics=("arbitrary",)),
        name="inproj",
    )(x2, ln_g, ln_b, w_in, b_in)


def _attn_kernel(sink_ref, q_ref, kc_ref, kp_ref, kn_ref, vc_ref, vp_ref, vn_ref,
                 bp_ref, bc_ref, bn_ref, o_ref, klo, khi, vlo, vhi, s_scr, p_scr, inv_scr):
    i = pl.program_id(1)
    n_tiles = pl.num_programs(1)
    tq = q_ref.shape[0]
    n_qb = tq // BLOCK

    lane = lax.broadcasted_iota(I32, (BLOCK, KV_DUP_WIDTH), 1)
    low = (lane & (LANES - 1)) < HEAD_DIM

    def put(dst_lo, dst_hi, r0, val):
        zero = jnp.zeros_like(val)
        dst_lo[r0:r0 + BLOCK, :] = jnp.where(low, val, zero)
        dst_hi[r0:r0 + BLOCK, :] = jnp.where(low, zero, val)

    put(klo, khi, 0, kp_ref[...])
    put(vlo, vhi, 0, vp_ref[...])
    for j in range(n_qb):
        put(klo, khi, (j + 1) * BLOCK, kc_ref[j * BLOCK:(j + 1) * BLOCK, :])
        put(vlo, vhi, (j + 1) * BLOCK, vc_ref[j * BLOCK:(j + 1) * BLOCK, :])
    put(klo, khi, (n_qb + 1) * BLOCK, kn_ref[...])
    put(vlo, vhi, (n_qb + 1) * BLOCK, vn_ref[...])

    first = (i == 0).astype(I32)
    last = (i == n_tiles - 1).astype(I32)

    def softmax_rows(s, sink):
        m = jnp.maximum(jnp.max(s, axis=-1, keepdims=True), sink)
        p = jnp.exp2(s - m)
        denom = jnp.sum(p, axis=-1, keepdims=True) + jnp.exp2(sink - m)
        return p.astype(BF16), jnp.broadcast_to(1.0 / denom, (s.shape[0], LANES))

    units = [(j, h, half) for j in range(n_qb) for h in range(N_KV_HEADS) for half in range(2)]
    always = i >= 0

    @pl.when(always)
    def _scores():
        for u, (j, h, half) in enumerate(units):
            r0 = j * BLOCK
            c0 = h * 2 * LANES
            pv = first if j == 0 else 0
            nv = last if j == n_qb - 1 else 0
            unit = h * 2 + half
            q_pairs = jnp.concatenate([q_ref[r0:r0 + BLOCK, c0:c0 + LANES],
                                       q_ref[r0:r0 + BLOCK, c0 + LANES:c0 + 2 * LANES]], axis=0)
            kx = (klo, khi)[half][r0:r0 + 3 * BLOCK, h * LANES:(h + 1) * LANES]
            s = lax.dot_general(q_pairs, kx, (((1,), (1,)), ((), ())), preferred_element_type=F32)
            bias = jnp.concatenate([bp_ref[pv, unit], bc_ref[unit], bn_ref[nv, unit]], axis=1)
            s_scr[u] = s + bias

    @pl.when(always)
    def _softmax():
        for u, (j, h, half) in enumerate(units):
            p_scr[u, :BLOCK], inv_scr[u, :BLOCK] = softmax_rows(s_scr[u, :BLOCK], sink_ref[4 * h + half])
            p_scr[u, BLOCK:], inv_scr[u, BLOCK:] = softmax_rows(s_scr[u, BLOCK:], sink_ref[4 * h + 2 + half])

    @pl.when(always)
    def _values():
        for j in range(n_qb):
            r0 = j * BLOCK
            for h in range(N_KV_HEADS):
                c0 = h * 2 * LANES
                acc = jnp.zeros((2 * BLOCK, LANES), F32)
                for half, vref in enumerate((vlo, vhi)):
                    u = units.index((j, h, half))
                    vx = vref[r0:r0 + 3 * BLOCK, h * LANES:(h + 1) * LANES]
                    acc = acc + jnp.dot(p_scr[u], vx, preferred_element_type=F32) * inv_scr[u]
                o_ref[r0:r0 + BLOCK, c0:c0 + LANES] = acc[:BLOCK].astype(BF16)
                o_ref[r0:r0 + BLOCK, c0 + LANES:c0 + 2 * LANES] = acc[BLOCK:].astype(BF16)


def _attn_call(q, kk, vv, sink, bias_prev, bias_cur, bias_next, batch, seq):
    tq = TQ_ATTN
    n_qb = tq // BLOCK
    nb = seq // BLOCK
    n_tiles = seq // tq
    t = batch * seq

    cur = lambda width: pl.BlockSpec((tq, width), lambda b, i: (b * n_tiles + i, 0))
    prev = pl.BlockSpec((BLOCK, KV_DUP_WIDTH), lambda b, i: (b * nb + jnp.maximum(i * n_qb - 1, 0), 0))
    nxt = pl.BlockSpec((BLOCK, KV_DUP_WIDTH), lambda b, i: (b * nb + jnp.minimum((i + 1) * n_qb, nb - 1), 0))
    full = lambda a: pl.BlockSpec(a.shape, lambda b, i: (0,) * a.ndim)
    ext = (n_qb + 2) * BLOCK
    n_units = n_qb * N_KV_HEADS * 2
    return pl.pallas_call(
        _attn_kernel,
        grid=(batch, n_tiles),
        in_specs=[pl.BlockSpec(memory_space=pltpu.SMEM),
                  cur(ATTN_WIDTH), cur(KV_DUP_WIDTH), prev, nxt, cur(KV_DUP_WIDTH), prev, nxt,
                  full(bias_prev), full(bias_cur), full(bias_next)],
        out_specs=cur(ATTN_WIDTH),
        out_shape=jax.ShapeDtypeStruct((t, ATTN_WIDTH), BF16),
        scratch_shapes=[pltpu.VMEM((ext, KV_DUP_WIDTH), BF16) for _ in range(4)]
                       + [pltpu.VMEM((n_units, 2 * BLOCK, 3 * BLOCK), F32),
                          pltpu.VMEM((n_units, 2 * BLOCK, 3 * BLOCK), BF16),
                          pltpu.VMEM((n_units, 2 * BLOCK, LANES), F32)],
        compiler_params=_params(dimension_semantics=("arbitrary", "arbitrary")),
        name="attn",
    )(sink, q, kk, kk, kk, vv, vv, vv, bias_prev, bias_cur, bias_next)


def _mix_kernel(xn_ref, u_ref, up_ref, un_ref, o_ref, qm_ref, gate_ref, memkv_ref,
                wpool_ref, pscale_ref, ppool_ref, pattn_ref, pmem_ref, wout_ref,
                g1_ref, b1_ref, wr_ref,
                x1r_ref, eid_ref, wts_ref, uext, *, seq):
    i = pl.program_id(1)
    n_tiles = pl.num_programs(1)
    tm = xn_ref.shape[0]

    halo = POOL_HALO
    uext[0:halo, :] = jnp.where(i > 0, up_ref[...], 0.0)
    uext[halo:halo + tm, :] = u_ref[...]
    uext[halo + tm:halo + tm + halo, :] = jnp.where(i < n_tiles - 1, un_ref[...], 0.0)
    pos = i * tm + lax.broadcasted_iota(I32, (tm, 1), 0)
    mixed = []
    for gi, win in enumerate(POOL_WINDOWS):
        c0 = gi * POOL_GROUP_CH
        half = win // 2
        total = jnp.zeros((tm, POOL_GROUP_CH), F32)
        for off in range(-half, half):
            total = total + uext[halo + off:halo + off + tm, c0:c0 + POOL_GROUP_CH]
        cnt = (jnp.minimum(pos + half, seq) - jnp.maximum(pos - half, 0)).astype(F32)
        pooled = total * (1.0 / cnt) - u_ref[:, c0:c0 + POOL_GROUP_CH]
        mixed.append(jnp.dot(pooled.astype(BF16), wpool_ref[gi], preferred_element_type=F32)
                     * pscale_ref[:, c0:c0 + POOL_GROUP_CH])
    mixed = jnp.concatenate(mixed, axis=1).astype(BF16)
    merged = gate_ref[:, 0:D_MODEL].astype(F32) * jnp.dot(
        mixed, ppool_ref[...], preferred_element_type=F32)

    merged = merged + gate_ref[:, D_MODEL:2 * D_MODEL].astype(F32) * jnp.dot(
        o_ref[...], pattn_ref[...], preferred_element_type=F32)

    heads = []
    for h in range(MEM_HEADS):
        c0 = h * MEM_HEAD_DIM
        km = memkv_ref[:, c0:c0 + MEM_HEAD_DIM]
        vm = memkv_ref[:, MEM_WIDTH + c0:MEM_WIDTH + c0 + MEM_HEAD_DIM]
        s = lax.dot_general(qm_ref[:, c0:c0 + MEM_HEAD_DIM], km, (((1,), (1,)), ((), ())),
                            preferred_element_type=F32) * (MEM_HEAD_DIM ** -0.5)
        p = jnp.exp(s - jnp.max(s, axis=-1, keepdims=True))
        p = (p / jnp.sum(p, axis=-1, keepdims=True)).astype(BF16)
        heads.append(jnp.dot(p, vm, preferred_element_type=F32).astype(BF16))
    om = jnp.concatenate(heads, axis=1)
    merged = merged + gate_ref[:, 2 * D_MODEL:3 * D_MODEL].astype(F32) * jnp.dot(
        om, pmem_ref[...], preferred_element_type=F32)

    y = jnp.dot(merged.astype(BF16), wout_ref[...], preferred_element_type=F32)
    x1 = _layer_norm(ALPHA * xn_ref[...] + y, g1_ref[...], b1_ref[...])
    x1r_ref[...] = _rows_to_tiles(x1)

    lt = lax.dot_general(wr_ref[...], x1.astype(BF16), (((1,), (1,)), ((), ())),
                         preferred_element_type=F32)
    gl = [lt[r:r + 1, :] for r in range(N_EXPERT_GROUPS)]
    gmax = gl[0]
    grp = jnp.zeros((1, tm), I32)
    for r in range(1, N_EXPERT_GROUPS):
        better = gl[r] > gmax
        grp = jnp.where(better, r, grp)
        gmax = jnp.where(better, gl[r], gmax)
    gsum = gl[0] * 0.0
    for r in range(N_EXPERT_GROUPS):
        gsum = gsum + jnp.exp(gl[r] - gmax)
    gp = 1.0 / gsum
    sel = jnp.zeros((EXPERTS_PER_GROUP, tm), F32)
    for r in range(N_EXPERT_GROUPS):
        rows = lt[SUBLANES + r * EXPERTS_PER_GROUP:SUBLANES + (r + 1) * EXPERTS_PER_GROUP, :]
        sel = jnp.where(grp == r, rows, sel)
    ridx = lax.broadcasted_iota(I32, (EXPERTS_PER_GROUP, tm), 0)
    top1 = jnp.max(sel, axis=0, keepdims=True)
    i1 = jnp.min(jnp.where(sel == top1, ridx, EXPERTS_PER_GROUP), axis=0, keepdims=True)
    rest = jnp.where(ridx == i1, -jnp.inf, sel)
    top2 = jnp.max(rest, axis=0, keepdims=True)
    i2 = jnp.min(jnp.where(rest == top2, ridx, EXPERTS_PER_GROUP), axis=0, keepdims=True)
    e2 = jnp.exp(top2 - top1)
    inv = gp / (1.0 + e2)
    eid_ref[0:1, :] = grp * EXPERTS_PER_GROUP + i1
    eid_ref[1:2, :] = grp * EXPERTS_PER_GROUP + i2
    wts_ref[0:1, :] = inv
    wts_ref[1:2, :] = e2 * inv


def _mix_call(xn, u, o, qm, gate, memkv, p, batch, seq):
    tm = TM_MIX
    n_tiles = seq // tm
    t = batch * seq
    hb = tm // POOL_HALO

    row = lambda width: pl.BlockSpec((tm, width), lambda b, i: (b * n_tiles + i, 0))
    prev = pl.BlockSpec((POOL_HALO, POOL_WIDTH),
                        lambda b, i: (jnp.maximum((b * n_tiles + i) * hb - 1, 0), 0))
    nxt = pl.BlockSpec((POOL_HALO, POOL_WIDTH),
                       lambda b, i: (jnp.minimum((b * n_tiles + i + 1) * hb, t // POOL_HALO - 1), 0))
    full = lambda a: pl.BlockSpec(a.shape, lambda b, i: (0,) * a.ndim)
    lane_row = pl.BlockSpec((TOP_K, tm), lambda b, i: (0, b * n_tiles + i))
    weights = (p["w_pool"], p["pool_scale"], p["p_pool"], p["p_attn"], p["p_mem"], p["w_out"],
               p["ln1_g"], p["ln1_b"], p["w_router"])
    return pl.pallas_call(
        functools.partial(_mix_kernel, seq=seq),
        grid=(batch, n_tiles),
        in_specs=[row(D_MODEL), row(POOL_WIDTH), prev, nxt, row(ATTN_WIDTH), row(MEM_WIDTH),
                  row(N_BRANCHES * D_MODEL),
                  pl.BlockSpec((MEM_TOKENS, 2 * MEM_WIDTH), lambda b, i: (b, 0))]
                 + [full(w) for w in weights],
        out_specs=[pl.BlockSpec((tm, ROW_TILE, LANES), lambda b, i: (b * n_tiles + i, 0, 0)),
                   lane_row, lane_row],
        out_shape=[jax.ShapeDtypeStruct((t, ROW_TILE, LANES), F32),
                   jax.ShapeDtypeStruct((TOP_K, t), I32),
                   jax.ShapeDtypeStruct((TOP_K, t), F32)],
        scratch_shapes=[pltpu.VMEM((tm + 2 * POOL_HALO, POOL_WIDTH), F32)],
        compiler_params=_params(dimension_semantics=("arbitrary", "arbitrary")),
        name="mix",
    )(xn, u, u, u, o, qm, gate, memkv, *weights)


def _moe_kernel(iblk_ref, ie_ref, inext_ref, irun_ref, ilo_ref, ihi_ref,
                x_ref, ws_ref, wgu_hbm, wd_hbm, y_ref,
                x_bf, h_bf, wgu_f32, wd_f32, wgu_bf, wd_bf, wsem):
    i = pl.program_id(0)
    bm = x_ref.shape[0]
    lo = ilo_ref[i]
    hi = ihi_ref[i]
    nonempty = hi > lo

    def weight_copies(e, ws):
        return (pltpu.make_async_copy(wgu_hbm.at[e], wgu_f32.at[ws], wsem.at[ws]),
                pltpu.make_async_copy(wd_hbm.at[e], wd_f32.at[ws], wsem.at[ws]))

    @pl.when(jnp.logical_or(i == 0, ie_ref[i] != ie_ref[jnp.maximum(i - 1, 0)]))
    def _():
        e = ie_ref[i]
        wslot = irun_ref[i] & 1

        @pl.when(i == 0)
        def _():
            for c in weight_copies(e, wslot):
                c.start()
        for c in weight_copies(e, wslot):
            c.wait()

        @pl.when(inext_ref[i] >= 0)
        def _():
            for c in weight_copies(inext_ref[i], 1 - wslot):
                c.start()
        wgu_bf[...] = wgu_f32[wslot].astype(BF16)
        wd_bf[...] = wd_f32[wslot].astype(BF16)

    @pl.when(nonempty)
    def _():
        x_bf[...] = _tiles_to_rows(x_ref[...]).astype(BF16)

    @pl.when(nonempty)
    def _():
        gu = jnp.dot(x_bf[...], wgu_bf[...], preferred_element_type=F32)
        gate = gu[:, :EXPERT_HIDDEN]
        h_bf[...] = ((gate * _sigmoid(gate)) * gu[:, EXPERT_HIDDEN:]).astype(BF16)

    @pl.when(nonempty)
    def _():
        rows = lax.broadcasted_iota(I32, (bm, 1), 0)
        mine = jnp.logical_and(rows >= lo, rows < hi)
        y = jnp.dot(h_bf[...], wd_bf[...], preferred_element_type=F32) * ws_ref[...]
        y = _rows_to_tiles(jnp.where(mine, y, 0.0))

        @pl.when(lo == 0)
        def _():
            y_ref[...] = y

        @pl.when(lo > 0)
        def _():
            y_ref[...] = y_ref[...] + y


def _moe_call(xs, wsorted, item_blk, item_e, item_lo, item_hi, w_gu, w_down):
    a = xs.shape[0]
    bm = BM_MOE
    n_items = item_blk.shape[0]
    change = jnp.concatenate([jnp.zeros((1,), I32), (item_e[1:] != item_e[:-1]).astype(I32)])
    item_run = jnp.cumsum(change).astype(I32)
    first_later = jnp.sum((item_e[None, :] <= item_e[:, None]).astype(I32), axis=1)
    item_next = jnp.where(first_later < n_items, item_e[jnp.minimum(first_later, n_items - 1)], -1).astype(I32)
    grid_spec = pltpu.PrefetchScalarGridSpec(
        num_scalar_prefetch=6,
        grid=(n_items,),
        in_specs=[pl.BlockSpec((bm, ROW_TILE, LANES), lambda i, blk, *_: (blk[i], 0, 0)),
                  pl.BlockSpec((bm, 1), lambda i, blk, *_: (blk[i], 0)),
                  pl.BlockSpec(memory_space=pl.ANY),
                  pl.BlockSpec(memory_space=pl.ANY)],
        out_specs=pl.BlockSpec((bm, ROW_TILE, LANES), lambda i, blk, *_: (blk[i], 0, 0)),
        scratch_shapes=[pltpu.VMEM((bm, D_MODEL), BF16),
                        pltpu.VMEM((bm, EXPERT_HIDDEN), BF16),
                        pltpu.VMEM((2, D_MODEL, 2 * EXPERT_HIDDEN), F32),
                        pltpu.VMEM((2, EXPERT_HIDDEN, D_MODEL), F32),
                        pltpu.VMEM((D_MODEL, 2 * EXPERT_HIDDEN), BF16),
                        pltpu.VMEM((EXPERT_HIDDEN, D_MODEL), BF16),
                        pltpu.SemaphoreType.DMA((2,))],
    )
    return pl.pallas_call(
        _moe_kernel,
        grid_spec=grid_spec,
        out_shape=jax.ShapeDtypeStruct((a, ROW_TILE, LANES), F32),
        compiler_params=_params(dimension_semantics=("arbitrary",)),
        name="moe",
    )(item_blk, item_e, item_next, item_run, item_lo, item_hi, xs, wsorted, w_gu, w_down)


def _sc_mesh():
    return plsc.VectorSubcoreMesh(core_axis_name="core", subcore_axis_name="subcore")


def _sc_gather_rows(rows, idx):
    m = idx.shape[0]
    w = SC_ROW_WINDOW

    @pl.kernel(out_type=jax.ShapeDtypeStruct((m,) + rows.shape[1:], rows.dtype), mesh=_sc_mesh(),
               scratch_types=[])
    def gather(rows_hbm, idx_hbm, out_hbm):
        def body(idx_vmem, out_vmem):
            pltpu.sync_copy(rows_hbm.at[idx_vmem.at[0]], out_vmem)

        pltpu.emit_pipeline(
            body,
            grid=(m // w,),
            in_specs=[pl.BlockSpec((1, w), index_map=lambda i: (i, 0))],
            out_specs=[pl.BlockSpec((w,) + rows.shape[1:], index_map=lambda i: (i, 0, 0))],
            core_axis_name=("core", "subcore"),
            dimension_semantics=(pltpu.PARALLEL,),
        )(idx_hbm, out_hbm)

    return gather(rows, idx.reshape(m // w, w))


def _sc_scatter_rows(rows, idx):
    m = idx.shape[0]
    w = SC_ROW_WINDOW

    @pl.kernel(out_type=jax.ShapeDtypeStruct(rows.shape, rows.dtype), mesh=_sc_mesh(), scratch_types=[])
    def scatter(rows_hbm, idx_hbm, out_hbm):
        def body(rows_vmem, idx_vmem):
            pltpu.sync_copy(rows_vmem, out_hbm.at[idx_vmem.at[0]])

        pltpu.emit_pipeline(
            body,
            grid=(m // w,),
            in_specs=[pl.BlockSpec((w,) + rows.shape[1:], index_map=lambda i: (i, 0, 0)),
                      pl.BlockSpec((1, w), index_map=lambda i: (i, 0))],
            out_specs=[],
            core_axis_name=("core", "subcore"),
            dimension_semantics=(pltpu.PARALLEL,),
        )(rows_hbm, idx_hbm)

    return scatter(rows, idx.reshape(m // w, w))


def _final_kernel(x1_ref, ya_ref, yb_ref, g_ref, b_ref, out_ref):
    moe = ya_ref[...] + yb_ref[...]
    out_ref[...] = _layer_norm(_tiles_to_rows(ALPHA * x1_ref[...] + moe), g_ref[...], b_ref[...])


def _final_call(x1, y2, ln_g, ln_b):
    t = x1.shape[0]
    tm = TM_FINAL
    n_tiles = t // tm
    return pl.pallas_call(
        _final_kernel,
        grid=(n_tiles,),
        in_specs=[pl.BlockSpec((tm, ROW_TILE, LANES), lambda i: (i, 0, 0)),
                  pl.BlockSpec((tm, ROW_TILE, LANES), lambda i: (i, 0, 0)),
                  pl.BlockSpec((tm, ROW_TILE, LANES), lambda i: (n_tiles + i, 0, 0)),
                  pl.BlockSpec((1, D_MODEL), lambda i: (0, 0)),
                  pl.BlockSpec((1, D_MODEL), lambda i: (0, 0))],
        out_specs=pl.BlockSpec((tm, D_MODEL), lambda i: (i, 0)),
        out_shape=jax.ShapeDtypeStruct((t, D_MODEL), F32),
        compiler_params=_params(dimension_semantics=("arbitrary",)),
        name="final",
    )(x1, y2, y2, ln_g, ln_b)


def _t5_bucket(rel):
    nb = N_REL_BUCKETS // 2
    max_exact = nb // 2
    ret = jnp.where(rel > 0, nb, 0)
    n = jnp.abs(rel)
    nf = jnp.maximum(n, 1).astype(F32)
    large = max_exact + (jnp.log(nf / max_exact) / math.log(REL_MAX_DISTANCE / max_exact)
                         * (nb - max_exact)).astype(I32)
    large = jnp.minimum(large, nb - 1)
    return ret + jnp.where(n < max_exact, n, large)


def _bias_tables(rel_table):
    n, m = BLOCK, 3 * BLOCK
    rel = jnp.arange(-(2 * BLOCK - 1), 2 * BLOCK)
    by_rel = jnp.where((jnp.abs(rel) <= WINDOW)[:, None],
                       rel_table[_t5_bucket(rel)].astype(F32) * LOG2_E, NEG_INF)
    by_rel = jnp.pad(by_rel.T, ((0, 0), (0, 1)))
    skew = jnp.tile(by_rel, (1, n))[:, :n * (m + n - 1)].reshape(N_Q_HEADS, n, m + n - 1)
    bias = skew[:, :, n - 1:n - 1 + m]
    heads = [4 * h + half for h in range(N_KV_HEADS) for half in range(2)]
    units = jnp.concatenate([jnp.stack([bias[a] for a in heads]),
                             jnp.stack([bias[a + 2] for a in heads])], axis=1)
    masked = jnp.full((2 * N_KV_HEADS, 2 * BLOCK, BLOCK), NEG_INF, F32)
    prev = jnp.stack([units[:, :, :BLOCK], masked])
    cur = units[:, :, BLOCK:2 * BLOCK]
    nxt = jnp.stack([units[:, :, 2 * BLOCK:], masked])
    return prev, cur, nxt


def _prepare(rel_bias_table, w_in, b_in, w_pool, pool_scale, p_pool, sink, p_attn, w_mem_kv, p_mem,
             w_out, ln1_g, ln1_b, w_router_group, w_router_expert, w_gu, w_down):
    l = 0
    w_router = jnp.zeros((ROUTER_ROWS, D_MODEL), F32)
    w_router = w_router.at[:N_EXPERT_GROUPS].set(w_router_group[l].T)
    w_router = w_router.at[SUBLANES:].set(w_router_expert[l].T)
    bias_prev, bias_cur, bias_next = _bias_tables(rel_bias_table)
    return dict(
        w_in=w_in[l].astype(BF16), b_in=b_in[l][None, :],
        w_pool=w_pool[l].astype(BF16), pool_scale=pool_scale[l][None, :],
        p_pool=p_pool[l].astype(BF16), p_attn=p_attn[l].astype(BF16), p_mem=p_mem[l].astype(BF16),
        w_out=w_out[l].astype(BF16), w_memkv=w_mem_kv[l].astype(BF16),
        ln1_g=ln1_g[l][None, :], ln1_b=ln1_b[l][None, :],
        w_router=w_router.astype(BF16), sink=sink[l] * LOG2_E,
        w_gu=w_gu[l], w_down=w_down[l],
        bias_prev=bias_prev, bias_cur=bias_cur, bias_next=bias_next,
    )


def _dispatch_plan(eid, wts):
    t = eid.shape[1]
    a = TOP_K * t
    bm = BM_MOE
    nblk = a // bm
    eflat = eid.reshape(a)
    wflat = wts.reshape(a)
    order = jnp.argsort(eflat, stable=True).astype(I32)
    wsorted = wflat[order][:, None]
    experts = jnp.arange(N_EXPERTS, dtype=I32)
    counts = jnp.sum((eflat[None, :] == experts[:, None]).astype(I32), axis=1)
    ends = jnp.cumsum(counts).astype(I32)
    starts = ends - counts
    cuts = jnp.sort(jnp.concatenate([jnp.arange(nblk, dtype=I32) * bm, starts, jnp.array([a], I32)]))
    lo_abs = cuts[:-1]
    hi_abs = cuts[1:]
    nonempty = hi_abs > lo_abs
    blk = jnp.minimum(lo_abs // bm, nblk - 1)
    probe = jnp.minimum(lo_abs, a - 1)
    item_e = jnp.sum((ends[None, :] <= probe[:, None]).astype(I32), axis=1)
    item_lo = jnp.where(nonempty, lo_abs - blk * bm, 0).astype(I32)
    item_hi = jnp.where(nonempty, hi_abs - blk * bm, 0).astype(I32)
    return order, wsorted, blk.astype(I32), item_e.astype(I32), item_lo, item_hi


def _trunk(x, mem, ln_in_g, ln_in_b, ln2_g, ln2_b, p):
    batch, seq, d = x.shape
    t = batch * seq
    assert seq % TM_MIX == 0 and seq % TQ_ATTN == 0 and t % TM_INPROJ == 0 and (TOP_K * t) % BM_MOE == 0
    memkv = _memkv_call(mem.reshape(batch * MEM_TOKENS, d), p["w_memkv"])
    xn, u, q, kk, vv, qm, gate = _inproj_call(x.reshape(t, d), ln_in_g[None, :], ln_in_b[None, :],
                                              p["w_in"], p["b_in"])
    o = _attn_call(q, kk, vv, p["sink"], p["bias_prev"], p["bias_cur"], p["bias_next"], batch, seq)
    x1r, eid, wts = _mix_call(xn, u, o, qm, gate, memkv, p, batch, seq)
    order, wsorted, item_blk, item_e, item_lo, item_hi = _dispatch_plan(eid, wts)
    sorted_tok = jnp.where(order >= t, order - t, order)
    xs = _sc_gather_rows(x1r, sorted_tok)
    ys = _moe_call(xs, wsorted, item_blk, item_e, item_lo, item_hi, p["w_gu"], p["w_down"])
    y2 = _sc_scatter_rows(ys, order)
    out = _final_call(x1r, y2, ln2_g[None, :], ln2_b[None, :])
    return out.reshape(batch, seq, d)


def kernel(x_prompt, x_sample, mem_prompt, mem_sample, ln_in_g, ln_in_b, rel_bias_table, w_in, b_in,
           w_pool, pool_scale, p_pool, sink, p_attn, w_mem_kv, p_mem, w_out, ln1_g, ln1_b,
           w_router_group, w_router_expert, w_gu, w_down, ln2_g, ln2_b):
    p = _prepare(rel_bias_table, w_in, b_in, w_pool, pool_scale, p_pool, sink, p_attn, w_mem_kv, p_mem,
                 w_out, ln1_g, ln1_b, w_router_group, w_router_expert, w_gu, w_down)
    y_prompt = _trunk(x_prompt, mem_prompt, ln_in_g, ln_in_b, ln2_g[0], ln2_b[0], p)
    y_sample = _trunk(x_sample, mem_sample, ln_in_g, ln_in_b, ln2_g[0], ln2_b[0], p)
    return (y_prompt, y_sample)
```

```python
import functools
import math

import jax
import jax.numpy as jnp
from jax import lax
from jax.experimental import pallas as pl
from jax.experimental.pallas import tpu as pltpu
from jax.experimental.pallas import tpu_sc as plsc

F32 = jnp.float32
BF16 = jnp.bfloat16
I32 = jnp.int32

D_MODEL = 1024
DEPTH = 1
POOL_WIDTH = 512
POOL_WINDOWS = (2, 4, 8, 16)
N_POOL_GROUPS = 4
POOL_GROUP_CH = POOL_WIDTH // N_POOL_GROUPS
N_Q_HEADS = 16
N_KV_HEADS = 4
HEAD_DIM = 64
ATTN_WIDTH = N_Q_HEADS * HEAD_DIM
KV_WIDTH = N_KV_HEADS * HEAD_DIM
WINDOW = 128
BLOCK = 128
N_REL_BUCKETS = 32
REL_MAX_DISTANCE = 128
MEM_TOKENS = 256
MEM_HEADS = 4
MEM_HEAD_DIM = 128
MEM_WIDTH = MEM_HEADS * MEM_HEAD_DIM
N_BRANCHES = 3
N_EXPERT_GROUPS = 4
EXPERTS_PER_GROUP = 8
N_EXPERTS = N_EXPERT_GROUPS * EXPERTS_PER_GROUP
TOP_K = 2
EXPERT_HIDDEN = 512
ALPHA = (2 * DEPTH) ** 0.25
LN_EPS = 1e-5
NEG_INF = -1e30
LOG2_E = math.log2(math.e)

LANES = 128
SUBLANES = 8
ROW_TILE = D_MODEL // LANES
VMEM_LIMIT_BYTES = 56 * 1024 * 1024

KV_DUP_WIDTH = N_KV_HEADS * LANES

C_U = 0
C_Q = C_U + POOL_WIDTH
C_K = C_Q + ATTN_WIDTH
C_V = C_K + KV_WIDTH
C_M = C_V + KV_WIDTH
C_G = C_M + MEM_WIDTH
C_END = C_G + N_BRANCHES * D_MODEL

TM_INPROJ = 512
TQ_ATTN = 256
TM_MIX = 512
BM_MOE = 512
SC_ROW_WINDOW = 32
TM_FINAL = 512
POOL_HALO = 8
ROUTER_ROWS = 40


def _layer_norm(x, g, b):
    mu = jnp.mean(x, axis=-1, keepdims=True)
    xc = x - mu
    var = jnp.mean(xc * xc, axis=-1, keepdims=True)
    return xc * lax.rsqrt(var + LN_EPS) * g + b


def _sigmoid(x):
    return 1.0 / (1.0 + jnp.exp(-x))


def _rows_to_tiles(x):
    return pltpu.einshape("r(cl)->rcl", x, c=ROW_TILE)


def _tiles_to_rows(x):
    return pltpu.einshape("rcl->r(cl)", x)


def _params(**kw):
    return pltpu.CompilerParams(vmem_limit_bytes=VMEM_LIMIT_BYTES, **kw)


def _memkv_kernel(mem_ref, w_ref, out_ref):
    out_ref[...] = jnp.dot(mem_ref[...].astype(BF16), w_ref[...],
                           preferred_element_type=F32).astype(BF16)


def _memkv_call(mem2, w_memkv):
    rows = mem2.shape[0]
    return pl.pallas_call(
        _memkv_kernel,
        grid=(rows // MEM_TOKENS,),
        in_specs=[pl.BlockSpec((MEM_TOKENS, D_MODEL), lambda i: (i, 0)),
                  pl.BlockSpec((D_MODEL, 2 * MEM_WIDTH), lambda i: (0, 0))],
        out_specs=pl.BlockSpec((MEM_TOKENS, 2 * MEM_WIDTH), lambda i: (i, 0)),
        out_shape=jax.ShapeDtypeStruct((rows, 2 * MEM_WIDTH), BF16),
        compiler_params=_params(dimension_semantics=("arbitrary",)),
        name="memkv",
    )(mem2, w_memkv)


def _inproj_kernel(x_ref, g_ref, b_ref, w_ref, bias_ref,
                   xn_ref, u_ref, q_ref, k_ref, v_ref, qm_ref, gate_ref):
    xn = _layer_norm(x_ref[...], g_ref[...], b_ref[...])
    xn_ref[...] = xn
    xb = xn.astype(BF16)

    def seg(lo, hi):
        return jnp.dot(xb, w_ref[:, lo:hi], preferred_element_type=F32) + bias_ref[:, lo:hi]

    u_ref[...] = seg(C_U, C_Q)
    q_ref[...] = (seg(C_Q, C_K) * (HEAD_DIM ** -0.5 * LOG2_E)).astype(BF16)
    kv = seg(C_K, C_M)
    low = lax.broadcasted_iota(I32, (kv.shape[0], LANES), 1) < HEAD_DIM
    for out_ref, c_lo in ((k_ref, 0), (v_ref, KV_WIDTH)):
        for c in range(KV_WIDTH // LANES):
            blk = kv[:, c_lo + c * LANES:c_lo + (c + 1) * LANES]
            rot = pltpu.roll(blk, HEAD_DIM, 1)
            out_ref[:, 2 * c * LANES:(2 * c + 1) * LANES] = jnp.where(low, blk, rot).astype(BF16)
            out_ref[:, (2 * c + 1) * LANES:(2 * c + 2) * LANES] = jnp.where(low, rot, blk).astype(BF16)
    qm_ref[...] = seg(C_M, C_G).astype(BF16)
    for j in range(N_BRANCHES):
        lo = C_G + j * D_MODEL
        gate_ref[:, j * D_MODEL:(j + 1) * D_MODEL] = _sigmoid(seg(lo, lo + D_MODEL)).astype(BF16)


def _inproj_call(x2, ln_g, ln_b, w_in, b_in):
    t = x2.shape[0]
    tm = TM_INPROJ
    row = lambda width: pl.BlockSpec((tm, width), lambda i: (i, 0))
    const = lambda shape: pl.BlockSpec(shape, lambda i: (0, 0))
    widths = (D_MODEL, POOL_WIDTH, ATTN_WIDTH, KV_DUP_WIDTH, KV_DUP_WIDTH, MEM_WIDTH, N_BRANCHES * D_MODEL)
    dtypes = (F32, F32, BF16, BF16, BF16, BF16, BF16)
    return pl.pallas_call(
        _inproj_kernel,
        grid=(t // tm,),
        in_specs=[row(D_MODEL), const((1, D_MODEL)), const((1, D_MODEL)),
                  const((D_MODEL, C_END)), const((1, C_END))],
        out_specs=[row(w) for w in widths],
        out_shape=[jax.ShapeDtypeStruct((t, w), d) for w, d in zip(widths, dtypes)],
        compiler_params=_params(dimension_semantics=("arbitrary",)),
        name="inproj",
    )(x2, ln_g, ln_b, w_in, b_in)


def _attn_kernel(sink_ref, q_ref, kc_ref, kp_ref, kn_ref, vc_ref, vp_ref, vn_ref,
                 bp_ref, bc_ref, bn_ref, o_ref, klo, khi, vlo, vhi, s_scr, p_scr, inv_scr):
    i = pl.program_id(1)
    n_tiles = pl.num_programs(1)
    tq = q_ref.shape[0]
    n_qb = tq // BLOCK

    lane = lax.broadcasted_iota(I32, (BLOCK, KV_DUP_WIDTH), 1)
    low = (lane & (LANES - 1)) < HEAD_DIM

    def put(dst_lo, dst_hi, r0, val):
        zero = jnp.zeros_like(val)
        dst_lo[r0:r0 + BLOCK, :] = jnp.where(low, val, zero)
        dst_hi[r0:r0 + BLOCK, :] = jnp.where(low, zero, val)

    put(klo, khi, 0, kp_ref[...])
    put(vlo, vhi, 0, vp_ref[...])
    for j in range(n_qb):
        put(klo, khi, (j + 1) * BLOCK, kc_ref[j * BLOCK:(j + 1) * BLOCK, :])
        put(vlo, vhi, (j + 1) * BLOCK, vc_ref[j * BLOCK:(j + 1) * BLOCK, :])
    put(klo, khi, (n_qb + 1) * BLOCK, kn_ref[...])
    put(vlo, vhi, (n_qb + 1) * BLOCK, vn_ref[...])

    first = (i == 0).astype(I32)
    last = (i == n_tiles - 1).astype(I32)

    def softmax_rows(s, sink):
        m = jnp.maximum(jnp.max(s, axis=-1, keepdims=True), sink)
        p = jnp.exp2(s - m)
        denom = jnp.sum(p, axis=-1, keepdims=True) + jnp.exp2(sink - m)
        return p.astype(BF16), jnp.broadcast_to(1.0 / denom, (s.shape[0], LANES))

    units = [(j, h, half) for j in range(n_qb) for h in range(N_KV_HEADS) for half in range(2)]
    always = i >= 0

    @pl.when(always)
    def _scores():
        for u, (j, h, half) in enumerate(units):
            r0 = j * BLOCK
            c0 = h * 2 * LANES
            pv = first if j == 0 else 0
            nv = last if j == n_qb - 1 else 0
            unit = h * 2 + half
            q_pairs = jnp.concatenate([q_ref[r0:r0 + BLOCK, c0:c0 + LANES],
                                       q_ref[r0:r0 + BLOCK, c0 + LANES:c0 + 2 * LANES]], axis=0)
            kx = (klo, khi)[half][r0:r0 + 3 * BLOCK, h * LANES:(h + 1) * LANES]
            s = lax.dot_general(q_pairs, kx, (((1,), (1,)), ((), ())), preferred_element_type=F32)
            bias = jnp.concatenate([bp_ref[pv, unit], bc_ref[unit], bn_ref[nv, unit]], axis=1)
            s_scr[u] = s + bias

    @pl.when(always)
    def _softmax():
        for u, (j, h, half) in enumerate(units):
            p_scr[u, :BLOCK], inv_scr[u, :BLOCK] = softmax_rows(s_scr[u, :BLOCK], sink_ref[4 * h + half])
            p_scr[u, BLOCK:], inv_scr[u, BLOCK:] = softmax_rows(s_scr[u, BLOCK:], sink_ref[4 * h + 2 + half])

    @pl.when(always)
    def _values():
        for j in range(n_qb):
            r0 = j * BLOCK
            for h in range(N_KV_HEADS):
                c0 = h * 2 * LANES
                acc = jnp.zeros((2 * BLOCK, LANES), F32)
                for half, vref in enumerate((vlo, vhi)):
                    u = units.index((j, h, half))
                    vx = vref[r0:r0 + 3 * BLOCK, h * LANES:(h + 1) * LANES]
                    acc = acc + jnp.dot(p_scr[u], vx, preferred_element_type=F32) * inv_scr[u]
                o_ref[r0:r0 + BLOCK, c0:c0 + LANES] = acc[:BLOCK].astype(BF16)
                o_ref[r0:r0 + BLOCK, c0 + LANES:c0 + 2 * LANES] = acc[BLOCK:].astype(BF16)


def _attn_call(q, kk, vv, sink, bias_prev, bias_cur, bias_next, batch, seq):
    tq = TQ_ATTN
    n_qb = tq // BLOCK
    nb = seq // BLOCK
    n_tiles = seq // tq
    t = batch * seq

    cur = lambda width: pl.BlockSpec((tq, width), lambda b, i: (b * n_tiles + i, 0))
    prev = pl.BlockSpec((BLOCK, KV_DUP_WIDTH), lambda b, i: (b * nb + jnp.maximum(i * n_qb - 1, 0), 0))
    nxt = pl.BlockSpec((BLOCK, KV_DUP_WIDTH), lambda b, i: (b * nb + jnp.minimum((i + 1) * n_qb, nb - 1), 0))
    full = lambda a: pl.BlockSpec(a.shape, lambda b, i: (0,) * a.ndim)
    ext = (n_qb + 2) * BLOCK
    n_units = n_qb * N_KV_HEADS * 2
    return pl.pallas_call(
        _attn_kernel,
        grid=(batch, n_tiles),
        in_specs=[pl.BlockSpec(memory_space=pltpu.SMEM),
                  cur(ATTN_WIDTH), cur(KV_DUP_WIDTH), prev, nxt, cur(KV_DUP_WIDTH), prev, nxt,
                  full(bias_prev), full(bias_cur), full(bias_next)],
        out_specs=cur(ATTN_WIDTH),
        out_shape=jax.ShapeDtypeStruct((t, ATTN_WIDTH), BF16),
        scratch_shapes=[pltpu.VMEM((ext, KV_DUP_WIDTH), BF16) for _ in range(4)]
                       + [pltpu.VMEM((n_units, 2 * BLOCK, 3 * BLOCK), F32),
                          pltpu.VMEM((n_units, 2 * BLOCK, 3 * BLOCK), BF16),
                          pltpu.VMEM((n_units, 2 * BLOCK, LANES), F32)],
        compiler_params=_params(dimension_semantics=("arbitrary", "arbitrary")),
        name="attn",
    )(sink, q, kk, kk, kk, vv, vv, vv, bias_prev, bias_cur, bias_next)


def _mix_kernel(xn_ref, u_ref, up_ref, un_ref, o_ref, qm_ref, gate_ref, memkv_ref,
                wpool_ref, pscale_ref, ppool_ref, pattn_ref, pmem_ref, wout_ref,
                g1_ref, b1_ref, wr_ref,
                x1r_ref, eid_ref, wts_ref, uext, *, seq):
    i = pl.program_id(1)
    n_tiles = pl.num_programs(1)
    tm = xn_ref.shape[0]

    halo = POOL_HALO
    uext[0:halo, :] = jnp.where(i > 0, up_ref[...], 0.0)
    uext[halo:halo + tm, :] = u_ref[...]
    uext[halo + tm:halo + tm + halo, :] = jnp.where(i < n_tiles - 1, un_ref[...], 0.0)
    pos = i * tm + lax.broadcasted_iota(I32, (tm, 1), 0)
    mixed = []
    for gi, win in enumerate(POOL_WINDOWS):
        c0 = gi * POOL_GROUP_CH
        half = win // 2
        total = jnp.zeros((tm, POOL_GROUP_CH), F32)
        for off in range(-half, half):
            total = total + uext[halo + off:halo + off + tm, c0:c0 + POOL_GROUP_CH]
        cnt = (jnp.minimum(pos + half, seq) - jnp.maximum(pos - half, 0)).astype(F32)
        pooled = total * (1.0 / cnt) - u_ref[:, c0:c0 + POOL_GROUP_CH]
        mixed.append(jnp.dot(pooled.astype(BF16), wpool_ref[gi], preferred_element_type=F32)
                     * pscale_ref[:, c0:c0 + POOL_GROUP_CH])
    mixed = jnp.concatenate(mixed, axis=1).astype(BF16)
    merged = gate_ref[:, 0:D_MODEL].astype(F32) * jnp.dot(
        mixed, ppool_ref[...], preferred_element_type=F32)

    merged = merged + gate_ref[:, D_MODEL:2 * D_MODEL].astype(F32) * jnp.dot(
        o_ref[...], pattn_ref[...], preferred_element_type=F32)

    heads = []
    for h in range(MEM_HEADS):
        c0 = h * MEM_HEAD_DIM
        km = memkv_ref[:, c0:c0 + MEM_HEAD_DIM]
        vm = memkv_ref[:, MEM_WIDTH + c0:MEM_WIDTH + c0 + MEM_HEAD_DIM]
        s = lax.dot_general(qm_ref[:, c0:c0 + MEM_HEAD_DIM], km, (((1,), (1,)), ((), ())),
                            preferred_element_type=F32) * (MEM_HEAD_DIM ** -0.5)
        p = jnp.exp(s - jnp.max(s, axis=-1, keepdims=True))
        p = (p / jnp.sum(p, axis=-1, keepdims=True)).astype(BF16)
        heads.append(jnp.dot(p, vm, preferred_element_type=F32).astype(BF16))
    om = jnp.concatenate(heads, axis=1)
    merged = merged + gate_ref[:, 2 * D_MODEL:3 * D_MODEL].astype(F32) * jnp.dot(
        om, pmem_ref[...], preferred_element_type=F32)

    y = jnp.dot(merged.astype(BF16), wout_ref[...], preferred_element_type=F32)
    x1 = _layer_norm(ALPHA * xn_ref[...] + y, g1_ref[...], b1_ref[...])
    x1r_ref[...] = _rows_to_tiles(x1)

    lt = lax.dot_general(wr_ref[...], x1.astype(BF16), (((1,), (1,)), ((), ())),
                         preferred_element_type=F32)
    gl = [lt[r:r + 1, :] for r in range(N_EXPERT_GROUPS)]
    gmax = gl[0]
    grp = jnp.zeros((1, tm), I32)
    for r in range(1, N_EXPERT_GROUPS):
        better = gl[r] > gmax
        grp = jnp.where(better, r, grp)
        gmax = jnp.where(better, gl[r], gmax)
    gsum = gl[0] * 0.0
    for r in range(N_EXPERT_GROUPS):
        gsum = gsum + jnp.exp(gl[r] - gmax)
    gp = 1.0 / gsum
    sel = jnp.zeros((EXPERTS_PER_GROUP, tm), F32)
    for r in range(N_EXPERT_GROUPS):
        rows = lt[SUBLANES + r * EXPERTS_PER_GROUP:SUBLANES + (r + 1) * EXPERTS_PER_GROUP, :]
        sel = jnp.where(grp == r, rows, sel)
    ridx = lax.broadcasted_iota(I32, (EXPERTS_PER_GROUP, tm), 0)
    top1 = jnp.max(sel, axis=0, keepdims=True)
    i1 = jnp.min(jnp.where(sel == top1, ridx, EXPERTS_PER_GROUP), axis=0, keepdims=True)
    rest = jnp.where(ridx == i1, -jnp.inf, sel)
    top2 = jnp.max(rest, axis=0, keepdims=True)
    i2 = jnp.min(jnp.where(rest == top2, ridx, EXPERTS_PER_GROUP), axis=0, keepdims=True)
    e2 = jnp.exp(top2 - top1)
    inv = gp / (1.0 + e2)
    eid_ref[0:1, :] = grp * EXPERTS_PER_GROUP + i1
    eid_ref[1:2, :] = grp * EXPERTS_PER_GROUP + i2
    wts_ref[0:1, :] = inv
    wts_ref[1:2, :] = e2 * inv


def _mix_call(xn, u, o, qm, gate, memkv, p, batch, seq):
    tm = TM_MIX
    n_tiles = seq // tm
    t = batch * seq
    hb = tm // POOL_HALO

    row = lambda width: pl.BlockSpec((tm, width), lambda b, i: (b * n_tiles + i, 0))
    prev = pl.BlockSpec((POOL_HALO, POOL_WIDTH),
                        lambda b, i: (jnp.maximum((b * n_tiles + i) * hb - 1, 0), 0))
    nxt = pl.BlockSpec((POOL_HALO, POOL_WIDTH),
                       lambda b, i: (jnp.minimum((b * n_tiles + i + 1) * hb, t // POOL_HALO - 1), 0))
    full = lambda a: pl.BlockSpec(a.shape, lambda b, i: (0,) * a.ndim)
    lane_row = pl.BlockSpec((TOP_K, tm), lambda b, i: (0, b * n_tiles + i))
    weights = (p["w_pool"], p["pool_scale"], p["p_pool"], p["p_attn"], p["p_mem"], p["w_out"],
               p["ln1_g"], p["ln1_b"], p["w_router"])
    return pl.pallas_call(
        functools.partial(_mix_kernel, seq=seq),
        grid=(batch, n_tiles),
        in_specs=[row(D_MODEL), row(POOL_WIDTH), prev, nxt, row(ATTN_WIDTH), row(MEM_WIDTH),
                  row(N_BRANCHES * D_MODEL),
                  pl.BlockSpec((MEM_TOKENS, 2 * MEM_WIDTH), lambda b, i: (b, 0))]
                 + [full(w) for w in weights],
        out_specs=[pl.BlockSpec((tm, ROW_TILE, LANES), lambda b, i: (b * n_tiles + i, 0, 0)),
                   lane_row, lane_row],
        out_shape=[jax.ShapeDtypeStruct((t, ROW_TILE, LANES), F32),
                   jax.ShapeDtypeStruct((TOP_K, t), I32),
                   jax.ShapeDtypeStruct((TOP_K, t), F32)],
        scratch_shapes=[pltpu.VMEM((tm + 2 * POOL_HALO, POOL_WIDTH), F32)],
        compiler_params=_params(dimension_semantics=("arbitrary", "arbitrary")),
        name="mix",
    )(xn, u, u, u, o, qm, gate, memkv, *weights)


def _moe_kernel(iblk_ref, ie_ref, inext_ref, irun_ref, ilo_ref, ihi_ref,
                x_ref, ws_ref, wgu_hbm, wd_hbm, y_ref,
                x_bf, h_bf, wgu_f32, wd_f32, wgu_bf, wd_bf, wsem):
    i = pl.program_id(0)
    bm = x_ref.shape[0]
    lo = ilo_ref[i]
    hi = ihi_ref[i]
    nonempty = hi > lo

    def weight_copies(e, ws):
        return (pltpu.make_async_copy(wgu_hbm.at[e], wgu_f32.at[ws], wsem.at[ws]),
                pltpu.make_async_copy(wd_hbm.at[e], wd_f32.at[ws], wsem.at[ws]))

    @pl.when(jnp.logical_or(i == 0, ie_ref[i] != ie_ref[jnp.maximum(i - 1, 0)]))
    def _():
        e = ie_ref[i]
        wslot = irun_ref[i] & 1

        @pl.when(i == 0)
        def _():
            for c in weight_copies(e, wslot):
                c.start()
        for c in weight_copies(e, wslot):
            c.wait()

        @pl.when(inext_ref[i] >= 0)
        def _():
            for c in weight_copies(inext_ref[i], 1 - wslot):
                c.start()
        wgu_bf[...] = wgu_f32[wslot].astype(BF16)
        wd_bf[...] = wd_f32[wslot].astype(BF16)

    @pl.when(nonempty)
    def _():
        x_bf[...] = _tiles_to_rows(x_ref[...]).astype(BF16)

    @pl.when(nonempty)
    def _():
        gu = jnp.dot(x_bf[...], wgu_bf[...], preferred_element_type=F32)
        gate = gu[:, :EXPERT_HIDDEN]
        h_bf[...] = ((gate * _sigmoid(gate)) * gu[:, EXPERT_HIDDEN:]).astype(BF16)

    @pl.when(nonempty)
    def _():
        rows = lax.broadcasted_iota(I32, (bm, 1), 0)
        mine = jnp.logical_and(rows >= lo, rows < hi)
        w_col = jnp.transpose(jnp.broadcast_to(ws_ref[0], (LANES, bm)))[:, 0:1]
        y = jnp.dot(h_bf[...], wd_bf[...], preferred_element_type=F32) * w_col
        y = _rows_to_tiles(jnp.where(mine, y, 0.0))

        @pl.when(lo == 0)
        def _():
            y_ref[...] = y

        @pl.when(lo > 0)
        def _():
            y_ref[...] = y_ref[...] + y


def _moe_call(xs, wsorted, item_blk, item_e, item_lo, item_hi, w_gu, w_down):
    a = xs.shape[0]
    bm = BM_MOE
    n_items = item_blk.shape[0]
    change = jnp.concatenate([jnp.zeros((1,), I32), (item_e[1:] != item_e[:-1]).astype(I32)])
    item_run = jnp.cumsum(change).astype(I32)
    first_later = jnp.sum((item_e[None, :] <= item_e[:, None]).astype(I32), axis=1)
    item_next = jnp.where(first_later < n_items, item_e[jnp.minimum(first_later, n_items - 1)], -1).astype(I32)
    grid_spec = pltpu.PrefetchScalarGridSpec(
        num_scalar_prefetch=6,
        grid=(n_items,),
        in_specs=[pl.BlockSpec((bm, ROW_TILE, LANES), lambda i, blk, *_: (blk[i], 0, 0)),
                  pl.BlockSpec((1, 1, bm), lambda i, blk, *_: (blk[i], 0, 0)),
                  pl.BlockSpec(memory_space=pl.ANY),
                  pl.BlockSpec(memory_space=pl.ANY)],
        out_specs=pl.BlockSpec((bm, ROW_TILE, LANES), lambda i, blk, *_: (blk[i], 0, 0)),
        scratch_shapes=[pltpu.VMEM((bm, D_MODEL), BF16),
                        pltpu.VMEM((bm, EXPERT_HIDDEN), BF16),
                        pltpu.VMEM((2, D_MODEL, 2 * EXPERT_HIDDEN), F32),
                        pltpu.VMEM((2, EXPERT_HIDDEN, D_MODEL), F32),
                        pltpu.VMEM((D_MODEL, 2 * EXPERT_HIDDEN), BF16),
                        pltpu.VMEM((EXPERT_HIDDEN, D_MODEL), BF16),
                        pltpu.SemaphoreType.DMA((2,))],
    )
    return pl.pallas_call(
        _moe_kernel,
        grid_spec=grid_spec,
        out_shape=jax.ShapeDtypeStruct((a, ROW_TILE, LANES), F32),
        compiler_params=_params(dimension_semantics=("arbitrary",)),
        name="moe",
    )(item_blk, item_e, item_next, item_run, item_lo, item_hi, xs, wsorted, w_gu, w_down)


def _sc_mesh():
    return plsc.VectorSubcoreMesh(core_axis_name="core", subcore_axis_name="subcore")


def _sc_gather_rows(rows, idx):
    m = idx.shape[0]
    w = SC_ROW_WINDOW

    @pl.kernel(out_type=jax.ShapeDtypeStruct((m,) + rows.shape[1:], rows.dtype), mesh=_sc_mesh(),
               scratch_types=[])
    def gather(rows_hbm, idx_hbm, out_hbm):
        def body(idx_vmem, out_vmem):
            pltpu.sync_copy(rows_hbm.at[idx_vmem.at[0]], out_vmem)

        pltpu.emit_pipeline(
            body,
            grid=(m // w,),
            in_specs=[pl.BlockSpec((1, w), index_map=lambda i: (i, 0))],
            out_specs=[pl.BlockSpec((w,) + rows.shape[1:], index_map=lambda i: (i, 0, 0))],
            core_axis_name=("core", "subcore"),
            dimension_semantics=(pltpu.PARALLEL,),
        )(idx_hbm, out_hbm)

    return gather(rows, idx.reshape(m // w, w))


def _sc_scatter_rows(rows, idx):
    m = idx.shape[0]
    w = SC_ROW_WINDOW

    @pl.kernel(out_type=jax.ShapeDtypeStruct(rows.shape, rows.dtype), mesh=_sc_mesh(), scratch_types=[])
    def scatter(rows_hbm, idx_hbm, out_hbm):
        def body(rows_vmem, idx_vmem):
            pltpu.sync_copy(rows_vmem, out_hbm.at[idx_vmem.at[0]])

        pltpu.emit_pipeline(
            body,
            grid=(m // w,),
            in_specs=[pl.BlockSpec((w,) + rows.shape[1:], index_map=lambda i: (i, 0, 0)),
                      pl.BlockSpec((1, w), index_map=lambda i: (i, 0))],
            out_specs=[],
            core_axis_name=("core", "subcore"),
            dimension_semantics=(pltpu.PARALLEL,),
        )(rows_hbm, idx_hbm)

    return scatter(rows, idx.reshape(m // w, w))


def _final_kernel(x1_ref, ya_ref, yb_ref, g_ref, b_ref, out_ref):
    moe = ya_ref[...] + yb_ref[...]
    out_ref[...] = _layer_norm(_tiles_to_rows(ALPHA * x1_ref[...] + moe), g_ref[...], b_ref[...])


def _final_call(x1, y2, ln_g, ln_b):
    t = x1.shape[0]
    tm = TM_FINAL
    n_tiles = t // tm
    return pl.pallas_call(
        _final_kernel,
        grid=(n_tiles,),
        in_specs=[pl.BlockSpec((tm, ROW_TILE, LANES), lambda i: (i, 0, 0)),
                  pl.BlockSpec((tm, ROW_TILE, LANES), lambda i: (i, 0, 0)),
                  pl.BlockSpec((tm, ROW_TILE, LANES), lambda i: (n_tiles + i, 0, 0)),
                  pl.BlockSpec((1, D_MODEL), lambda i: (0, 0)),
                  pl.BlockSpec((1, D_MODEL), lambda i: (0, 0))],
        out_specs=pl.BlockSpec((tm, D_MODEL), lambda i: (i, 0)),
        out_shape=jax.ShapeDtypeStruct((t, D_MODEL), F32),
        compiler_params=_params(dimension_semantics=("arbitrary",)),
        name="final",
    )(x1, y2, y2, ln_g, ln_b)


def _t5_bucket(rel):
    nb = N_REL_BUCKETS // 2
    max_exact = nb // 2
    ret = jnp.where(rel > 0, nb, 0)
    n = jnp.abs(rel)
    nf = jnp.maximum(n, 1).astype(F32)
    large = max_exact + (jnp.log(nf / max_exact) / math.log(REL_MAX_DISTANCE / max_exact)
                         * (nb - max_exact)).astype(I32)
    large = jnp.minimum(large, nb - 1)
    return ret + jnp.where(n < max_exact, n, large)


def _bias_tables(rel_table):
    n, m = BLOCK, 3 * BLOCK
    rel = jnp.arange(-(2 * BLOCK - 1), 2 * BLOCK)
    by_rel = jnp.where((jnp.abs(rel) <= WINDOW)[:, None],
                       rel_table[_t5_bucket(rel)].astype(F32) * LOG2_E, NEG_INF)
    by_rel = jnp.pad(by_rel.T, ((0, 0), (0, 1)))
    skew = jnp.tile(by_rel, (1, n))[:, :n * (m + n - 1)].reshape(N_Q_HEADS, n, m + n - 1)
    bias = skew[:, :, n - 1:n - 1 + m]
    heads = [4 * h + half for h in range(N_KV_HEADS) for half in range(2)]
    units = jnp.concatenate([jnp.stack([bias[a] for a in heads]),
                             jnp.stack([bias[a + 2] for a in heads])], axis=1)
    masked = jnp.full((2 * N_KV_HEADS, 2 * BLOCK, BLOCK), NEG_INF, F32)
    prev = jnp.stack([units[:, :, :BLOCK], masked])
    cur = units[:, :, BLOCK:2 * BLOCK]
    nxt = jnp.stack([units[:, :, 2 * BLOCK:], masked])
    return prev, cur, nxt


def _prepare(rel_bias_table, w_in, b_in, w_pool, pool_scale, p_pool, sink, p_attn, w_mem_kv, p_mem,
             w_out, ln1_g, ln1_b, w_router_group, w_router_expert, w_gu, w_down):
    l = 0
    w_router = jnp.zeros((ROUTER_ROWS, D_MODEL), F32)
    w_router = w_router.at[:N_EXPERT_GROUPS].set(w_router_group[l].T)
    w_router = w_router.at[SUBLANES:].set(w_router_expert[l].T)
    bias_prev, bias_cur, bias_next = _bias_tables(rel_bias_table)
    return dict(
        w_in=w_in[l].astype(BF16), b_in=b_in[l][None, :],
        w_pool=w_pool[l].astype(BF16), pool_scale=pool_scale[l][None, :],
        p_pool=p_pool[l].astype(BF16), p_attn=p_attn[l].astype(BF16), p_mem=p_mem[l].astype(BF16),
        w_out=w_out[l].astype(BF16), w_memkv=w_mem_kv[l].astype(BF16),
        ln1_g=ln1_g[l][None, :], ln1_b=ln1_b[l][None, :],
        w_router=w_router.astype(BF16), sink=sink[l] * LOG2_E,
        w_gu=w_gu[l], w_down=w_down[l],
        bias_prev=bias_prev, bias_cur=bias_cur, bias_next=bias_next,
    )


def _dispatch_plan(eid, wts):
    t = eid.shape[1]
    a = TOP_K * t
    bm = BM_MOE
    nblk = a // bm
    eflat = eid.reshape(a)
    wflat = wts.reshape(a)
    order = jnp.argsort(eflat, stable=True).astype(I32)
    wsorted = wflat[order].reshape(nblk, 1, bm)
    experts = jnp.arange(N_EXPERTS, dtype=I32)
    counts = jnp.sum((eflat[None, :] == experts[:, None]).astype(I32), axis=1)
    ends = jnp.cumsum(counts).astype(I32)
    starts = ends - counts
    cuts = jnp.sort(jnp.concatenate([jnp.arange(nblk, dtype=I32) * bm, starts, jnp.array([a], I32)]))
    lo_abs = cuts[:-1]
    hi_abs = cuts[1:]
    nonempty = hi_abs > lo_abs
    blk = jnp.minimum(lo_abs // bm, nblk - 1)
    probe = jnp.minimum(lo_abs, a - 1)
    item_e = jnp.sum((ends[None, :] <= probe[:, None]).astype(I32), axis=1)
    item_lo = jnp.where(nonempty, lo_abs - blk * bm, 0).astype(I32)
    item_hi = jnp.where(nonempty, hi_abs - blk * bm, 0).astype(I32)
    return order, wsorted, blk.astype(I32), item_e.astype(I32), item_lo, item_hi


def _trunk(x, mem, ln_in_g, ln_in_b, ln2_g, ln2_b, p):
    batch, seq, d = x.shape
    t = batch * seq
    assert seq % TM_MIX == 0 and seq % TQ_ATTN == 0 and t % TM_INPROJ == 0 and (TOP_K * t) % BM_MOE == 0
    memkv = _memkv_call(mem.reshape(batch * MEM_TOKENS, d), p["w_memkv"])
    xn, u, q, kk, vv, qm, gate = _inproj_call(x.reshape(t, d), ln_in_g[None, :], ln_in_b[None, :],
                                              p["w_in"], p["b_in"])
    o = _attn_call(q, kk, vv, p["sink"], p["bias_prev"], p["bias_cur"], p["bias_next"], batch, seq)
    x1r, eid, wts = _mix_call(xn, u, o, qm, gate, memkv, p, batch, seq)
    order, wsorted, item_blk, item_e, item_lo, item_hi = _dispatch_plan(eid, wts)
    sorted_tok = jnp.where(order >= t, order - t, order)
    xs = _sc_gather_rows(x1r, sorted_tok)
    ys = _moe_call(xs, wsorted, item_blk, item_e, item_lo, item_hi, p["w_gu"], p["w_down"])
    y2 = _sc_scatter_rows(ys, order)
    out = _final_call(x1r, y2, ln2_g[None, :], ln2_b[None, :])
    return out.reshape(batch, seq, d)


def kernel(x_prompt, x_sample, mem_prompt, mem_sample, ln_in_g, ln_in_b, rel_bias_table, w_in, b_in,
           w_pool, pool_scale, p_pool, sink, p_attn, w_mem_kv, p_mem, w_out, ln1_g, ln1_b,
           w_router_group, w_router_expert, w_gu, w_down, ln2_g, ln2_b):
    p = _prepare(rel_bias_table, w_in, b_in, w_pool, pool_scale, p_pool, sink, p_attn, w_mem_kv, p_mem,
                 w_out, ln1_g, ln1_b, w_router_group, w_router_expert, w_gu, w_down)
    y_prompt = _trunk(x_prompt, mem_prompt, ln_in_g, ln_in_b, ln2_g[0], ln2_b[0], p)
    y_sample = _trunk(x_sample, mem_sample, ln_in_g, ln_in_b, ln2_g[0], ln2_b[0], p)
    return (y_prompt, y_sample)
```

```python
import functools
import math

import jax
import jax.numpy as jnp
from jax import lax
from jax.experimental import pallas as pl
from jax.experimental.pallas import tpu as pltpu
from jax.experimental.pallas import tpu_sc as plsc

F32 = jnp.float32
BF16 = jnp.bfloat16
I32 = jnp.int32

D_MODEL = 1024
DEPTH = 1
POOL_WIDTH = 512
POOL_WINDOWS = (2, 4, 8, 16)
N_POOL_GROUPS = 4
POOL_GROUP_CH = POOL_WIDTH // N_POOL_GROUPS
N_Q_HEADS = 16
N_KV_HEADS = 4
HEAD_DIM = 64
ATTN_WIDTH = N_Q_HEADS * HEAD_DIM
KV_WIDTH = N_KV_HEADS * HEAD_DIM
WINDOW = 128
BLOCK = 128
N_REL_BUCKETS = 32
REL_MAX_DISTANCE = 128
MEM_TOKENS = 256
MEM_HEADS = 4
MEM_HEAD_DIM = 128
MEM_WIDTH = MEM_HEADS * MEM_HEAD_DIM
N_BRANCHES = 3
N_EXPERT_GROUPS = 4
EXPERTS_PER_GROUP = 8
N_EXPERTS = N_EXPERT_GROUPS * EXPERTS_PER_GROUP
TOP_K = 2
EXPERT_HIDDEN = 512
ALPHA = (2 * DEPTH) ** 0.25
LN_EPS = 1e-5
NEG_INF = -1e30
LOG2_E = math.log2(math.e)

LANES = 128
SUBLANES = 8
ROW_TILE = D_MODEL // LANES
VMEM_LIMIT_BYTES = 56 * 1024 * 1024

KV_DUP_WIDTH = N_KV_HEADS * LANES

C_U = 0
C_Q = C_U + POOL_WIDTH
C_K = C_Q + ATTN_WIDTH
C_V = C_K + KV_WIDTH
C_M = C_V + KV_WIDTH
C_G = C_M + MEM_WIDTH
C_END = C_G + N_BRANCHES * D_MODEL

TM_INPROJ = 512
TQ_ATTN = 256
TM_MIX = 512
BM_MOE = 512
SC_ROW_WINDOW = 32
TM_FINAL = 1024
POOL_HALO = 8
ROUTER_ROWS = 40


def _layer_norm(x, g, b):
    mu = jnp.mean(x, axis=-1, keepdims=True)
    xc = x - mu
    var = jnp.mean(xc * xc, axis=-1, keepdims=True)
    return xc * lax.rsqrt(var + LN_EPS) * g + b


def _sigmoid(x):
    return 1.0 / (1.0 + jnp.exp(-x))


def _rows_to_tiles(x):
    return pltpu.einshape("r(cl)->rcl", x, c=ROW_TILE)


def _tiles_to_rows(x):
    return pltpu.einshape("rcl->r(cl)", x)


def _params(**kw):
    return pltpu.CompilerParams(vmem_limit_bytes=VMEM_LIMIT_BYTES, **kw)


def _memkv_kernel(mem_ref, w_ref, out_ref):
    out_ref[...] = jnp.dot(mem_ref[...].astype(BF16), w_ref[...],
                           preferred_element_type=F32).astype(BF16)


def _memkv_call(mem2, w_memkv):
    rows = mem2.shape[0]
    return pl.pallas_call(
        _memkv_kernel,
        grid=(rows // MEM_TOKENS,),
        in_specs=[pl.BlockSpec((MEM_TOKENS, D_MODEL), lambda i: (i, 0)),
                  pl.BlockSpec((D_MODEL, 2 * MEM_WIDTH), lambda i: (0, 0))],
        out_specs=pl.BlockSpec((MEM_TOKENS, 2 * MEM_WIDTH), lambda i: (i, 0)),
        out_shape=jax.ShapeDtypeStruct((rows, 2 * MEM_WIDTH), BF16),
        compiler_params=_params(dimension_semantics=("arbitrary",)),
        name="memkv",
    )(mem2, w_memkv)


def _inproj_kernel(x_ref, g_ref, b_ref, w_ref, bias_ref,
                   xn_ref, u_ref, q_ref, k_ref, v_ref, qm_ref, gate_ref):
    xn = _layer_norm(x_ref[...], g_ref[...], b_ref[...])
    xn_ref[...] = xn
    xb = xn.astype(BF16)

    def seg(lo, hi):
        return jnp.dot(xb, w_ref[:, lo:hi], preferred_element_type=F32) + bias_ref[:, lo:hi]

    u_ref[...] = seg(C_U, C_Q)
    q_ref[...] = (seg(C_Q, C_K) * (HEAD_DIM ** -0.5 * LOG2_E)).astype(BF16)
    kv = seg(C_K, C_M)
    low = lax.broadcasted_iota(I32, (kv.shape[0], LANES), 1) < HEAD_DIM
    for out_ref, c_lo in ((k_ref, 0), (v_ref, KV_WIDTH)):
        for c in range(KV_WIDTH // LANES):
            blk = kv[:, c_lo + c * LANES:c_lo + (c + 1) * LANES]
            rot = pltpu.roll(blk, HEAD_DIM, 1)
            out_ref[:, 2 * c * LANES:(2 * c + 1) * LANES] = jnp.where(low, blk, rot).astype(BF16)
            out_ref[:, (2 * c + 1) * LANES:(2 * c + 2) * LANES] = jnp.where(low, rot, blk).astype(BF16)
    qm_ref[...] = seg(C_M, C_G).astype(BF16)
    for j in range(N_BRANCHES):
        lo = C_G + j * D_MODEL
        gate_ref[:, j * D_MODEL:(j + 1) * D_MODEL] = _sigmoid(seg(lo, lo + D_MODEL)).astype(BF16)


def _inproj_call(x2, ln_g, ln_b, w_in, b_in):
    t = x2.shape[0]
    tm = TM_INPROJ
    row = lambda width: pl.BlockSpec((tm, width), lambda i: (i, 0))
    const = lambda shape: pl.BlockSpec(shape, lambda i: (0, 0))
    widths = (D_MODEL, POOL_WIDTH, ATTN_WIDTH, KV_DUP_WIDTH, KV_DUP_WIDTH, MEM_WIDTH, N_BRANCHES * D_MODEL)
    dtypes = (F32, F32, BF16, BF16, BF16, BF16, BF16)
    return pl.pallas_call(
        _inproj_kernel,
        grid=(t // tm,),
        in_specs=[row(D_MODEL), const((1, D_MODEL)), const((1, D_MODEL)),
                  const((D_MODEL, C_END)), const((1, C_END))],
        out_specs=[row(w) for w in widths],
        out_shape=[jax.ShapeDtypeStruct((t, w), d) for w, d in zip(widths, dtypes)],
        compiler_params=_params(dimension_semantics=("arbitrary",)),
        name="inproj",
    )(x2, ln_g, ln_b, w_in, b_in)


def _attn_kernel(sink_ref, q_ref, kc_ref, kp_ref, kn_ref, vc_ref, vp_ref, vn_ref,
                 bp_ref, bc_ref, bn_ref, o_ref, klo, khi, vlo, vhi, s_scr, p_scr, inv_scr):
    i = pl.program_id(1)
    n_tiles = pl.num_programs(1)
    tq = q_ref.shape[0]
    n_qb = tq // BLOCK

    lane = lax.broadcasted_iota(I32, (BLOCK, KV_DUP_WIDTH), 1)
    low = (lane & (LANES - 1)) < HEAD_DIM

    def put(dst_lo, dst_hi, r0, val):
        zero = jnp.zeros_like(val)
        dst_lo[r0:r0 + BLOCK, :] = jnp.where(low, val, zero)
        dst_hi[r0:r0 + BLOCK, :] = jnp.where(low, zero, val)

    put(klo, khi, 0, kp_ref[...])
    put(vlo, vhi, 0, vp_ref[...])
    for j in range(n_qb):
        put(klo, khi, (j + 1) * BLOCK, kc_ref[j * BLOCK:(j + 1) * BLOCK, :])
        put(vlo, vhi, (j + 1) * BLOCK, vc_ref[j * BLOCK:(j + 1) * BLOCK, :])
    put(klo, khi, (n_qb + 1) * BLOCK, kn_ref[...])
    put(vlo, vhi, (n_qb + 1) * BLOCK, vn_ref[...])

    first = (i == 0).astype(I32)
    last = (i == n_tiles - 1).astype(I32)

    def softmax_rows(s, sink):
        m = jnp.maximum(jnp.max(s, axis=-1, keepdims=True), sink)
        p = jnp.exp2(s - m)
        denom = jnp.sum(p, axis=-1, keepdims=True) + jnp.exp2(sink - m)
        return p.astype(BF16), jnp.broadcast_to(1.0 / denom, (s.shape[0], LANES))

    units = [(j, h, half) for j in range(n_qb) for h in range(N_KV_HEADS) for half in range(2)]
    always = i >= 0

    @pl.when(always)
    def _scores():
        for u, (j, h, half) in enumerate(units):
            r0 = j * BLOCK
            c0 = h * 2 * LANES
            pv = first if j == 0 else 0
            nv = last if j == n_qb - 1 else 0
            unit = h * 2 + half
            q_pairs = jnp.concatenate([q_ref[r0:r0 + BLOCK, c0:c0 + LANES],
                                       q_ref[r0:r0 + BLOCK, c0 + LANES:c0 + 2 * LANES]], axis=0)
            kx = (klo, khi)[half][r0:r0 + 3 * BLOCK, h * LANES:(h + 1) * LANES]
            s = lax.dot_general(q_pairs, kx, (((1,), (1,)), ((), ())), preferred_element_type=F32)
            bias = jnp.concatenate([bp_ref[pv, unit], bc_ref[unit], bn_ref[nv, unit]], axis=1)
            s_scr[u] = s + bias

    @pl.when(always)
    def _softmax():
        for u, (j, h, half) in enumerate(units):
            p_scr[u, :BLOCK], inv_scr[u, :BLOCK] = softmax_rows(s_scr[u, :BLOCK], sink_ref[4 * h + half])
            p_scr[u, BLOCK:], inv_scr[u, BLOCK:] = softmax_rows(s_scr[u, BLOCK:], sink_ref[4 * h + 2 + half])

    @pl.when(always)
    def _values():
        for j in range(n_qb):
            r0 = j * BLOCK
            for h in range(N_KV_HEADS):
                c0 = h * 2 * LANES
                acc = jnp.zeros((2 * BLOCK, LANES), F32)
                for half, vref in enumerate((vlo, vhi)):
                    u = units.index((j, h, half))
                    vx = vref[r0:r0 + 3 * BLOCK, h * LANES:(h + 1) * LANES]
                    acc = acc + jnp.dot(p_scr[u], vx, preferred_element_type=F32) * inv_scr[u]
                o_ref[r0:r0 + BLOCK, c0:c0 + LANES] = acc[:BLOCK].astype(BF16)
                o_ref[r0:r0 + BLOCK, c0 + LANES:c0 + 2 * LANES] = acc[BLOCK:].astype(BF16)


def _attn_call(q, kk, vv, sink, bias_prev, bias_cur, bias_next, batch, seq):
    tq = TQ_ATTN
    n_qb = tq // BLOCK
    nb = seq // BLOCK
    n_tiles = seq // tq
    t = batch * seq

    cur = lambda width: pl.BlockSpec((tq, width), lambda b, i: (b * n_tiles + i, 0))
    prev = pl.BlockSpec((BLOCK, KV_DUP_WIDTH), lambda b, i: (b * nb + jnp.maximum(i * n_qb - 1, 0), 0))
    nxt = pl.BlockSpec((BLOCK, KV_DUP_WIDTH), lambda b, i: (b * nb + jnp.minimum((i + 1) * n_qb, nb - 1), 0))
    full = lambda a: pl.BlockSpec(a.shape, lambda b, i: (0,) * a.ndim)
    ext = (n_qb + 2) * BLOCK
    n_units = n_qb * N_KV_HEADS * 2
    return pl.pallas_call(
        _attn_kernel,
        grid=(batch, n_tiles),
        in_specs=[pl.BlockSpec(memory_space=pltpu.SMEM),
                  cur(ATTN_WIDTH), cur(KV_DUP_WIDTH), prev, nxt, cur(KV_DUP_WIDTH), prev, nxt,
                  full(bias_prev), full(bias_cur), full(bias_next)],
        out_specs=cur(ATTN_WIDTH),
        out_shape=jax.ShapeDtypeStruct((t, ATTN_WIDTH), BF16),
        scratch_shapes=[pltpu.VMEM((ext, KV_DUP_WIDTH), BF16) for _ in range(4)]
                       + [pltpu.VMEM((n_units, 2 * BLOCK, 3 * BLOCK), F32),
                          pltpu.VMEM((n_units, 2 * BLOCK, 3 * BLOCK), BF16),
                          pltpu.VMEM((n_units, 2 * BLOCK, LANES), F32)],
        compiler_params=_params(dimension_semantics=("arbitrary", "arbitrary")),
        name="attn",
    )(sink, q, kk, kk, kk, vv, vv, vv, bias_prev, bias_cur, bias_next)


def _mix_kernel(xn_ref, u_ref, up_ref, un_ref, o_ref, qm_ref, gate_ref, memkv_ref,
                wpool_ref, pscale_ref, ppool_ref, pattn_ref, pmem_ref, wout_ref,
                g1_ref, b1_ref, wr_ref,
                x1r_ref, eid_ref, wts_ref, uext, *, seq):
    i = pl.program_id(1)
    n_tiles = pl.num_programs(1)
    tm = xn_ref.shape[0]

    halo = POOL_HALO
    uext[0:halo, :] = jnp.where(i > 0, up_ref[...], 0.0)
    uext[halo:halo + tm, :] = u_ref[...]
    uext[halo + tm:halo + tm + halo, :] = jnp.where(i < n_tiles - 1, un_ref[...], 0.0)
    pos = i * tm + lax.broadcasted_iota(I32, (tm, 1), 0)
    mixed = []
    for gi, win in enumerate(POOL_WINDOWS):
        c0 = gi * POOL_GROUP_CH
        half = win // 2
        total = jnp.zeros((tm, POOL_GROUP_CH), F32)
        for off in range(-half, half):
            total = total + uext[halo + off:halo + off + tm, c0:c0 + POOL_GROUP_CH]
        cnt = (jnp.minimum(pos + half, seq) - jnp.maximum(pos - half, 0)).astype(F32)
        pooled = total * (1.0 / cnt) - u_ref[:, c0:c0 + POOL_GROUP_CH]
        mixed.append(jnp.dot(pooled.astype(BF16), wpool_ref[gi], preferred_element_type=F32)
                     * pscale_ref[:, c0:c0 + POOL_GROUP_CH])
    mixed = jnp.concatenate(mixed, axis=1).astype(BF16)
    merged = gate_ref[:, 0:D_MODEL].astype(F32) * jnp.dot(
        mixed, ppool_ref[...], preferred_element_type=F32)

    merged = merged + gate_ref[:, D_MODEL:2 * D_MODEL].astype(F32) * jnp.dot(
        o_ref[...], pattn_ref[...], preferred_element_type=F32)

    heads = []
    for h in range(MEM_HEADS):
        c0 = h * MEM_HEAD_DIM
        km = memkv_ref[:, c0:c0 + MEM_HEAD_DIM]
        vm = memkv_ref[:, MEM_WIDTH + c0:MEM_WIDTH + c0 + MEM_HEAD_DIM]
        s = lax.dot_general(qm_ref[:, c0:c0 + MEM_HEAD_DIM], km, (((1,), (1,)), ((), ())),
                            preferred_element_type=F32) * (MEM_HEAD_DIM ** -0.5)
        p = jnp.exp(s - jnp.max(s, axis=-1, keepdims=True))
        p = (p / jnp.sum(p, axis=-1, keepdims=True)).astype(BF16)
        heads.append(jnp.dot(p, vm, preferred_element_type=F32).astype(BF16))
    om = jnp.concatenate(heads, axis=1)
    merged = merged + gate_ref[:, 2 * D_MODEL:3 * D_MODEL].astype(F32) * jnp.dot(
        om, pmem_ref[...], preferred_element_type=F32)

    y = jnp.dot(merged.astype(BF16), wout_ref[...], preferred_element_type=F32)
    x1 = _layer_norm(ALPHA * xn_ref[...] + y, g1_ref[...], b1_ref[...])
    x1r_ref[...] = _rows_to_tiles(x1)

    lt = lax.dot_general(wr_ref[...], x1.astype(BF16), (((1,), (1,)), ((), ())),
                         preferred_element_type=F32)
    gl = [lt[r:r + 1, :] for r in range(N_EXPERT_GROUPS)]
    gmax = gl[0]
    grp = jnp.zeros((1, tm), I32)
    for r in range(1, N_EXPERT_GROUPS):
        better = gl[r] > gmax
        grp = jnp.where(better, r, grp)
        gmax = jnp.where(better, gl[r], gmax)
    gsum = gl[0] * 0.0
    for r in range(N_EXPERT_GROUPS):
        gsum = gsum + jnp.exp(gl[r] - gmax)
    gp = 1.0 / gsum
    sel = jnp.zeros((EXPERTS_PER_GROUP, tm), F32)
    for r in range(N_EXPERT_GROUPS):
        rows = lt[SUBLANES + r * EXPERTS_PER_GROUP:SUBLANES + (r + 1) * EXPERTS_PER_GROUP, :]
        sel = jnp.where(grp == r, rows, sel)
    ridx = lax.broadcasted_iota(I32, (EXPERTS_PER_GROUP, tm), 0)
    top1 = jnp.max(sel, axis=0, keepdims=True)
    i1 = jnp.min(jnp.where(sel == top1, ridx, EXPERTS_PER_GROUP), axis=0, keepdims=True)
    rest = jnp.where(ridx == i1, -jnp.inf, sel)
    top2 = jnp.max(rest, axis=0, keepdims=True)
    i2 = jnp.min(jnp.where(rest == top2, ridx, EXPERTS_PER_GROUP), axis=0, keepdims=True)
    e2 = jnp.exp(top2 - top1)
    inv = gp / (1.0 + e2)
    eid_ref[0:1, :] = grp * EXPERTS_PER_GROUP + i1
    eid_ref[1:2, :] = grp * EXPERTS_PER_GROUP + i2
    wts_ref[0:1, :] = inv
    wts_ref[1:2, :] = e2 * inv


def _mix_call(xn, u, o, qm, gate, memkv, p, batch, seq):
    tm = TM_MIX
    n_tiles = seq // tm
    t = batch * seq
    hb = tm // POOL_HALO

    row = lambda width: pl.BlockSpec((tm, width), lambda b, i: (b * n_tiles + i, 0))
    prev = pl.BlockSpec((POOL_HALO, POOL_WIDTH),
                        lambda b, i: (jnp.maximum((b * n_tiles + i) * hb - 1, 0), 0))
    nxt = pl.BlockSpec((POOL_HALO, POOL_WIDTH),
                       lambda b, i: (jnp.minimum((b * n_tiles + i + 1) * hb, t // POOL_HALO - 1), 0))
    full = lambda a: pl.BlockSpec(a.shape, lambda b, i: (0,) * a.ndim)
    lane_row = pl.BlockSpec((TOP_K, tm), lambda b, i: (0, b * n_tiles + i))
    weights = (p["w_pool"], p["pool_scale"], p["p_pool"], p["p_attn"], p["p_mem"], p["w_out"],
               p["ln1_g"], p["ln1_b"], p["w_router"])
    return pl.pallas_call(
        functools.partial(_mix_kernel, seq=seq),
        grid=(batch, n_tiles),
        in_specs=[row(D_MODEL), row(POOL_WIDTH), prev, nxt, row(ATTN_WIDTH), row(MEM_WIDTH),
                  row(N_BRANCHES * D_MODEL),
                  pl.BlockSpec((MEM_TOKENS, 2 * MEM_WIDTH), lambda b, i: (b, 0))]
                 + [full(w) for w in weights],
        out_specs=[pl.BlockSpec((tm, ROW_TILE, LANES), lambda b, i: (b * n_tiles + i, 0, 0)),
                   lane_row, lane_row],
        out_shape=[jax.ShapeDtypeStruct((t, ROW_TILE, LANES), F32),
                   jax.ShapeDtypeStruct((TOP_K, t), I32),
                   jax.ShapeDtypeStruct((TOP_K, t), F32)],
        scratch_shapes=[pltpu.VMEM((tm + 2 * POOL_HALO, POOL_WIDTH), F32)],
        compiler_params=_params(dimension_semantics=("arbitrary", "arbitrary")),
        name="mix",
    )(xn, u, u, u, o, qm, gate, memkv, *weights)


def _moe_kernel(iblk_ref, ie_ref, inext_ref, irun_ref, ilo_ref, ihi_ref,
                x_ref, ws_ref, wgu_hbm, wd_hbm, y_ref,
                x_bf, h_bf, wgu_f32, wd_f32, wgu_bf, wd_bf, wsem):
    i = pl.program_id(0)
    bm = x_ref.shape[0]
    lo = ilo_ref[i]
    hi = ihi_ref[i]
    nonempty = hi > lo

    def weight_copies(e, ws):
        return (pltpu.make_async_copy(wgu_hbm.at[e], wgu_f32.at[ws], wsem.at[ws]),
                pltpu.make_async_copy(wd_hbm.at[e], wd_f32.at[ws], wsem.at[ws]))

    @pl.when(jnp.logical_or(i == 0, ie_ref[i] != ie_ref[jnp.maximum(i - 1, 0)]))
    def _():
        e = ie_ref[i]
        wslot = irun_ref[i] & 1

        @pl.when(i == 0)
        def _():
            for c in weight_copies(e, wslot):
                c.start()
        for c in weight_copies(e, wslot):
            c.wait()

        @pl.when(inext_ref[i] >= 0)
        def _():
            for c in weight_copies(inext_ref[i], 1 - wslot):
                c.start()
        wgu_bf[...] = wgu_f32[wslot].astype(BF16)
        wd_bf[...] = wd_f32[wslot].astype(BF16)

    @pl.when(nonempty)
    def _():
        x_bf[...] = _tiles_to_rows(x_ref[...]).astype(BF16)

    @pl.when(nonempty)
    def _():
        gu = jnp.dot(x_bf[...], wgu_bf[...], preferred_element_type=F32)
        gate = gu[:, :EXPERT_HIDDEN]
        h_bf[...] = ((gate * _sigmoid(gate)) * gu[:, EXPERT_HIDDEN:]).astype(BF16)

    @pl.when(nonempty)
    def _():
        rows = lax.broadcasted_iota(I32, (bm, 1), 0)
        mine = jnp.logical_and(rows >= lo, rows < hi)
        w_col = jnp.transpose(jnp.broadcast_to(ws_ref[0], (LANES, bm)))[:, 0:1]
        y = jnp.dot(h_bf[...], wd_bf[...], preferred_element_type=F32) * w_col
        y = _rows_to_tiles(jnp.where(mine, y, 0.0))

        @pl.when(lo == 0)
        def _():
            y_ref[...] = y

        @pl.when(lo > 0)
        def _():
            y_ref[...] = y_ref[...] + y


def _moe_call(xs, wsorted, item_blk, item_e, item_lo, item_hi, w_gu, w_down):
    a = xs.shape[0]
    bm = BM_MOE
    n_items = item_blk.shape[0]
    change = jnp.concatenate([jnp.zeros((1,), I32), (item_e[1:] != item_e[:-1]).astype(I32)])
    item_run = jnp.cumsum(change).astype(I32)
    first_later = jnp.sum((item_e[None, :] <= item_e[:, None]).astype(I32), axis=1)
    item_next = jnp.where(first_later < n_items, item_e[jnp.minimum(first_later, n_items - 1)], -1).astype(I32)
    grid_spec = pltpu.PrefetchScalarGridSpec(
        num_scalar_prefetch=6,
        grid=(n_items,),
        in_specs=[pl.BlockSpec((bm, ROW_TILE, LANES), lambda i, blk, *_: (blk[i], 0, 0)),
                  pl.BlockSpec((1, 1, bm), lambda i, blk, *_: (blk[i], 0, 0)),
                  pl.BlockSpec(memory_space=pl.ANY),
                  pl.BlockSpec(memory_space=pl.ANY)],
        out_specs=pl.BlockSpec((bm, ROW_TILE, LANES), lambda i, blk, *_: (blk[i], 0, 0)),
        scratch_shapes=[pltpu.VMEM((bm, D_MODEL), BF16),
                        pltpu.VMEM((bm, EXPERT_HIDDEN), BF16),
                        pltpu.VMEM((2, D_MODEL, 2 * EXPERT_HIDDEN), F32),
                        pltpu.VMEM((2, EXPERT_HIDDEN, D_MODEL), F32),
                        pltpu.VMEM((D_MODEL, 2 * EXPERT_HIDDEN), BF16),
                        pltpu.VMEM((EXPERT_HIDDEN, D_MODEL), BF16),
                        pltpu.SemaphoreType.DMA((2,))],
    )
    return pl.pallas_call(
        _moe_kernel,
        grid_spec=grid_spec,
        out_shape=jax.ShapeDtypeStruct((a, ROW_TILE, LANES), F32),
        compiler_params=_params(dimension_semantics=("arbitrary",)),
        name="moe",
    )(item_blk, item_e, item_next, item_run, item_lo, item_hi, xs, wsorted, w_gu, w_down)


def _sc_mesh():
    return plsc.VectorSubcoreMesh(core_axis_name="core", subcore_axis_name="subcore")


def _sc_gather_rows(rows, idx):
    m = idx.shape[0]
    w = SC_ROW_WINDOW

    @pl.kernel(out_type=jax.ShapeDtypeStruct((m,) + rows.shape[1:], rows.dtype), mesh=_sc_mesh(),
               scratch_types=[])
    def gather(rows_hbm, idx_hbm, out_hbm):
        def body(idx_vmem, out_vmem):
            pltpu.sync_copy(rows_hbm.at[idx_vmem.at[0]], out_vmem)

        pltpu.emit_pipeline(
            body,
            grid=(m // w,),
            in_specs=[pl.BlockSpec((1, w), index_map=lambda i: (i, 0))],
            out_specs=[pl.BlockSpec((w,) + rows.shape[1:], index_map=lambda i: (i, 0, 0))],
            core_axis_name=("core", "subcore"),
            dimension_semantics=(pltpu.PARALLEL,),
        )(idx_hbm, out_hbm)

    return gather(rows, idx.reshape(m // w, w))


def _sc_scatter_rows(rows, idx):
    m = idx.shape[0]
    w = SC_ROW_WINDOW

    @pl.kernel(out_type=jax.ShapeDtypeStruct(rows.shape, rows.dtype), mesh=_sc_mesh(), scratch_types=[])
    def scatter(rows_hbm, idx_hbm, out_hbm):
        def body(rows_vmem, idx_vmem):
            pltpu.sync_copy(rows_vmem, out_hbm.at[idx_vmem.at[0]])

        pltpu.emit_pipeline(
            body,
            grid=(m // w,),
            in_specs=[pl.BlockSpec((w,) + rows.shape[1:], index_map=lambda i: (i, 0, 0)),
                      pl.BlockSpec((1, w), index_map=lambda i: (i, 0))],
            out_specs=[],
            core_axis_name=("core", "subcore"),
            dimension_semantics=(pltpu.PARALLEL,),
        )(rows_hbm, idx_hbm)

    return scatter(rows, idx.reshape(m // w, w))


def _final_kernel(x1_ref, ya_ref, yb_ref, g_ref, b_ref, out_ref):
    moe = ya_ref[...] + yb_ref[...]
    out_ref[...] = _layer_norm(_tiles_to_rows(ALPHA * x1_ref[...] + moe), g_ref[...], b_ref[...])


def _final_call(x1, y2, ln_g, ln_b):
    t = x1.shape[0]
    tm = TM_FINAL
    n_tiles = t // tm
    return pl.pallas_call(
        _final_kernel,
        grid=(n_tiles,),
        in_specs=[pl.BlockSpec((tm, ROW_TILE, LANES), lambda i: (i, 0, 0)),
                  pl.BlockSpec((tm, ROW_TILE, LANES), lambda i: (i, 0, 0)),
                  pl.BlockSpec((tm, ROW_TILE, LANES), lambda i: (n_tiles + i, 0, 0)),
                  pl.BlockSpec((1, D_MODEL), lambda i: (0, 0)),
                  pl.BlockSpec((1, D_MODEL), lambda i: (0, 0))],
        out_specs=pl.BlockSpec((tm, D_MODEL), lambda i: (i, 0)),
        out_shape=jax.ShapeDtypeStruct((t, D_MODEL), F32),
        compiler_params=_params(dimension_semantics=("arbitrary",)),
        name="final",
    )(x1, y2, y2, ln_g, ln_b)


def _t5_bucket(rel):
    nb = N_REL_BUCKETS // 2
    max_exact = nb // 2
    ret = jnp.where(rel > 0, nb, 0)
    n = jnp.abs(rel)
    nf = jnp.maximum(n, 1).astype(F32)
    large = max_exact + (jnp.log(nf / max_exact) / math.log(REL_MAX_DISTANCE / max_exact)
                         * (nb - max_exact)).astype(I32)
    large = jnp.minimum(large, nb - 1)
    return ret + jnp.where(n < max_exact, n, large)


def _bias_tables(rel_table):
    n, m = BLOCK, 3 * BLOCK
    rel = jnp.arange(-(2 * BLOCK - 1), 2 * BLOCK)
    by_rel = jnp.where((jnp.abs(rel) <= WINDOW)[:, None],
                       rel_table[_t5_bucket(rel)].astype(F32) * LOG2_E, NEG_INF)
    by_rel = jnp.pad(by_rel.T, ((0, 0), (0, 1)))
    skew = jnp.tile(by_rel, (1, n))[:, :n * (m + n - 1)].reshape(N_Q_HEADS, n, m + n - 1)
    bias = skew[:, :, n - 1:n - 1 + m]
    heads = [4 * h + half for h in range(N_KV_HEADS) for half in range(2)]
    units = jnp.concatenate([jnp.stack([bias[a] for a in heads]),
                             jnp.stack([bias[a + 2] for a in heads])], axis=1)
    masked = jnp.full((2 * N_KV_HEADS, 2 * BLOCK, BLOCK), NEG_INF, F32)
    prev = jnp.stack([units[:, :, :BLOCK], masked])
    cur = units[:, :, BLOCK:2 * BLOCK]
    nxt = jnp.stack([units[:, :, 2 * BLOCK:], masked])
    return prev, cur, nxt


def _prepare(rel_bias_table, w_in, b_in, w_pool, pool_scale, p_pool, sink, p_attn, w_mem_kv, p_mem,
             w_out, ln1_g, ln1_b, w_router_group, w_router_expert, w_gu, w_down):
    l = 0
    w_router = jnp.zeros((ROUTER_ROWS, D_MODEL), F32)
    w_router = w_router.at[:N_EXPERT_GROUPS].set(w_router_group[l].T)
    w_router = w_router.at[SUBLANES:].set(w_router_expert[l].T)
    bias_prev, bias_cur, bias_next = _bias_tables(rel_bias_table)
    return dict(
        w_in=w_in[l].astype(BF16), b_in=b_in[l][None, :],
        w_pool=w_pool[l].astype(BF16), pool_scale=pool_scale[l][None, :],
        p_pool=p_pool[l].astype(BF16), p_attn=p_attn[l].astype(BF16), p_mem=p_mem[l].astype(BF16),
        w_out=w_out[l].astype(BF16), w_memkv=w_mem_kv[l].astype(BF16),
        ln1_g=ln1_g[l][None, :], ln1_b=ln1_b[l][None, :],
        w_router=w_router.astype(BF16), sink=sink[l] * LOG2_E,
        w_gu=w_gu[l], w_down=w_down[l],
        bias_prev=bias_prev, bias_cur=bias_cur, bias_next=bias_next,
    )


def _dispatch_plan(eid, wts):
    t = eid.shape[1]
    a = TOP_K * t
    bm = BM_MOE
    nblk = a // bm
    eflat = eid.reshape(a)
    wflat = wts.reshape(a)
    shift = (a - 1).bit_length()
    assert N_EXPERTS << shift < 2 ** 31
    packed = jnp.sort(eflat * (1 << shift) + jnp.arange(a, dtype=I32))
    order = packed & ((1 << shift) - 1)
    wsorted = wflat[order].reshape(nblk, 1, bm)
    experts = jnp.arange(N_EXPERTS, dtype=I32)
    counts = jnp.sum((eflat[None, :] == experts[:, None]).astype(I32), axis=1)
    ends = jnp.cumsum(counts).astype(I32)
    starts = ends - counts
    cuts = jnp.sort(jnp.concatenate([jnp.arange(nblk, dtype=I32) * bm, starts, jnp.array([a], I32)]))
    lo_abs = cuts[:-1]
    hi_abs = cuts[1:]
    nonempty = hi_abs > lo_abs
    blk = jnp.minimum(lo_abs // bm, nblk - 1)
    probe = jnp.minimum(lo_abs, a - 1)
    item_e = jnp.sum((ends[None, :] <= probe[:, None]).astype(I32), axis=1)
    item_lo = jnp.where(nonempty, lo_abs - blk * bm, 0).astype(I32)
    item_hi = jnp.where(nonempty, hi_abs - blk * bm, 0).astype(I32)
    return order, wsorted, blk.astype(I32), item_e.astype(I32), item_lo, item_hi


def _trunk(x, mem, ln_in_g, ln_in_b, ln2_g, ln2_b, p):
    batch, seq, d = x.shape
    t = batch * seq
    assert seq % TM_MIX == 0 and seq % TQ_ATTN == 0 and t % TM_INPROJ == 0 and (TOP_K * t) % BM_MOE == 0
    memkv = _memkv_call(mem.reshape(batch * MEM_TOKENS, d), p["w_memkv"])
    xn, u, q, kk, vv, qm, gate = _inproj_call(x.reshape(t, d), ln_in_g[None, :], ln_in_b[None, :],
                                              p["w_in"], p["b_in"])
    o = _attn_call(q, kk, vv, p["sink"], p["bias_prev"], p["bias_cur"], p["bias_next"], batch, seq)
    x1r, eid, wts = _mix_call(xn, u, o, qm, gate, memkv, p, batch, seq)
    order, wsorted, item_blk, item_e, item_lo, item_hi = _dispatch_plan(eid, wts)
    sorted_tok = jnp.where(order >= t, order - t, order)
    xs = _sc_gather_rows(x1r, sorted_tok)
    ys = _moe_call(xs, wsorted, item_blk, item_e, item_lo, item_hi, p["w_gu"], p["w_down"])
    y2 = _sc_scatter_rows(ys, order)
    out = _final_call(x1r, y2, ln2_g[None, :], ln2_b[None, :])
    return out.reshape(batch, seq, d)


def kernel(x_prompt, x_sample, mem_prompt, mem_sample, ln_in_g, ln_in_b, rel_bias_table, w_in, b_in,
           w_pool, pool_scale, p_pool, sink, p_attn, w_mem_kv, p_mem, w_out, ln1_g, ln1_b,
           w_router_group, w_router_expert, w_gu, w_down, ln2_g, ln2_b):
    p = _prepare(rel_bias_table, w_in, b_in, w_pool, pool_scale, p_pool, sink, p_attn, w_mem_kv, p_mem,
                 w_out, ln1_g, ln1_b, w_router_group, w_router_expert, w_gu, w_down)
    y_prompt = _trunk(x_prompt, mem_prompt, ln_in_g, ln_in_b, ln2_g[0], ln2_b[0], p)
    y_sample = _trunk(x_sample, mem_sample, ln_in_g, ln_in_b, ln2_g[0], ln2_b[0], p)
    return (y_prompt, y_sample)
```

```python
import functools
import math

import jax
import jax.numpy as jnp
from jax import lax
from jax.experimental import pallas as pl
from jax.experimental.pallas import tpu as pltpu
from jax.experimental.pallas import tpu_sc as plsc

F32 = jnp.float32
BF16 = jnp.bfloat16
I32 = jnp.int32

D_MODEL = 1024
DEPTH = 1
POOL_WIDTH = 512
POOL_WINDOWS = (2, 4, 8, 16)
N_POOL_GROUPS = 4
POOL_GROUP_CH = POOL_WIDTH // N_POOL_GROUPS
N_Q_HEADS = 16
N_KV_HEADS = 4
HEAD_DIM = 64
ATTN_WIDTH = N_Q_HEADS * HEAD_DIM
KV_WIDTH = N_KV_HEADS * HEAD_DIM
WINDOW = 128
BLOCK = 128
N_REL_BUCKETS = 32
REL_MAX_DISTANCE = 128
MEM_TOKENS = 256
MEM_HEADS = 4
MEM_HEAD_DIM = 128
MEM_WIDTH = MEM_HEADS * MEM_HEAD_DIM
N_BRANCHES = 3
N_EXPERT_GROUPS = 4
EXPERTS_PER_GROUP = 8
N_EXPERTS = N_EXPERT_GROUPS * EXPERTS_PER_GROUP
TOP_K = 2
EXPERT_HIDDEN = 512
ALPHA = (2 * DEPTH) ** 0.25
LN_EPS = 1e-5
NEG_INF = -1e30
LOG2_E = math.log2(math.e)

LANES = 128
SUBLANES = 8
ROW_TILE = D_MODEL // LANES
VMEM_LIMIT_BYTES = 56 * 1024 * 1024

KV_DUP_WIDTH = N_KV_HEADS * LANES

C_U = 0
C_Q = C_U + POOL_WIDTH
C_K = C_Q + ATTN_WIDTH
C_V = C_K + KV_WIDTH
C_M = C_V + KV_WIDTH
C_G = C_M + MEM_WIDTH
C_END = C_G + N_BRANCHES * D_MODEL

TM_INPROJ = 512
TQ_ATTN = 512
TM_MIX = 256
BM_MOE = 512
SC_ROW_WINDOW = 32
TM_FINAL = 1024
POOL_HALO = 8
ROUTER_ROWS = 40


def _layer_norm(x, g, b):
    mu = jnp.mean(x, axis=-1, keepdims=True)
    xc = x - mu
    var = jnp.mean(xc * xc, axis=-1, keepdims=True)
    return xc * lax.rsqrt(var + LN_EPS) * g + b


def _sigmoid(x):
    return 1.0 / (1.0 + jnp.exp(-x))


def _rows_to_tiles(x):
    return pltpu.einshape("r(cl)->rcl", x, c=ROW_TILE)


def _tiles_to_rows(x):
    return pltpu.einshape("rcl->r(cl)", x)


def _params(**kw):
    return pltpu.CompilerParams(vmem_limit_bytes=VMEM_LIMIT_BYTES, **kw)


def _memkv_kernel(mem_ref, w_ref, out_ref):
    out_ref[...] = jnp.dot(mem_ref[...].astype(BF16), w_ref[...],
                           preferred_element_type=F32).astype(BF16)


def _memkv_call(mem2, w_memkv):
    rows = mem2.shape[0]
    return pl.pallas_call(
        _memkv_kernel,
        grid=(rows // MEM_TOKENS,),
        in_specs=[pl.BlockSpec((MEM_TOKENS, D_MODEL), lambda i: (i, 0)),
                  pl.BlockSpec((D_MODEL, 2 * MEM_WIDTH), lambda i: (0, 0))],
        out_specs=pl.BlockSpec((MEM_TOKENS, 2 * MEM_WIDTH), lambda i: (i, 0)),
        out_shape=jax.ShapeDtypeStruct((rows, 2 * MEM_WIDTH), BF16),
        compiler_params=_params(dimension_semantics=("arbitrary",)),
        name="memkv",
    )(mem2, w_memkv)


def _inproj_kernel(x_ref, g_ref, b_ref, w_ref, bias_ref,
                   xn_ref, u_ref, q_ref, k_ref, v_ref, qm_ref, gate_ref):
    xn = _layer_norm(x_ref[...], g_ref[...], b_ref[...])
    xn_ref[...] = xn
    xb = xn.astype(BF16)

    def seg(lo, hi):
        return jnp.dot(xb, w_ref[:, lo:hi], preferred_element_type=F32) + bias_ref[:, lo:hi]

    u_ref[...] = seg(C_U, C_Q)
    q_ref[...] = (seg(C_Q, C_K) * (HEAD_DIM ** -0.5 * LOG2_E)).astype(BF16)
    kv = seg(C_K, C_M)
    low = lax.broadcasted_iota(I32, (kv.shape[0], LANES), 1) < HEAD_DIM
    for out_ref, c_lo in ((k_ref, 0), (v_ref, KV_WIDTH)):
        for c in range(KV_WIDTH // LANES):
            blk = kv[:, c_lo + c * LANES:c_lo + (c + 1) * LANES]
            rot = pltpu.roll(blk, HEAD_DIM, 1)
            out_ref[:, 2 * c * LANES:(2 * c + 1) * LANES] = jnp.where(low, blk, rot).astype(BF16)
            out_ref[:, (2 * c + 1) * LANES:(2 * c + 2) * LANES] = jnp.where(low, rot, blk).astype(BF16)
    qm_ref[...] = seg(C_M, C_G).astype(BF16)
    for j in range(N_BRANCHES):
        lo = C_G + j * D_MODEL
        gate_ref[:, j * D_MODEL:(j + 1) * D_MODEL] = _sigmoid(seg(lo, lo + D_MODEL)).astype(BF16)


def _inproj_call(x2, ln_g, ln_b, w_in, b_in):
    t = x2.shape[0]
    tm = TM_INPROJ
    row = lambda width: pl.BlockSpec((tm, width), lambda i: (i, 0))
    const = lambda shape: pl.BlockSpec(shape, lambda i: (0, 0))
    widths = (D_MODEL, POOL_WIDTH, ATTN_WIDTH, KV_DUP_WIDTH, KV_DUP_WIDTH, MEM_WIDTH, N_BRANCHES * D_MODEL)
    dtypes = (F32, F32, BF16, BF16, BF16, BF16, BF16)
    return pl.pallas_call(
        _inproj_kernel,
        grid=(t // tm,),
        in_specs=[row(D_MODEL), const((1, D_MODEL)), const((1, D_MODEL)),
                  const((D_MODEL, C_END)), const((1, C_END))],
        out_specs=[row(w) for w in widths],
        out_shape=[jax.ShapeDtypeStruct((t, w), d) for w, d in zip(widths, dtypes)],
        compiler_params=_params(dimension_semantics=("arbitrary",)),
        name="inproj",
    )(x2, ln_g, ln_b, w_in, b_in)


def _attn_kernel(sink_ref, q_ref, kc_ref, kp_ref, kn_ref, vc_ref, vp_ref, vn_ref,
                 bp_ref, bc_ref, bn_ref, o_ref, klo, khi, vlo, vhi, s_scr, p_scr, inv_scr):
    i = pl.program_id(1)
    n_tiles = pl.num_programs(1)
    tq = q_ref.shape[0]
    n_qb = tq // BLOCK

    lane = lax.broadcasted_iota(I32, (BLOCK, KV_DUP_WIDTH), 1)
    low = (lane & (LANES - 1)) < HEAD_DIM

    def put(dst_lo, dst_hi, r0, val):
        zero = jnp.zeros_like(val)
        dst_lo[r0:r0 + BLOCK, :] = jnp.where(low, val, zero)
        dst_hi[r0:r0 + BLOCK, :] = jnp.where(low, zero, val)

    put(klo, khi, 0, kp_ref[...])
    put(vlo, vhi, 0, vp_ref[...])
    for j in range(n_qb):
        put(klo, khi, (j + 1) * BLOCK, kc_ref[j * BLOCK:(j + 1) * BLOCK, :])
        put(vlo, vhi, (j + 1) * BLOCK, vc_ref[j * BLOCK:(j + 1) * BLOCK, :])
    put(klo, khi, (n_qb + 1) * BLOCK, kn_ref[...])
    put(vlo, vhi, (n_qb + 1) * BLOCK, vn_ref[...])

    first = (i == 0).astype(I32)
    last = (i == n_tiles - 1).astype(I32)

    def softmax_rows(s, sink):
        m = jnp.maximum(jnp.max(s, axis=-1, keepdims=True), sink)
        p = jnp.exp2(s - m)
        denom = jnp.sum(p, axis=-1, keepdims=True) + jnp.exp2(sink - m)
        return p.astype(BF16), jnp.broadcast_to(1.0 / denom, (s.shape[0], LANES))

    units = [(j, h, half) for j in range(n_qb) for h in range(N_KV_HEADS) for half in range(2)]
    always = i >= 0

    @pl.when(always)
    def _scores():
        for u, (j, h, half) in enumerate(units):
            r0 = j * BLOCK
            c0 = h * 2 * LANES
            pv = first if j == 0 else 0
            nv = last if j == n_qb - 1 else 0
            unit = h * 2 + half
            q_pairs = jnp.concatenate([q_ref[r0:r0 + BLOCK, c0:c0 + LANES],
                                       q_ref[r0:r0 + BLOCK, c0 + LANES:c0 + 2 * LANES]], axis=0)
            kx = (klo, khi)[half][r0:r0 + 3 * BLOCK, h * LANES:(h + 1) * LANES]
            s = lax.dot_general(q_pairs, kx, (((1,), (1,)), ((), ())), preferred_element_type=F32)
            bias = jnp.concatenate([bp_ref[pv, unit], bc_ref[unit], bn_ref[nv, unit]], axis=1)
            s_scr[u] = s + bias

    @pl.when(always)
    def _softmax():
        for u, (j, h, half) in enumerate(units):
            p_scr[u, :BLOCK], inv_scr[u, :BLOCK] = softmax_rows(s_scr[u, :BLOCK], sink_ref[4 * h + half])
            p_scr[u, BLOCK:], inv_scr[u, BLOCK:] = softmax_rows(s_scr[u, BLOCK:], sink_ref[4 * h + 2 + half])

    @pl.when(always)
    def _values():
        for j in range(n_qb):
            r0 = j * BLOCK
            for h in range(N_KV_HEADS):
                c0 = h * 2 * LANES
                acc = jnp.zeros((2 * BLOCK, LANES), F32)
                for half, vref in enumerate((vlo, vhi)):
                    u = units.index((j, h, half))
                    vx = vref[r0:r0 + 3 * BLOCK, h * LANES:(h + 1) * LANES]
                    acc = acc + jnp.dot(p_scr[u], vx, preferred_element_type=F32) * inv_scr[u]
                o_ref[r0:r0 + BLOCK, c0:c0 + LANES] = acc[:BLOCK].astype(BF16)
                o_ref[r0:r0 + BLOCK, c0 + LANES:c0 + 2 * LANES] = acc[BLOCK:].astype(BF16)


def _attn_call(q, kk, vv, sink, bias_prev, bias_cur, bias_next, batch, seq):
    tq = TQ_ATTN
    n_qb = tq // BLOCK
    nb = seq // BLOCK
    n_tiles = seq // tq
    t = batch * seq

    cur = lambda width: pl.BlockSpec((tq, width), lambda b, i: (b * n_tiles + i, 0))
    prev = pl.BlockSpec((BLOCK, KV_DUP_WIDTH), lambda b, i: (b * nb + jnp.maximum(i * n_qb - 1, 0), 0))
    nxt = pl.BlockSpec((BLOCK, KV_DUP_WIDTH), lambda b, i: (b * nb + jnp.minimum((i + 1) * n_qb, nb - 1), 0))
    full = lambda a: pl.BlockSpec(a.shape, lambda b, i: (0,) * a.ndim)
    ext = (n_qb + 2) * BLOCK
    n_units = n_qb * N_KV_HEADS * 2
    return pl.pallas_call(
        _attn_kernel,
        grid=(batch, n_tiles),
        in_specs=[pl.BlockSpec(memory_space=pltpu.SMEM),
                  cur(ATTN_WIDTH), cur(KV_DUP_WIDTH), prev, nxt, cur(KV_DUP_WIDTH), prev, nxt,
                  full(bias_prev), full(bias_cur), full(bias_next)],
        out_specs=cur(ATTN_WIDTH),
        out_shape=jax.ShapeDtypeStruct((t, ATTN_WIDTH), BF16),
        scratch_shapes=[pltpu.VMEM((ext, KV_DUP_WIDTH), BF16) for _ in range(4)]
                       + [pltpu.VMEM((n_units, 2 * BLOCK, 3 * BLOCK), F32),
                          pltpu.VMEM((n_units, 2 * BLOCK, 3 * BLOCK), BF16),
                          pltpu.VMEM((n_units, 2 * BLOCK, LANES), F32)],
        compiler_params=_params(dimension_semantics=("arbitrary", "arbitrary")),
        name="attn",
    )(sink, q, kk, kk, kk, vv, vv, vv, bias_prev, bias_cur, bias_next)


def _mix_kernel(xn_ref, u_ref, up_ref, un_ref, o_ref, qm_ref, gate_ref, memkv_ref,
                wpool_ref, pscale_ref, ppool_ref, pattn_ref, pmem_ref, wout_ref,
                g1_ref, b1_ref, wr_ref,
                x1r_ref, eid_ref, wts_ref, uext, *, seq):
    i = pl.program_id(1)
    n_tiles = pl.num_programs(1)
    tm = xn_ref.shape[0]

    halo = POOL_HALO
    uext[0:halo, :] = jnp.where(i > 0, up_ref[...], 0.0)
    uext[halo:halo + tm, :] = u_ref[...]
    uext[halo + tm:halo + tm + halo, :] = jnp.where(i < n_tiles - 1, un_ref[...], 0.0)
    pos = i * tm + lax.broadcasted_iota(I32, (tm, 1), 0)
    mixed = []
    for gi, win in enumerate(POOL_WINDOWS):
        c0 = gi * POOL_GROUP_CH
        half = win // 2
        total = jnp.zeros((tm, POOL_GROUP_CH), F32)
        for off in range(-half, half):
            total = total + uext[halo + off:halo + off + tm, c0:c0 + POOL_GROUP_CH]
        cnt = (jnp.minimum(pos + half, seq) - jnp.maximum(pos - half, 0)).astype(F32)
        pooled = total * (1.0 / cnt) - u_ref[:, c0:c0 + POOL_GROUP_CH]
        mixed.append(jnp.dot(pooled.astype(BF16), wpool_ref[gi], preferred_element_type=F32)
                     * pscale_ref[:, c0:c0 + POOL_GROUP_CH])
    mixed = jnp.concatenate(mixed, axis=1).astype(BF16)
    merged = gate_ref[:, 0:D_MODEL].astype(F32) * jnp.dot(
        mixed, ppool_ref[...], preferred_element_type=F32)

    merged = merged + gate_ref[:, D_MODEL:2 * D_MODEL].astype(F32) * jnp.dot(
        o_ref[...], pattn_ref[...], preferred_element_type=F32)

    heads = []
    for h in range(MEM_HEADS):
        c0 = h * MEM_HEAD_DIM
        km = memkv_ref[:, c0:c0 + MEM_HEAD_DIM]
        vm = memkv_ref[:, MEM_WIDTH + c0:MEM_WIDTH + c0 + MEM_HEAD_DIM]
        s = lax.dot_general(qm_ref[:, c0:c0 + MEM_HEAD_DIM], km, (((1,), (1,)), ((), ())),
                            preferred_element_type=F32) * (MEM_HEAD_DIM ** -0.5)
        p = jnp.exp(s - jnp.max(s, axis=-1, keepdims=True))
        p = (p / jnp.sum(p, axis=-1, keepdims=True)).astype(BF16)
        heads.append(jnp.dot(p, vm, preferred_element_type=F32).astype(BF16))
    om = jnp.concatenate(heads, axis=1)
    merged = merged + gate_ref[:, 2 * D_MODEL:3 * D_MODEL].astype(F32) * jnp.dot(
        om, pmem_ref[...], preferred_element_type=F32)

    y = jnp.dot(merged.astype(BF16), wout_ref[...], preferred_element_type=F32)
    x1 = _layer_norm(ALPHA * xn_ref[...] + y, g1_ref[...], b1_ref[...])
    x1r_ref[...] = _rows_to_tiles(x1)

    lt = lax.dot_general(wr_ref[...], x1.astype(BF16), (((1,), (1,)), ((), ())),
                         preferred_element_type=F32)
    gl = [lt[r:r + 1, :] for r in range(N_EXPERT_GROUPS)]
    gmax = gl[0]
    grp = jnp.zeros((1, tm), I32)
    for r in range(1, N_EXPERT_GROUPS):
        better = gl[r] > gmax
        grp = jnp.where(better, r, grp)
        gmax = jnp.where(better, gl[r], gmax)
    gsum = gl[0] * 0.0
    for r in range(N_EXPERT_GROUPS):
        gsum = gsum + jnp.exp(gl[r] - gmax)
    gp = 1.0 / gsum
    sel = jnp.zeros((EXPERTS_PER_GROUP, tm), F32)
    for r in range(N_EXPERT_GROUPS):
        rows = lt[SUBLANES + r * EXPERTS_PER_GROUP:SUBLANES + (r + 1) * EXPERTS_PER_GROUP, :]
        sel = jnp.where(grp == r, rows, sel)
    ridx = lax.broadcasted_iota(I32, (EXPERTS_PER_GROUP, tm), 0)
    top1 = jnp.max(sel, axis=0, keepdims=True)
    i1 = jnp.min(jnp.where(sel == top1, ridx, EXPERTS_PER_GROUP), axis=0, keepdims=True)
    rest = jnp.where(ridx == i1, -jnp.inf, sel)
    top2 = jnp.max(rest, axis=0, keepdims=True)
    i2 = jnp.min(jnp.where(rest == top2, ridx, EXPERTS_PER_GROUP), axis=0, keepdims=True)
    e2 = jnp.exp(top2 - top1)
    inv = gp / (1.0 + e2)
    eid_ref[0:1, :] = grp * EXPERTS_PER_GROUP + i1
    eid_ref[1:2, :] = grp * EXPERTS_PER_GROUP + i2
    wts_ref[0:1, :] = inv
    wts_ref[1:2, :] = e2 * inv


def _mix_call(xn, u, o, qm, gate, memkv, p, batch, seq):
    tm = TM_MIX
    n_tiles = seq // tm
    t = batch * seq
    hb = tm // POOL_HALO

    row = lambda width: pl.BlockSpec((tm, width), lambda b, i: (b * n_tiles + i, 0))
    prev = pl.BlockSpec((POOL_HALO, POOL_WIDTH),
                        lambda b, i: (jnp.maximum((b * n_tiles + i) * hb - 1, 0), 0))
    nxt = pl.BlockSpec((POOL_HALO, POOL_WIDTH),
                       lambda b, i: (jnp.minimum((b * n_tiles + i + 1) * hb, t // POOL_HALO - 1), 0))
    full = lambda a: pl.BlockSpec(a.shape, lambda b, i: (0,) * a.ndim)
    lane_row = pl.BlockSpec((TOP_K, tm), lambda b, i: (0, b * n_tiles + i))
    weights = (p["w_pool"], p["pool_scale"], p["p_pool"], p["p_attn"], p["p_mem"], p["w_out"],
               p["ln1_g"], p["ln1_b"], p["w_router"])
    return pl.pallas_call(
        functools.partial(_mix_kernel, seq=seq),
        grid=(batch, n_tiles),
        in_specs=[row(D_MODEL), row(POOL_WIDTH), prev, nxt, row(ATTN_WIDTH), row(MEM_WIDTH),
                  row(N_BRANCHES * D_MODEL),
                  pl.BlockSpec((MEM_TOKENS, 2 * MEM_WIDTH), lambda b, i: (b, 0))]
                 + [full(w) for w in weights],
        out_specs=[pl.BlockSpec((tm, ROW_TILE, LANES), lambda b, i: (b * n_tiles + i, 0, 0)),
                   lane_row, lane_row],
        out_shape=[jax.ShapeDtypeStruct((t, ROW_TILE, LANES), F32),
                   jax.ShapeDtypeStruct((TOP_K, t), I32),
                   jax.ShapeDtypeStruct((TOP_K, t), F32)],
        scratch_shapes=[pltpu.VMEM((tm + 2 * POOL_HALO, POOL_WIDTH), F32)],
        compiler_params=_params(dimension_semantics=("arbitrary", "arbitrary")),
        name="mix",
    )(xn, u, u, u, o, qm, gate, memkv, *weights)


def _moe_kernel(iblk_ref, ie_ref, inext_ref, irun_ref, ilo_ref, ihi_ref,
                x_ref, ws_ref, wgu_hbm, wd_hbm, y_ref,
                x_bf, h_bf, wgu_f32, wd_f32, wgu_bf, wd_bf, wsem):
    i = pl.program_id(0)
    bm = x_ref.shape[0]
    lo = ilo_ref[i]
    hi = ihi_ref[i]
    nonempty = hi > lo

    def weight_copies(e, ws):
        return (pltpu.make_async_copy(wgu_hbm.at[e], wgu_f32.at[ws], wsem.at[ws]),
                pltpu.make_async_copy(wd_hbm.at[e], wd_f32.at[ws], wsem.at[ws]))

    @pl.when(jnp.logical_or(i == 0, ie_ref[i] != ie_ref[jnp.maximum(i - 1, 0)]))
    def _():
        e = ie_ref[i]
        wslot = irun_ref[i] & 1

        @pl.when(i == 0)
        def _():
            for c in weight_copies(e, wslot):
                c.start()
        for c in weight_copies(e, wslot):
            c.wait()

        @pl.when(inext_ref[i] >= 0)
        def _():
            for c in weight_copies(inext_ref[i], 1 - wslot):
                c.start()
        wgu_bf[...] = wgu_f32[wslot].astype(BF16)
        wd_bf[...] = wd_f32[wslot].astype(BF16)

    @pl.when(nonempty)
    def _():
        x_bf[...] = _tiles_to_rows(x_ref[...]).astype(BF16)

    @pl.when(nonempty)
    def _():
        gu = jnp.dot(x_bf[...], wgu_bf[...], preferred_element_type=F32)
        gate = gu[:, :EXPERT_HIDDEN]
        h_bf[...] = ((gate * _sigmoid(gate)) * gu[:, EXPERT_HIDDEN:]).astype(BF16)

    @pl.when(nonempty)
    def _():
        rows = lax.broadcasted_iota(I32, (bm, 1), 0)
        mine = jnp.logical_and(rows >= lo, rows < hi)
        w_col = jnp.transpose(jnp.broadcast_to(ws_ref[0], (LANES, bm)))[:, 0:1]
        y = jnp.dot(h_bf[...], wd_bf[...], preferred_element_type=F32) * w_col
        y = _rows_to_tiles(jnp.where(mine, y, 0.0))

        @pl.when(lo == 0)
        def _():
            y_ref[...] = y

        @pl.when(lo > 0)
        def _():
            y_ref[...] = y_ref[...] + y


def _moe_call(xs, wsorted, item_blk, item_e, item_lo, item_hi, w_gu, w_down):
    a = xs.shape[0]
    bm = BM_MOE
    n_items = item_blk.shape[0]
    change = jnp.concatenate([jnp.zeros((1,), I32), (item_e[1:] != item_e[:-1]).astype(I32)])
    item_run = jnp.cumsum(change).astype(I32)
    first_later = jnp.sum((item_e[None, :] <= item_e[:, None]).astype(I32), axis=1)
    item_next = jnp.where(first_later < n_items, item_e[jnp.minimum(first_later, n_items - 1)], -1).astype(I32)
    grid_spec = pltpu.PrefetchScalarGridSpec(
        num_scalar_prefetch=6,
        grid=(n_items,),
        in_specs=[pl.BlockSpec((bm, ROW_TILE, LANES), lambda i, blk, *_: (blk[i], 0, 0)),
                  pl.BlockSpec((1, 1, bm), lambda i, blk, *_: (blk[i], 0, 0)),
                  pl.BlockSpec(memory_space=pl.ANY),
                  pl.BlockSpec(memory_space=pl.ANY)],
        out_specs=pl.BlockSpec((bm, ROW_TILE, LANES), lambda i, blk, *_: (blk[i], 0, 0)),
        scratch_shapes=[pltpu.VMEM((bm, D_MODEL), BF16),
                        pltpu.VMEM((bm, EXPERT_HIDDEN), BF16),
                        pltpu.VMEM((2, D_MODEL, 2 * EXPERT_HIDDEN), F32),
                        pltpu.VMEM((2, EXPERT_HIDDEN, D_MODEL), F32),
                        pltpu.VMEM((D_MODEL, 2 * EXPERT_HIDDEN), BF16),
                        pltpu.VMEM((EXPERT_HIDDEN, D_MODEL), BF16),
                        pltpu.SemaphoreType.DMA((2,))],
    )
    return pl.pallas_call(
        _moe_kernel,
        grid_spec=grid_spec,
        out_shape=jax.ShapeDtypeStruct((a, ROW_TILE, LANES), F32),
        compiler_params=_params(dimension_semantics=("arbitrary",)),
        name="moe",
    )(item_blk, item_e, item_next, item_run, item_lo, item_hi, xs, wsorted, w_gu, w_down)


def _sc_mesh():
    return plsc.VectorSubcoreMesh(core_axis_name="core", subcore_axis_name="subcore")


def _sc_gather_rows(rows, idx):
    m = idx.shape[0]
    w = SC_ROW_WINDOW

    @pl.kernel(out_type=jax.ShapeDtypeStruct((m,) + rows.shape[1:], rows.dtype), mesh=_sc_mesh(),
               scratch_types=[])
    def gather(rows_hbm, idx_hbm, out_hbm):
        def body(idx_vmem, out_vmem):
            pltpu.sync_copy(rows_hbm.at[idx_vmem.at[0]], out_vmem)

        pltpu.emit_pipeline(
            body,
            grid=(m // w,),
            in_specs=[pl.BlockSpec((1, w), index_map=lambda i: (i, 0))],
            out_specs=[pl.BlockSpec((w,) + rows.shape[1:], index_map=lambda i: (i, 0, 0))],
            core_axis_name=("core", "subcore"),
            dimension_semantics=(pltpu.PARALLEL,),
        )(idx_hbm, out_hbm)

    return gather(rows, idx.reshape(m // w, w))


def _sc_scatter_rows(rows, idx):
    m = idx.shape[0]
    w = SC_ROW_WINDOW

    @pl.kernel(out_type=jax.ShapeDtypeStruct(rows.shape, rows.dtype), mesh=_sc_mesh(), scratch_types=[])
    def scatter(rows_hbm, idx_hbm, out_hbm):
        def body(rows_vmem, idx_vmem):
            pltpu.sync_copy(rows_vmem, out_hbm.at[idx_vmem.at[0]])

        pltpu.emit_pipeline(
            body,
            grid=(m // w,),
            in_specs=[pl.BlockSpec((w,) + rows.shape[1:], index_map=lambda i: (i, 0, 0)),
                      pl.BlockSpec((1, w), index_map=lambda i: (i, 0))],
            out_specs=[],
            core_axis_name=("core", "subcore"),
            dimension_semantics=(pltpu.PARALLEL,),
        )(rows_hbm, idx_hbm)

    return scatter(rows, idx.reshape(m // w, w))


def _final_kernel(x1_ref, ya_ref, yb_ref, g_ref, b_ref, out_ref):
    moe = ya_ref[...] + yb_ref[...]
    out_ref[...] = _layer_norm(_tiles_to_rows(ALPHA * x1_ref[...] + moe), g_ref[...], b_ref[...])


def _final_call(x1, y2, ln_g, ln_b):
    t = x1.shape[0]
    tm = TM_FINAL
    n_tiles = t // tm
    return pl.pallas_call(
        _final_kernel,
        grid=(n_tiles,),
        in_specs=[pl.BlockSpec((tm, ROW_TILE, LANES), lambda i: (i, 0, 0)),
                  pl.BlockSpec((tm, ROW_TILE, LANES), lambda i: (i, 0, 0)),
                  pl.BlockSpec((tm, ROW_TILE, LANES), lambda i: (n_tiles + i, 0, 0)),
                  pl.BlockSpec((1, D_MODEL), lambda i: (0, 0)),
                  pl.BlockSpec((1, D_MODEL), lambda i: (0, 0))],
        out_specs=pl.BlockSpec((tm, D_MODEL), lambda i: (i, 0)),
        out_shape=jax.ShapeDtypeStruct((t, D_MODEL), F32),
        compiler_params=_params(dimension_semantics=("arbitrary",)),
        name="final",
    )(x1, y2, y2, ln_g, ln_b)


def _t5_bucket(rel):
    nb = N_REL_BUCKETS // 2
    max_exact = nb // 2
    ret = jnp.where(rel > 0, nb, 0)
    n = jnp.abs(rel)
    nf = jnp.maximum(n, 1).astype(F32)
    large = max_exact + (jnp.log(nf / max_exact) / math.log(REL_MAX_DISTANCE / max_exact)
                         * (nb - max_exact)).astype(I32)
    large = jnp.minimum(large, nb - 1)
    return ret + jnp.where(n < max_exact, n, large)


def _bias_tables(rel_table):
    n, m = BLOCK, 3 * BLOCK
    rel = jnp.arange(-(2 * BLOCK - 1), 2 * BLOCK)
    by_rel = jnp.where((jnp.abs(rel) <= WINDOW)[:, None],
                       rel_table[_t5_bucket(rel)].astype(F32) * LOG2_E, NEG_INF)
    by_rel = jnp.pad(by_rel.T, ((0, 0), (0, 1)))
    skew = jnp.tile(by_rel, (1, n))[:, :n * (m + n - 1)].reshape(N_Q_HEADS, n, m + n - 1)
    bias = skew[:, :, n - 1:n - 1 + m]
    heads = [4 * h + half for h in range(N_KV_HEADS) for half in range(2)]
    units = jnp.concatenate([jnp.stack([bias[a] for a in heads]),
                             jnp.stack([bias[a + 2] for a in heads])], axis=1)
    masked = jnp.full((2 * N_KV_HEADS, 2 * BLOCK, BLOCK), NEG_INF, F32)
    prev = jnp.stack([units[:, :, :BLOCK], masked])
    cur = units[:, :, BLOCK:2 * BLOCK]
    nxt = jnp.stack([units[:, :, 2 * BLOCK:], masked])
    return prev, cur, nxt


def _prepare(rel_bias_table, w_in, b_in, w_pool, pool_scale, p_pool, sink, p_attn, w_mem_kv, p_mem,
             w_out, ln1_g, ln1_b, w_router_group, w_router_expert, w_gu, w_down):
    l = 0
    w_router = jnp.zeros((ROUTER_ROWS, D_MODEL), F32)
    w_router = w_router.at[:N_EXPERT_GROUPS].set(w_router_group[l].T)
    w_router = w_router.at[SUBLANES:].set(w_router_expert[l].T)
    bias_prev, bias_cur, bias_next = _bias_tables(rel_bias_table)
    return dict(
        w_in=w_in[l].astype(BF16), b_in=b_in[l][None, :],
        w_pool=w_pool[l].astype(BF16), pool_scale=pool_scale[l][None, :],
        p_pool=p_pool[l].astype(BF16), p_attn=p_attn[l].astype(BF16), p_mem=p_mem[l].astype(BF16),
        w_out=w_out[l].astype(BF16), w_memkv=w_mem_kv[l].astype(BF16),
        ln1_g=ln1_g[l][None, :], ln1_b=ln1_b[l][None, :],
        w_router=w_router.astype(BF16), sink=sink[l] * LOG2_E,
        w_gu=w_gu[l], w_down=w_down[l],
        bias_prev=bias_prev, bias_cur=bias_cur, bias_next=bias_next,
    )


def _dispatch_plan(eid, wts):
    t = eid.shape[1]
    a = TOP_K * t
    bm = BM_MOE
    nblk = a // bm
    eflat = eid.reshape(a)
    wflat = wts.reshape(a)
    shift = (a - 1).bit_length()
    assert N_EXPERTS << shift < 2 ** 31
    packed = jnp.sort(eflat * (1 << shift) + jnp.arange(a, dtype=I32))
    order = packed & ((1 << shift) - 1)
    wsorted = wflat[order].reshape(nblk, 1, bm)
    experts = jnp.arange(N_EXPERTS, dtype=I32)
    counts = jnp.sum((eflat[None, :] == experts[:, None]).astype(I32), axis=1)
    ends = jnp.cumsum(counts).astype(I32)
    starts = ends - counts
    cuts = jnp.sort(jnp.concatenate([jnp.arange(nblk, dtype=I32) * bm, starts, jnp.array([a], I32)]))
    lo_abs = cuts[:-1]
    hi_abs = cuts[1:]
    nonempty = hi_abs > lo_abs
    blk = jnp.minimum(lo_abs // bm, nblk - 1)
    probe = jnp.minimum(lo_abs, a - 1)
    item_e = jnp.sum((ends[None, :] <= probe[:, None]).astype(I32), axis=1)
    item_lo = jnp.where(nonempty, lo_abs - blk * bm, 0).astype(I32)
    item_hi = jnp.where(nonempty, hi_abs - blk * bm, 0).astype(I32)
    return order, wsorted, blk.astype(I32), item_e.astype(I32), item_lo, item_hi


def _trunk(x, mem, ln_in_g, ln_in_b, ln2_g, ln2_b, p):
    batch, seq, d = x.shape
    t = batch * seq
    assert seq % TM_MIX == 0 and seq % TQ_ATTN == 0 and t % TM_INPROJ == 0 and (TOP_K * t) % BM_MOE == 0
    memkv = _memkv_call(mem.reshape(batch * MEM_TOKENS, d), p["w_memkv"])
    xn, u, q, kk, vv, qm, gate = _inproj_call(x.reshape(t, d), ln_in_g[None, :], ln_in_b[None, :],
                                              p["w_in"], p["b_in"])
    o = _attn_call(q, kk, vv, p["sink"], p["bias_prev"], p["bias_cur"], p["bias_next"], batch, seq)
    x1r, eid, wts = _mix_call(xn, u, o, qm, gate, memkv, p, batch, seq)
    order, wsorted, item_blk, item_e, item_lo, item_hi = _dispatch_plan(eid, wts)
    sorted_tok = jnp.where(order >= t, order - t, order)
    xs = _sc_gather_rows(x1r, sorted_tok)
    ys = _moe_call(xs, wsorted, item_blk, item_e, item_lo, item_hi, p["w_gu"], p["w_down"])
    y2 = _sc_scatter_rows(ys, order)
    out = _final_call(x1r, y2, ln2_g[None, :], ln2_b[None, :])
    return out.reshape(batch, seq, d)


def kernel(x_prompt, x_sample, mem_prompt, mem_sample, ln_in_g, ln_in_b, rel_bias_table, w_in, b_in,
           w_pool, pool_scale, p_pool, sink, p_attn, w_mem_kv, p_mem, w_out, ln1_g, ln1_b,
           w_router_group, w_router_expert, w_gu, w_down, ln2_g, ln2_b):
    p = _prepare(rel_bias_table, w_in, b_in, w_pool, pool_scale, p_pool, sink, p_attn, w_mem_kv, p_mem,
                 w_out, ln1_g, ln1_b, w_router_group, w_router_expert, w_gu, w_down)
    y_prompt = _trunk(x_prompt, mem_prompt, ln_in_g, ln_in_b, ln2_g[0], ln2_b[0], p)
    y_sample = _trunk(x_sample, mem_sample, ln_in_g, ln_in_b, ln2_g[0], ln2_b[0], p)
    return (y_prompt, y_sample)
```

```python
import functools
import math

import jax
import jax.numpy as jnp
from jax import lax
from jax.experimental import pallas as pl
from jax.experimental.pallas import tpu as pltpu
from jax.experimental.pallas import tpu_sc as plsc

F32 = jnp.float32
BF16 = jnp.bfloat16
I32 = jnp.int32

D_MODEL = 1024
DEPTH = 1
POOL_WIDTH = 512
POOL_WINDOWS = (2, 4, 8, 16)
N_POOL_GROUPS = 4
POOL_GROUP_CH = POOL_WIDTH // N_POOL_GROUPS
N_Q_HEADS = 16
N_KV_HEADS = 4
HEAD_DIM = 64
ATTN_WIDTH = N_Q_HEADS * HEAD_DIM
KV_WIDTH = N_KV_HEADS * HEAD_DIM
WINDOW = 128
BLOCK = 128
N_REL_BUCKETS = 32
REL_MAX_DISTANCE = 128
MEM_TOKENS = 256
MEM_HEADS = 4
MEM_HEAD_DIM = 128
MEM_WIDTH = MEM_HEADS * MEM_HEAD_DIM
N_BRANCHES = 3
N_EXPERT_GROUPS = 4
EXPERTS_PER_GROUP = 8
N_EXPERTS = N_EXPERT_GROUPS * EXPERTS_PER_GROUP
TOP_K = 2
EXPERT_HIDDEN = 512
ALPHA = (2 * DEPTH) ** 0.25
LN_EPS = 1e-5
NEG_INF = -1e30
LOG2_E = math.log2(math.e)

LANES = 128
SUBLANES = 8
ROW_TILE = D_MODEL // LANES
VMEM_LIMIT_BYTES = 56 * 1024 * 1024

KV_DUP_WIDTH = N_KV_HEADS * LANES

C_U = 0
C_Q = C_U + POOL_WIDTH
C_K = C_Q + ATTN_WIDTH
C_V = C_K + KV_WIDTH
C_M = C_V + KV_WIDTH
C_G = C_M + MEM_WIDTH
C_END = C_G + N_BRANCHES * D_MODEL

TM_INPROJ = 512
TQ_ATTN = 512
TM_MIX = 512
BM_MOE = 512
SC_ROW_WINDOW = 32
TM_FINAL = 1024
POOL_HALO = 8
ROUTER_ROWS = 40


def _layer_norm(x, g, b):
    mu = jnp.mean(x, axis=-1, keepdims=True)
    xc = x - mu
    var = jnp.mean(xc * xc, axis=-1, keepdims=True)
    return xc * lax.rsqrt(var + LN_EPS) * g + b


def _sigmoid(x):
    return 1.0 / (1.0 + jnp.exp(-x))


def _rows_to_tiles(x):
    return pltpu.einshape("r(cl)->rcl", x, c=ROW_TILE)


def _tiles_to_rows(x):
    return pltpu.einshape("rcl->r(cl)", x)


def _params(**kw):
    return pltpu.CompilerParams(vmem_limit_bytes=VMEM_LIMIT_BYTES, **kw)


def _memkv_kernel(mem_ref, w_ref, out_ref):
    out_ref[...] = jnp.dot(mem_ref[...].astype(BF16), w_ref[...],
                           preferred_element_type=F32).astype(BF16)


def _memkv_call(mem2, w_memkv):
    rows = mem2.shape[0]
    return pl.pallas_call(
        _memkv_kernel,
        grid=(rows // MEM_TOKENS,),
        in_specs=[pl.BlockSpec((MEM_TOKENS, D_MODEL), lambda i: (i, 0)),
                  pl.BlockSpec((D_MODEL, 2 * MEM_WIDTH), lambda i: (0, 0))],
        out_specs=pl.BlockSpec((MEM_TOKENS, 2 * MEM_WIDTH), lambda i: (i, 0)),
        out_shape=jax.ShapeDtypeStruct((rows, 2 * MEM_WIDTH), BF16),
        compiler_params=_params(dimension_semantics=("arbitrary",)),
        name="memkv",
    )(mem2, w_memkv)


def _inproj_kernel(x_ref, g_ref, b_ref, w_ref, bias_ref,
                   xn_ref, u_ref, q_ref, k_ref, v_ref, qm_ref, gate_ref):
    xn = _layer_norm(x_ref[...], g_ref[...], b_ref[...])
    xn_ref[...] = xn
    xb = xn.astype(BF16)

    def seg(lo, hi):
        return jnp.dot(xb, w_ref[:, lo:hi], preferred_element_type=F32) + bias_ref[:, lo:hi]

    u_ref[...] = seg(C_U, C_Q)
    q_ref[...] = (seg(C_Q, C_K) * (HEAD_DIM ** -0.5 * LOG2_E)).astype(BF16)
    kv = seg(C_K, C_M)
    low = lax.broadcasted_iota(I32, (kv.shape[0], LANES), 1) < HEAD_DIM
    for out_ref, c_lo in ((k_ref, 0), (v_ref, KV_WIDTH)):
        for c in range(KV_WIDTH // LANES):
            blk = kv[:, c_lo + c * LANES:c_lo + (c + 1) * LANES]
            rot = pltpu.roll(blk, HEAD_DIM, 1)
            out_ref[:, 2 * c * LANES:(2 * c + 1) * LANES] = jnp.where(low, blk, rot).astype(BF16)
            out_ref[:, (2 * c + 1) * LANES:(2 * c + 2) * LANES] = jnp.where(low, rot, blk).astype(BF16)
    qm_ref[...] = seg(C_M, C_G).astype(BF16)
    for j in range(N_BRANCHES):
        lo = C_G + j * D_MODEL
        gate_ref[:, j * D_MODEL:(j + 1) * D_MODEL] = _sigmoid(seg(lo, lo + D_MODEL)).astype(BF16)


def _inproj_call(x2, ln_g, ln_b, w_in, b_in):
    t = x2.shape[0]
    tm = TM_INPROJ
    row = lambda width: pl.BlockSpec((tm, width), lambda i: (i, 0))
    const = lambda shape: pl.BlockSpec(shape, lambda i: (0, 0))
    widths = (D_MODEL, POOL_WIDTH, ATTN_WIDTH, KV_DUP_WIDTH, KV_DUP_WIDTH, MEM_WIDTH, N_BRANCHES * D_MODEL)
    dtypes = (F32, F32, BF16, BF16, BF16, BF16, BF16)
    return pl.pallas_call(
        _inproj_kernel,
        grid=(t // tm,),
        in_specs=[row(D_MODEL), const((1, D_MODEL)), const((1, D_MODEL)),
                  const((D_MODEL, C_END)), const((1, C_END))],
        out_specs=[row(w) for w in widths],
        out_shape=[jax.ShapeDtypeStruct((t, w), d) for w, d in zip(widths, dtypes)],
        compiler_params=_params(dimension_semantics=("arbitrary",)),
        name="inproj",
    )(x2, ln_g, ln_b, w_in, b_in)


def _attn_kernel(sink_ref, q_ref, kc_ref, kp_ref, kn_ref, vc_ref, vp_ref, vn_ref,
                 bp_ref, bc_ref, bn_ref, o_ref, klo, khi, vlo, vhi, s_scr, p_scr, inv_scr):
    i = pl.program_id(1)
    n_tiles = pl.num_programs(1)
    tq = q_ref.shape[0]
    n_qb = tq // BLOCK

    lane = lax.broadcasted_iota(I32, (BLOCK, KV_DUP_WIDTH), 1)
    low = (lane & (LANES - 1)) < HEAD_DIM

    def put(dst_lo, dst_hi, r0, val):
        zero = jnp.zeros_like(val)
        dst_lo[r0:r0 + BLOCK, :] = jnp.where(low, val, zero)
        dst_hi[r0:r0 + BLOCK, :] = jnp.where(low, zero, val)

    put(klo, khi, 0, kp_ref[...])
    put(vlo, vhi, 0, vp_ref[...])
    for j in range(n_qb):
        put(klo, khi, (j + 1) * BLOCK, kc_ref[j * BLOCK:(j + 1) * BLOCK, :])
        put(vlo, vhi, (j + 1) * BLOCK, vc_ref[j * BLOCK:(j + 1) * BLOCK, :])
    put(klo, khi, (n_qb + 1) * BLOCK, kn_ref[...])
    put(vlo, vhi, (n_qb + 1) * BLOCK, vn_ref[...])

    first = (i == 0).astype(I32)
    last = (i == n_tiles - 1).astype(I32)

    def softmax_rows(s, sink):
        m = jnp.maximum(jnp.max(s, axis=-1, keepdims=True), sink)
        p = jnp.exp2(s - m)
        denom = jnp.sum(p, axis=-1, keepdims=True) + jnp.exp2(sink - m)
        return p.astype(BF16), jnp.broadcast_to(1.0 / denom, (s.shape[0], LANES))

    units = [(j, h, half) for j in range(n_qb) for h in range(N_KV_HEADS) for half in range(2)]
    always = i >= 0

    @pl.when(always)
    def _scores():
        for u, (j, h, half) in enumerate(units):
            r0 = j * BLOCK
            c0 = h * 2 * LANES
            pv = first if j == 0 else 0
            nv = last if j == n_qb - 1 else 0
            unit = h * 2 + half
            q_pairs = jnp.concatenate([q_ref[r0:r0 + BLOCK, c0:c0 + LANES],
                                       q_ref[r0:r0 + BLOCK, c0 + LANES:c0 + 2 * LANES]], axis=0)
            kx = (klo, khi)[half][r0:r0 + 3 * BLOCK, h * LANES:(h + 1) * LANES]
            s = lax.dot_general(q_pairs, kx, (((1,), (1,)), ((), ())), preferred_element_type=F32)
            bias = jnp.concatenate([bp_ref[pv, unit], bc_ref[unit], bn_ref[nv, unit]], axis=1)
            s_scr[u] = s + bias

    @pl.when(always)
    def _softmax():
        for u, (j, h, half) in enumerate(units):
            p_scr[u, :BLOCK], inv_scr[u, :BLOCK] = softmax_rows(s_scr[u, :BLOCK], sink_ref[4 * h + half])
            p_scr[u, BLOCK:], inv_scr[u, BLOCK:] = softmax_rows(s_scr[u, BLOCK:], sink_ref[4 * h + 2 + half])

    @pl.when(always)
    def _values():
        for j in range(n_qb):
            r0 = j * BLOCK
            for h in range(N_KV_HEADS):
                c0 = h * 2 * LANES
                acc = jnp.zeros((2 * BLOCK, LANES), F32)
                for half, vref in enumerate((vlo, vhi)):
                    u = units.index((j, h, half))
                    vx = vref[r0:r0 + 3 * BLOCK, h * LANES:(h + 1) * LANES]
                    acc = acc + jnp.dot(p_scr[u], vx, preferred_element_type=F32) * inv_scr[u]
                o_ref[r0:r0 + BLOCK, c0:c0 + LANES] = acc[:BLOCK].astype(BF16)
                o_ref[r0:r0 + BLOCK, c0 + LANES:c0 + 2 * LANES] = acc[BLOCK:].astype(BF16)


def _attn_call(q, kk, vv, sink, bias_prev, bias_cur, bias_next, batch, seq):
    tq = TQ_ATTN
    n_qb = tq // BLOCK
    nb = seq // BLOCK
    n_tiles = seq // tq
    t = batch * seq

    cur = lambda width: pl.BlockSpec((tq, width), lambda b, i: (b * n_tiles + i, 0))
    prev = pl.BlockSpec((BLOCK, KV_DUP_WIDTH), lambda b, i: (b * nb + jnp.maximum(i * n_qb - 1, 0), 0))
    nxt = pl.BlockSpec((BLOCK, KV_DUP_WIDTH), lambda b, i: (b * nb + jnp.minimum((i + 1) * n_qb, nb - 1), 0))
    full = lambda a: pl.BlockSpec(a.shape, lambda b, i: (0,) * a.ndim)
    ext = (n_qb + 2) * BLOCK
    n_units = n_qb * N_KV_HEADS * 2
    return pl.pallas_call(
        _attn_kernel,
        grid=(batch, n_tiles),
        in_specs=[pl.BlockSpec(memory_space=pltpu.SMEM),
                  cur(ATTN_WIDTH), cur(KV_DUP_WIDTH), prev, nxt, cur(KV_DUP_WIDTH), prev, nxt,
                  full(bias_prev), full(bias_cur), full(bias_next)],
        out_specs=cur(ATTN_WIDTH),
        out_shape=jax.ShapeDtypeStruct((t, ATTN_WIDTH), BF16),
        scratch_shapes=[pltpu.VMEM((ext, KV_DUP_WIDTH), BF16) for _ in range(4)]
                       + [pltpu.VMEM((n_units, 2 * BLOCK, 3 * BLOCK), F32),
                          pltpu.VMEM((n_units, 2 * BLOCK, 3 * BLOCK), BF16),
                          pltpu.VMEM((n_units, 2 * BLOCK, LANES), F32)],
        compiler_params=_params(dimension_semantics=("arbitrary", "arbitrary")),
        name="attn",
    )(sink, q, kk, kk, kk, vv, vv, vv, bias_prev, bias_cur, bias_next)


def _mix_kernel(xn_ref, u_ref, up_ref, un_ref, o_ref, qm_ref, gate_ref, memkv_ref,
                wpool_ref, pscale_ref, ppool_ref, pattn_ref, pmem_ref, wout_ref,
                g1_ref, b1_ref, wr_ref,
                x1r_ref, eid_ref, wts_ref, uext, *, seq):
    i = pl.program_id(1)
    n_tiles = pl.num_programs(1)
    tm = xn_ref.shape[0]

    halo = POOL_HALO
    uext[0:halo, :] = jnp.where(i > 0, up_ref[...], 0.0)
    uext[halo:halo + tm, :] = u_ref[...]
    uext[halo + tm:halo + tm + halo, :] = jnp.where(i < n_tiles - 1, un_ref[...], 0.0)
    pos = i * tm + lax.broadcasted_iota(I32, (tm, 1), 0)
    mixed = []
    for gi, win in enumerate(POOL_WINDOWS):
        c0 = gi * POOL_GROUP_CH
        half = win // 2
        total = jnp.zeros((tm, POOL_GROUP_CH), F32)
        for off in range(-half, half):
            total = total + uext[halo + off:halo + off + tm, c0:c0 + POOL_GROUP_CH]
        cnt = (jnp.minimum(pos + half, seq) - jnp.maximum(pos - half, 0)).astype(F32)
        pooled = total * (1.0 / cnt) - u_ref[:, c0:c0 + POOL_GROUP_CH]
        mixed.append(jnp.dot(pooled.astype(BF16), wpool_ref[gi], preferred_element_type=F32)
                     * pscale_ref[:, c0:c0 + POOL_GROUP_CH])
    mixed = jnp.concatenate(mixed, axis=1).astype(BF16)
    merged = gate_ref[:, 0:D_MODEL].astype(F32) * jnp.dot(
        mixed, ppool_ref[...], preferred_element_type=F32)

    merged = merged + gate_ref[:, D_MODEL:2 * D_MODEL].astype(F32) * jnp.dot(
        o_ref[...], pattn_ref[...], preferred_element_type=F32)

    heads = []
    for h in range(MEM_HEADS):
        c0 = h * MEM_HEAD_DIM
        km = memkv_ref[:, c0:c0 + MEM_HEAD_DIM]
        vm = memkv_ref[:, MEM_WIDTH + c0:MEM_WIDTH + c0 + MEM_HEAD_DIM]
        s = lax.dot_general(qm_ref[:, c0:c0 + MEM_HEAD_DIM], km, (((1,), (1,)), ((), ())),
                            preferred_element_type=F32) * (MEM_HEAD_DIM ** -0.5)
        p = jnp.exp(s - jnp.max(s, axis=-1, keepdims=True))
        p = (p / jnp.sum(p, axis=-1, keepdims=True)).astype(BF16)
        heads.append(jnp.dot(p, vm, preferred_element_type=F32).astype(BF16))
    om = jnp.concatenate(heads, axis=1)
    merged = merged + gate_ref[:, 2 * D_MODEL:3 * D_MODEL].astype(F32) * jnp.dot(
        om, pmem_ref[...], preferred_element_type=F32)

    y = jnp.dot(merged.astype(BF16), wout_ref[...], preferred_element_type=F32)
    x1 = _layer_norm(ALPHA * xn_ref[...] + y, g1_ref[...], b1_ref[...])
    x1r_ref[...] = _rows_to_tiles(x1)

    lt = lax.dot_general(wr_ref[...], x1.astype(BF16), (((1,), (1,)), ((), ())),
                         preferred_element_type=F32)
    gl = [lt[r:r + 1, :] for r in range(N_EXPERT_GROUPS)]
    gmax = gl[0]
    grp = jnp.zeros((1, tm), I32)
    for r in range(1, N_EXPERT_GROUPS):
        better = gl[r] > gmax
        grp = jnp.where(better, r, grp)
        gmax = jnp.where(better, gl[r], gmax)
    gsum = gl[0] * 0.0
    for r in range(N_EXPERT_GROUPS):
        gsum = gsum + jnp.exp(gl[r] - gmax)
    gp = 1.0 / gsum
    sel = jnp.zeros((EXPERTS_PER_GROUP, tm), F32)
    for r in range(N_EXPERT_GROUPS):
        rows = lt[SUBLANES + r * EXPERTS_PER_GROUP:SUBLANES + (r + 1) * EXPERTS_PER_GROUP, :]
        sel = jnp.where(grp == r, rows, sel)
    ridx = lax.broadcasted_iota(I32, (EXPERTS_PER_GROUP, tm), 0)
    top1 = jnp.max(sel, axis=0, keepdims=True)
    i1 = jnp.min(jnp.where(sel == top1, ridx, EXPERTS_PER_GROUP), axis=0, keepdims=True)
    rest = jnp.where(ridx == i1, -jnp.inf, sel)
    top2 = jnp.max(rest, axis=0, keepdims=True)
    i2 = jnp.min(jnp.where(rest == top2, ridx, EXPERTS_PER_GROUP), axis=0, keepdims=True)
    e2 = jnp.exp(top2 - top1)
    inv = gp / (1.0 + e2)
    eid_ref[0:1, :] = grp * EXPERTS_PER_GROUP + i1
    eid_ref[1:2, :] = grp * EXPERTS_PER_GROUP + i2
    wts_ref[0:1, :] = inv
    wts_ref[1:2, :] = e2 * inv


def _mix_call(xn, u, o, qm, gate, memkv, p, batch, seq):
    tm = TM_MIX
    n_tiles = seq // tm
    t = batch * seq
    hb = tm // POOL_HALO

    row = lambda width: pl.BlockSpec((tm, width), lambda b, i: (b * n_tiles + i, 0))
    prev = pl.BlockSpec((POOL_HALO, POOL_WIDTH),
                        lambda b, i: (jnp.maximum((b * n_tiles + i) * hb - 1, 0), 0))
    nxt = pl.BlockSpec((POOL_HALO, POOL_WIDTH),
                       lambda b, i: (jnp.minimum((b * n_tiles + i + 1) * hb, t // POOL_HALO - 1), 0))
    full = lambda a: pl.BlockSpec(a.shape, lambda b, i: (0,) * a.ndim)
    lane_row = pl.BlockSpec((TOP_K, tm), lambda b, i: (0, b * n_tiles + i))
    weights = (p["w_pool"], p["pool_scale"], p["p_pool"], p["p_attn"], p["p_mem"], p["w_out"],
               p["ln1_g"], p["ln1_b"], p["w_router"])
    return pl.pallas_call(
        functools.partial(_mix_kernel, seq=seq),
        grid=(batch, n_tiles),
        in_specs=[row(D_MODEL), row(POOL_WIDTH), prev, nxt, row(ATTN_WIDTH), row(MEM_WIDTH),
                  row(N_BRANCHES * D_MODEL),
                  pl.BlockSpec((MEM_TOKENS, 2 * MEM_WIDTH), lambda b, i: (b, 0))]
                 + [full(w) for w in weights],
        out_specs=[pl.BlockSpec((tm, ROW_TILE, LANES), lambda b, i: (b * n_tiles + i, 0, 0)),
                   lane_row, lane_row],
        out_shape=[jax.ShapeDtypeStruct((t, ROW_TILE, LANES), F32),
                   jax.ShapeDtypeStruct((TOP_K, t), I32),
                   jax.ShapeDtypeStruct((TOP_K, t), F32)],
        scratch_shapes=[pltpu.VMEM((tm + 2 * POOL_HALO, POOL_WIDTH), F32)],
        compiler_params=_params(dimension_semantics=("arbitrary", "arbitrary")),
        name="mix",
    )(xn, u, u, u, o, qm, gate, memkv, *weights)


def _moe_kernel(iblk_ref, ie_ref, inext_ref, irun_ref, ilo_ref, ihi_ref,
                x_ref, ws_ref, wgu_hbm, wd_hbm, y_ref,
                wgu_f32, wd_f32, wgu_bf, wd_bf, wsem):
    i = pl.program_id(0)
    bm = x_ref.shape[0]
    lo = ilo_ref[i]
    hi = ihi_ref[i]
    nonempty = hi > lo

    def weight_copies(e, ws):
        return (pltpu.make_async_copy(wgu_hbm.at[e], wgu_f32.at[ws], wsem.at[ws]),
                pltpu.make_async_copy(wd_hbm.at[e], wd_f32.at[ws], wsem.at[ws]))

    @pl.when(jnp.logical_or(i == 0, ie_ref[i] != ie_ref[jnp.maximum(i - 1, 0)]))
    def _():
        e = ie_ref[i]
        wslot = irun_ref[i] & 1

        @pl.when(i == 0)
        def _():
            for c in weight_copies(e, wslot):
                c.start()
        for c in weight_copies(e, wslot):
            c.wait()

        @pl.when(inext_ref[i] >= 0)
        def _():
            for c in weight_copies(inext_ref[i], 1 - wslot):
                c.start()
        wgu_bf[...] = wgu_f32[wslot].astype(BF16)
        wd_bf[...] = wd_f32[wslot].astype(BF16)

    @pl.when(nonempty)
    def _():
        x = _tiles_to_rows(x_ref[...]).astype(BF16)
        gu = jnp.dot(x, wgu_bf[...], preferred_element_type=F32)
        gate = gu[:, :EXPERT_HIDDEN]
        hid = ((gate * _sigmoid(gate)) * gu[:, EXPERT_HIDDEN:]).astype(BF16)
        rows = lax.broadcasted_iota(I32, (bm, 1), 0)
        mine = jnp.logical_and(rows >= lo, rows < hi)
        w_col = jnp.transpose(jnp.broadcast_to(ws_ref[0], (LANES, bm)))[:, 0:1]
        y = jnp.dot(hid, wd_bf[...], preferred_element_type=F32) * w_col
        y = _rows_to_tiles(jnp.where(mine, y, 0.0))

        @pl.when(lo == 0)
        def _():
            y_ref[...] = y

        @pl.when(lo > 0)
        def _():
            y_ref[...] = y_ref[...] + y


def _moe_call(xs, wsorted, item_blk, item_e, item_lo, item_hi, w_gu, w_down):
    a = xs.shape[0]
    bm = BM_MOE
    n_items = item_blk.shape[0]
    change = jnp.concatenate([jnp.zeros((1,), I32), (item_e[1:] != item_e[:-1]).astype(I32)])
    item_run = jnp.cumsum(change).astype(I32)
    first_later = jnp.sum((item_e[None, :] <= item_e[:, None]).astype(I32), axis=1)
    item_next = jnp.where(first_later < n_items, item_e[jnp.minimum(first_later, n_items - 1)], -1).astype(I32)
    grid_spec = pltpu.PrefetchScalarGridSpec(
        num_scalar_prefetch=6,
        grid=(n_items,),
        in_specs=[pl.BlockSpec((bm, ROW_TILE, LANES), lambda i, blk, *_: (blk[i], 0, 0)),
                  pl.BlockSpec((1, 1, bm), lambda i, blk, *_: (blk[i], 0, 0)),
                  pl.BlockSpec(memory_space=pl.ANY),
                  pl.BlockSpec(memory_space=pl.ANY)],
        out_specs=pl.BlockSpec((bm, ROW_TILE, LANES), lambda i, blk, *_: (blk[i], 0, 0)),
        scratch_shapes=[pltpu.VMEM((2, D_MODEL, 2 * EXPERT_HIDDEN), F32),
                        pltpu.VMEM((2, EXPERT_HIDDEN, D_MODEL), F32),
                        pltpu.VMEM((D_MODEL, 2 * EXPERT_HIDDEN), BF16),
                        pltpu.VMEM((EXPERT_HIDDEN, D_MODEL), BF16),
                        pltpu.SemaphoreType.DMA((2,))],
    )
    return pl.pallas_call(
        _moe_kernel,
        grid_spec=grid_spec,
        out_shape=jax.ShapeDtypeStruct((a, ROW_TILE, LANES), F32),
        compiler_params=_params(dimension_semantics=("arbitrary",)),
        name="moe",
    )(item_blk, item_e, item_next, item_run, item_lo, item_hi, xs, wsorted, w_gu, w_down)


def _sc_mesh():
    return plsc.VectorSubcoreMesh(core_axis_name="core", subcore_axis_name="subcore")


def _sc_gather_rows(rows, idx):
    m = idx.shape[0]
    w = SC_ROW_WINDOW

    @pl.kernel(out_type=jax.ShapeDtypeStruct((m,) + rows.shape[1:], rows.dtype), mesh=_sc_mesh(),
               scratch_types=[])
    def gather(rows_hbm, idx_hbm, out_hbm):
        def body(idx_vmem, out_vmem):
            pltpu.sync_copy(rows_hbm.at[idx_vmem.at[0]], out_vmem)

        pltpu.emit_pipeline(
            body,
            grid=(m // w,),
            in_specs=[pl.BlockSpec((1, w), index_map=lambda i: (i, 0))],
            out_specs=[pl.BlockSpec((w,) + rows.shape[1:], index_map=lambda i: (i, 0, 0))],
            core_axis_name=("core", "subcore"),
            dimension_semantics=(pltpu.PARALLEL,),
        )(idx_hbm, out_hbm)

    return gather(rows, idx.reshape(m // w, w))


def _sc_scatter_rows(rows, idx):
    m = idx.shape[0]
    w = SC_ROW_WINDOW

    @pl.kernel(out_type=jax.ShapeDtypeStruct(rows.shape, rows.dtype), mesh=_sc_mesh(), scratch_types=[])
    def scatter(rows_hbm, idx_hbm, out_hbm):
        def body(rows_vmem, idx_vmem):
            pltpu.sync_copy(rows_vmem, out_hbm.at[idx_vmem.at[0]])

        pltpu.emit_pipeline(
            body,
            grid=(m // w,),
            in_specs=[pl.BlockSpec((w,) + rows.shape[1:], index_map=lambda i: (i, 0, 0)),
                      pl.BlockSpec((1, w), index_map=lambda i: (i, 0))],
            out_specs=[],
            core_axis_name=("core", "subcore"),
            dimension_semantics=(pltpu.PARALLEL,),
        )(rows_hbm, idx_hbm)

    return scatter(rows, idx.reshape(m // w, w))


def _final_kernel(x1_ref, ya_ref, yb_ref, g_ref, b_ref, out_ref):
    moe = ya_ref[...] + yb_ref[...]
    out_ref[...] = _layer_norm(_tiles_to_rows(ALPHA * x1_ref[...] + moe), g_ref[...], b_ref[...])


def _final_call(x1, y2, ln_g, ln_b):
    t = x1.shape[0]
    tm = TM_FINAL
    n_tiles = t // tm
    return pl.pallas_call(
        _final_kernel,
        grid=(n_tiles,),
        in_specs=[pl.BlockSpec((tm, ROW_TILE, LANES), lambda i: (i, 0, 0)),
                  pl.BlockSpec((tm, ROW_TILE, LANES), lambda i: (i, 0, 0)),
                  pl.BlockSpec((tm, ROW_TILE, LANES), lambda i: (n_tiles + i, 0, 0)),
                  pl.BlockSpec((1, D_MODEL), lambda i: (0, 0)),
                  pl.BlockSpec((1, D_MODEL), lambda i: (0, 0))],
        out_specs=pl.BlockSpec((tm, D_MODEL), lambda i: (i, 0)),
        out_shape=jax.ShapeDtypeStruct((t, D_MODEL), F32),
        compiler_params=_params(dimension_semantics=("arbitrary",)),
        name="final",
    )(x1, y2, y2, ln_g, ln_b)


def _t5_bucket(rel):
    nb = N_REL_BUCKETS // 2
    max_exact = nb // 2
    ret = jnp.where(rel > 0, nb, 0)
    n = jnp.abs(rel)
    nf = jnp.maximum(n, 1).astype(F32)
    large = max_exact + (jnp.log(nf / max_exact) / math.log(REL_MAX_DISTANCE / max_exact)
                         * (nb - max_exact)).astype(I32)
    large = jnp.minimum(large, nb - 1)
    return ret + jnp.where(n < max_exact, n, large)


def _bias_tables(rel_table):
    n, m = BLOCK, 3 * BLOCK
    rel = jnp.arange(-(2 * BLOCK - 1), 2 * BLOCK)
    by_rel = jnp.where((jnp.abs(rel) <= WINDOW)[:, None],
                       rel_table[_t5_bucket(rel)].astype(F32) * LOG2_E, NEG_INF)
    by_rel = jnp.pad(by_rel.T, ((0, 0), (0, 1)))
    skew = jnp.tile(by_rel, (1, n))[:, :n * (m + n - 1)].reshape(N_Q_HEADS, n, m + n - 1)
    bias = skew[:, :, n - 1:n - 1 + m]
    heads = [4 * h + half for h in range(N_KV_HEADS) for half in range(2)]
    units = jnp.concatenate([jnp.stack([bias[a] for a in heads]),
                             jnp.stack([bias[a + 2] for a in heads])], axis=1)
    masked = jnp.full((2 * N_KV_HEADS, 2 * BLOCK, BLOCK), NEG_INF, F32)
    prev = jnp.stack([units[:, :, :BLOCK], masked])
    cur = units[:, :, BLOCK:2 * BLOCK]
    nxt = jnp.stack([units[:, :, 2 * BLOCK:], masked])
    return prev, cur, nxt


def _prepare(rel_bias_table, w_in, b_in, w_pool, pool_scale, p_pool, sink, p_attn, w_mem_kv, p_mem,
             w_out, ln1_g, ln1_b, w_router_group, w_router_expert, w_gu, w_down):
    l = 0
    w_router = jnp.zeros((ROUTER_ROWS, D_MODEL), F32)
    w_router = w_router.at[:N_EXPERT_GROUPS].set(w_router_group[l].T)
    w_router = w_router.at[SUBLANES:].set(w_router_expert[l].T)
    bias_prev, bias_cur, bias_next = _bias_tables(rel_bias_table)
    return dict(
        w_in=w_in[l].astype(BF16), b_in=b_in[l][None, :],
        w_pool=w_pool[l].astype(BF16), pool_scale=pool_scale[l][None, :],
        p_pool=p_pool[l].astype(BF16), p_attn=p_attn[l].astype(BF16), p_mem=p_mem[l].astype(BF16),
        w_out=w_out[l].astype(BF16), w_memkv=w_mem_kv[l].astype(BF16),
        ln1_g=ln1_g[l][None, :], ln1_b=ln1_b[l][None, :],
        w_router=w_router.astype(BF16), sink=sink[l] * LOG2_E,
        w_gu=w_gu[l], w_down=w_down[l],
        bias_prev=bias_prev, bias_cur=bias_cur, bias_next=bias_next,
    )


def _dispatch_plan(eid, wts):
    t = eid.shape[1]
    a = TOP_K * t
    bm = BM_MOE
    nblk = a // bm
    eflat = eid.reshape(a)
    wflat = wts.reshape(a)
    shift = (a - 1).bit_length()
    assert N_EXPERTS << shift < 2 ** 31
    packed = jnp.sort(eflat * (1 << shift) + jnp.arange(a, dtype=I32))
    order = packed & ((1 << shift) - 1)
    wsorted = wflat[order].reshape(nblk, 1, bm)
    experts = jnp.arange(N_EXPERTS, dtype=I32)
    counts = jnp.sum((eflat[None, :] == experts[:, None]).astype(I32), axis=1)
    ends = jnp.cumsum(counts).astype(I32)
    starts = ends - counts
    cuts = jnp.sort(jnp.concatenate([jnp.arange(nblk, dtype=I32) * bm, starts, jnp.array([a], I32)]))
    lo_abs = cuts[:-1]
    hi_abs = cuts[1:]
    nonempty = hi_abs > lo_abs
    blk = jnp.minimum(lo_abs // bm, nblk - 1)
    probe = jnp.minimum(lo_abs, a - 1)
    item_e = jnp.sum((ends[None, :] <= probe[:, None]).astype(I32), axis=1)
    item_lo = jnp.where(nonempty, lo_abs - blk * bm, 0).astype(I32)
    item_hi = jnp.where(nonempty, hi_abs - blk * bm, 0).astype(I32)
    return order, wsorted, blk.astype(I32), item_e.astype(I32), item_lo, item_hi


def _trunk(x, mem, ln_in_g, ln_in_b, ln2_g, ln2_b, p):
    batch, seq, d = x.shape
    t = batch * seq
    assert seq % TM_MIX == 0 and seq % TQ_ATTN == 0 and t % TM_INPROJ == 0 and (TOP_K * t) % BM_MOE == 0
    memkv = _memkv_call(mem.reshape(batch * MEM_TOKENS, d), p["w_memkv"])
    xn, u, q, kk, vv, qm, gate = _inproj_call(x.reshape(t, d), ln_in_g[None, :], ln_in_b[None, :],
                                              p["w_in"], p["b_in"])
    o = _attn_call(q, kk, vv, p["sink"], p["bias_prev"], p["bias_cur"], p["bias_next"], batch, seq)
    x1r, eid, wts = _mix_call(xn, u, o, qm, gate, memkv, p, batch, seq)
    order, wsorted, item_blk, item_e, item_lo, item_hi = _dispatch_plan(eid, wts)
    sorted_tok = jnp.where(order >= t, order - t, order)
    xs = _sc_gather_rows(x1r, sorted_tok)
    ys = _moe_call(xs, wsorted, item_blk, item_e, item_lo, item_hi, p["w_gu"], p["w_down"])
    y2 = _sc_scatter_rows(ys, order)
    out = _final_call(x1r, y2, ln2_g[None, :], ln2_b[None, :])
    return out.reshape(batch, seq, d)


def kernel(x_prompt, x_sample, mem_prompt, mem_sample, ln_in_g, ln_in_b, rel_bias_table, w_in, b_in,
           w_pool, pool_scale, p_pool, sink, p_attn, w_mem_kv, p_mem, w_out, ln1_g, ln1_b,
           w_router_group, w_router_expert, w_gu, w_down, ln2_g, ln2_b):
    p = _prepare(rel_bias_table, w_in, b_in, w_pool, pool_scale, p_pool, sink, p_attn, w_mem_kv, p_mem,
                 w_out, ln1_g, ln1_b, w_router_group, w_router_expert, w_gu, w_down)
    y_prompt = _trunk(x_prompt, mem_prompt, ln_in_g, ln_in_b, ln2_g[0], ln2_b[0], p)
    y_sample = _trunk(x_sample, mem_sample, ln_in_g, ln_in_b, ln2_g[0], ln2_b[0], p)
    return (y_prompt, y_sample)
```

```python
import functools
import math

import jax
import jax.numpy as jnp
from jax import lax
from jax.experimental import pallas as pl
from jax.experimental.pallas import tpu as pltpu
from jax.experimental.pallas import tpu_sc as plsc

F32 = jnp.float32
BF16 = jnp.bfloat16
I32 = jnp.int32

D_MODEL = 1024
DEPTH = 1
POOL_WIDTH = 512
POOL_WINDOWS = (2, 4, 8, 16)
N_POOL_GROUPS = 4
POOL_GROUP_CH = POOL_WIDTH // N_POOL_GROUPS
N_Q_HEADS = 16
N_KV_HEADS = 4
HEAD_DIM = 64
ATTN_WIDTH = N_Q_HEADS * HEAD_DIM
KV_WIDTH = N_KV_HEADS * HEAD_DIM
WINDOW = 128
BLOCK = 128
N_REL_BUCKETS = 32
REL_MAX_DISTANCE = 128
MEM_TOKENS = 256
MEM_HEADS = 4
MEM_HEAD_DIM = 128
MEM_WIDTH = MEM_HEADS * MEM_HEAD_DIM
N_BRANCHES = 3
N_EXPERT_GROUPS = 4
EXPERTS_PER_GROUP = 8
N_EXPERTS = N_EXPERT_GROUPS * EXPERTS_PER_GROUP
TOP_K = 2
EXPERT_HIDDEN = 512
ALPHA = (2 * DEPTH) ** 0.25
LN_EPS = 1e-5
NEG_INF = -1e30
LOG2_E = math.log2(math.e)

LANES = 128
SUBLANES = 8
ROW_TILE = D_MODEL // LANES
VMEM_LIMIT_BYTES = 56 * 1024 * 1024

KV_DUP_WIDTH = N_KV_HEADS * LANES

C_U = 0
C_Q = C_U + POOL_WIDTH
C_K = C_Q + ATTN_WIDTH
C_V = C_K + KV_WIDTH
C_M = C_V + KV_WIDTH
C_G = C_M + MEM_WIDTH
C_END = C_G + N_BRANCHES * D_MODEL

TM_INPROJ = 512
TQ_ATTN = 512
TM_MIX = 512
BM_MOE = 512
SC_ROW_WINDOW = 32
TM_FINAL = 1024
POOL_HALO = 8
ROUTER_ROWS = 40


def _layer_norm(x, g, b):
    mu = jnp.mean(x, axis=-1, keepdims=True)
    xc = x - mu
    var = jnp.mean(xc * xc, axis=-1, keepdims=True)
    return xc * lax.rsqrt(var + LN_EPS) * g + b


def _sigmoid(x):
    return 1.0 / (1.0 + jnp.exp(-x))


def _rows_to_tiles(x):
    return pltpu.einshape("r(cl)->rcl", x, c=ROW_TILE)


def _tiles_to_rows(x):
    return pltpu.einshape("rcl->r(cl)", x)


def _params(**kw):
    return pltpu.CompilerParams(vmem_limit_bytes=VMEM_LIMIT_BYTES, **kw)


def _memkv_kernel(mem_ref, w_ref, out_ref):
    out_ref[...] = jnp.dot(mem_ref[...].astype(BF16), w_ref[...],
                           preferred_element_type=F32).astype(BF16)


def _memkv_call(mem2, w_memkv):
    rows = mem2.shape[0]
    return pl.pallas_call(
        _memkv_kernel,
        grid=(rows // MEM_TOKENS,),
        in_specs=[pl.BlockSpec((MEM_TOKENS, D_MODEL), lambda i: (i, 0)),
                  pl.BlockSpec((D_MODEL, 2 * MEM_WIDTH), lambda i: (0, 0))],
        out_specs=pl.BlockSpec((MEM_TOKENS, 2 * MEM_WIDTH), lambda i: (i, 0)),
        out_shape=jax.ShapeDtypeStruct((rows, 2 * MEM_WIDTH), BF16),
        compiler_params=_params(dimension_semantics=("arbitrary",)),
        name="memkv",
    )(mem2, w_memkv)


def _inproj_kernel(x_ref, g_ref, b_ref, w_ref, bias_ref,
                   xn_ref, u_ref, q_ref, k_ref, v_ref, qm_ref, gate_ref):
    xn = _layer_norm(x_ref[...], g_ref[...], b_ref[...])
    xn_ref[...] = xn
    xb = xn.astype(BF16)

    def seg(lo, hi):
        return jnp.dot(xb, w_ref[:, lo:hi], preferred_element_type=F32) + bias_ref[:, lo:hi]

    u_ref[...] = seg(C_U, C_Q)
    q_ref[...] = (seg(C_Q, C_K) * (HEAD_DIM ** -0.5 * LOG2_E)).astype(BF16)
    kv = seg(C_K, C_M)
    low = lax.broadcasted_iota(I32, (kv.shape[0], LANES), 1) < HEAD_DIM
    for out_ref, c_lo in ((k_ref, 0), (v_ref, KV_WIDTH)):
        for c in range(KV_WIDTH // LANES):
            blk = kv[:, c_lo + c * LANES:c_lo + (c + 1) * LANES]
            rot = pltpu.roll(blk, HEAD_DIM, 1)
            out_ref[:, 2 * c * LANES:(2 * c + 1) * LANES] = jnp.where(low, blk, rot).astype(BF16)
            out_ref[:, (2 * c + 1) * LANES:(2 * c + 2) * LANES] = jnp.where(low, rot, blk).astype(BF16)
    qm_ref[...] = seg(C_M, C_G).astype(BF16)
    for j in range(N_BRANCHES):
        lo = C_G + j * D_MODEL
        gate_ref[:, j * D_MODEL:(j + 1) * D_MODEL] = _sigmoid(seg(lo, lo + D_MODEL)).astype(BF16)


def _inproj_call(x2, ln_g, ln_b, w_in, b_in):
    t = x2.shape[0]
    tm = TM_INPROJ
    row = lambda width: pl.BlockSpec((tm, width), lambda i: (i, 0))
    const = lambda shape: pl.BlockSpec(shape, lambda i: (0, 0))
    widths = (D_MODEL, POOL_WIDTH, ATTN_WIDTH, KV_DUP_WIDTH, KV_DUP_WIDTH, MEM_WIDTH, N_BRANCHES * D_MODEL)
    dtypes = (F32, F32, BF16, BF16, BF16, BF16, BF16)
    return pl.pallas_call(
        _inproj_kernel,
        grid=(t // tm,),
        in_specs=[row(D_MODEL), const((1, D_MODEL)), const((1, D_MODEL)),
                  const((D_MODEL, C_END)), const((1, C_END))],
        out_specs=[row(w) for w in widths],
        out_shape=[jax.ShapeDtypeStruct((t, w), d) for w, d in zip(widths, dtypes)],
        compiler_params=_params(dimension_semantics=("arbitrary",)),
        name="inproj",
    )(x2, ln_g, ln_b, w_in, b_in)


def _attn_kernel(sink_ref, q_ref, kc_ref, kp_ref, kn_ref, vc_ref, vp_ref, vn_ref,
                 bp_ref, bc_ref, bn_ref, o_ref, klo, khi, vlo, vhi, s_scr, p_scr, inv_scr):
    i = pl.program_id(1)
    n_tiles = pl.num_programs(1)
    tq = q_ref.shape[0]
    n_qb = tq // BLOCK

    lane = lax.broadcasted_iota(I32, (BLOCK, KV_DUP_WIDTH), 1)
    low = (lane & (LANES - 1)) < HEAD_DIM

    def put(dst_lo, dst_hi, r0, val):
        zero = jnp.zeros_like(val)
        dst_lo[r0:r0 + BLOCK, :] = jnp.where(low, val, zero)
        dst_hi[r0:r0 + BLOCK, :] = jnp.where(low, zero, val)

    put(klo, khi, 0, kp_ref[...])
    put(vlo, vhi, 0, vp_ref[...])
    for j in range(n_qb):
        put(klo, khi, (j + 1) * BLOCK, kc_ref[j * BLOCK:(j + 1) * BLOCK, :])
        put(vlo, vhi, (j + 1) * BLOCK, vc_ref[j * BLOCK:(j + 1) * BLOCK, :])
    put(klo, khi, (n_qb + 1) * BLOCK, kn_ref[...])
    put(vlo, vhi, (n_qb + 1) * BLOCK, vn_ref[...])

    first = (i == 0).astype(I32)
    last = (i == n_tiles - 1).astype(I32)

    def softmax_rows(s, sink):
        m = jnp.maximum(jnp.max(s, axis=-1, keepdims=True), sink)
        p = jnp.exp2(s - m)
        denom = jnp.sum(p, axis=-1, keepdims=True) + jnp.exp2(sink - m)
        return p.astype(BF16), jnp.broadcast_to(1.0 / denom, (s.shape[0], LANES))

    units = [(j, h, half) for j in range(n_qb) for h in range(N_KV_HEADS) for half in range(2)]
    always = i >= 0

    @pl.when(always)
    def _scores():
        for u, (j, h, half) in enumerate(units):
            r0 = j * BLOCK
            c0 = h * 2 * LANES
            pv = first if j == 0 else 0
            nv = last if j == n_qb - 1 else 0
            unit = h * 2 + half
            q_pairs = jnp.concatenate([q_ref[r0:r0 + BLOCK, c0:c0 + LANES],
                                       q_ref[r0:r0 + BLOCK, c0 + LANES:c0 + 2 * LANES]], axis=0)
            kx = (klo, khi)[half][r0:r0 + 3 * BLOCK, h * LANES:(h + 1) * LANES]
            s = lax.dot_general(q_pairs, kx, (((1,), (1,)), ((), ())), preferred_element_type=F32)
            bias = jnp.concatenate([bp_ref[pv, unit], bc_ref[unit], bn_ref[nv, unit]], axis=1)
            s_scr[u] = s + bias

    @pl.when(always)
    def _softmax():
        for u, (j, h, half) in enumerate(units):
            p_scr[u, :BLOCK], inv_scr[u, :BLOCK] = softmax_rows(s_scr[u, :BLOCK], sink_ref[4 * h + half])
            p_scr[u, BLOCK:], inv_scr[u, BLOCK:] = softmax_rows(s_scr[u, BLOCK:], sink_ref[4 * h + 2 + half])

    @pl.when(always)
    def _values():
        for j in range(n_qb):
            r0 = j * BLOCK
            for h in range(N_KV_HEADS):
                c0 = h * 2 * LANES
                acc = jnp.zeros((2 * BLOCK, LANES), F32)
                for half, vref in enumerate((vlo, vhi)):
                    u = units.index((j, h, half))
                    vx = vref[r0:r0 + 3 * BLOCK, h * LANES:(h + 1) * LANES]
                    acc = acc + jnp.dot(p_scr[u], vx, preferred_element_type=F32) * inv_scr[u]
                o_ref[r0:r0 + BLOCK, c0:c0 + LANES] = acc[:BLOCK].astype(BF16)
                o_ref[r0:r0 + BLOCK, c0 + LANES:c0 + 2 * LANES] = acc[BLOCK:].astype(BF16)


def _attn_call(q, kk, vv, sink, bias_prev, bias_cur, bias_next, batch, seq):
    tq = TQ_ATTN
    n_qb = tq // BLOCK
    nb = seq // BLOCK
    n_tiles = seq // tq
    t = batch * seq

    cur = lambda width: pl.BlockSpec((tq, width), lambda b, i: (b * n_tiles + i, 0))
    prev = pl.BlockSpec((BLOCK, KV_DUP_WIDTH), lambda b, i: (b * nb + jnp.maximum(i * n_qb - 1, 0), 0))
    nxt = pl.BlockSpec((BLOCK, KV_DUP_WIDTH), lambda b, i: (b * nb + jnp.minimum((i + 1) * n_qb, nb - 1), 0))
    full = lambda a: pl.BlockSpec(a.shape, lambda b, i: (0,) * a.ndim)
    ext = (n_qb + 2) * BLOCK
    n_units = n_qb * N_KV_HEADS * 2
    return pl.pallas_call(
        _attn_kernel,
        grid=(batch, n_tiles),
        in_specs=[pl.BlockSpec(memory_space=pltpu.SMEM),
                  cur(ATTN_WIDTH), cur(KV_DUP_WIDTH), prev, nxt, cur(KV_DUP_WIDTH), prev, nxt,
                  full(bias_prev), full(bias_cur), full(bias_next)],
        out_specs=cur(ATTN_WIDTH),
        out_shape=jax.ShapeDtypeStruct((t, ATTN_WIDTH), BF16),
        scratch_shapes=[pltpu.VMEM((ext, KV_DUP_WIDTH), BF16) for _ in range(4)]
                       + [pltpu.VMEM((n_units, 2 * BLOCK, 3 * BLOCK), F32),
                          pltpu.VMEM((n_units, 2 * BLOCK, 3 * BLOCK), BF16),
                          pltpu.VMEM((n_units, 2 * BLOCK, LANES), F32)],
        compiler_params=_params(dimension_semantics=("arbitrary", "arbitrary")),
        name="attn",
    )(sink, q, kk, kk, kk, vv, vv, vv, bias_prev, bias_cur, bias_next)


def _mix_kernel(xn_ref, u_ref, up_ref, un_ref, o_ref, qm_ref, gate_ref, memkv_ref,
                wpool_ref, pscale_ref, ppool_ref, pattn_ref, pmem_ref, wout_ref,
                g1_ref, b1_ref, wr_ref,
                x1r_ref, eid_ref, wts_ref, uext, *, seq):
    i = pl.program_id(1)
    n_tiles = pl.num_programs(1)
    tm = xn_ref.shape[0]

    halo = POOL_HALO
    uext[0:halo, :] = jnp.where(i > 0, up_ref[...], 0.0)
    uext[halo:halo + tm, :] = u_ref[...]
    uext[halo + tm:halo + tm + halo, :] = jnp.where(i < n_tiles - 1, un_ref[...], 0.0)
    pos = i * tm + lax.broadcasted_iota(I32, (tm, 1), 0)
    mixed = []
    for gi, win in enumerate(POOL_WINDOWS):
        c0 = gi * POOL_GROUP_CH
        half = win // 2
        total = jnp.zeros((tm, POOL_GROUP_CH), F32)
        for off in range(-half, half):
            total = total + uext[halo + off:halo + off + tm, c0:c0 + POOL_GROUP_CH]
        cnt = (jnp.minimum(pos + half, seq) - jnp.maximum(pos - half, 0)).astype(F32)
        pooled = total * (1.0 / cnt) - u_ref[:, c0:c0 + POOL_GROUP_CH]
        mixed.append(jnp.dot(pooled.astype(BF16), wpool_ref[gi], preferred_element_type=F32)
                     * pscale_ref[:, c0:c0 + POOL_GROUP_CH])
    mixed = jnp.concatenate(mixed, axis=1).astype(BF16)
    merged = gate_ref[:, 0:D_MODEL].astype(F32) * jnp.dot(
        mixed, ppool_ref[...], preferred_element_type=F32)

    merged = merged + gate_ref[:, D_MODEL:2 * D_MODEL].astype(F32) * jnp.dot(
        o_ref[...], pattn_ref[...], preferred_element_type=F32)

    heads = []
    for h in range(MEM_HEADS):
        c0 = h * MEM_HEAD_DIM
        km = memkv_ref[:, c0:c0 + MEM_HEAD_DIM]
        vm = memkv_ref[:, MEM_WIDTH + c0:MEM_WIDTH + c0 + MEM_HEAD_DIM]
        s = lax.dot_general(qm_ref[:, c0:c0 + MEM_HEAD_DIM], km, (((1,), (1,)), ((), ())),
                            preferred_element_type=F32) * (MEM_HEAD_DIM ** -0.5)
        p = jnp.exp(s - jnp.max(s, axis=-1, keepdims=True))
        p = (p * (1.0 / jnp.sum(p, axis=-1, keepdims=True))).astype(BF16)
        heads.append(jnp.dot(p, vm, preferred_element_type=F32).astype(BF16))
    om = jnp.concatenate(heads, axis=1)
    merged = merged + gate_ref[:, 2 * D_MODEL:3 * D_MODEL].astype(F32) * jnp.dot(
        om, pmem_ref[...], preferred_element_type=F32)

    y = jnp.dot(merged.astype(BF16), wout_ref[...], preferred_element_type=F32)
    x1 = _layer_norm(ALPHA * xn_ref[...] + y, g1_ref[...], b1_ref[...])
    x1r_ref[...] = _rows_to_tiles(x1)

    lt = lax.dot_general(wr_ref[...], x1.astype(BF16), (((1,), (1,)), ((), ())),
                         preferred_element_type=F32)
    gl = [lt[r:r + 1, :] for r in range(N_EXPERT_GROUPS)]
    gmax = gl[0]
    grp = jnp.zeros((1, tm), I32)
    for r in range(1, N_EXPERT_GROUPS):
        better = gl[r] > gmax
        grp = jnp.where(better, r, grp)
        gmax = jnp.where(better, gl[r], gmax)
    gsum = gl[0] * 0.0
    for r in range(N_EXPERT_GROUPS):
        gsum = gsum + jnp.exp(gl[r] - gmax)
    gp = 1.0 / gsum
    sel = jnp.zeros((EXPERTS_PER_GROUP, tm), F32)
    for r in range(N_EXPERT_GROUPS):
        rows = lt[SUBLANES + r * EXPERTS_PER_GROUP:SUBLANES + (r + 1) * EXPERTS_PER_GROUP, :]
        sel = jnp.where(grp == r, rows, sel)
    ridx = lax.broadcasted_iota(I32, (EXPERTS_PER_GROUP, tm), 0)
    top1 = jnp.max(sel, axis=0, keepdims=True)
    i1 = jnp.min(jnp.where(sel == top1, ridx, EXPERTS_PER_GROUP), axis=0, keepdims=True)
    rest = jnp.where(ridx == i1, -jnp.inf, sel)
    top2 = jnp.max(rest, axis=0, keepdims=True)
    i2 = jnp.min(jnp.where(rest == top2, ridx, EXPERTS_PER_GROUP), axis=0, keepdims=True)
    e2 = jnp.exp(top2 - top1)
    inv = gp / (1.0 + e2)
    eid_ref[0:1, :] = grp * EXPERTS_PER_GROUP + i1
    eid_ref[1:2, :] = grp * EXPERTS_PER_GROUP + i2
    wts_ref[0:1, :] = inv
    wts_ref[1:2, :] = e2 * inv


def _mix_call(xn, u, o, qm, gate, memkv, p, batch, seq):
    tm = TM_MIX
    n_tiles = seq // tm
    t = batch * seq
    hb = tm // POOL_HALO

    row = lambda width: pl.BlockSpec((tm, width), lambda b, i: (b * n_tiles + i, 0))
    prev = pl.BlockSpec((POOL_HALO, POOL_WIDTH),
                        lambda b, i: (jnp.maximum((b * n_tiles + i) * hb - 1, 0), 0))
    nxt = pl.BlockSpec((POOL_HALO, POOL_WIDTH),
                       lambda b, i: (jnp.minimum((b * n_tiles + i + 1) * hb, t // POOL_HALO - 1), 0))
    full = lambda a: pl.BlockSpec(a.shape, lambda b, i: (0,) * a.ndim)
    lane_row = pl.BlockSpec((TOP_K, tm), lambda b, i: (0, b * n_tiles + i))
    weights = (p["w_pool"], p["pool_scale"], p["p_pool"], p["p_attn"], p["p_mem"], p["w_out"],
               p["ln1_g"], p["ln1_b"], p["w_router"])
    return pl.pallas_call(
        functools.partial(_mix_kernel, seq=seq),
        grid=(batch, n_tiles),
        in_specs=[row(D_MODEL), row(POOL_WIDTH), prev, nxt, row(ATTN_WIDTH), row(MEM_WIDTH),
                  row(N_BRANCHES * D_MODEL),
                  pl.BlockSpec((MEM_TOKENS, 2 * MEM_WIDTH), lambda b, i: (b, 0))]
                 + [full(w) for w in weights],
        out_specs=[pl.BlockSpec((tm, ROW_TILE, LANES), lambda b, i: (b * n_tiles + i, 0, 0)),
                   lane_row, lane_row],
        out_shape=[jax.ShapeDtypeStruct((t, ROW_TILE, LANES), F32),
                   jax.ShapeDtypeStruct((TOP_K, t), I32),
                   jax.ShapeDtypeStruct((TOP_K, t), F32)],
        scratch_shapes=[pltpu.VMEM((tm + 2 * POOL_HALO, POOL_WIDTH), F32)],
        compiler_params=_params(dimension_semantics=("arbitrary", "arbitrary")),
        name="mix",
    )(xn, u, u, u, o, qm, gate, memkv, *weights)


def _moe_kernel(iblk_ref, ie_ref, inext_ref, irun_ref, ilo_ref, ihi_ref,
                x_ref, ws_ref, wgu_hbm, wd_hbm, y_ref,
                x_bf, h_bf, wgu_f32, wd_f32, wgu_bf, wd_bf, wsem):
    i = pl.program_id(0)
    bm = x_ref.shape[0]
    lo = ilo_ref[i]
    hi = ihi_ref[i]
    nonempty = hi > lo

    def weight_copies(e, ws):
        return (pltpu.make_async_copy(wgu_hbm.at[e], wgu_f32.at[ws], wsem.at[ws]),
                pltpu.make_async_copy(wd_hbm.at[e], wd_f32.at[ws], wsem.at[ws]))

    @pl.when(jnp.logical_or(i == 0, ie_ref[i] != ie_ref[jnp.maximum(i - 1, 0)]))
    def _():
        e = ie_ref[i]
        wslot = irun_ref[i] & 1

        @pl.when(i == 0)
        def _():
            for c in weight_copies(e, wslot):
                c.start()
        for c in weight_copies(e, wslot):
            c.wait()

        @pl.when(inext_ref[i] >= 0)
        def _():
            for c in weight_copies(inext_ref[i], 1 - wslot):
                c.start()
        wgu_bf[...] = wgu_f32[wslot].astype(BF16)
        wd_bf[...] = wd_f32[wslot].astype(BF16)

    @pl.when(nonempty)
    def _():
        x_bf[...] = _tiles_to_rows(x_ref[...]).astype(BF16)

    @pl.when(nonempty)
    def _():
        gu = jnp.dot(x_bf[...], wgu_bf[...], preferred_element_type=F32)
        gate = gu[:, :EXPERT_HIDDEN]
        h_bf[...] = ((gate * _sigmoid(gate)) * gu[:, EXPERT_HIDDEN:]).astype(BF16)

    @pl.when(nonempty)
    def _():
        rows = lax.broadcasted_iota(I32, (bm, 1), 0)
        mine = jnp.logical_and(rows >= lo, rows < hi)
        w_col = jnp.transpose(jnp.broadcast_to(ws_ref[0], (LANES, bm)))[:, 0:1]
        y = jnp.dot(h_bf[...], wd_bf[...], preferred_element_type=F32) * w_col
        y = _rows_to_tiles(jnp.where(mine, y, 0.0))

        @pl.when(lo == 0)
        def _():
            y_ref[...] = y

        @pl.when(lo > 0)
        def _():
            y_ref[...] = y_ref[...] + y


def _moe_call(xs, wsorted, item_blk, item_e, item_lo, item_hi, w_gu, w_down):
    a = xs.shape[0]
    bm = BM_MOE
    n_items = item_blk.shape[0]
    change = jnp.concatenate([jnp.zeros((1,), I32), (item_e[1:] != item_e[:-1]).astype(I32)])
    item_run = jnp.cumsum(change).astype(I32)
    first_later = jnp.sum((item_e[None, :] <= item_e[:, None]).astype(I32), axis=1)
    item_next = jnp.where(first_later < n_items, item_e[jnp.minimum(first_later, n_items - 1)], -1).astype(I32)
    grid_spec = pltpu.PrefetchScalarGridSpec(
        num_scalar_prefetch=6,
        grid=(n_items,),
        in_specs=[pl.BlockSpec((bm, ROW_TILE, LANES), lambda i, blk, *_: (blk[i], 0, 0)),
                  pl.BlockSpec((1, 1, bm), lambda i, blk, *_: (blk[i], 0, 0)),
                  pl.BlockSpec(memory_space=pl.ANY),
                  pl.BlockSpec(memory_space=pl.ANY)],
        out_specs=pl.BlockSpec((bm, ROW_TILE, LANES), lambda i, blk, *_: (blk[i], 0, 0)),
        scratch_shapes=[pltpu.VMEM((bm, D_MODEL), BF16),
                        pltpu.VMEM((bm, EXPERT_HIDDEN), BF16),
                        pltpu.VMEM((2, D_MODEL, 2 * EXPERT_HIDDEN), F32),
                        pltpu.VMEM((2, EXPERT_HIDDEN, D_MODEL), F32),
                        pltpu.VMEM((D_MODEL, 2 * EXPERT_HIDDEN), BF16),
                        pltpu.VMEM((EXPERT_HIDDEN, D_MODEL), BF16),
                        pltpu.SemaphoreType.DMA((2,))],
    )
    return pl.pallas_call(
        _moe_kernel,
        grid_spec=grid_spec,
        out_shape=jax.ShapeDtypeStruct((a, ROW_TILE, LANES), F32),
        compiler_params=_params(dimension_semantics=("arbitrary",)),
        name="moe",
    )(item_blk, item_e, item_next, item_run, item_lo, item_hi, xs, wsorted, w_gu, w_down)


def _sc_mesh():
    return plsc.VectorSubcoreMesh(core_axis_name="core", subcore_axis_name="subcore")


def _sc_gather_rows(rows, idx):
    m = idx.shape[0]
    w = SC_ROW_WINDOW

    @pl.kernel(out_type=jax.ShapeDtypeStruct((m,) + rows.shape[1:], rows.dtype), mesh=_sc_mesh(),
               scratch_types=[])
    def gather(rows_hbm, idx_hbm, out_hbm):
        def body(idx_vmem, out_vmem):
            pltpu.sync_copy(rows_hbm.at[idx_vmem.at[0]], out_vmem)

        pltpu.emit_pipeline(
            body,
            grid=(m // w,),
            in_specs=[pl.BlockSpec((1, w), index_map=lambda i: (i, 0))],
            out_specs=[pl.BlockSpec((w,) + rows.shape[1:], index_map=lambda i: (i, 0, 0))],
            core_axis_name=("core", "subcore"),
            dimension_semantics=(pltpu.PARALLEL,),
        )(idx_hbm, out_hbm)

    return gather(rows, idx.reshape(m // w, w))


def _sc_scatter_rows(rows, idx):
    m = idx.shape[0]
    w = SC_ROW_WINDOW

    @pl.kernel(out_type=jax.ShapeDtypeStruct(rows.shape, rows.dtype), mesh=_sc_mesh(), scratch_types=[])
    def scatter(rows_hbm, idx_hbm, out_hbm):
        def body(rows_vmem, idx_vmem):
            pltpu.sync_copy(rows_vmem, out_hbm.at[idx_vmem.at[0]])

        pltpu.emit_pipeline(
            body,
            grid=(m // w,),
            in_specs=[pl.BlockSpec((w,) + rows.shape[1:], index_map=lambda i: (i, 0, 0)),
                      pl.BlockSpec((1, w), index_map=lambda i: (i, 0))],
            out_specs=[],
            core_axis_name=("core", "subcore"),
            dimension_semantics=(pltpu.PARALLEL,),
        )(rows_hbm, idx_hbm)

    return scatter(rows, idx.reshape(m // w, w))


def _final_kernel(x1_ref, ya_ref, yb_ref, g_ref, b_ref, out_ref):
    moe = ya_ref[...] + yb_ref[...]
    out_ref[...] = _layer_norm(_tiles_to_rows(ALPHA * x1_ref[...] + moe), g_ref[...], b_ref[...])


def _final_call(x1, y2, ln_g, ln_b):
    t = x1.shape[0]
    tm = TM_FINAL
    n_tiles = t // tm
    return pl.pallas_call(
        _final_kernel,
        grid=(n_tiles,),
        in_specs=[pl.BlockSpec((tm, ROW_TILE, LANES), lambda i: (i, 0, 0)),
                  pl.BlockSpec((tm, ROW_TILE, LANES), lambda i: (i, 0, 0)),
                  pl.BlockSpec((tm, ROW_TILE, LANES), lambda i: (n_tiles + i, 0, 0)),
                  pl.BlockSpec((1, D_MODEL), lambda i: (0, 0)),
                  pl.BlockSpec((1, D_MODEL), lambda i: (0, 0))],
        out_specs=pl.BlockSpec((tm, D_MODEL), lambda i: (i, 0)),
        out_shape=jax.ShapeDtypeStruct((t, D_MODEL), F32),
        compiler_params=_params(dimension_semantics=("arbitrary",)),
        name="final",
    )(x1, y2, y2, ln_g, ln_b)


def _t5_bucket(rel):
    nb = N_REL_BUCKETS // 2
    max_exact = nb // 2
    ret = jnp.where(rel > 0, nb, 0)
    n = jnp.abs(rel)
    nf = jnp.maximum(n, 1).astype(F32)
    large = max_exact + (jnp.log(nf / max_exact) / math.log(REL_MAX_DISTANCE / max_exact)
                         * (nb - max_exact)).astype(I32)
    large = jnp.minimum(large, nb - 1)
    return ret + jnp.where(n < max_exact, n, large)


def _bias_tables(rel_table):
    n, m = BLOCK, 3 * BLOCK
    rel = jnp.arange(-(2 * BLOCK - 1), 2 * BLOCK)
    by_rel = jnp.where((jnp.abs(rel) <= WINDOW)[:, None],
                       rel_table[_t5_bucket(rel)].astype(F32) * LOG2_E, NEG_INF)
    by_rel = jnp.pad(by_rel.T, ((0, 0), (0, 1)))
    skew = jnp.tile(by_rel, (1, n))[:, :n * (m + n - 1)].reshape(N_Q_HEADS, n, m + n - 1)
    bias = skew[:, :, n - 1:n - 1 + m]
    heads = [4 * h + half for h in range(N_KV_HEADS) for half in range(2)]
    units = jnp.concatenate([jnp.stack([bias[a] for a in heads]),
                             jnp.stack([bias[a + 2] for a in heads])], axis=1)
    masked = jnp.full((2 * N_KV_HEADS, 2 * BLOCK, BLOCK), NEG_INF, F32)
    prev = jnp.stack([units[:, :, :BLOCK], masked])
    cur = units[:, :, BLOCK:2 * BLOCK]
    nxt = jnp.stack([units[:, :, 2 * BLOCK:], masked])
    return prev, cur, nxt


def _prepare(rel_bias_table, w_in, b_in, w_pool, pool_scale, p_pool, sink, p_attn, w_mem_kv, p_mem,
             w_out, ln1_g, ln1_b, w_router_group, w_router_expert, w_gu, w_down):
    l = 0
    w_router = jnp.zeros((ROUTER_ROWS, D_MODEL), F32)
    w_router = w_router.at[:N_EXPERT_GROUPS].set(w_router_group[l].T)
    w_router = w_router.at[SUBLANES:].set(w_router_expert[l].T)
    bias_prev, bias_cur, bias_next = _bias_tables(rel_bias_table)
    return dict(
        w_in=w_in[l].astype(BF16), b_in=b_in[l][None, :],
        w_pool=w_pool[l].astype(BF16), pool_scale=pool_scale[l][None, :],
        p_pool=p_pool[l].astype(BF16), p_attn=p_attn[l].astype(BF16), p_mem=p_mem[l].astype(BF16),
        w_out=w_out[l].astype(BF16), w_memkv=w_mem_kv[l].astype(BF16),
        ln1_g=ln1_g[l][None, :], ln1_b=ln1_b[l][None, :],
        w_router=w_router.astype(BF16), sink=sink[l] * LOG2_E,
        w_gu=w_gu[l], w_down=w_down[l],
        bias_prev=bias_prev, bias_cur=bias_cur, bias_next=bias_next,
    )


def _dispatch_plan(eid, wts):
    t = eid.shape[1]
    a = TOP_K * t
    bm = BM_MOE
    nblk = a // bm
    eflat = eid.reshape(a)
    wflat = wts.reshape(a)
    shift = (a - 1).bit_length()
    assert N_EXPERTS << shift < 2 ** 31
    packed = jnp.sort(eflat * (1 << shift) + jnp.arange(a, dtype=I32))
    order = packed & ((1 << shift) - 1)
    wsorted = wflat[order].reshape(nblk, 1, bm)
    experts = jnp.arange(N_EXPERTS, dtype=I32)
    counts = jnp.sum((eflat[None, :] == experts[:, None]).astype(I32), axis=1)
    ends = jnp.cumsum(counts).astype(I32)
    starts = ends - counts
    cuts = jnp.sort(jnp.concatenate([jnp.arange(nblk, dtype=I32) * bm, starts, jnp.array([a], I32)]))
    lo_abs = cuts[:-1]
    hi_abs = cuts[1:]
    nonempty = hi_abs > lo_abs
    blk = jnp.minimum(lo_abs // bm, nblk - 1)
    probe = jnp.minimum(lo_abs, a - 1)
    item_e = jnp.sum((ends[None, :] <= probe[:, None]).astype(I32), axis=1)
    item_lo = jnp.where(nonempty, lo_abs - blk * bm, 0).astype(I32)
    item_hi = jnp.where(nonempty, hi_abs - blk * bm, 0).astype(I32)
    return order, wsorted, blk.astype(I32), item_e.astype(I32), item_lo, item_hi


def _trunk(x, mem, ln_in_g, ln_in_b, ln2_g, ln2_b, p):
    batch, seq, d = x.shape
    t = batch * seq
    assert seq % TM_MIX == 0 and seq % TQ_ATTN == 0 and t % TM_INPROJ == 0 and (TOP_K * t) % BM_MOE == 0
    memkv = _memkv_call(mem.reshape(batch * MEM_TOKENS, d), p["w_memkv"])
    xn, u, q, kk, vv, qm, gate = _inproj_call(x.reshape(t, d), ln_in_g[None, :], ln_in_b[None, :],
                                              p["w_in"], p["b_in"])
    o = _attn_call(q, kk, vv, p["sink"], p["bias_prev"], p["bias_cur"], p["bias_next"], batch, seq)
    x1r, eid, wts = _mix_call(xn, u, o, qm, gate, memkv, p, batch, seq)
    order, wsorted, item_blk, item_e, item_lo, item_hi = _dispatch_plan(eid, wts)
    sorted_tok = jnp.where(order >= t, order - t, order)
    xs = _sc_gather_rows(x1r, sorted_tok)
    ys = _moe_call(xs, wsorted, item_blk, item_e, item_lo, item_hi, p["w_gu"], p["w_down"])
    y2 = _sc_scatter_rows(ys, order)
    out = _final_call(x1r, y2, ln2_g[None, :], ln2_b[None, :])
    return out.reshape(batch, seq, d)


def kernel(x_prompt, x_sample, mem_prompt, mem_sample, ln_in_g, ln_in_b, rel_bias_table, w_in, b_in,
           w_pool, pool_scale, p_pool, sink, p_attn, w_mem_kv, p_mem, w_out, ln1_g, ln1_b,
           w_router_group, w_router_expert, w_gu, w_down, ln2_g, ln2_b):
    p = _prepare(rel_bias_table, w_in, b_in, w_pool, pool_scale, p_pool, sink, p_attn, w_mem_kv, p_mem,
                 w_out, ln1_g, ln1_b, w_router_group, w_router_expert, w_gu, w_down)
    y_prompt = _trunk(x_prompt, mem_prompt, ln_in_g, ln_in_b, ln2_g[0], ln2_b[0], p)
    y_sample = _trunk(x_sample, mem_sample, ln_in_g, ln_in_b, ln2_g[0], ln2_b[0], p)
    return (y_prompt, y_sample)
```

```python
import functools
import math

import jax
import jax.numpy as jnp
from jax import lax
from jax.experimental import pallas as pl
from jax.experimental.pallas import tpu as pltpu
from jax.experimental.pallas import tpu_sc as plsc

F32 = jnp.float32
BF16 = jnp.bfloat16
I32 = jnp.int32

D_MODEL = 1024
DEPTH = 1
POOL_WIDTH = 512
POOL_WINDOWS = (2, 4, 8, 16)
N_POOL_GROUPS = 4
POOL_GROUP_CH = POOL_WIDTH // N_POOL_GROUPS
N_Q_HEADS = 16
N_KV_HEADS = 4
HEAD_DIM = 64
ATTN_WIDTH = N_Q_HEADS * HEAD_DIM
KV_WIDTH = N_KV_HEADS * HEAD_DIM
WINDOW = 128
BLOCK = 128
N_REL_BUCKETS = 32
REL_MAX_DISTANCE = 128
MEM_TOKENS = 256
MEM_HEADS = 4
MEM_HEAD_DIM = 128
MEM_WIDTH = MEM_HEADS * MEM_HEAD_DIM
N_BRANCHES = 3
N_EXPERT_GROUPS = 4
EXPERTS_PER_GROUP = 8
N_EXPERTS = N_EXPERT_GROUPS * EXPERTS_PER_GROUP
TOP_K = 2
EXPERT_HIDDEN = 512
ALPHA = (2 * DEPTH) ** 0.25
LN_EPS = 1e-5
NEG_INF = -1e30
LOG2_E = math.log2(math.e)

LANES = 128
SUBLANES = 8
ROW_TILE = D_MODEL // LANES
VMEM_LIMIT_BYTES = 56 * 1024 * 1024

KV_DUP_WIDTH = N_KV_HEADS * LANES

C_U = 0
C_Q = C_U + POOL_WIDTH
C_K = C_Q + ATTN_WIDTH
C_V = C_K + KV_WIDTH
C_M = C_V + KV_WIDTH
C_G = C_M + MEM_WIDTH
C_END = C_G + N_BRANCHES * D_MODEL

TM_INPROJ = 512
TQ_ATTN = 512
TM_MIX = 512
BM_MOE = 512
SC_ROW_WINDOW = 32
TM_FINAL = 1024
POOL_HALO = 8
POOL_PAD = 64
ROUTER_ROWS = 40


def _layer_norm(x, g, b):
    mu = jnp.mean(x, axis=-1, keepdims=True)
    xc = x - mu
    var = jnp.mean(xc * xc, axis=-1, keepdims=True)
    return xc * lax.rsqrt(var + LN_EPS) * g + b


def _sigmoid(x):
    return 1.0 / (1.0 + jnp.exp(-x))


def _rows_to_tiles(x):
    return pltpu.einshape("r(cl)->rcl", x, c=ROW_TILE)


def _tiles_to_rows(x):
    return pltpu.einshape("rcl->r(cl)", x)


def _params(**kw):
    return pltpu.CompilerParams(vmem_limit_bytes=VMEM_LIMIT_BYTES, **kw)


def _memkv_kernel(mem_ref, w_ref, out_ref):
    out_ref[...] = jnp.dot(mem_ref[...].astype(BF16), w_ref[...],
                           preferred_element_type=F32).astype(BF16)


def _memkv_call(mem2, w_memkv):
    rows = mem2.shape[0]
    return pl.pallas_call(
        _memkv_kernel,
        grid=(rows // MEM_TOKENS,),
        in_specs=[pl.BlockSpec((MEM_TOKENS, D_MODEL), lambda i: (i, 0)),
                  pl.BlockSpec((D_MODEL, 2 * MEM_WIDTH), lambda i: (0, 0))],
        out_specs=pl.BlockSpec((MEM_TOKENS, 2 * MEM_WIDTH), lambda i: (i, 0)),
        out_shape=jax.ShapeDtypeStruct((rows, 2 * MEM_WIDTH), BF16),
        compiler_params=_params(dimension_semantics=("arbitrary",)),
        name="memkv",
    )(mem2, w_memkv)


def _inproj_kernel(x_ref, g_ref, b_ref, w_ref, bias_ref,
                   xn_ref, u_ref, q_ref, k_ref, v_ref, qm_ref, gate_ref):
    xn = _layer_norm(x_ref[...], g_ref[...], b_ref[...])
    xn_ref[...] = xn
    xb = xn.astype(BF16)

    def seg(lo, hi):
        return jnp.dot(xb, w_ref[:, lo:hi], preferred_element_type=F32) + bias_ref[:, lo:hi]

    u_ref[...] = seg(C_U, C_Q)
    q_ref[...] = (seg(C_Q, C_K) * (HEAD_DIM ** -0.5 * LOG2_E)).astype(BF16)
    kv = seg(C_K, C_M)
    low = lax.broadcasted_iota(I32, (kv.shape[0], LANES), 1) < HEAD_DIM
    for out_ref, c_lo in ((k_ref, 0), (v_ref, KV_WIDTH)):
        for c in range(KV_WIDTH // LANES):
            blk = kv[:, c_lo + c * LANES:c_lo + (c + 1) * LANES]
            rot = pltpu.roll(blk, HEAD_DIM, 1)
            out_ref[:, 2 * c * LANES:(2 * c + 1) * LANES] = jnp.where(low, blk, rot).astype(BF16)
            out_ref[:, (2 * c + 1) * LANES:(2 * c + 2) * LANES] = jnp.where(low, rot, blk).astype(BF16)
    qm_ref[...] = seg(C_M, C_G).astype(BF16)
    for j in range(N_BRANCHES):
        lo = C_G + j * D_MODEL
        gate_ref[:, j * D_MODEL:(j + 1) * D_MODEL] = _sigmoid(seg(lo, lo + D_MODEL)).astype(BF16)


def _inproj_call(x2, ln_g, ln_b, w_in, b_in):
    t = x2.shape[0]
    tm = TM_INPROJ
    row = lambda width: pl.BlockSpec((tm, width), lambda i: (i, 0))
    const = lambda shape: pl.BlockSpec(shape, lambda i: (0, 0))
    widths = (D_MODEL, POOL_WIDTH, ATTN_WIDTH, KV_DUP_WIDTH, KV_DUP_WIDTH, MEM_WIDTH, N_BRANCHES * D_MODEL)
    dtypes = (F32, F32, BF16, BF16, BF16, BF16, BF16)
    return pl.pallas_call(
        _inproj_kernel,
        grid=(t // tm,),
        in_specs=[row(D_MODEL), const((1, D_MODEL)), const((1, D_MODEL)),
                  const((D_MODEL, C_END)), const((1, C_END))],
        out_specs=[row(w) for w in widths],
        out_shape=[jax.ShapeDtypeStruct((t, w), d) for w, d in zip(widths, dtypes)],
        compiler_params=_params(dimension_semantics=("arbitrary",)),
        name="inproj",
    )(x2, ln_g, ln_b, w_in, b_in)


def _attn_kernel(sink_ref, q_ref, kc_ref, kp_ref, kn_ref, vc_ref, vp_ref, vn_ref,
                 bp_ref, bc_ref, bn_ref, o_ref, klo, khi, vlo, vhi, s_scr, p_scr, inv_scr):
    i = pl.program_id(1)
    n_tiles = pl.num_programs(1)
    tq = q_ref.shape[0]
    n_qb = tq // BLOCK

    lane = lax.broadcasted_iota(I32, (BLOCK, KV_DUP_WIDTH), 1)
    low = (lane & (LANES - 1)) < HEAD_DIM

    def put(dst_lo, dst_hi, r0, val):
        zero = jnp.zeros_like(val)
        dst_lo[r0:r0 + BLOCK, :] = jnp.where(low, val, zero)
        dst_hi[r0:r0 + BLOCK, :] = jnp.where(low, zero, val)

    put(klo, khi, 0, kp_ref[...])
    put(vlo, vhi, 0, vp_ref[...])
    for j in range(n_qb):
        put(klo, khi, (j + 1) * BLOCK, kc_ref[j * BLOCK:(j + 1) * BLOCK, :])
        put(vlo, vhi, (j + 1) * BLOCK, vc_ref[j * BLOCK:(j + 1) * BLOCK, :])
    put(klo, khi, (n_qb + 1) * BLOCK, kn_ref[...])
    put(vlo, vhi, (n_qb + 1) * BLOCK, vn_ref[...])

    first = (i == 0).astype(I32)
    last = (i == n_tiles - 1).astype(I32)

    def softmax_rows(s, sink):
        m = jnp.maximum(jnp.max(s, axis=-1, keepdims=True), sink)
        p = jnp.exp2(s - m)
        denom = jnp.sum(p, axis=-1, keepdims=True) + jnp.exp2(sink - m)
        return p.astype(BF16), jnp.broadcast_to(1.0 / denom, (s.shape[0], LANES))

    units = [(j, h, half) for j in range(n_qb) for h in range(N_KV_HEADS) for half in range(2)]
    always = i >= 0

    @pl.when(always)
    def _scores():
        for u, (j, h, half) in enumerate(units):
            r0 = j * BLOCK
            c0 = h * 2 * LANES
            pv = first if j == 0 else 0
            nv = last if j == n_qb - 1 else 0
            unit = h * 2 + half
            q_pairs = jnp.concatenate([q_ref[r0:r0 + BLOCK, c0:c0 + LANES],
                                       q_ref[r0:r0 + BLOCK, c0 + LANES:c0 + 2 * LANES]], axis=0)
            kx = (klo, khi)[half][r0:r0 + 3 * BLOCK, h * LANES:(h + 1) * LANES]
            s = lax.dot_general(q_pairs, kx, (((1,), (1,)), ((), ())), preferred_element_type=F32)
            bias = jnp.concatenate([bp_ref[pv, unit], bc_ref[unit], bn_ref[nv, unit]], axis=1)
            s_scr[u] = s + bias

    @pl.when(always)
    def _softmax():
        for u, (j, h, half) in enumerate(units):
            p_scr[u, :BLOCK], inv_scr[u, :BLOCK] = softmax_rows(s_scr[u, :BLOCK], sink_ref[4 * h + half])
            p_scr[u, BLOCK:], inv_scr[u, BLOCK:] = softmax_rows(s_scr[u, BLOCK:], sink_ref[4 * h + 2 + half])

    @pl.when(always)
    def _values():
        for j in range(n_qb):
            r0 = j * BLOCK
            for h in range(N_KV_HEADS):
                c0 = h * 2 * LANES
                acc = jnp.zeros((2 * BLOCK, LANES), F32)
                for half, vref in enumerate((vlo, vhi)):
                    u = units.index((j, h, half))
                    vx = vref[r0:r0 + 3 * BLOCK, h * LANES:(h + 1) * LANES]
                    acc = acc + jnp.dot(p_scr[u], vx, preferred_element_type=F32) * inv_scr[u]
                o_ref[r0:r0 + BLOCK, c0:c0 + LANES] = acc[:BLOCK].astype(BF16)
                o_ref[r0:r0 + BLOCK, c0 + LANES:c0 + 2 * LANES] = acc[BLOCK:].astype(BF16)


def _attn_call(q, kk, vv, sink, bias_prev, bias_cur, bias_next, batch, seq):
    tq = TQ_ATTN
    n_qb = tq // BLOCK
    nb = seq // BLOCK
    n_tiles = seq // tq
    t = batch * seq

    cur = lambda width: pl.BlockSpec((tq, width), lambda b, i: (b * n_tiles + i, 0))
    prev = pl.BlockSpec((BLOCK, KV_DUP_WIDTH), lambda b, i: (b * nb + jnp.maximum(i * n_qb - 1, 0), 0))
    nxt = pl.BlockSpec((BLOCK, KV_DUP_WIDTH), lambda b, i: (b * nb + jnp.minimum((i + 1) * n_qb, nb - 1), 0))
    full = lambda a: pl.BlockSpec(a.shape, lambda b, i: (0,) * a.ndim)
    ext = (n_qb + 2) * BLOCK
    n_units = n_qb * N_KV_HEADS * 2
    return pl.pallas_call(
        _attn_kernel,
        grid=(batch, n_tiles),
        in_specs=[pl.BlockSpec(memory_space=pltpu.SMEM),
                  cur(ATTN_WIDTH), cur(KV_DUP_WIDTH), prev, nxt, cur(KV_DUP_WIDTH), prev, nxt,
                  full(bias_prev), full(bias_cur), full(bias_next)],
        out_specs=cur(ATTN_WIDTH),
        out_shape=jax.ShapeDtypeStruct((t, ATTN_WIDTH), BF16),
        scratch_shapes=[pltpu.VMEM((ext, KV_DUP_WIDTH), BF16) for _ in range(4)]
                       + [pltpu.VMEM((n_units, 2 * BLOCK, 3 * BLOCK), F32),
                          pltpu.VMEM((n_units, 2 * BLOCK, 3 * BLOCK), BF16),
                          pltpu.VMEM((n_units, 2 * BLOCK, LANES), F32)],
        compiler_params=_params(dimension_semantics=("arbitrary", "arbitrary")),
        name="attn",
    )(sink, q, kk, kk, kk, vv, vv, vv, bias_prev, bias_cur, bias_next)


def _mix_kernel(xn_ref, u_ref, up_ref, un_ref, o_ref, qm_ref, gate_ref, memkv_ref,
                wpool_ref, pscale_ref, ppool_ref, pattn_ref, pmem_ref, wout_ref,
                g1_ref, b1_ref, wr_ref, band_ref,
                x1r_ref, eid_ref, wts_ref, uext, *, seq):
    i = pl.program_id(1)
    n_tiles = pl.num_programs(1)
    tm = xn_ref.shape[0]

    halo = POOL_HALO
    pad = POOL_PAD
    uext[0:pad - halo, :] = jnp.zeros((pad - halo, POOL_WIDTH), F32)
    uext[pad - halo:pad, :] = jnp.where(i > 0, up_ref[...], 0.0)
    uext[pad:pad + tm, :] = u_ref[...]
    uext[pad + tm:pad + tm + halo, :] = jnp.where(i < n_tiles - 1, un_ref[...], 0.0)
    uext[pad + tm + halo:, :] = jnp.zeros((pad - halo, POOL_WIDTH), F32)
    ue = uext[...]
    ue_hi = ue.astype(BF16)
    ue_lo = (ue - ue_hi.astype(F32)).astype(BF16)
    pos = i * tm + lax.broadcasted_iota(I32, (tm, 1), 0)
    mixed = []
    for gi, win in enumerate(POOL_WINDOWS):
        c0 = gi * POOL_GROUP_CH
        half = win // 2
        total = (jnp.dot(band_ref[gi], ue_hi[:, c0:c0 + POOL_GROUP_CH], preferred_element_type=F32)
                 + jnp.dot(band_ref[gi], ue_lo[:, c0:c0 + POOL_GROUP_CH], preferred_element_type=F32))
        cnt = (jnp.minimum(pos + half, seq) - jnp.maximum(pos - half, 0)).astype(F32)
        pooled = total * (1.0 / cnt) - u_ref[:, c0:c0 + POOL_GROUP_CH]
        mixed.append(jnp.dot(pooled.astype(BF16), wpool_ref[gi], preferred_element_type=F32)
                     * pscale_ref[:, c0:c0 + POOL_GROUP_CH])
    mixed = jnp.concatenate(mixed, axis=1).astype(BF16)
    merged = gate_ref[:, 0:D_MODEL].astype(F32) * jnp.dot(
        mixed, ppool_ref[...], preferred_element_type=F32)

    merged = merged + gate_ref[:, D_MODEL:2 * D_MODEL].astype(F32) * jnp.dot(
        o_ref[...], pattn_ref[...], preferred_element_type=F32)

    heads = []
    for h in range(MEM_HEADS):
        c0 = h * MEM_HEAD_DIM
        km = memkv_ref[:, c0:c0 + MEM_HEAD_DIM]
        vm = memkv_ref[:, MEM_WIDTH + c0:MEM_WIDTH + c0 + MEM_HEAD_DIM]
        s = lax.dot_general(qm_ref[:, c0:c0 + MEM_HEAD_DIM], km, (((1,), (1,)), ((), ())),
                            preferred_element_type=F32) * (MEM_HEAD_DIM ** -0.5)
        p = jnp.exp(s - jnp.max(s, axis=-1, keepdims=True))
        p = (p / jnp.sum(p, axis=-1, keepdims=True)).astype(BF16)
        heads.append(jnp.dot(p, vm, preferred_element_type=F32).astype(BF16))
    om = jnp.concatenate(heads, axis=1)
    merged = merged + gate_ref[:, 2 * D_MODEL:3 * D_MODEL].astype(F32) * jnp.dot(
        om, pmem_ref[...], preferred_element_type=F32)

    y = jnp.dot(merged.astype(BF16), wout_ref[...], preferred_element_type=F32)
    x1 = _layer_norm(ALPHA * xn_ref[...] + y, g1_ref[...], b1_ref[...])
    x1r_ref[...] = _rows_to_tiles(x1)

    lt = lax.dot_general(wr_ref[...], x1.astype(BF16), (((1,), (1,)), ((), ())),
                         preferred_element_type=F32)
    gl = [lt[r:r + 1, :] for r in range(N_EXPERT_GROUPS)]
    gmax = gl[0]
    grp = jnp.zeros((1, tm), I32)
    for r in range(1, N_EXPERT_GROUPS):
        better = gl[r] > gmax
        grp = jnp.where(better, r, grp)
        gmax = jnp.where(better, gl[r], gmax)
    gsum = gl[0] * 0.0
    for r in range(N_EXPERT_GROUPS):
        gsum = gsum + jnp.exp(gl[r] - gmax)
    gp = 1.0 / gsum
    sel = jnp.zeros((EXPERTS_PER_GROUP, tm), F32)
    for r in range(N_EXPERT_GROUPS):
        rows = lt[SUBLANES + r * EXPERTS_PER_GROUP:SUBLANES + (r + 1) * EXPERTS_PER_GROUP, :]
        sel = jnp.where(grp == r, rows, sel)
    ridx = lax.broadcasted_iota(I32, (EXPERTS_PER_GROUP, tm), 0)
    top1 = jnp.max(sel, axis=0, keepdims=True)
    i1 = jnp.min(jnp.where(sel == top1, ridx, EXPERTS_PER_GROUP), axis=0, keepdims=True)
    rest = jnp.where(ridx == i1, -jnp.inf, sel)
    top2 = jnp.max(rest, axis=0, keepdims=True)
    i2 = jnp.min(jnp.where(rest == top2, ridx, EXPERTS_PER_GROUP), axis=0, keepdims=True)
    e2 = jnp.exp(top2 - top1)
    inv = gp / (1.0 + e2)
    eid_ref[0:1, :] = grp * EXPERTS_PER_GROUP + i1
    eid_ref[1:2, :] = grp * EXPERTS_PER_GROUP + i2
    wts_ref[0:1, :] = inv
    wts_ref[1:2, :] = e2 * inv


def _mix_call(xn, u, o, qm, gate, memkv, p, batch, seq):
    tm = TM_MIX
    n_tiles = seq // tm
    t = batch * seq
    hb = tm // POOL_HALO

    row = lambda width: pl.BlockSpec((tm, width), lambda b, i: (b * n_tiles + i, 0))
    prev = pl.BlockSpec((POOL_HALO, POOL_WIDTH),
                        lambda b, i: (jnp.maximum((b * n_tiles + i) * hb - 1, 0), 0))
    nxt = pl.BlockSpec((POOL_HALO, POOL_WIDTH),
                       lambda b, i: (jnp.minimum((b * n_tiles + i + 1) * hb, t // POOL_HALO - 1), 0))
    full = lambda a: pl.BlockSpec(a.shape, lambda b, i: (0,) * a.ndim)
    lane_row = pl.BlockSpec((TOP_K, tm), lambda b, i: (0, b * n_tiles + i))
    col = jnp.arange(tm + 2 * POOL_PAD)[None, :] - POOL_PAD - jnp.arange(tm)[:, None]
    band = jnp.stack([((col >= -(w // 2)) & (col < w // 2)).astype(BF16) for w in POOL_WINDOWS])
    weights = (p["w_pool"], p["pool_scale"], p["p_pool"], p["p_attn"], p["p_mem"], p["w_out"],
               p["ln1_g"], p["ln1_b"], p["w_router"], band)
    return pl.pallas_call(
        functools.partial(_mix_kernel, seq=seq),
        grid=(batch, n_tiles),
        in_specs=[row(D_MODEL), row(POOL_WIDTH), prev, nxt, row(ATTN_WIDTH), row(MEM_WIDTH),
                  row(N_BRANCHES * D_MODEL),
                  pl.BlockSpec((MEM_TOKENS, 2 * MEM_WIDTH), lambda b, i: (b, 0))]
                 + [full(w) for w in weights],
        out_specs=[pl.BlockSpec((tm, ROW_TILE, LANES), lambda b, i: (b * n_tiles + i, 0, 0)),
                   lane_row, lane_row],
        out_shape=[jax.ShapeDtypeStruct((t, ROW_TILE, LANES), F32),
                   jax.ShapeDtypeStruct((TOP_K, t), I32),
                   jax.ShapeDtypeStruct((TOP_K, t), F32)],
        scratch_shapes=[pltpu.VMEM((tm + 2 * POOL_PAD, POOL_WIDTH), F32)],
        compiler_params=_params(dimension_semantics=("arbitrary", "arbitrary")),
        name="mix",
    )(xn, u, u, u, o, qm, gate, memkv, *weights)


def _moe_kernel(iblk_ref, ie_ref, inext_ref, irun_ref, ilo_ref, ihi_ref,
                x_ref, ws_ref, wgu_hbm, wd_hbm, y_ref,
                x_bf, h_bf, wgu_f32, wd_f32, wgu_bf, wd_bf, wsem):
    i = pl.program_id(0)
    bm = x_ref.shape[0]
    lo = ilo_ref[i]
    hi = ihi_ref[i]
    nonempty = hi > lo

    def weight_copies(e, ws):
        return (pltpu.make_async_copy(wgu_hbm.at[e], wgu_f32.at[ws], wsem.at[ws]),
                pltpu.make_async_copy(wd_hbm.at[e], wd_f32.at[ws], wsem.at[ws]))

    @pl.when(jnp.logical_or(i == 0, ie_ref[i] != ie_ref[jnp.maximum(i - 1, 0)]))
    def _():
        e = ie_ref[i]
        wslot = irun_ref[i] & 1

        @pl.when(i == 0)
        def _():
            for c in weight_copies(e, wslot):
                c.start()
        for c in weight_copies(e, wslot):
            c.wait()

        @pl.when(inext_ref[i] >= 0)
        def _():
            for c in weight_copies(inext_ref[i], 1 - wslot):
                c.start()
        wgu_bf[...] = wgu_f32[wslot].astype(BF16)
        wd_bf[...] = wd_f32[wslot].astype(BF16)

    @pl.when(nonempty)
    def _():
        x_bf[...] = _tiles_to_rows(x_ref[...]).astype(BF16)

    @pl.when(nonempty)
    def _():
        gu = jnp.dot(x_bf[...], wgu_bf[...], preferred_element_type=F32)
        gate = gu[:, :EXPERT_HIDDEN]
        h_bf[...] = ((gate * _sigmoid(gate)) * gu[:, EXPERT_HIDDEN:]).astype(BF16)

    @pl.when(nonempty)
    def _():
        rows = lax.broadcasted_iota(I32, (bm, 1), 0)
        mine = jnp.logical_and(rows >= lo, rows < hi)
        w_col = jnp.transpose(jnp.broadcast_to(ws_ref[0], (LANES, bm)))[:, 0:1]
        y = jnp.dot(h_bf[...], wd_bf[...], preferred_element_type=F32) * w_col
        y = _rows_to_tiles(jnp.where(mine, y, 0.0))

        @pl.when(lo == 0)
        def _():
            y_ref[...] = y

        @pl.when(lo > 0)
        def _():
            y_ref[...] = y_ref[...] + y


def _moe_call(xs, wsorted, item_blk, item_e, item_lo, item_hi, w_gu, w_down):
    a = xs.shape[0]
    bm = BM_MOE
    n_items = item_blk.shape[0]
    change = jnp.concatenate([jnp.zeros((1,), I32), (item_e[1:] != item_e[:-1]).astype(I32)])
    item_run = jnp.cumsum(change).astype(I32)
    first_later = jnp.sum((item_e[None, :] <= item_e[:, None]).astype(I32), axis=1)
    item_next = jnp.where(first_later < n_items, item_e[jnp.minimum(first_later, n_items - 1)], -1).astype(I32)
    grid_spec = pltpu.PrefetchScalarGridSpec(
        num_scalar_prefetch=6,
        grid=(n_items,),
        in_specs=[pl.BlockSpec((bm, ROW_TILE, LANES), lambda i, blk, *_: (blk[i], 0, 0)),
                  pl.BlockSpec((1, 1, bm), lambda i, blk, *_: (blk[i], 0, 0)),
                  pl.BlockSpec(memory_space=pl.ANY),
                  pl.BlockSpec(memory_space=pl.ANY)],
        out_specs=pl.BlockSpec((bm, ROW_TILE, LANES), lambda i, blk, *_: (blk[i], 0, 0)),
        scratch_shapes=[pltpu.VMEM((bm, D_MODEL), BF16),
                        pltpu.VMEM((bm, EXPERT_HIDDEN), BF16),
                        pltpu.VMEM((2, D_MODEL, 2 * EXPERT_HIDDEN), F32),
                        pltpu.VMEM((2, EXPERT_HIDDEN, D_MODEL), F32),
                        pltpu.VMEM((D_MODEL, 2 * EXPERT_HIDDEN), BF16),
                        pltpu.VMEM((EXPERT_HIDDEN, D_MODEL), BF16),
                        pltpu.SemaphoreType.DMA((2,))],
    )
    return pl.pallas_call(
        _moe_kernel,
        grid_spec=grid_spec,
        out_shape=jax.ShapeDtypeStruct((a, ROW_TILE, LANES), F32),
        compiler_params=_params(dimension_semantics=("arbitrary",)),
        name="moe",
    )(item_blk, item_e, item_next, item_run, item_lo, item_hi, xs, wsorted, w_gu, w_down)


def _sc_mesh():
    return plsc.VectorSubcoreMesh(core_axis_name="core", subcore_axis_name="subcore")


def _sc_gather_rows(rows, idx):
    m = idx.shape[0]
    w = SC_ROW_WINDOW

    @pl.kernel(out_type=jax.ShapeDtypeStruct((m,) + rows.shape[1:], rows.dtype), mesh=_sc_mesh(),
               scratch_types=[])
    def gather(rows_hbm, idx_hbm, out_hbm):
        def body(idx_vmem, out_vmem):
            pltpu.sync_copy(rows_hbm.at[idx_vmem.at[0]], out_vmem)

        pltpu.emit_pipeline(
            body,
            grid=(m // w,),
            in_specs=[pl.BlockSpec((1, w), index_map=lambda i: (i, 0))],
            out_specs=[pl.BlockSpec((w,) + rows.shape[1:], index_map=lambda i: (i, 0, 0))],
            core_axis_name=("core", "subcore"),
            dimension_semantics=(pltpu.PARALLEL,),
        )(idx_hbm, out_hbm)

    return gather(rows, idx.reshape(m // w, w))


def _sc_scatter_rows(rows, idx):
    m = idx.shape[0]
    w = SC_ROW_WINDOW

    @pl.kernel(out_type=jax.ShapeDtypeStruct(rows.shape, rows.dtype), mesh=_sc_mesh(), scratch_types=[])
    def scatter(rows_hbm, idx_hbm, out_hbm):
        def body(rows_vmem, idx_vmem):
            pltpu.sync_copy(rows_vmem, out_hbm.at[idx_vmem.at[0]])

        pltpu.emit_pipeline(
            body,
            grid=(m // w,),
            in_specs=[pl.BlockSpec((w,) + rows.shape[1:], index_map=lambda i: (i, 0, 0)),
                      pl.BlockSpec((1, w), index_map=lambda i: (i, 0))],
            out_specs=[],
            core_axis_name=("core", "subcore"),
            dimension_semantics=(pltpu.PARALLEL,),
        )(rows_hbm, idx_hbm)

    return scatter(rows, idx.reshape(m // w, w))


def _final_kernel(x1_ref, ya_ref, yb_ref, g_ref, b_ref, out_ref):
    moe = ya_ref[...] + yb_ref[...]
    out_ref[...] = _layer_norm(_tiles_to_rows(ALPHA * x1_ref[...] + moe), g_ref[...], b_ref[...])


def _final_call(x1, y2, ln_g, ln_b):
    t = x1.shape[0]
    tm = TM_FINAL
    n_tiles = t // tm
    return pl.pallas_call(
        _final_kernel,
        grid=(n_tiles,),
        in_specs=[pl.BlockSpec((tm, ROW_TILE, LANES), lambda i: (i, 0, 0)),
                  pl.BlockSpec((tm, ROW_TILE, LANES), lambda i: (i, 0, 0)),
                  pl.BlockSpec((tm, ROW_TILE, LANES), lambda i: (n_tiles + i, 0, 0)),
                  pl.BlockSpec((1, D_MODEL), lambda i: (0, 0)),
                  pl.BlockSpec((1, D_MODEL), lambda i: (0, 0))],
        out_specs=pl.BlockSpec((tm, D_MODEL), lambda i: (i, 0)),
        out_shape=jax.ShapeDtypeStruct((t, D_MODEL), F32),
        compiler_params=_params(dimension_semantics=("arbitrary",)),
        name="final",
    )(x1, y2, y2, ln_g, ln_b)


def _t5_bucket(rel):
    nb = N_REL_BUCKETS // 2
    max_exact = nb // 2
    ret = jnp.where(rel > 0, nb, 0)
    n = jnp.abs(rel)
    nf = jnp.maximum(n, 1).astype(F32)
    large = max_exact + (jnp.log(nf / max_exact) / math.log(REL_MAX_DISTANCE / max_exact)
                         * (nb - max_exact)).astype(I32)
    large = jnp.minimum(large, nb - 1)
    return ret + jnp.where(n < max_exact, n, large)


def _bias_tables(rel_table):
    n, m = BLOCK, 3 * BLOCK
    rel = jnp.arange(-(2 * BLOCK - 1), 2 * BLOCK)
    by_rel = jnp.where((jnp.abs(rel) <= WINDOW)[:, None],
                       rel_table[_t5_bucket(rel)].astype(F32) * LOG2_E, NEG_INF)
    by_rel = jnp.pad(by_rel.T, ((0, 0), (0, 1)))
    skew = jnp.tile(by_rel, (1, n))[:, :n * (m + n - 1)].reshape(N_Q_HEADS, n, m + n - 1)
    bias = skew[:, :, n - 1:n - 1 + m]
    heads = [4 * h + half for h in range(N_KV_HEADS) for half in range(2)]
    units = jnp.concatenate([jnp.stack([bias[a] for a in heads]),
                             jnp.stack([bias[a + 2] for a in heads])], axis=1)
    masked = jnp.full((2 * N_KV_HEADS, 2 * BLOCK, BLOCK), NEG_INF, F32)
    prev = jnp.stack([units[:, :, :BLOCK], masked])
    cur = units[:, :, BLOCK:2 * BLOCK]
    nxt = jnp.stack([units[:, :, 2 * BLOCK:], masked])
    return prev, cur, nxt


def _prepare(rel_bias_table, w_in, b_in, w_pool, pool_scale, p_pool, sink, p_attn, w_mem_kv, p_mem,
             w_out, ln1_g, ln1_b, w_router_group, w_router_expert, w_gu, w_down):
    l = 0
    w_router = jnp.zeros((ROUTER_ROWS, D_MODEL), F32)
    w_router = w_router.at[:N_EXPERT_GROUPS].set(w_router_group[l].T)
    w_router = w_router.at[SUBLANES:].set(w_router_expert[l].T)
    bias_prev, bias_cur, bias_next = _bias_tables(rel_bias_table)
    return dict(
        w_in=w_in[l].astype(BF16), b_in=b_in[l][None, :],
        w_pool=w_pool[l].astype(BF16), pool_scale=pool_scale[l][None, :],
        p_pool=p_pool[l].astype(BF16), p_attn=p_attn[l].astype(BF16), p_mem=p_mem[l].astype(BF16),
        w_out=w_out[l].astype(BF16), w_memkv=w_mem_kv[l].astype(BF16),
        ln1_g=ln1_g[l][None, :], ln1_b=ln1_b[l][None, :],
        w_router=w_router.astype(BF16), sink=sink[l] * LOG2_E,
        w_gu=w_gu[l], w_down=w_down[l],
        bias_prev=bias_prev, bias_cur=bias_cur, bias_next=bias_next,
    )


def _dispatch_plan(eid, wts):
    t = eid.shape[1]
    a = TOP_K * t
    bm = BM_MOE
    nblk = a // bm
    eflat = eid.reshape(a)
    wflat = wts.reshape(a)
    shift = (a - 1).bit_length()
    assert N_EXPERTS << shift < 2 ** 31
    packed = jnp.sort(eflat * (1 << shift) + jnp.arange(a, dtype=I32))
    order = packed & ((1 << shift) - 1)
    wsorted = wflat[order].reshape(nblk, 1, bm)
    experts = jnp.arange(N_EXPERTS, dtype=I32)
    counts = jnp.sum((eflat[None, :] == experts[:, None]).astype(I32), axis=1)
    ends = jnp.cumsum(counts).astype(I32)
    starts = ends - counts
    cuts = jnp.sort(jnp.concatenate([jnp.arange(nblk, dtype=I32) * bm, starts, jnp.array([a], I32)]))
    lo_abs = cuts[:-1]
    hi_abs = cuts[1:]
    nonempty = hi_abs > lo_abs
    blk = jnp.minimum(lo_abs // bm, nblk - 1)
    probe = jnp.minimum(lo_abs, a - 1)
    item_e = jnp.sum((ends[None, :] <= probe[:, None]).astype(I32), axis=1)
    item_lo = jnp.where(nonempty, lo_abs - blk * bm, 0).astype(I32)
    item_hi = jnp.where(nonempty, hi_abs - blk * bm, 0).astype(I32)
    return order, wsorted, blk.astype(I32), item_e.astype(I32), item_lo, item_hi


def _trunk(x, mem, ln_in_g, ln_in_b, ln2_g, ln2_b, p):
    batch, seq, d = x.shape
    t = batch * seq
    assert seq % TM_MIX == 0 and seq % TQ_ATTN == 0 and t % TM_INPROJ == 0 and (TOP_K * t) % BM_MOE == 0
    memkv = _memkv_call(mem.reshape(batch * MEM_TOKENS, d), p["w_memkv"])
    xn, u, q, kk, vv, qm, gate = _inproj_call(x.reshape(t, d), ln_in_g[None, :], ln_in_b[None, :],
                                              p["w_in"], p["b_in"])
    o = _attn_call(q, kk, vv, p["sink"], p["bias_prev"], p["bias_cur"], p["bias_next"], batch, seq)
    x1r, eid, wts = _mix_call(xn, u, o, qm, gate, memkv, p, batch, seq)
    order, wsorted, item_blk, item_e, item_lo, item_hi = _dispatch_plan(eid, wts)
    sorted_tok = jnp.where(order >= t, order - t, order)
    xs = _sc_gather_rows(x1r, sorted_tok)
    ys = _moe_call(xs, wsorted, item_blk, item_e, item_lo, item_hi, p["w_gu"], p["w_down"])
    y2 = _sc_scatter_rows(ys, order)
    out = _final_call(x1r, y2, ln2_g[None, :], ln2_b[None, :])
    return out.reshape(batch, seq, d)


def kernel(x_prompt, x_sample, mem_prompt, mem_sample, ln_in_g, ln_in_b, rel_bias_table, w_in, b_in,
           w_pool, pool_scale, p_pool, sink, p_attn, w_mem_kv, p_mem, w_out, ln1_g, ln1_b,
           w_router_group, w_router_expert, w_gu, w_down, ln2_g, ln2_b):
    p = _prepare(rel_bias_table, w_in, b_in, w_pool, pool_scale, p_pool, sink, p_attn, w_mem_kv, p_mem,
                 w_out, ln1_g, ln1_b, w_router_group, w_router_expert, w_gu, w_down)
    y_prompt = _trunk(x_prompt, mem_prompt, ln_in_g, ln_in_b, ln2_g[0], ln2_b[0], p)
    y_sample = _trunk(x_sample, mem_sample, ln_in_g, ln_in_b, ln2_g[0], ln2_b[0], p)
    return (y_prompt, y_sample)
```
